```python
import jax, jax.numpy as jnp
from jax import lax
import numpy as np

D_MODEL = 1024
BATCH = 4
SEQ = 4096
DEPTH = 2

GRID_W = 64
CTX_LEN = 256
HEAD_DIM = 64
Q_BLOCK = 128
WINDOW = 128
ROPE_THETA = 10000.0
RMS_EPS = 1e-6
GN_EPS = 64e-5
N_EVEN = (DEPTH + 1) // 2
N_ODD = DEPTH // 2
A_HEADS = D_MODEL // (2 * HEAD_DIM)
A_KV_HEADS = A_HEADS // 4
B_HEADS = D_MODEL // (2 * HEAD_DIM)
B_LORA = 64
A_WIDTH = A_HEADS * HEAD_DIM
A_KV_WIDTH = A_KV_HEADS * HEAD_DIM
B_WIDTH = B_HEADS * HEAD_DIM
B_PROJ = 3 * B_WIDTH + 4 * B_LORA
EVEN_IN = A_WIDTH + 2 * A_KV_WIDTH + B_PROJ + A_WIDTH + B_WIDTH
EVEN_MIX = A_WIDTH + B_WIDTH
C_HEADS = D_MODEL // HEAD_DIM
C_KV_HEADS = C_HEADS // 4
C_WIDTH = C_HEADS * HEAD_DIM
C_KV_WIDTH = C_KV_HEADS * HEAD_DIM
ODD_IN = C_WIDTH + 2 * C_KV_WIDTH + C_WIDTH

kernel_name = 'hybrid_dit_gqa_rwkv7_swa_prefix'


def _rms(x, gain):
    xf = x.astype(jnp.float32)
    xf = xf * lax.rsqrt(jnp.mean(xf * xf, axis=-1, keepdims=True) + RMS_EPS)
    return xf.astype(x.dtype) * gain


def _heads(x, n):
    return x.reshape(x.shape[:-1] + (n, HEAD_DIM))


def _axial_rope(n_tokens, dtype):
    n_rows = n_tokens // GRID_W
    t = jnp.arange(n_rows * GRID_W)
    row = (t // GRID_W).astype(jnp.float32)
    col = (t % GRID_W).astype(jnp.float32)
    axis_dim = HEAD_DIM // 2
    inv = ROPE_THETA ** (-jnp.arange(0, axis_dim, 2, dtype=jnp.float32) / axis_dim)
    ang = jnp.concatenate([row[:, None] * inv, col[:, None] * inv], axis=-1)
    return jnp.cos(ang).astype(dtype), jnp.sin(ang).astype(dtype)


def _rope(x, cos, sin):
    x1, x2 = x[..., 0::2], x[..., 1::2]
    c, s = cos[None, :, None, :], sin[None, :, None, :]
    return jnp.stack([x1 * c - x2 * s, x1 * s + x2 * c], axis=-1).reshape(x.shape)


def _dense_attention(q, k, v):
    B, T, Hq, D = q.shape
    G = k.shape[2]
    R = Hq // G
    nb = T // Q_BLOCK
    scale = HEAD_DIM ** -0.5
    qb = jnp.moveaxis(q.reshape(B, nb, Q_BLOCK, G, R, D), 1, 0)

    def block(qi):
        s = jnp.einsum('bqgrd,bsgd->bgrqs', qi, k).astype(jnp.float32) * scale
        p = jax.nn.softmax(s, axis=-1).astype(v.dtype)
        return jnp.einsum('bgrqs,bsgd->bqgrd', p, v)

    o = lax.map(block, qb)
    return jnp.moveaxis(o, 0, 1).reshape(B, T, Hq * D)


def _window_attention(q, k, v, kc, vc, sink):
    B, T, Hq, D = q.shape
    G = k.shape[2]
    R = Hq // G
    L = kc.shape[1]
    nb = T // Q_BLOCK
    band = 3 * Q_BLOCK
    scale = HEAD_DIM ** -0.5
    pad = ((0, 0), (Q_BLOCK, Q_BLOCK), (0, 0), (0, 0))
    kp, vp = jnp.pad(k, pad), jnp.pad(v, pad)
    sink_b = jnp.broadcast_to(sink.reshape(G, R, 1, 1).astype(jnp.float32), (B, G, R, Q_BLOCK, 1))

    def block(i):
        start = i * Q_BLOCK
        qi = lax.dynamic_slice_in_dim(q, start, Q_BLOCK, axis=1).reshape(B, Q_BLOCK, G, R, D)
        ki = lax.dynamic_slice_in_dim(kp, start, band, axis=1)
        vi = lax.dynamic_slice_in_dim(vp, start, band, axis=1)
        qpos = start + jnp.arange(Q_BLOCK)
        kpos = start - Q_BLOCK + jnp.arange(band)
        mask = (kpos[None, :] >= 0) & (kpos[None, :] < T) & (jnp.abs(kpos[None, :] - qpos[:, None]) <= WINDOW)
        s_loc = jnp.einsum('bqgrd,bkgd->bgrqk', qi, ki).astype(jnp.float32) * scale
        s_loc = jnp.where(mask, s_loc, -jnp.inf)
        s_ctx = jnp.einsum('bqgrd,bmgd->bgrqm', qi, kc).astype(jnp.float32) * scale
        p = jax.nn.softmax(jnp.concatenate([s_loc, s_ctx, sink_b], axis=-1), axis=-1).astype(v.dtype)
        return (jnp.einsum('bgrqk,bkgd->bqgrd', p[..., :band], vi)
                + jnp.einsum('bgrqm,bmgd->bqgrd', p[..., band:band + L], vc))

    o = lax.map(block, jnp.arange(nb))
    return jnp.moveaxis(o, 0, 1).reshape(B, T, Hq * D)


def _context_attention(q, k, v, sink):
    B, L, Hq, D = q.shape
    G = k.shape[2]
    R = Hq // G
    scale = HEAD_DIM ** -0.5
    s = jnp.einsum('blgrd,bmgd->bgrlm', q.reshape(B, L, G, R, D), k).astype(jnp.float32) * scale
    if sink is not None:
        s = jnp.concatenate([s, jnp.broadcast_to(sink.reshape(G, R, 1, 1).astype(jnp.float32), (B, G, R, L, 1))], axis=-1)
    p = jax.nn.softmax(s, axis=-1)[..., :L].astype(v.dtype)
    return jnp.einsum('bgrlm,bmgd->blgrd', p, v).reshape(B, L, Hq * D)


def _rwkv_streams(p, mu, w0, w2, a0, a2, k_k, k_a):
    B, T = p.shape[:2]
    pp = jnp.pad(p, ((0, 0), (1, 1), (0, 0)))
    p = p + mu * (0.5 * (pp[:, :-2] + pp[:, 2:]) - p)
    r, k, v, zw, za = jnp.split(p, [B_WIDTH, 2 * B_WIDTH, 3 * B_WIDTH, 3 * B_WIDTH + 2 * B_LORA], axis=-1)
    zw = zw.reshape(B, T, 2, B_LORA)
    za = za.reshape(B, T, 2, B_LORA)
    w_log = -jax.nn.softplus(-(w0[:, None, None, :] + jnp.einsum('btdr,drc->dbtc', jnp.tanh(zw), w2))) - 0.5
    decay = jnp.exp(-jnp.exp(w_log.astype(jnp.float32))).astype(p.dtype)
    a = _heads(jax.nn.sigmoid(a0[:, None, None, :] + jnp.einsum('btdr,drc->dbtc', za, a2)), B_HEADS)
    kkf = _heads(k * k_k, B_HEADS).astype(jnp.float32)
    kk = (kkf / jnp.maximum(jnp.sqrt(jnp.sum(kkf * kkf, axis=-1, keepdims=True)), 1e-12)).astype(p.dtype)
    kmod = _heads(k, B_HEADS)[None] * (1 + (a - 1) * _heads(k_a, B_HEADS))
    return _heads(r, B_HEADS), _heads(decay, B_HEADS), kk, kk[None] * a, kmod, _heads(v, B_HEADS)


def _rwkv_bidir_scan(r, decay, kk, b, kmod, v, state0):
    def time_major(x):
        return jnp.moveaxis(jnp.stack([x[0], jnp.flip(x[1], axis=1)], axis=0), 2, 0)

    def both(x):
        return jnp.broadcast_to(x[None], (2,) + x.shape)

    xs = (time_major(both(r)), time_major(decay), time_major(both(kk)),
          time_major(b), time_major(kmod), time_major(both(v)))

    def step(S, inp):
        r_t, w_t, kk_t, b_t, k_t, v_t = inp
        S = (S * w_t[..., None, :]
             - jnp.einsum('dbhvk,dbhk->dbhv', S, kk_t)[..., None] * b_t[..., None, :]
             + v_t[..., None] * k_t[..., None, :])
        return S, jnp.einsum('dbhvk,dbhk->dbhv', S, r_t)

    S, y = lax.scan(step, state0, xs)
    y = jnp.moveaxis(y, 0, 2)
    return jnp.stack([y[0], jnp.flip(y[1], axis=1)], axis=0), S


def _rwkv_readout(y, r, kmod, v, rk, gn_w, gn_b):
    yf = y.astype(jnp.float32)
    mean = jnp.mean(yf, axis=-1, keepdims=True)
    var = jnp.mean(jnp.square(yf - mean), axis=-1, keepdims=True)
    yn = ((yf - mean) * lax.rsqrt(var + GN_EPS)).astype(y.dtype) * gn_w + gn_b
    bonus = jnp.sum(r[None] * kmod * rk, axis=-1, keepdims=True) * v[None]
    o = jnp.sum(yn + bonus, axis=0)
    return o.reshape(o.shape[:2] + (B_WIDTH,))


def _even_mixer(hx, hc, cos, sin, need_ctx, w_in, w_out, qn, kn, mu, w0, w2, a0, a2, k_k, k_a, rk, gn_w, gn_b):
    cuts = np.cumsum([A_WIDTH, A_KV_WIDTH, A_KV_WIDTH, B_PROJ, A_WIDTH]).tolist()
    qx, kx, vx, bx, gax, gbx = jnp.split(hx @ w_in, cuts, axis=-1)
    qc, kc, vc, bc, gac, gbc = jnp.split(hc @ w_in, cuts, axis=-1)
    qx = _rope(_rms(_heads(qx, A_HEADS), qn), cos, sin)
    kx = _rope(_rms(_heads(kx, A_KV_HEADS), kn), cos, sin)
    kc = _rms(_heads(kc, A_KV_HEADS), kn)
    vx, vc = _heads(vx, A_KV_HEADS), _heads(vc, A_KV_HEADS)
    oa_x = _dense_attention(qx, jnp.concatenate([kx, kc], axis=1), jnp.concatenate([vx, vc], axis=1))
    gn_w, gn_b = gn_w.reshape(B_HEADS, HEAD_DIM), gn_b.reshape(B_HEADS, HEAD_DIM)
    r_c, d_c, kk_c, b_c, km_c, v_c = _rwkv_streams(bc, mu, w0, w2, a0, a2, k_k, k_a)
    r_x, d_x, kk_x, b_x, km_x, v_x = _rwkv_streams(bx, mu, w0, w2, a0, a2, k_k, k_a)
    state0 = jnp.zeros((2, hx.shape[0], B_HEADS, HEAD_DIM, HEAD_DIM), hx.dtype)
    y_c, state_c = _rwkv_bidir_scan(r_c, d_c, kk_c, b_c, km_c, v_c, state0)
    y_x, _ = _rwkv_bidir_scan(r_x, d_x, kk_x, b_x, km_x, v_x, state_c)
    ob_x = _rwkv_readout(y_x, r_x, km_x, v_x, rk, gn_w, gn_b)
    out_x = jnp.concatenate([oa_x * jax.nn.silu(gax), ob_x * jax.nn.silu(gbx)], axis=-1) @ w_out
    if not need_ctx:
        return out_x, None
    oa_c = _context_attention(_rms(_heads(qc, A_HEADS), qn), kc, vc, None)
    ob_c = _rwkv_readout(y_c, r_c, km_c, v_c, rk, gn_w, gn_b)
    out_c = jnp.concatenate([oa_c * jax.nn.silu(gac), ob_c * jax.nn.silu(gbc)], axis=-1) @ w_out
    return out_x, out_c


def _odd_mixer(hx, hc, cos, sin, need_ctx, w_in, w_out, qn, kn, sink):
    cuts = np.cumsum([C_WIDTH, C_KV_WIDTH, C_KV_WIDTH]).tolist()
    qx, kx, vx, gx = jnp.split(hx @ w_in, cuts, axis=-1)
    if need_ctx:
        qc, kc, vc, gc = jnp.split(hc @ w_in, cuts, axis=-1)
    else:
        kc, vc = jnp.split(hc @ w_in[:, C_WIDTH:C_WIDTH + 2 * C_KV_WIDTH], 2, axis=-1)
    qx = _rope(_rms(_heads(qx, C_HEADS), qn), cos, sin)
    kx = _rope(_rms(_heads(kx, C_KV_HEADS), kn), cos, sin)
    kc = _rms(_heads(kc, C_KV_HEADS), kn)
    vc = _heads(vc, C_KV_HEADS)
    ox = _window_attention(qx, kx, _heads(vx, C_KV_HEADS), kc, vc, sink)
    out_x = (ox * jax.nn.silu(gx)) @ w_out
    if not need_ctx:
        return out_x, None
    oc = _context_attention(_rms(_heads(qc, C_HEADS), qn), kc, vc, sink)
    return out_x, (oc * jax.nn.silu(gc)) @ w_out


def setup_inputs(seed: int = 0) -> dict:
    key = jax.random.key(seed)
    ks = jax.random.split(key, 27)
    f32 = jnp.float32
    D = D_MODEL

    def nrm(k, shape, s):
        return jax.random.normal(k, shape, f32) * s

    return {
        'x': nrm(ks[0], (BATCH, SEQ, D), 1.0),
        'c': nrm(ks[1], (BATCH, D), 1.0),
        'ctx': nrm(ks[2], (BATCH, CTX_LEN, D), 1.0),
        'c_ctx': nrm(ks[3], (D,), 1.0),
        'w_mod': nrm(ks[4], (DEPTH, D, 3 * D), 0.5 * D ** -0.5),
        'b_mod': nrm(ks[5], (DEPTH, 3 * D), 0.01),
        'g_pre': 1.0 + nrm(ks[6], (DEPTH, D), 0.05),
        'g_post': 1.0 + nrm(ks[7], (DEPTH, D), 0.05),
        'w_in_even': nrm(ks[8], (N_EVEN, D, EVEN_IN), D ** -0.5),
        'w_out_even': nrm(ks[9], (N_EVEN, EVEN_MIX, D), EVEN_MIX ** -0.5),
        'qn_a': 1.0 + nrm(ks[10], (N_EVEN, HEAD_DIM), 0.05),
        'kn_a': 1.0 + nrm(ks[11], (N_EVEN, HEAD_DIM), 0.05),
        'mu_b': jax.random.uniform(ks[12], (N_EVEN, B_PROJ), f32, 0.1, 0.9),
        'w0_b': nrm(ks[13], (N_EVEN, 2, B_WIDTH), 0.5),
        'w2_b': nrm(ks[14], (N_EVEN, 2, B_LORA, B_WIDTH), B_LORA ** -0.5),
        'a0_b': nrm(ks[15], (N_EVEN, 2, B_WIDTH), 0.5),
        'a2_b': nrm(ks[16], (N_EVEN, 2, B_LORA, B_WIDTH), B_LORA ** -0.5),
        'kk_b': 0.85 + nrm(ks[17], (N_EVEN, B_WIDTH), 0.05),
        'ka_b': 1.0 + nrm(ks[18], (N_EVEN, B_WIDTH), 0.05),
        'rk_b': nrm(ks[19], (N_EVEN, B_HEADS, HEAD_DIM), 0.1),
        'gn_w_b': 1.0 + nrm(ks[20], (N_EVEN, B_WIDTH), 0.05),
        'gn_b_b': nrm(ks[21], (N_EVEN, B_WIDTH), 0.01),
        'w_in_odd': nrm(ks[22], (N_ODD, D, ODD_IN), D ** -0.5),
        'w_out_odd': nrm(ks[23], (N_ODD, C_WIDTH, D), C_WIDTH ** -0.5),
        'qn_c': 1.0 + nrm(ks[24], (N_ODD, HEAD_DIM), 0.05),
        'kn_c': 1.0 + nrm(ks[25], (N_ODD, HEAD_DIM), 0.05),
        'sink_c': nrm(ks[26], (N_ODD, C_HEADS), 0.5),
    }


def reference(x, c, ctx, c_ctx, w_mod, b_mod, g_pre, g_post, w_in_even, w_out_even, qn_a, kn_a,
              mu_b, w0_b, w2_b, a0_b, a2_b, kk_b, ka_b, rk_b, gn_w_b, gn_b_b,
              w_in_odd, w_out_odd, qn_c, kn_c, sink_c):
    cos, sin = _axial_rope(x.shape[1], x.dtype)
    for l in range(DEPTH):
        need_ctx = l < DEPTH - 1
        mod_x = jax.nn.silu(c) @ w_mod[l] + b_mod[l]
        mod_c = jax.nn.silu(c_ctx) @ w_mod[l] + b_mod[l]
        sh_x, sc_x, gt_x = jnp.split(mod_x[:, None, :], 3, axis=-1)
        sh_c, sc_c, gt_c = jnp.split(mod_c, 3, axis=-1)
        hx = _rms(x, g_pre[l]) * (1 + sc_x) + sh_x
        hc = _rms(ctx, g_pre[l]) * (1 + sc_c) + sh_c
        if l % 2 == 0:
            i = l // 2
            yx, yc = _even_mixer(hx, hc, cos, sin, need_ctx, w_in_even[i], w_out_even[i], qn_a[i], kn_a[i],
                                 mu_b[i], w0_b[i], w2_b[i], a0_b[i], a2_b[i], kk_b[i], ka_b[i], rk_b[i],
                                 gn_w_b[i], gn_b_b[i])
        else:
            j = l // 2
            yx, yc = _odd_mixer(hx, hc, cos, sin, need_ctx, w_in_odd[j], w_out_odd[j], qn_c[j], kn_c[j], sink_c[j])
        x = x + gt_x * _rms(yx, g_post[l])
        if need_ctx:
            ctx = ctx + gt_c * _rms(yc, g_post[l])
    return x
```

```python
import functools

import numpy as np
import jax
import jax.numpy as jnp
from jax import lax
from jax.experimental import pallas as pl
from jax.experimental.pallas import tpu as pltpu

F32 = jnp.float32
BF16 = jnp.bfloat16
HIGHEST = lax.Precision.HIGHEST

HEAD_DIM = 64
LANES = 128
GRID_W = 64
Q_BLOCK = 128
WINDOW = 128
ROPE_THETA = 10000.0
RMS_EPS = 1e-6
GN_EPS = 64e-5
B_LORA = 64
CHUNK = 64
ROW_TILE = 256
VMEM_LIMIT = 56 * 1024 * 1024


def _cparams(sem):
    return pltpu.CompilerParams(dimension_semantics=sem, vmem_limit_bytes=VMEM_LIMIT)


def _split_dot(a, g_bf16):
    hi = a.astype(BF16)
    lo = (a - hi.astype(F32)).astype(BF16)
    return (jnp.dot(hi, g_bf16, preferred_element_type=F32)
            + jnp.dot(lo, g_bf16, preferred_element_type=F32))


def _sigmoid(z):
    return 1.0 / (1.0 + jnp.exp(-z))


def _silu(z):
    return z * _sigmoid(z)


def _mod_kernel(c_ref, w_ref, b_ref, o_ref):
    o_ref[...] = jnp.dot(_silu(c_ref[...]), w_ref[...], precision=HIGHEST,
                         preferred_element_type=F32) + b_ref[...]


def _modulation(cc, w_mod, b_mod):
    depth, d, d3 = w_mod.shape
    nj = d3 // d
    return pl.pallas_call(
        _mod_kernel,
        grid=(depth, nj),
        in_specs=[pl.BlockSpec((8, d), lambda l, j: (0, 0)),
                  pl.BlockSpec((None, d, d), lambda l, j: (l, 0, j)),
                  pl.BlockSpec((None, 1, d), lambda l, j: (l, 0, j))],
        out_specs=pl.BlockSpec((None, 8, d), lambda l, j: (l, 0, j)),
        out_shape=jax.ShapeDtypeStruct((depth, 8, d3), F32),
        compiler_params=_cparams(("arbitrary", "arbitrary")),
        name="modulation",
    )(cc, w_mod, b_mod.reshape(depth, 1, d3))


def _inproj_kernel(x_ref, mod_ref, gpre_ref, w_ref, cos_ref, sin_ref, gain_ref, seg_ref,
                   *out_refs, q_width, qk_width, splits):
    x = x_ref[...]
    ms = jnp.mean(x * x, axis=-1, keepdims=True)
    h = x * lax.rsqrt(ms + RMS_EPS) * gpre_ref[...]
    h = h * (1.0 + mod_ref[1:2, :]) + mod_ref[0:1, :]
    acc = jnp.dot(h.astype(BF16), w_ref[...], preferred_element_type=F32)

    cos = cos_ref[...]
    sin = sin_ref[...]
    seg = seg_ref[...]
    lane = lax.broadcasted_iota(jnp.int32, (1, LANES), 1)
    first_half = (lane % HEAD_DIM) < (HEAD_DIM // 2)
    qk_ref = out_refs[0]
    for g in range(qk_width // LANES):
        xg = acc[:, g * LANES:(g + 1) * LANES]
        msq = _split_dot(xg * xg, seg)
        xn = xg * lax.rsqrt(msq + RMS_EPS) * gain_ref[:, g * LANES:(g + 1) * LANES]
        swapped = jnp.where(first_half,
                            pltpu.roll(xn, LANES - HEAD_DIM // 2, 1),
                            pltpu.roll(xn, HEAD_DIM // 2, 1))
        y = xn * cos + swapped * sin
        if g * LANES < q_width:
            y = y * (HEAD_DIM ** -0.5)
        qk_ref[:, g * LANES:(g + 1) * LANES] = y.astype(qk_ref.dtype)
    for ref, (lo, hi) in zip(out_refs[1:], splits):
        ref[...] = acc[:, lo:hi].astype(ref.dtype)


def _inproj(xc, modsel, gpre, w_bf16, cos_t, sin_t, gain, seg, *, q_width, qk_width, splits,
            out_dtypes, ctx_tiles):
    b, s, d = xc.shape
    n = w_bf16.shape[1]
    tm = ROW_TILE
    grid = (b, s // tm)
    row = lambda bi, i: (bi, i, 0)
    out_shapes = [jax.ShapeDtypeStruct((b, s, qk_width), BF16)]
    out_specs = [pl.BlockSpec((None, tm, qk_width), row)]
    for (lo, hi), dt in zip(splits, out_dtypes):
        out_shapes.append(jax.ShapeDtypeStruct((b, s, hi - lo), dt))
        out_specs.append(pl.BlockSpec((None, tm, hi - lo), row))
    kern = functools.partial(_inproj_kernel, q_width=q_width, qk_width=qk_width, splits=splits)
    return pl.pallas_call(
        kern,
        grid=grid,
        in_specs=[pl.BlockSpec((None, tm, d), row),
                  pl.BlockSpec((None, None, 3, d),
                               lambda bi, i: (bi, jnp.minimum(i // ctx_tiles, 1), 0, 0)),
                  pl.BlockSpec((1, d), lambda bi, i: (0, 0)),
                  pl.BlockSpec((d, n), lambda bi, i: (0, 0)),
                  pl.BlockSpec((tm, LANES), lambda bi, i: (i, 0)),
                  pl.BlockSpec((tm, LANES), lambda bi, i: (i, 0)),
                  pl.BlockSpec((1, qk_width), lambda bi, i: (0, 0)),
                  pl.BlockSpec((LANES, LANES), lambda bi, i: (0, 0))],
        out_specs=out_specs,
        out_shape=out_shapes,
        compiler_params=_cparams(("parallel", "parallel")),
        name="inproj",
    )(xc, modsel, gpre, w_bf16, cos_t, sin_t, gain, seg)


def _dense_attn_kernel(q_ref, k_ref, v_ref, o_ref, *, tk, ctx_tiles, n_kv_ctx, n_kv_all):
    i = pl.program_id(2)
    r, tq, dh = q_ref.shape
    m_rows = r * tq
    q = q_ref[...].reshape(m_rows, dh)
    n_kv = jnp.where(i < ctx_tiles, n_kv_ctx, n_kv_all)

    def body(j, carry):
        m, l, acc = carry
        start = pl.multiple_of(j * tk, tk)
        ks = k_ref[pl.ds(start, tk), :]
        vs = v_ref[pl.ds(start, tk), :]
        s = lax.dot_general(q, ks, (((1,), (1,)), ((), ())), preferred_element_type=F32)
        m_new = jnp.maximum(m, jnp.max(s, axis=-1, keepdims=True))
        alpha = jnp.exp(m - m_new)
        p = jnp.exp(s - m_new)
        l = alpha * l + jnp.sum(p, axis=-1, keepdims=True)
        acc = alpha * acc + jnp.dot(p.astype(BF16), vs, preferred_element_type=F32)
        return m_new, l, acc

    m0 = jnp.full((m_rows, 1), -jnp.inf, F32)
    l0 = jnp.zeros((m_rows, 1), F32)
    a0 = jnp.zeros((m_rows, dh), F32)
    _, l, acc = lax.fori_loop(0, n_kv, body, (m0, l0, a0))
    o_ref[...] = (acc / l).reshape(r, tq, dh).astype(o_ref.dtype)


def _dense_attention(qh, kh, vh, *, ctx_len):
    b, hq, s, dh = qh.shape
    g = kh.shape[1]
    r = hq // g
    tq = Q_BLOCK
    tk = 256
    kern = functools.partial(_dense_attn_kernel, tk=tk, ctx_tiles=ctx_len // tq,
                             n_kv_ctx=ctx_len // tk, n_kv_all=s // tk)
    return pl.pallas_call(
        kern,
        grid=(b, g, s // tq),
        in_specs=[pl.BlockSpec((None, r, tq, dh), lambda bi, gi, i: (bi, gi, i, 0)),
                  pl.BlockSpec((None, None, s, dh), lambda bi, gi, i: (bi, gi, 0, 0)),
                  pl.BlockSpec((None, None, s, dh), lambda bi, gi, i: (bi, gi, 0, 0))],
        out_specs=pl.BlockSpec((None, r, tq, dh), lambda bi, gi, i: (bi, gi, i, 0)),
        out_shape=jax.ShapeDtypeStruct((b, hq, s, dh), F32),
        compiler_params=_cparams(("parallel", "parallel", "parallel")),
        name="dense_attention",
    )(qh, kh, vh)


def _window_attn_kernel(sink_ref, q_ref, kc_ref, vc_ref, kp_ref, kq_ref, kn_ref,
                        vp_ref, vq_ref, vn_ref, o_ref, *, n_blocks):
    gi = pl.program_id(1)
    i = pl.program_id(2)
    r, tq, dh = q_ref.shape
    m_rows = r * tq
    q = q_ref[...].reshape(m_rows, dh)
    lctx = kc_ref.shape[0]
    kcat = jnp.concatenate([kp_ref[...], kq_ref[...], kn_ref[...], kc_ref[...]], axis=0)
    vcat = jnp.concatenate([vp_ref[...], vq_ref[...], vn_ref[...], vc_ref[...]], axis=0)
    s = lax.dot_general(q, kcat, (((1,), (1,)), ((), ())), preferred_element_type=F32)
    n_keys = 3 * tq + lctx
    row = lax.broadcasted_iota(jnp.int32, (m_rows, n_keys), 0)
    col = lax.broadcasted_iota(jnp.int32, (m_rows, n_keys), 1)
    qpos = row % tq
    kpos = col - tq
    band = (jnp.abs(kpos - qpos) <= WINDOW)
    band = band & ((col >= tq) | (i > 0)) & ((col < 2 * tq) | (i < n_blocks - 1))
    mask = band | (col >= 3 * tq)
    s = jnp.where(mask, s, -jnp.inf)
    head = lax.broadcasted_iota(jnp.int32, (m_rows, 1), 0) // tq
    sink = jnp.zeros((m_rows, 1), F32)
    for hh in range(r):
        sink = jnp.where(head == hh, sink_ref[gi * r + hh], sink)
    m = jnp.maximum(jnp.max(s, axis=-1, keepdims=True), sink)
    p = jnp.exp(s - m)
    l = jnp.sum(p, axis=-1, keepdims=True) + jnp.exp(sink - m)
    o = jnp.dot(p.astype(BF16), vcat, preferred_element_type=F32) / l
    o_ref[...] = o.reshape(r, tq, dh).astype(o_ref.dtype)


def _window_attention(qh, kh, vh, sink, *, ctx_len):
    b, hq, s, dh = qh.shape
    g = kh.shape[1]
    r = hq // g
    tq = Q_BLOCK
    t = s - ctx_len
    nb = t // tq
    off = ctx_len // tq
    cur = lambda bi, gi, i: (bi, gi, i + off, 0)
    prev = lambda bi, gi, i: (bi, gi, jnp.maximum(i - 1, 0) + off, 0)
    nxt = lambda bi, gi, i: (bi, gi, jnp.minimum(i + 1, nb - 1) + off, 0)
    ctx = lambda bi, gi, i: (bi, gi, 0, 0)
    blk = (None, None, tq, dh)
    kern = functools.partial(_window_attn_kernel, n_blocks=nb)
    return pl.pallas_call(
        kern,
        grid=(b, g, nb),
        in_specs=[pl.BlockSpec(memory_space=pltpu.SMEM),
                  pl.BlockSpec((None, r, tq, dh), cur),
                  pl.BlockSpec((None, None, ctx_len, dh), ctx),
                  pl.BlockSpec((None, None, ctx_len, dh), ctx),
                  pl.BlockSpec(blk, prev), pl.BlockSpec(blk, cur), pl.BlockSpec(blk, nxt),
                  pl.BlockSpec(blk, prev), pl.BlockSpec(blk, cur), pl.BlockSpec(blk, nxt)],
        out_specs=pl.BlockSpec((None, r, tq, dh), lambda bi, gi, i: (bi, gi, i, 0)),
        out_shape=jax.ShapeDtypeStruct((b, hq, t, dh), F32),
        compiler_params=_cparams(("parallel", "parallel", "parallel")),
        name="window_attention",
    )(sink, qh, kh, vh, kh, kh, kh, vh, vh, vh)


def _rwkv_prep_kernel(p_ref, hp_ref, hn_ref, mu_ref, w0_ref, w2_ref, a0_ref, a2_ref,
                      kk_ref, ka_ref, rk_ref, ones_ref,
                      m_ref, n_ref, q_ref, y0_ref, bonus_ref, *, ctx_chunks, n_chunks):
    j = pl.program_id(1)
    c = CHUNK
    bw = kk_ref.shape[1]
    n_pairs = bw // LANES

    p = p_ref[...]
    has_prev = jnp.logical_and(j != 0, j != ctx_chunks)
    has_next = jnp.logical_and(j != ctx_chunks - 1, j != n_chunks - 1)
    prev_row = jnp.where(has_prev, hp_ref[7:8, :], 0.0)
    next_row = jnp.where(has_next, hn_ref[0:1, :], 0.0)
    rowi = lax.broadcasted_iota(jnp.int32, (c, 1), 0)
    p_prev = jnp.where(rowi == 0, prev_row, pltpu.roll(p, 1, 0))
    p_next = jnp.where(rowi == c - 1, next_row, pltpu.roll(p, c - 1, 0))
    ps = p + mu_ref[...] * (0.5 * (p_prev + p_next) - p)

    r = ps[:, 0:bw]
    k = ps[:, bw:2 * bw]
    v = ps[:, 2 * bw:3 * bw]
    zw = ps[:, 3 * bw:3 * bw + 2 * B_LORA]
    za = ps[:, 3 * bw + 2 * B_LORA:3 * bw + 4 * B_LORA]

    wl = w0_ref[...] + jnp.dot(jnp.tanh(zw), w2_ref[...], precision=HIGHEST,
                               preferred_element_type=F32)
    z = -wl
    softplus = jnp.maximum(z, 0.0) + jnp.log(1.0 + jnp.exp(-jnp.abs(z)))
    logw = -jnp.exp(-softplus - 0.5)
    a = _sigmoid(a0_ref[...] + jnp.dot(za, a2_ref[...], precision=HIGHEST,
                                       preferred_element_type=F32))

    ones_seg = ones_ref[...]
    kf = k * kk_ref[...]
    kf2 = kf * kf
    ss = jnp.concatenate([_split_dot(kf2[:, g * LANES:(g + 1) * LANES], ones_seg)
                          for g in range(n_pairs)], axis=1)
    kk = kf / jnp.maximum(jnp.sqrt(ss), 1e-12)

    ti = lax.broadcasted_iota(jnp.int32, (c, c), 0)
    tj = lax.broadcasted_iota(jnp.int32, (c, c), 1)
    eye = (ti == tj).astype(F32)
    lane = lax.broadcasted_iota(jnp.int32, (1, LANES), 1)
    head_mask = [lane < HEAD_DIM, lane >= HEAD_DIM]
    lane2 = lax.broadcasted_iota(jnp.int32, (1, 2 * LANES), 1)
    head0_wide = (lane2 % LANES) < HEAD_DIM
    bi = lax.broadcasted_iota(jnp.int32, (LANES, LANES), 0)
    bj = lax.broadcasted_iota(jnp.int32, (LANES, LANES), 1)
    same_head = (bi // HEAD_DIM) == (bj // HEAD_DIM)
    diag = bi == bj

    bonus = jnp.zeros((c, bw), F32)
    for d in range(2):
        lw = logw[:, d * bw:(d + 1) * bw]
        a_d = a[:, d * bw:(d + 1) * bw]
        b_d = kk * a_d
        kmod = k * (1.0 + (a_d - 1.0) * ka_ref[...])
        rkk = r * kmod * rk_ref[...]
        bsum = jnp.concatenate([_split_dot(rkk[:, g * LANES:(g + 1) * LANES], ones_seg)
                                for g in range(n_pairs)], axis=1)
        bonus = bonus + bsum * v

        if d == 0:
            tri = (tj <= ti).astype(F32)
            strict = tj < ti
            incl = tj <= ti
            end_row = c - 1
        else:
            tri = (tj >= ti).astype(F32)
            strict = tj > ti
            incl = tj >= ti
            end_row = 0
        cum = jnp.dot(tri, lw, precision=HIGHEST, preferred_element_type=F32)
        cum_end = cum[end_row:end_row + 1, :]
        e_in = jnp.exp(cum)
        e_out = jnp.exp(-cum)
        r_t = r * e_in
        kk_t = kk * jnp.exp(cum - lw)
        k_h = kmod * e_out
        b_h = b_d * e_out
        tail = jnp.exp(cum_end - cum)
        k_bar = kmod * tail
        b_bar = b_d * tail
        w_end = jnp.exp(cum_end)

        for pr in range(n_pairs):
            sl = slice(pr * LANES, (pr + 1) * LANES)
            v_p = v[:, sl].astype(BF16)
            kkt_p = kk_t[:, sl]
            rt_p = r_t[:, sl]
            x_full = jnp.concatenate([kkt_p, rt_p], axis=0)
            kh_p = k_h[:, sl].astype(BF16)
            bh_p = b_h[:, sl].astype(BF16)
            for hh in range(2):
                xm = jnp.where(head_mask[hh], x_full, 0.0).astype(BF16)
                ak = lax.dot_general(xm, kh_p, (((1,), (1,)), ((), ())), preferred_element_type=F32)
                ab = lax.dot_general(xm, bh_p, (((1,), (1,)), ((), ())), preferred_element_type=F32)
                a_kk = jnp.where(strict, ak[0:c], 0.0)
                a_qk = jnp.where(incl, ak[c:2 * c], 0.0)
                a_kb = jnp.where(strict, ab[0:c], 0.0)
                a_qb = jnp.where(incl, ab[c:2 * c], 0.0)
                pw = -a_kb
                tinv = eye + pw
                for _ in range(5):
                    pw = jnp.dot(pw, pw, precision=HIGHEST, preferred_element_type=F32)
                    tinv = tinv + jnp.dot(tinv, pw, precision=HIGHEST, preferred_element_type=F32)
                av = jnp.dot(jnp.concatenate([a_kk, a_qk], axis=0).astype(BF16), v_p,
                             preferred_element_type=F32)
                tp = jnp.dot(tinv.astype(BF16),
                             jnp.concatenate([kkt_p, av[0:c]], axis=1).astype(BF16),
                             preferred_element_type=F32)
                qp = jnp.dot(a_qb.astype(BF16), tp.astype(BF16), preferred_element_type=F32)
                q_h = rt_p - qp[:, 0:LANES]
                y0_h = av[c:2 * c] - qp[:, LANES:2 * LANES]
                if hh == 0:
                    p12_pair, q_pair, y0_pair = tp, q_h, y0_h
                else:
                    p12_pair = jnp.where(head0_wide, p12_pair, tp)
                    q_pair = jnp.where(head_mask[0], q_pair, q_h)
                    y0_pair = jnp.where(head_mask[0], y0_pair, y0_h)
            bp = lax.dot_general(b_bar[:, sl].astype(BF16), p12_pair.astype(BF16),
                                 (((0,), (0,)), ((), ())), preferred_element_type=F32)
            kv = lax.dot_general(k_bar[:, sl].astype(BF16), v_p,
                                 (((0,), (0,)), ((), ())), preferred_element_type=F32)
            w_diag = jnp.where(diag, w_end[:, sl], 0.0)
            m_ref[d, pr] = w_diag - jnp.where(same_head, bp[:, 0:LANES], 0.0)
            n_ref[d, pr] = jnp.where(same_head, kv - bp[:, LANES:2 * LANES], 0.0)
            q_ref[d, :, sl] = q_pair
            y0_ref[d, :, sl] = y0_pair
    bonus_ref[...] = bonus


def _rwkv_prep(bp, mu, w0, w2cat, a0, a2cat, kkw, kaw, rkw, ones_seg, *, ctx_len):
    b, s, pw = bp.shape
    bw = kkw.shape[1]
    n_pairs = bw // LANES
    nc = s // CHUNK
    ctx_chunks = ctx_len // CHUNK
    rb = CHUNK // 8
    kern = functools.partial(_rwkv_prep_kernel, ctx_chunks=ctx_chunks, n_chunks=nc)
    const = lambda bi, j: (0, 0)
    return pl.pallas_call(
        kern,
        grid=(b, nc),
        in_specs=[pl.BlockSpec((None, CHUNK, pw), lambda bi, j: (bi, j, 0)),
                  pl.BlockSpec((None, 8, pw), lambda bi, j: (bi, jnp.maximum(j * rb - 1, 0), 0)),
                  pl.BlockSpec((None, 8, pw), lambda bi, j: (bi, jnp.minimum((j + 1) * rb, s // 8 - 1), 0)),
                  pl.BlockSpec((1, pw), const),
                  pl.BlockSpec((1, 2 * bw), const),
                  pl.BlockSpec((2 * B_LORA, 2 * bw), const),
                  pl.BlockSpec((1, 2 * bw), const),
                  pl.BlockSpec((2 * B_LORA, 2 * bw), const),
                  pl.BlockSpec((1, bw), const),
                  pl.BlockSpec((1, bw), const),
                  pl.BlockSpec((1, bw), const),
                  pl.BlockSpec((LANES, LANES), const)],
        out_specs=[pl.BlockSpec((None, None, 2, n_pairs, LANES, LANES), lambda bi, j: (bi, j, 0, 0, 0, 0)),
                   pl.BlockSpec((None, None, 2, n_pairs, LANES, LANES), lambda bi, j: (bi, j, 0, 0, 0, 0)),
                   pl.BlockSpec((None, 2, CHUNK, bw), lambda bi, j: (bi, 0, j, 0)),
                   pl.BlockSpec((None, 2, CHUNK, bw), lambda bi, j: (bi, 0, j, 0)),
                   pl.BlockSpec((None, CHUNK, bw), lambda bi, j: (bi, j, 0))],
        out_shape=[jax.ShapeDtypeStruct((b, nc, 2, n_pairs, LANES, LANES), F32),
                   jax.ShapeDtypeStruct((b, nc, 2, n_pairs, LANES, LANES), F32),
                   jax.ShapeDtypeStruct((b, 2, s, bw), F32),
                   jax.ShapeDtypeStruct((b, 2, s, bw), F32),
                   jax.ShapeDtypeStruct((b, s, bw), F32)],
        compiler_params=_cparams(("parallel", "parallel")),
        name="rwkv_prep",
    )(bp, bp, bp, mu, w0, w2cat, a0, a2cat, kkw, kaw, rkw, ones_seg)


def _rwkv_scan_kernel(m0_ref, m1_ref, n0_ref, n1_ref, q0_ref, q1_ref, y00_ref, y01_ref,
                      o0_ref, o1_ref, h_ref):
    j = pl.program_id(0)

    @pl.when(j == 0)
    def _():
        h_ref[...] = jnp.zeros_like(h_ref)

    nb, n_pairs = m0_ref.shape[0], m0_ref.shape[1]
    dirs = ((m0_ref, n0_ref, q0_ref, y00_ref, o0_ref), (m1_ref, n1_ref, q1_ref, y01_ref, o1_ref))
    for d, (m_ref, n_ref, q_ref, y0_ref, o_ref) in enumerate(dirs):
        for bi in range(nb):
            for pr in range(n_pairs):
                sl = slice(pr * LANES, (pr + 1) * LANES)
                h = h_ref[d, bi, pr]
                hb = h.astype(BF16)
                o_ref[bi, :, sl] = y0_ref[bi, :, sl] + jnp.dot(
                    q_ref[bi, :, sl].astype(BF16), hb, preferred_element_type=F32)
                h_ref[d, bi, pr] = n_ref[bi, pr] + jnp.dot(
                    m_ref[bi, pr].astype(BF16), hb, preferred_element_type=F32)


def _rwkv_scan(m, n, q, y0, *, ctx_len):
    b, nc, _, n_pairs, _, _ = m.shape
    s, bw = q.shape[2], q.shape[3]
    cc = ctx_len // CHUNK

    def mem_chunk(d, j):
        if d == 0:
            return j
        return jnp.where(j < cc, cc - 1 - j, nc - 1 + cc - j)

    def mn_spec(d):
        return pl.BlockSpec((b, None, None, n_pairs, LANES, LANES),
                            lambda j: (0, mem_chunk(d, j), d, 0, 0, 0))

    def row_spec(d):
        return pl.BlockSpec((b, None, CHUNK, bw), lambda j: (0, d, mem_chunk(d, j), 0))

    return pl.pallas_call(
        _rwkv_scan_kernel,
        grid=(nc,),
        in_specs=[mn_spec(0), mn_spec(1), mn_spec(0), mn_spec(1),
                  row_spec(0), row_spec(1), row_spec(0), row_spec(1)],
        out_specs=[pl.BlockSpec((b, CHUNK, bw), lambda j: (0, mem_chunk(0, j), 0)),
                   pl.BlockSpec((b, CHUNK, bw), lambda j: (0, mem_chunk(1, j), 0))],
        out_shape=[jax.ShapeDtypeStruct((b, s, bw), F32), jax.ShapeDtypeStruct((b, s, bw), F32)],
        scratch_shapes=[pltpu.VMEM((2, b, n_pairs, LANES, LANES), F32)],
        compiler_params=_cparams(("arbitrary",)),
        name="rwkv_scan",
    )(m, m, n, n, q, q, y0, y0)


def _outproj_kernel(*refs, n_parts, gn):
    o_refs = refs[:n_parts]
    (g_ref, w_ref, x_ref, mod_ref, gpost_ref, seg_ref, gnw_ref, gnb_ref, bonus_ref,
     out_ref) = refs[n_parts:]
    parts = [r[...] for r in o_refs]
    if gn:
        seg = seg_ref[...]
        y_sum = bonus_ref[...]
        for y in parts[-2:]:
            cols = []
            for g in range(y.shape[1] // LANES):
                yg = y[:, g * LANES:(g + 1) * LANES]
                mean = _split_dot(yg, seg)
                yc = yg - mean
                var = _split_dot(yc * yc, seg)
                cols.append(yc * lax.rsqrt(var + GN_EPS))
            yn = jnp.concatenate(cols, axis=1)
            y_sum = y_sum + yn * gnw_ref[...] + gnb_ref[...]
        parts = parts[:-2] + [y_sum]
    o = jnp.concatenate(parts, axis=1) if len(parts) > 1 else parts[0]
    u = (o * _silu(g_ref[...])).astype(BF16)
    y = jnp.dot(u, w_ref[...], preferred_element_type=F32)
    ms = jnp.mean(y * y, axis=-1, keepdims=True)
    yn = y * lax.rsqrt(ms + RMS_EPS) * gpost_ref[...]
    out_ref[...] = x_ref[...] + mod_ref[2:3, :] * yn


def _outproj(parts, gate, w_bf16, xc, modsel, gpost, seg, gnw, gnb, bonus, *, gn, ctx_tiles,
             latent_only):
    b, s, d = xc.shape
    tm = ROW_TILE
    off = ctx_tiles if latent_only else 0
    n_tiles = s // tm - off
    row = lambda bi, i: (bi, i + off, 0)
    const = lambda bi, i: (0, 0)
    in_specs = [pl.BlockSpec((None, tm, p.shape[2]), lambda bi, i: (bi, i, 0)) for p in parts]
    in_specs += [pl.BlockSpec((None, tm, gate.shape[2]), row),
                 pl.BlockSpec(w_bf16.shape, const),
                 pl.BlockSpec((None, tm, d), row),
                 pl.BlockSpec((None, None, 3, d),
                              lambda bi, i: (bi, jnp.minimum((i + off) // ctx_tiles, 1), 0, 0)),
                 pl.BlockSpec((1, d), const),
                 pl.BlockSpec((LANES, LANES), const),
                 pl.BlockSpec(gnw.shape, const),
                 pl.BlockSpec(gnb.shape, const),
                 pl.BlockSpec((None, tm, bonus.shape[2]), row)]
    kern = functools.partial(_outproj_kernel, n_parts=len(parts), gn=gn)
    return pl.pallas_call(
        kern,
        grid=(b, n_tiles),
        in_specs=in_specs,
        out_specs=pl.BlockSpec((None, tm, d), lambda bi, i: (bi, i, 0)),
        out_shape=jax.ShapeDtypeStruct((b, n_tiles * tm, d), F32),
        compiler_params=_cparams(("parallel", "parallel")),
        name="outproj",
    )(*parts, gate, w_bf16, xc, modsel, gpost, seg, gnw, gnb, bonus)


def _deinterleave_perm(n_heads):
    within = np.concatenate([np.arange(0, HEAD_DIM, 2), np.arange(1, HEAD_DIM, 2)])
    return np.concatenate([h * HEAD_DIM + within for h in range(n_heads)])


def _rope_tables(n_latent, ctx_len):
    t = jnp.arange(n_latent)
    rowp = (t // GRID_W).astype(F32)
    colp = (t % GRID_W).astype(F32)
    axis_dim = HEAD_DIM // 2
    inv = ROPE_THETA ** (-jnp.arange(0, axis_dim, 2, dtype=F32) / axis_dim)
    ang = jnp.concatenate([rowp[:, None] * inv, colp[:, None] * inv], axis=-1)
    cos, sin = jnp.cos(ang), jnp.sin(ang)
    cos = jnp.concatenate([jnp.ones((ctx_len, axis_dim), F32), cos], axis=0)
    sin = jnp.concatenate([jnp.zeros((ctx_len, axis_dim), F32), sin], axis=0)
    cos_t = jnp.concatenate([cos, cos, cos, cos], axis=1)
    sin_t = jnp.concatenate([-sin, sin, -sin, sin], axis=1)
    return cos_t, sin_t


def _to_heads(a, n_heads):
    b, s, _ = a.shape
    return a.reshape(b, s, n_heads, HEAD_DIM).transpose(0, 2, 1, 3)


def _from_heads(a):
    b, h, s, dh = a.shape
    return a.transpose(0, 2, 1, 3).reshape(b, s, h * dh)


def _block_diag_lora(w2):
    r, w = w2.shape[1], w2.shape[2]
    z = jnp.zeros((r, w), w2.dtype)
    return jnp.concatenate([jnp.concatenate([w2[0], z], axis=1),
                            jnp.concatenate([z, w2[1]], axis=1)], axis=0)


def kernel(x, c, ctx, c_ctx, w_mod, b_mod, g_pre, g_post, w_in_even, w_out_even, qn_a, kn_a, mu_b, w0_b, w2_b, a0_b, a2_b, kk_b, ka_b, rk_b, gn_w_b, gn_b_b, w_in_odd, w_out_odd, qn_c, kn_c, sink_c):
    b, t, d = x.shape
    ctx_len = ctx.shape[1]
    s = ctx_len + t
    assert ctx_len % ROW_TILE == 0 and t % ROW_TILE == 0 and b + 1 <= 8
    ctx_tiles = ctx_len // ROW_TILE
    depth = w_mod.shape[0]

    xc = jnp.concatenate([ctx, x], axis=1)

    cc = jnp.concatenate([c, c_ctx[None, :], jnp.zeros((8 - b - 1, d), F32)], axis=0)
    mod = _modulation(cc, w_mod, b_mod)
    mod = mod.reshape(depth, 8, 3, d)
    modsel = jnp.stack([jnp.broadcast_to(mod[:, b][:, None], (depth, b, 3, d)), mod[:, :b]], axis=2)

    cos_t, sin_t = _rope_tables(t, ctx_len)
    seg_mean = jnp.asarray(np.kron(np.eye(2), np.full((HEAD_DIM, HEAD_DIM), 1.0 / HEAD_DIM)), BF16)
    seg_ones = jnp.asarray(np.kron(np.eye(2), np.ones((HEAD_DIM, HEAD_DIM))), BF16)

    bw = kk_b.shape[1]
    a_width = w_out_even.shape[1] - bw
    a_heads = a_width // HEAD_DIM
    n_in = w_in_even.shape[2]
    b_proj = 3 * bw + 4 * B_LORA
    kv_width = (n_in - 2 * a_width - b_proj - bw) // 2
    a_kv_heads = kv_width // HEAD_DIM
    qk_width = a_width + kv_width
    perm = np.arange(n_in)
    perm[:a_width] = _deinterleave_perm(a_heads)
    perm[a_width:qk_width] = a_width + _deinterleave_perm(a_kv_heads)
    w_in0 = w_in_even[0][:, perm].astype(BF16)
    within = _deinterleave_perm(1)
    gain0 = jnp.concatenate([jnp.tile(qn_a[0][within], a_heads), jnp.tile(kn_a[0][within], a_kv_heads)])[None, :]
    v_lo = qk_width
    bp_lo = v_lo + kv_width
    g_lo = bp_lo + b_proj
    qk0, v0, bproj, gate0 = _inproj(
        xc, modsel[0], g_pre[0][None, :], w_in0, cos_t, sin_t, gain0, seg_mean,
        q_width=a_width, qk_width=qk_width,
        splits=((v_lo, bp_lo), (bp_lo, g_lo), (g_lo, n_in)), out_dtypes=(BF16, F32, F32),
        ctx_tiles=ctx_tiles)
    qh = _to_heads(qk0[..., :a_width], a_heads)
    kh = _to_heads(qk0[..., a_width:], a_kv_heads)
    vh = _to_heads(v0, a_kv_heads)
    oa = _from_heads(_dense_attention(qh, kh, vh, ctx_len=ctx_len))

    m_c, n_c, q_c, y0_c, bonus = _rwkv_prep(
        bproj, mu_b[0][None, :], w0_b[0].reshape(1, 2 * bw), _block_diag_lora(w2_b[0]),
        a0_b[0].reshape(1, 2 * bw), _block_diag_lora(a2_b[0]), kk_b[0][None, :], ka_b[0][None, :],
        rk_b[0].reshape(1, bw), seg_ones, ctx_len=ctx_len)
    y_f, y_b = _rwkv_scan(m_c, n_c, q_c, y0_c, ctx_len=ctx_len)

    xc = _outproj([oa, y_f, y_b], gate0, w_out_even[0].astype(BF16), xc, modsel[0], g_post[0][None, :],
                  seg_mean, gn_w_b[0][None, :], gn_b_b[0][None, :], bonus,
                  gn=True, ctx_tiles=ctx_tiles, latent_only=False)

    c_heads = sink_c.shape[1]
    c_width = c_heads * HEAD_DIM
    n_in1 = w_in_odd.shape[2]
    ckv_width = (n_in1 - 2 * c_width) // 2
    c_kv_heads = ckv_width // HEAD_DIM
    qk_width1 = c_width + ckv_width
    perm1 = np.arange(n_in1)
    perm1[:c_width] = _deinterleave_perm(c_heads)
    perm1[c_width:qk_width1] = c_width + _deinterleave_perm(c_kv_heads)
    w_in1 = w_in_odd[0][:, perm1].astype(BF16)
    gain1 = jnp.concatenate([jnp.tile(qn_c[0][within], c_heads), jnp.tile(kn_c[0][within], c_kv_heads)])[None, :]
    qk1, v1, gate1 = _inproj(
        xc, modsel[1], g_pre[1][None, :], w_in1, cos_t, sin_t, gain1, seg_mean,
        q_width=c_width, qk_width=qk_width1,
        splits=((qk_width1, qk_width1 + ckv_width), (qk_width1 + ckv_width, n_in1)),
        out_dtypes=(BF16, F32), ctx_tiles=ctx_tiles)
    qh1 = _to_heads(qk1[..., :c_width], c_heads)
    kh1 = _to_heads(qk1[..., c_width:], c_kv_heads)
    vh1 = _to_heads(v1, c_kv_heads)
    ow = _from_heads(_window_attention(qh1, kh1, vh1, sink_c[0], ctx_len=ctx_len))
    dummy = jnp.zeros((b, s, LANES), F32)
    return _outproj([ow], gate1, w_out_odd[0].astype(BF16), xc, modsel[1], g_post[1][None, :],
                    seg_mean, jnp.zeros((1, LANES), F32), jnp.zeros((1, LANES), F32), dummy,
                    gn=False, ctx_tiles=ctx_tiles, latent_only=True)
```

```python
import functools

import numpy as np
import jax
import jax.numpy as jnp
from jax import lax
from jax.experimental import pallas as pl
from jax.experimental.pallas import tpu as pltpu

F32 = jnp.float32
BF16 = jnp.bfloat16
HIGHEST = lax.Precision.HIGHEST

HEAD_DIM = 64
LANES = 128
GRID_W = 64
Q_BLOCK = 128
WINDOW = 128
ROPE_THETA = 10000.0
RMS_EPS = 1e-6
GN_EPS = 64e-5
B_LORA = 64
CHUNK = 64
ROW_TILE = 256
VMEM_LIMIT = 56 * 1024 * 1024


def _cparams(sem):
    return pltpu.CompilerParams(dimension_semantics=sem, vmem_limit_bytes=VMEM_LIMIT)


def _split_dot(a, g_bf16):
    hi = a.astype(BF16)
    lo = (a - hi.astype(F32)).astype(BF16)
    return (jnp.dot(hi, g_bf16, preferred_element_type=F32)
            + jnp.dot(lo, g_bf16, preferred_element_type=F32))


def _sigmoid(z):
    return 1.0 / (1.0 + jnp.exp(-z))


def _silu(z):
    return z * _sigmoid(z)


def _mod_kernel(c_ref, w_ref, b_ref, o_ref):
    o_ref[...] = jnp.dot(_silu(c_ref[...]), w_ref[...], precision=HIGHEST,
                         preferred_element_type=F32) + b_ref[...]


def _modulation(cc, w_mod, b_mod):
    depth, d, d3 = w_mod.shape
    nj = d3 // d
    return pl.pallas_call(
        _mod_kernel,
        grid=(depth, nj),
        in_specs=[pl.BlockSpec((8, d), lambda l, j: (0, 0)),
                  pl.BlockSpec((None, d, d), lambda l, j: (l, 0, j)),
                  pl.BlockSpec((None, 1, d), lambda l, j: (l, 0, j))],
        out_specs=pl.BlockSpec((None, 8, d), lambda l, j: (l, 0, j)),
        out_shape=jax.ShapeDtypeStruct((depth, 8, d3), F32),
        compiler_params=_cparams(("arbitrary", "arbitrary")),
        name="modulation",
    )(cc, w_mod, b_mod.reshape(depth, 1, d3))


def _inproj_kernel(x_ref, mod_ref, gpre_ref, w_ref, cos_ref, sin_ref, gain_ref, seg_ref,
                   *out_refs, q_width, qk_width, splits):
    x = x_ref[...]
    ms = jnp.mean(x * x, axis=-1, keepdims=True)
    h = x * lax.rsqrt(ms + RMS_EPS) * gpre_ref[...]
    h = h * (1.0 + mod_ref[1:2, :]) + mod_ref[0:1, :]
    acc = jnp.dot(h.astype(BF16), w_ref[...], preferred_element_type=F32)

    cos = cos_ref[...]
    sin = sin_ref[...]
    seg = seg_ref[...]
    lane = lax.broadcasted_iota(jnp.int32, (1, LANES), 1)
    first_half = (lane % HEAD_DIM) < (HEAD_DIM // 2)
    qk_ref = out_refs[0]
    for g in range(qk_width // LANES):
        xg = acc[:, g * LANES:(g + 1) * LANES]
        msq = _split_dot(xg * xg, seg)
        xn = xg * lax.rsqrt(msq + RMS_EPS) * gain_ref[:, g * LANES:(g + 1) * LANES]
        swapped = jnp.where(first_half,
                            pltpu.roll(xn, LANES - HEAD_DIM // 2, 1),
                            pltpu.roll(xn, HEAD_DIM // 2, 1))
        y = xn * cos + swapped * sin
        if g * LANES < q_width:
            y = y * (HEAD_DIM ** -0.5)
        qk_ref[:, g * LANES:(g + 1) * LANES] = y.astype(qk_ref.dtype)
    for ref, (lo, hi) in zip(out_refs[1:], splits):
        ref[...] = acc[:, lo:hi].astype(ref.dtype)


def _inproj(xc, modsel, gpre, w_bf16, cos_t, sin_t, gain, seg, *, q_width, qk_width, splits,
            out_dtypes, ctx_tiles):
    b, s, d = xc.shape
    n = w_bf16.shape[1]
    tm = ROW_TILE
    grid = (b, s // tm)
    row = lambda bi, i: (bi, i, 0)
    out_shapes = [jax.ShapeDtypeStruct((b, s, qk_width), BF16)]
    out_specs = [pl.BlockSpec((None, tm, qk_width), row)]
    for (lo, hi), dt in zip(splits, out_dtypes):
        out_shapes.append(jax.ShapeDtypeStruct((b, s, hi - lo), dt))
        out_specs.append(pl.BlockSpec((None, tm, hi - lo), row))
    kern = functools.partial(_inproj_kernel, q_width=q_width, qk_width=qk_width, splits=splits)
    return pl.pallas_call(
        kern,
        grid=grid,
        in_specs=[pl.BlockSpec((None, tm, d), row),
                  pl.BlockSpec((None, None, 3, d),
                               lambda bi, i: (bi, jnp.minimum(i // ctx_tiles, 1), 0, 0)),
                  pl.BlockSpec((1, d), lambda bi, i: (0, 0)),
                  pl.BlockSpec((d, n), lambda bi, i: (0, 0)),
                  pl.BlockSpec((tm, LANES), lambda bi, i: (i, 0)),
                  pl.BlockSpec((tm, LANES), lambda bi, i: (i, 0)),
                  pl.BlockSpec((1, qk_width), lambda bi, i: (0, 0)),
                  pl.BlockSpec((LANES, LANES), lambda bi, i: (0, 0))],
        out_specs=out_specs,
        out_shape=out_shapes,
        compiler_params=_cparams(("parallel", "parallel")),
        name="inproj",
    )(xc, modsel, gpre, w_bf16, cos_t, sin_t, gain, seg)


def _dense_attn_kernel(q_ref, k_ref, v_ref, o_ref, *, tk, ctx_tiles, n_kv_ctx, n_kv_all):
    i = pl.program_id(2)
    r, tq, dh = q_ref.shape
    m_rows = r * tq
    q = q_ref[...].reshape(m_rows, dh)
    n_kv = jnp.where(i < ctx_tiles, n_kv_ctx, n_kv_all)

    def body(j, carry):
        m, l, acc = carry
        start = pl.multiple_of(j * tk, tk)
        ks = k_ref[pl.ds(start, tk), :]
        vs = v_ref[pl.ds(start, tk), :]
        s = lax.dot_general(q, ks, (((1,), (1,)), ((), ())), preferred_element_type=F32)
        m_new = jnp.maximum(m, jnp.max(s, axis=-1, keepdims=True))
        alpha = jnp.exp(m - m_new)
        p = jnp.exp(s - m_new)
        l = alpha * l + jnp.sum(p, axis=-1, keepdims=True)
        acc = alpha * acc + jnp.dot(p.astype(BF16), vs, preferred_element_type=F32)
        return m_new, l, acc

    m0 = jnp.full((m_rows, 1), -jnp.inf, F32)
    l0 = jnp.zeros((m_rows, 1), F32)
    a0 = jnp.zeros((m_rows, dh), F32)
    _, l, acc = lax.fori_loop(0, n_kv, body, (m0, l0, a0))
    o_ref[...] = (acc / l).reshape(r, tq, dh).astype(o_ref.dtype)


def _dense_attention(qh, kh, vh, *, ctx_len):
    b, hq, s, dh = qh.shape
    g = kh.shape[1]
    r = hq // g
    tq = Q_BLOCK
    tk = 256
    kern = functools.partial(_dense_attn_kernel, tk=tk, ctx_tiles=ctx_len // tq,
                             n_kv_ctx=ctx_len // tk, n_kv_all=s // tk)
    return pl.pallas_call(
        kern,
        grid=(b, g, s // tq),
        in_specs=[pl.BlockSpec((None, r, tq, dh), lambda bi, gi, i: (bi, gi, i, 0)),
                  pl.BlockSpec((None, None, s, dh), lambda bi, gi, i: (bi, gi, 0, 0)),
                  pl.BlockSpec((None, None, s, dh), lambda bi, gi, i: (bi, gi, 0, 0))],
        out_specs=pl.BlockSpec((None, r, tq, dh), lambda bi, gi, i: (bi, gi, i, 0)),
        out_shape=jax.ShapeDtypeStruct((b, hq, s, dh), F32),
        compiler_params=_cparams(("parallel", "parallel", "parallel")),
        name="dense_attention",
    )(qh, kh, vh)


def _window_attn_kernel(sink_ref, q_ref, kc_ref, vc_ref, kp_ref, kq_ref, kn_ref,
                        vp_ref, vq_ref, vn_ref, o_ref, *, n_blocks):
    gi = pl.program_id(1)
    i = pl.program_id(2)
    r, tq, dh = q_ref.shape
    m_rows = r * tq
    q = q_ref[...].reshape(m_rows, dh)
    lctx = kc_ref.shape[0]
    kcat = jnp.concatenate([kp_ref[...], kq_ref[...], kn_ref[...], kc_ref[...]], axis=0)
    vcat = jnp.concatenate([vp_ref[...], vq_ref[...], vn_ref[...], vc_ref[...]], axis=0)
    s = lax.dot_general(q, kcat, (((1,), (1,)), ((), ())), preferred_element_type=F32)
    n_keys = 3 * tq + lctx
    row = lax.broadcasted_iota(jnp.int32, (m_rows, n_keys), 0)
    col = lax.broadcasted_iota(jnp.int32, (m_rows, n_keys), 1)
    qpos = row % tq
    kpos = col - tq
    band = (jnp.abs(kpos - qpos) <= WINDOW)
    band = band & ((col >= tq) | (i > 0)) & ((col < 2 * tq) | (i < n_blocks - 1))
    mask = band | (col >= 3 * tq)
    s = jnp.where(mask, s, -jnp.inf)
    head = lax.broadcasted_iota(jnp.int32, (m_rows, 1), 0) // tq
    sink = jnp.zeros((m_rows, 1), F32)
    for hh in range(r):
        sink = jnp.where(head == hh, sink_ref[gi * r + hh], sink)
    m = jnp.maximum(jnp.max(s, axis=-1, keepdims=True), sink)
    p = jnp.exp(s - m)
    l = jnp.sum(p, axis=-1, keepdims=True) + jnp.exp(sink - m)
    o = jnp.dot(p.astype(BF16), vcat, preferred_element_type=F32) / l
    o_ref[...] = o.reshape(r, tq, dh).astype(o_ref.dtype)


def _window_attention(qh, kh, vh, sink, *, ctx_len):
    b, hq, s, dh = qh.shape
    g = kh.shape[1]
    r = hq // g
    tq = Q_BLOCK
    t = s - ctx_len
    nb = t // tq
    off = ctx_len // tq
    cur = lambda bi, gi, i: (bi, gi, i + off, 0)
    prev = lambda bi, gi, i: (bi, gi, jnp.maximum(i - 1, 0) + off, 0)
    nxt = lambda bi, gi, i: (bi, gi, jnp.minimum(i + 1, nb - 1) + off, 0)
    ctx = lambda bi, gi, i: (bi, gi, 0, 0)
    blk = (None, None, tq, dh)
    kern = functools.partial(_window_attn_kernel, n_blocks=nb)
    return pl.pallas_call(
        kern,
        grid=(b, g, nb),
        in_specs=[pl.BlockSpec(memory_space=pltpu.SMEM),
                  pl.BlockSpec((None, r, tq, dh), cur),
                  pl.BlockSpec((None, None, ctx_len, dh), ctx),
                  pl.BlockSpec((None, None, ctx_len, dh), ctx),
                  pl.BlockSpec(blk, prev), pl.BlockSpec(blk, cur), pl.BlockSpec(blk, nxt),
                  pl.BlockSpec(blk, prev), pl.BlockSpec(blk, cur), pl.BlockSpec(blk, nxt)],
        out_specs=pl.BlockSpec((None, r, tq, dh), lambda bi, gi, i: (bi, gi, i, 0)),
        out_shape=jax.ShapeDtypeStruct((b, hq, t, dh), F32),
        compiler_params=_cparams(("parallel", "parallel", "parallel")),
        name="window_attention",
    )(sink, qh, kh, vh, kh, kh, kh, vh, vh, vh)


def _rwkv_prep_kernel(p_ref, hp_ref, hn_ref, mu_ref, w0_ref, w2_ref, a0_ref, a2_ref,
                      kk_ref, ka_ref, rk_ref, ones_ref,
                      m_ref, n_ref, q_ref, y0_ref, bonus_ref, *, ctx_chunks, n_chunks):
    j = pl.program_id(1)
    c = CHUNK
    bw = kk_ref.shape[1]
    n_pairs = bw // LANES

    p = p_ref[...]
    has_prev = jnp.logical_and(j != 0, j != ctx_chunks)
    has_next = jnp.logical_and(j != ctx_chunks - 1, j != n_chunks - 1)
    prev_row = jnp.where(has_prev, hp_ref[7:8, :], 0.0)
    next_row = jnp.where(has_next, hn_ref[0:1, :], 0.0)
    rowi = lax.broadcasted_iota(jnp.int32, (c, 1), 0)
    p_prev = jnp.where(rowi == 0, prev_row, pltpu.roll(p, 1, 0))
    p_next = jnp.where(rowi == c - 1, next_row, pltpu.roll(p, c - 1, 0))
    ps = p + mu_ref[...] * (0.5 * (p_prev + p_next) - p)

    r = ps[:, 0:bw]
    k = ps[:, bw:2 * bw]
    v = ps[:, 2 * bw:3 * bw]
    zw = ps[:, 3 * bw:3 * bw + 2 * B_LORA]
    za = ps[:, 3 * bw + 2 * B_LORA:3 * bw + 4 * B_LORA]

    wl = w0_ref[...] + jnp.dot(jnp.tanh(zw), w2_ref[...], precision=HIGHEST,
                               preferred_element_type=F32)
    z = -wl
    softplus = jnp.maximum(z, 0.0) + jnp.log(1.0 + jnp.exp(-jnp.abs(z)))
    logw = -jnp.exp(-softplus - 0.5)
    a = _sigmoid(a0_ref[...] + jnp.dot(za, a2_ref[...], precision=HIGHEST,
                                       preferred_element_type=F32))

    ones_seg = ones_ref[...]
    kf = k * kk_ref[...]
    kf2 = kf * kf
    ss = jnp.concatenate([_split_dot(kf2[:, g * LANES:(g + 1) * LANES], ones_seg)
                          for g in range(n_pairs)], axis=1)
    kk = kf / jnp.maximum(jnp.sqrt(ss), 1e-12)

    ti = lax.broadcasted_iota(jnp.int32, (c, c), 0)
    tj = lax.broadcasted_iota(jnp.int32, (c, c), 1)
    eye = (ti == tj).astype(F32)
    lane = lax.broadcasted_iota(jnp.int32, (1, LANES), 1)
    head_mask = [lane < HEAD_DIM, lane >= HEAD_DIM]
    lane2 = lax.broadcasted_iota(jnp.int32, (1, 2 * LANES), 1)
    head0_wide = (lane2 % LANES) < HEAD_DIM
    bi = lax.broadcasted_iota(jnp.int32, (LANES, LANES), 0)
    bj = lax.broadcasted_iota(jnp.int32, (LANES, LANES), 1)
    same_head = (bi // HEAD_DIM) == (bj // HEAD_DIM)
    diag = bi == bj

    bonus = jnp.zeros((c, bw), F32)
    pairs = []
    for d in range(2):
        lw = logw[:, d * bw:(d + 1) * bw]
        a_d = a[:, d * bw:(d + 1) * bw]
        b_d = kk * a_d
        kmod = k * (1.0 + (a_d - 1.0) * ka_ref[...])
        rkk = r * kmod * rk_ref[...]
        bsum = jnp.concatenate([_split_dot(rkk[:, g * LANES:(g + 1) * LANES], ones_seg)
                                for g in range(n_pairs)], axis=1)
        bonus = bonus + bsum * v

        if d == 0:
            tri = (tj <= ti).astype(F32)
            strict = tj < ti
            incl = tj <= ti
            end_row = c - 1
        else:
            tri = (tj >= ti).astype(F32)
            strict = tj > ti
            incl = tj >= ti
            end_row = 0
        cum = jnp.dot(tri, lw, precision=HIGHEST, preferred_element_type=F32)
        cum_end = cum[end_row:end_row + 1, :]
        e_in = jnp.exp(cum)
        e_out = jnp.exp(-cum)
        r_t = r * e_in
        kk_t = kk * jnp.exp(cum - lw)
        k_h = kmod * e_out
        b_h = b_d * e_out
        tail = jnp.exp(cum_end - cum)
        k_bar = kmod * tail
        b_bar = b_d * tail
        w_end = jnp.exp(cum_end)

        for pr in range(n_pairs):
            sl = slice(pr * LANES, (pr + 1) * LANES)
            x_full = jnp.concatenate([kk_t[:, sl], r_t[:, sl]], axis=0)
            pairs.append(dict(d=d, sl=sl, strict=strict, incl=incl, x_full=x_full,
                              v=v[:, sl].astype(BF16), kkt=kk_t[:, sl], rt=r_t[:, sl],
                              kh=k_h[:, sl].astype(BF16), bh=b_h[:, sl].astype(BF16),
                              kbar=k_bar[:, sl].astype(BF16), bbar=b_bar[:, sl].astype(BF16),
                              w_end=w_end[:, sl]))
    bonus_ref[...] = bonus

    contract_lanes = (((1,), (1,)), ((), ()))
    contract_rows = (((0,), (0,)), ((), ()))
    heads = []
    for pi, pp in enumerate(pairs):
        for hh in range(2):
            xm = jnp.where(head_mask[hh], pp["x_full"], 0.0).astype(BF16)
            ak = lax.dot_general(xm, pp["kh"], contract_lanes, preferred_element_type=F32)
            ab = lax.dot_general(xm, pp["bh"], contract_lanes, preferred_element_type=F32)
            heads.append(dict(
                pi=pi, hh=hh,
                a_k=jnp.concatenate([jnp.where(pp["strict"], ak[0:c], 0.0),
                                     jnp.where(pp["incl"], ak[c:2 * c], 0.0)], axis=0).astype(BF16),
                a_kb=jnp.where(pp["strict"], ab[0:c], 0.0),
                a_qb=jnp.where(pp["incl"], ab[c:2 * c], 0.0).astype(BF16)))

    pws = [-hd["a_kb"] for hd in heads]
    tinvs = [eye + pw for pw in pws]
    for _ in range(5):
        pws = [jnp.dot(pw.astype(BF16), pw.astype(BF16), preferred_element_type=F32) for pw in pws]
        tinvs = [t + jnp.dot(t.astype(BF16), pw.astype(BF16), preferred_element_type=F32)
                 for t, pw in zip(tinvs, pws)]

    avs = [jnp.dot(hd["a_k"], pairs[hd["pi"]]["v"], preferred_element_type=F32) for hd in heads]
    tps = [jnp.dot(t.astype(BF16),
                   jnp.concatenate([pairs[hd["pi"]]["kkt"], av[0:c]], axis=1).astype(BF16),
                   preferred_element_type=F32)
           for hd, t, av in zip(heads, tinvs, avs)]
    qps = [jnp.dot(hd["a_qb"], tp.astype(BF16), preferred_element_type=F32) for hd, tp in zip(heads, tps)]

    for pi, pp in enumerate(pairs):
        h0, h1 = 2 * pi, 2 * pi + 1
        p12_pair = jnp.where(head0_wide, tps[h0], tps[h1]).astype(BF16)
        qp_pair = jnp.where(head0_wide, qps[h0], qps[h1])
        yloc_pair = jnp.where(head_mask[0], avs[h0][c:2 * c], avs[h1][c:2 * c])
        bp = lax.dot_general(pp["bbar"], p12_pair, contract_rows, preferred_element_type=F32)
        kv = lax.dot_general(pp["kbar"], pp["v"], contract_rows, preferred_element_type=F32)
        d, sl = pp["d"], pp["sl"]
        pr = pi % n_pairs
        w_diag = jnp.where(diag, pp["w_end"], 0.0)
        m_ref[d, pr] = w_diag - jnp.where(same_head, bp[:, 0:LANES], 0.0)
        n_ref[d, pr] = jnp.where(same_head, kv - bp[:, LANES:2 * LANES], 0.0)
        q_ref[d, :, sl] = pp["rt"] - qp_pair[:, 0:LANES]
        y0_ref[d, :, sl] = yloc_pair - qp_pair[:, LANES:2 * LANES]


def _rwkv_prep(bp, mu, w0, w2cat, a0, a2cat, kkw, kaw, rkw, ones_seg, *, ctx_len):
    b, s, pw = bp.shape
    bw = kkw.shape[1]
    n_pairs = bw // LANES
    nc = s // CHUNK
    ctx_chunks = ctx_len // CHUNK
    rb = CHUNK // 8
    kern = functools.partial(_rwkv_prep_kernel, ctx_chunks=ctx_chunks, n_chunks=nc)
    const = lambda bi, j: (0, 0)
    return pl.pallas_call(
        kern,
        grid=(b, nc),
        in_specs=[pl.BlockSpec((None, CHUNK, pw), lambda bi, j: (bi, j, 0)),
                  pl.BlockSpec((None, 8, pw), lambda bi, j: (bi, jnp.maximum(j * rb - 1, 0), 0)),
                  pl.BlockSpec((None, 8, pw), lambda bi, j: (bi, jnp.minimum((j + 1) * rb, s // 8 - 1), 0)),
                  pl.BlockSpec((1, pw), const),
                  pl.BlockSpec((1, 2 * bw), const),
                  pl.BlockSpec((2 * B_LORA, 2 * bw), const),
                  pl.BlockSpec((1, 2 * bw), const),
                  pl.BlockSpec((2 * B_LORA, 2 * bw), const),
                  pl.BlockSpec((1, bw), const),
                  pl.BlockSpec((1, bw), const),
                  pl.BlockSpec((1, bw), const),
                  pl.BlockSpec((LANES, LANES), const)],
        out_specs=[pl.BlockSpec((None, None, 2, n_pairs, LANES, LANES), lambda bi, j: (bi, j, 0, 0, 0, 0)),
                   pl.BlockSpec((None, None, 2, n_pairs, LANES, LANES), lambda bi, j: (bi, j, 0, 0, 0, 0)),
                   pl.BlockSpec((None, 2, CHUNK, bw), lambda bi, j: (bi, 0, j, 0)),
                   pl.BlockSpec((None, 2, CHUNK, bw), lambda bi, j: (bi, 0, j, 0)),
                   pl.BlockSpec((None, CHUNK, bw), lambda bi, j: (bi, j, 0))],
        out_shape=[jax.ShapeDtypeStruct((b, nc, 2, n_pairs, LANES, LANES), F32),
                   jax.ShapeDtypeStruct((b, nc, 2, n_pairs, LANES, LANES), F32),
                   jax.ShapeDtypeStruct((b, 2, s, bw), F32),
                   jax.ShapeDtypeStruct((b, 2, s, bw), F32),
                   jax.ShapeDtypeStruct((b, s, bw), F32)],
        compiler_params=_cparams(("parallel", "parallel")),
        name="rwkv_prep",
    )(bp, bp, bp, mu, w0, w2cat, a0, a2cat, kkw, kaw, rkw, ones_seg)


def _rwkv_scan_kernel(m0_ref, m1_ref, n0_ref, n1_ref, q0_ref, q1_ref, y00_ref, y01_ref,
                      o0_ref, o1_ref, h_ref):
    j = pl.program_id(0)

    @pl.when(j == 0)
    def _():
        h_ref[...] = jnp.zeros_like(h_ref)

    nb, n_pairs = m0_ref.shape[0], m0_ref.shape[1]
    dirs = ((m0_ref, n0_ref, q0_ref, y00_ref, o0_ref), (m1_ref, n1_ref, q1_ref, y01_ref, o1_ref))
    for d, (m_ref, n_ref, q_ref, y0_ref, o_ref) in enumerate(dirs):
        for bi in range(nb):
            for pr in range(n_pairs):
                sl = slice(pr * LANES, (pr + 1) * LANES)
                h = h_ref[d, bi, pr]
                hb = h.astype(BF16)
                o_ref[bi, :, sl] = y0_ref[bi, :, sl] + jnp.dot(
                    q_ref[bi, :, sl].astype(BF16), hb, preferred_element_type=F32)
                h_ref[d, bi, pr] = n_ref[bi, pr] + jnp.dot(
                    m_ref[bi, pr].astype(BF16), hb, preferred_element_type=F32)


def _rwkv_scan(m, n, q, y0, *, ctx_len):
    b, nc, _, n_pairs, _, _ = m.shape
    s, bw = q.shape[2], q.shape[3]
    cc = ctx_len // CHUNK

    def mem_chunk(d, j):
        if d == 0:
            return j
        return jnp.where(j < cc, cc - 1 - j, nc - 1 + cc - j)

    def mn_spec(d):
        return pl.BlockSpec((b, None, None, n_pairs, LANES, LANES),
                            lambda j: (0, mem_chunk(d, j), d, 0, 0, 0))

    def row_spec(d):
        return pl.BlockSpec((b, None, CHUNK, bw), lambda j: (0, d, mem_chunk(d, j), 0))

    return pl.pallas_call(
        _rwkv_scan_kernel,
        grid=(nc,),
        in_specs=[mn_spec(0), mn_spec(1), mn_spec(0), mn_spec(1),
                  row_spec(0), row_spec(1), row_spec(0), row_spec(1)],
        out_specs=[pl.BlockSpec((b, CHUNK, bw), lambda j: (0, mem_chunk(0, j), 0)),
                   pl.BlockSpec((b, CHUNK, bw), lambda j: (0, mem_chunk(1, j), 0))],
        out_shape=[jax.ShapeDtypeStruct((b, s, bw), F32), jax.ShapeDtypeStruct((b, s, bw), F32)],
        scratch_shapes=[pltpu.VMEM((2, b, n_pairs, LANES, LANES), F32)],
        compiler_params=_cparams(("arbitrary",)),
        name="rwkv_scan",
    )(m, m, n, n, q, q, y0, y0)


def _outproj_kernel(*refs, n_parts, gn):
    o_refs = refs[:n_parts]
    (g_ref, w_ref, x_ref, mod_ref, gpost_ref, seg_ref, gnw_ref, gnb_ref, bonus_ref,
     out_ref) = refs[n_parts:]
    parts = [r[...] for r in o_refs]
    if gn:
        seg = seg_ref[...]
        y_sum = bonus_ref[...]
        for y in parts[-2:]:
            cols = []
            for g in range(y.shape[1] // LANES):
                yg = y[:, g * LANES:(g + 1) * LANES]
                mean = _split_dot(yg, seg)
                yc = yg - mean
                var = _split_dot(yc * yc, seg)
                cols.append(yc * lax.rsqrt(var + GN_EPS))
            yn = jnp.concatenate(cols, axis=1)
            y_sum = y_sum + yn * gnw_ref[...] + gnb_ref[...]
        parts = parts[:-2] + [y_sum]
    o = jnp.concatenate(parts, axis=1) if len(parts) > 1 else parts[0]
    u = (o * _silu(g_ref[...])).astype(BF16)
    y = jnp.dot(u, w_ref[...], preferred_element_type=F32)
    ms = jnp.mean(y * y, axis=-1, keepdims=True)
    yn = y * lax.rsqrt(ms + RMS_EPS) * gpost_ref[...]
    out_ref[...] = x_ref[...] + mod_ref[2:3, :] * yn


def _outproj(parts, gate, w_bf16, xc, modsel, gpost, seg, gnw, gnb, bonus, *, gn, ctx_tiles,
             latent_only):
    b, s, d = xc.shape
    tm = ROW_TILE
    off = ctx_tiles if latent_only else 0
    n_tiles = s // tm - off
    row = lambda bi, i: (bi, i + off, 0)
    const = lambda bi, i: (0, 0)
    in_specs = [pl.BlockSpec((None, tm, p.shape[2]), lambda bi, i: (bi, i, 0)) for p in parts]
    in_specs += [pl.BlockSpec((None, tm, gate.shape[2]), row),
                 pl.BlockSpec(w_bf16.shape, const),
                 pl.BlockSpec((None, tm, d), row),
                 pl.BlockSpec((None, None, 3, d),
                              lambda bi, i: (bi, jnp.minimum((i + off) // ctx_tiles, 1), 0, 0)),
                 pl.BlockSpec((1, d), const),
                 pl.BlockSpec((LANES, LANES), const),
                 pl.BlockSpec(gnw.shape, const),
                 pl.BlockSpec(gnb.shape, const),
                 pl.BlockSpec((None, tm, bonus.shape[2]), row)]
    kern = functools.partial(_outproj_kernel, n_parts=len(parts), gn=gn)
    return pl.pallas_call(
        kern,
        grid=(b, n_tiles),
        in_specs=in_specs,
        out_specs=pl.BlockSpec((None, tm, d), lambda bi, i: (bi, i, 0)),
        out_shape=jax.ShapeDtypeStruct((b, n_tiles * tm, d), F32),
        compiler_params=_cparams(("parallel", "parallel")),
        name="outproj",
    )(*parts, gate, w_bf16, xc, modsel, gpost, seg, gnw, gnb, bonus)


def _deinterleave_perm(n_heads):
    within = np.concatenate([np.arange(0, HEAD_DIM, 2), np.arange(1, HEAD_DIM, 2)])
    return np.concatenate([h * HEAD_DIM + within for h in range(n_heads)])


def _rope_tables(n_latent, ctx_len):
    t = jnp.arange(n_latent)
    rowp = (t // GRID_W).astype(F32)
    colp = (t % GRID_W).astype(F32)
    axis_dim = HEAD_DIM // 2
    inv = ROPE_THETA ** (-jnp.arange(0, axis_dim, 2, dtype=F32) / axis_dim)
    ang = jnp.concatenate([rowp[:, None] * inv, colp[:, None] * inv], axis=-1)
    cos, sin = jnp.cos(ang), jnp.sin(ang)
    cos = jnp.concatenate([jnp.ones((ctx_len, axis_dim), F32), cos], axis=0)
    sin = jnp.concatenate([jnp.zeros((ctx_len, axis_dim), F32), sin], axis=0)
    cos_t = jnp.concatenate([cos, cos, cos, cos], axis=1)
    sin_t = jnp.concatenate([-sin, sin, -sin, sin], axis=1)
    return cos_t, sin_t


def _to_heads(a, n_heads):
    b, s, _ = a.shape
    return a.reshape(b, s, n_heads, HEAD_DIM).transpose(0, 2, 1, 3)


def _from_heads(a):
    b, h, s, dh = a.shape
    return a.transpose(0, 2, 1, 3).reshape(b, s, h * dh)


def _block_diag_lora(w2):
    r, w = w2.shape[1], w2.shape[2]
    z = jnp.zeros((r, w), w2.dtype)
    return jnp.concatenate([jnp.concatenate([w2[0], z], axis=1),
                            jnp.concatenate([z, w2[1]], axis=1)], axis=0)


def kernel(x, c, ctx, c_ctx, w_mod, b_mod, g_pre, g_post, w_in_even, w_out_even, qn_a, kn_a, mu_b, w0_b, w2_b, a0_b, a2_b, kk_b, ka_b, rk_b, gn_w_b, gn_b_b, w_in_odd, w_out_odd, qn_c, kn_c, sink_c):
    b, t, d = x.shape
    ctx_len = ctx.shape[1]
    s = ctx_len + t
    assert ctx_len % ROW_TILE == 0 and t % ROW_TILE == 0 and b + 1 <= 8
    ctx_tiles = ctx_len // ROW_TILE
    depth = w_mod.shape[0]

    xc = jnp.concatenate([ctx, x], axis=1)

    cc = jnp.concatenate([c, c_ctx[None, :], jnp.zeros((8 - b - 1, d), F32)], axis=0)
    mod = _modulation(cc, w_mod, b_mod)
    mod = mod.reshape(depth, 8, 3, d)
    modsel = jnp.stack([jnp.broadcast_to(mod[:, b][:, None], (depth, b, 3, d)), mod[:, :b]], axis=2)

    cos_t, sin_t = _rope_tables(t, ctx_len)
    seg_mean = jnp.asarray(np.kron(np.eye(2), np.full((HEAD_DIM, HEAD_DIM), 1.0 / HEAD_DIM)), BF16)
    seg_ones = jnp.asarray(np.kron(np.eye(2), np.ones((HEAD_DIM, HEAD_DIM))), BF16)

    bw = kk_b.shape[1]
    a_width = w_out_even.shape[1] - bw
    a_heads = a_width // HEAD_DIM
    n_in = w_in_even.shape[2]
    b_proj = 3 * bw + 4 * B_LORA
    kv_width = (n_in - 2 * a_width - b_proj - bw) // 2
    a_kv_heads = kv_width // HEAD_DIM
    qk_width = a_width + kv_width
    perm = np.arange(n_in)
    perm[:a_width] = _deinterleave_perm(a_heads)
    perm[a_width:qk_width] = a_width + _deinterleave_perm(a_kv_heads)
    w_in0 = w_in_even[0][:, perm].astype(BF16)
    within = _deinterleave_perm(1)
    gain0 = jnp.concatenate([jnp.tile(qn_a[0][within], a_heads), jnp.tile(kn_a[0][within], a_kv_heads)])[None, :]
    v_lo = qk_width
    bp_lo = v_lo + kv_width
    g_lo = bp_lo + b_proj
    qk0, v0, bproj, gate0 = _inproj(
        xc, modsel[0], g_pre[0][None, :], w_in0, cos_t, sin_t, gain0, seg_mean,
        q_width=a_width, qk_width=qk_width,
        splits=((v_lo, bp_lo), (bp_lo, g_lo), (g_lo, n_in)), out_dtypes=(BF16, F32, F32),
        ctx_tiles=ctx_tiles)
    qh = _to_heads(qk0[..., :a_width], a_heads)
    kh = _to_heads(qk0[..., a_width:], a_kv_heads)
    vh = _to_heads(v0, a_kv_heads)
    oa = _from_heads(_dense_attention(qh, kh, vh, ctx_len=ctx_len))

    m_c, n_c, q_c, y0_c, bonus = _rwkv_prep(
        bproj, mu_b[0][None, :], w0_b[0].reshape(1, 2 * bw), _block_diag_lora(w2_b[0]),
        a0_b[0].reshape(1, 2 * bw), _block_diag_lora(a2_b[0]), kk_b[0][None, :], ka_b[0][None, :],
        rk_b[0].reshape(1, bw), seg_ones, ctx_len=ctx_len)
    y_f, y_b = _rwkv_scan(m_c, n_c, q_c, y0_c, ctx_len=ctx_len)

    xc = _outproj([oa, y_f, y_b], gate0, w_out_even[0].astype(BF16), xc, modsel[0], g_post[0][None, :],
                  seg_mean, gn_w_b[0][None, :], gn_b_b[0][None, :], bonus,
                  gn=True, ctx_tiles=ctx_tiles, latent_only=False)

    c_heads = sink_c.shape[1]
    c_width = c_heads * HEAD_DIM
    n_in1 = w_in_odd.shape[2]
    ckv_width = (n_in1 - 2 * c_width) // 2
    c_kv_heads = ckv_width // HEAD_DIM
    qk_width1 = c_width + ckv_width
    perm1 = np.arange(n_in1)
    perm1[:c_width] = _deinterleave_perm(c_heads)
    perm1[c_width:qk_width1] = c_width + _deinterleave_perm(c_kv_heads)
    w_in1 = w_in_odd[0][:, perm1].astype(BF16)
    gain1 = jnp.concatenate([jnp.tile(qn_c[0][within], c_heads), jnp.tile(kn_c[0][within], c_kv_heads)])[None, :]
    qk1, v1, gate1 = _inproj(
        xc, modsel[1], g_pre[1][None, :], w_in1, cos_t, sin_t, gain1, seg_mean,
        q_width=c_width, qk_width=qk_width1,
        splits=((qk_width1, qk_width1 + ckv_width), (qk_width1 + ckv_width, n_in1)),
        out_dtypes=(BF16, F32), ctx_tiles=ctx_tiles)
    qh1 = _to_heads(qk1[..., :c_width], c_heads)
    kh1 = _to_heads(qk1[..., c_width:], c_kv_heads)
    vh1 = _to_heads(v1, c_kv_heads)
    ow = _from_heads(_window_attention(qh1, kh1, vh1, sink_c[0], ctx_len=ctx_len))
    dummy = jnp.zeros((b, s, LANES), F32)
    return _outproj([ow], gate1, w_out_odd[0].astype(BF16), xc, modsel[1], g_post[1][None, :],
                    seg_mean, jnp.zeros((1, LANES), F32), jnp.zeros((1, LANES), F32), dummy,
                    gn=False, ctx_tiles=ctx_tiles, latent_only=True)
```

```python
import functools

import numpy as np
import jax
import jax.numpy as jnp
from jax import lax
from jax.experimental import pallas as pl
from jax.experimental.pallas import tpu as pltpu

F32 = jnp.float32
BF16 = jnp.bfloat16
HIGHEST = lax.Precision.HIGHEST

HEAD_DIM = 64
LANES = 128
GRID_W = 64
Q_BLOCK = 128
WINDOW = 128
ROPE_THETA = 10000.0
RMS_EPS = 1e-6
GN_EPS = 64e-5
LOG2_E = float(np.log2(np.e))
Q_SCALE = HEAD_DIM ** -0.5 * LOG2_E
B_LORA = 64
CHUNK = 64
ROW_TILE = 256
DENSE_KV_TILE = 1024
VMEM_LIMIT = 56 * 1024 * 1024


def _cparams(sem):
    return pltpu.CompilerParams(dimension_semantics=sem, vmem_limit_bytes=VMEM_LIMIT)


def _split_dot(a, g_bf16):
    hi = a.astype(BF16)
    lo = (a - hi.astype(F32)).astype(BF16)
    return (jnp.dot(hi, g_bf16, preferred_element_type=F32)
            + jnp.dot(lo, g_bf16, preferred_element_type=F32))


def _sigmoid(z):
    return 1.0 / (1.0 + jnp.exp(-z))


def _silu(z):
    return z * _sigmoid(z)


def _mod_kernel(c_ref, w_ref, b_ref, o_ref):
    o_ref[...] = jnp.dot(_silu(c_ref[...]), w_ref[...], precision=HIGHEST,
                         preferred_element_type=F32) + b_ref[...]


def _modulation(cc, w_mod, b_mod):
    depth, d, d3 = w_mod.shape
    nj = d3 // d
    return pl.pallas_call(
        _mod_kernel,
        grid=(depth, nj),
        in_specs=[pl.BlockSpec((8, d), lambda l, j: (0, 0)),
                  pl.BlockSpec((None, d, d), lambda l, j: (l, 0, j)),
                  pl.BlockSpec((None, 1, d), lambda l, j: (l, 0, j))],
        out_specs=pl.BlockSpec((None, 8, d), lambda l, j: (l, 0, j)),
        out_shape=jax.ShapeDtypeStruct((depth, 8, d3), F32),
        compiler_params=_cparams(("arbitrary", "arbitrary")),
        name="modulation",
    )(cc, w_mod, b_mod.reshape(depth, 1, d3))


def _inproj_kernel(x_ref, mod_ref, gpre_ref, w_ref, cos_ref, sin_ref, gain_ref, seg_ref,
                   *out_refs, q_width, qk_width, splits):
    x = x_ref[...]
    ms = jnp.mean(x * x, axis=-1, keepdims=True)
    h = x * lax.rsqrt(ms + RMS_EPS) * gpre_ref[...]
    h = h * (1.0 + mod_ref[1:2, :]) + mod_ref[0:1, :]
    acc = jnp.dot(h.astype(BF16), w_ref[...], preferred_element_type=F32)

    cos = cos_ref[...]
    sin = sin_ref[...]
    seg = seg_ref[...]
    lane = lax.broadcasted_iota(jnp.int32, (1, LANES), 1)
    first_half = (lane % HEAD_DIM) < (HEAD_DIM // 2)
    qk_ref = out_refs[0]
    for g in range(qk_width // LANES):
        xg = acc[:, g * LANES:(g + 1) * LANES]
        msq = _split_dot(xg * xg, seg)
        xn = xg * lax.rsqrt(msq + RMS_EPS) * gain_ref[:, g * LANES:(g + 1) * LANES]
        swapped = jnp.where(first_half,
                            pltpu.roll(xn, LANES - HEAD_DIM // 2, 1),
                            pltpu.roll(xn, HEAD_DIM // 2, 1))
        y = xn * cos + swapped * sin
        if g * LANES < q_width:
            y = y * Q_SCALE
        qk_ref[:, g * LANES:(g + 1) * LANES] = y.astype(qk_ref.dtype)
    for ref, (lo, hi) in zip(out_refs[1:], splits):
        ref[...] = acc[:, lo:hi].astype(ref.dtype)


def _inproj(xc, modsel, gpre, w_bf16, cos_t, sin_t, gain, seg, *, q_width, qk_width, splits,
            out_dtypes, ctx_tiles):
    b, s, d = xc.shape
    n = w_bf16.shape[1]
    tm = ROW_TILE
    grid = (b, s // tm)
    row = lambda bi, i: (bi, i, 0)
    out_shapes = [jax.ShapeDtypeStruct((b, s, qk_width), BF16)]
    out_specs = [pl.BlockSpec((None, tm, qk_width), row)]
    for (lo, hi), dt in zip(splits, out_dtypes):
        out_shapes.append(jax.ShapeDtypeStruct((b, s, hi - lo), dt))
        out_specs.append(pl.BlockSpec((None, tm, hi - lo), row))
    kern = functools.partial(_inproj_kernel, q_width=q_width, qk_width=qk_width, splits=splits)
    return pl.pallas_call(
        kern,
        grid=grid,
        in_specs=[pl.BlockSpec((None, tm, d), row),
                  pl.BlockSpec((None, None, 3, d),
                               lambda bi, i: (bi, jnp.minimum(i // ctx_tiles, 1), 0, 0)),
                  pl.BlockSpec((1, d), lambda bi, i: (0, 0)),
                  pl.BlockSpec((d, n), lambda bi, i: (0, 0)),
                  pl.BlockSpec((tm, LANES), lambda bi, i: (i, 0)),
                  pl.BlockSpec((tm, LANES), lambda bi, i: (i, 0)),
                  pl.BlockSpec((1, qk_width), lambda bi, i: (0, 0)),
                  pl.BlockSpec((LANES, LANES), lambda bi, i: (0, 0))],
        out_specs=out_specs,
        out_shape=out_shapes,
        compiler_params=_cparams(("parallel", "parallel")),
        name="inproj",
    )(xc, modsel, gpre, w_bf16, cos_t, sin_t, gain, seg)


def _dense_attn_kernel(q_ref, k_ref, vt_ref, o_ref, *, tk, ctx_len):
    i = pl.program_id(2)
    r, tq, dh = q_ref.shape
    qs = [q_ref[hh] for hh in range(r)]
    contract_lanes = (((1,), (1,)), ((), ()))

    def step(carries, ks, vts):
        sts = [lax.dot_general(ks, q, contract_lanes, preferred_element_type=F32) for q in qs]
        out = []
        for (m, l, acc), st in zip(carries, sts):
            m_new = jnp.maximum(m, jnp.max(st, axis=0, keepdims=True))
            alpha = jnp.exp2(m - m_new)
            p = jnp.exp2(st - m_new)
            l = alpha * l + jnp.sum(p, axis=0, keepdims=True)
            acc = alpha * acc + jnp.dot(vts, p.astype(BF16), preferred_element_type=F32)
            out.append((m_new, l, acc))
        return out

    def body(j, carries):
        start = pl.multiple_of(ctx_len + j * tk, LANES)
        return step(carries, k_ref[pl.ds(start, tk), :], vt_ref[:, pl.ds(start, tk)])

    carries = [(jnp.full((1, tq), -jnp.inf, F32), jnp.zeros((1, tq), F32), jnp.zeros((dh, tq), F32))
               for _ in range(r)]
    carries = step(carries, k_ref[0:ctx_len, :], vt_ref[:, 0:ctx_len])
    n_latent_steps = jnp.where(i < ctx_len // tq, 0, (k_ref.shape[0] - ctx_len) // tk)
    carries = lax.fori_loop(0, n_latent_steps, body, carries)
    for hh, (_, l, acc) in enumerate(carries):
        o_ref[hh] = (acc / l).astype(o_ref.dtype)


def _dense_attention(qh, kh, vth, *, ctx_len):
    b, hq, s, dh = qh.shape
    g = kh.shape[1]
    r = hq // g
    tq = Q_BLOCK
    assert (s - ctx_len) % DENSE_KV_TILE == 0 and ctx_len % tq == 0
    kern = functools.partial(_dense_attn_kernel, tk=DENSE_KV_TILE, ctx_len=ctx_len)
    return pl.pallas_call(
        kern,
        grid=(b, g, s // tq),
        in_specs=[pl.BlockSpec((None, r, tq, dh), lambda bi, gi, i: (bi, gi, i, 0)),
                  pl.BlockSpec((None, None, s, dh), lambda bi, gi, i: (bi, gi, 0, 0)),
                  pl.BlockSpec((None, None, dh, s), lambda bi, gi, i: (bi, gi, 0, 0))],
        out_specs=pl.BlockSpec((None, r, dh, tq), lambda bi, gi, i: (bi, gi, 0, i)),
        out_shape=jax.ShapeDtypeStruct((b, hq, dh, s), F32),
        compiler_params=_cparams(("parallel", "parallel", "parallel")),
        name="dense_attention",
    )(qh, kh, vth)


def _window_attn_kernel(sink_ref, q_ref, kc_ref, kp_ref, kq_ref, kn_ref,
                        vc_ref, vp_ref, vq_ref, vn_ref, o_ref, *, n_blocks):
    gi = pl.program_id(1)
    i = pl.program_id(2)
    r, tq, dh = q_ref.shape
    contract_lanes = (((1,), (1,)), ((), ()))
    k_loc = jnp.concatenate([kp_ref[...], kq_ref[...], kn_ref[...]], axis=0)
    vt_loc = jnp.concatenate([vp_ref[...], vq_ref[...], vn_ref[...]], axis=1)
    k_ctx = kc_ref[...]
    vt_ctx = vc_ref[...]
    key = lax.broadcasted_iota(jnp.int32, (3 * tq, tq), 0)
    qpos = lax.broadcasted_iota(jnp.int32, (3 * tq, tq), 1)
    band = jnp.abs(key - tq - qpos) <= WINDOW
    band = band & ((key >= tq) | (i > 0)) & ((key < 2 * tq) | (i < n_blocks - 1))
    for hh in range(r):
        q = q_ref[hh]
        s_loc = lax.dot_general(k_loc, q, contract_lanes, preferred_element_type=F32)
        s_ctx = lax.dot_general(k_ctx, q, contract_lanes, preferred_element_type=F32)
        s_loc = jnp.where(band, s_loc, -jnp.inf)
        sink = sink_ref[gi * r + hh] * LOG2_E
        m = jnp.maximum(jnp.maximum(jnp.max(s_loc, axis=0, keepdims=True),
                                    jnp.max(s_ctx, axis=0, keepdims=True)), sink)
        p_loc = jnp.exp2(s_loc - m)
        p_ctx = jnp.exp2(s_ctx - m)
        l = (jnp.sum(p_loc, axis=0, keepdims=True) + jnp.sum(p_ctx, axis=0, keepdims=True)
             + jnp.exp2(sink - m))
        o = (jnp.dot(vt_loc, p_loc.astype(BF16), preferred_element_type=F32)
             + jnp.dot(vt_ctx, p_ctx.astype(BF16), preferred_element_type=F32))
        o_ref[hh] = (o / l).astype(o_ref.dtype)


def _window_attention(qh, kh, vth, sink, *, ctx_len):
    b, hq, s, dh = qh.shape
    g = kh.shape[1]
    r = hq // g
    tq = Q_BLOCK
    t = s - ctx_len
    nb = t // tq
    off = ctx_len // tq
    cur = lambda bi, gi, i: (bi, gi, i + off, 0)
    prev = lambda bi, gi, i: (bi, gi, jnp.maximum(i - 1, 0) + off, 0)
    nxt = lambda bi, gi, i: (bi, gi, jnp.minimum(i + 1, nb - 1) + off, 0)
    ctx = lambda bi, gi, i: (bi, gi, 0, 0)
    tcur = lambda bi, gi, i: (bi, gi, 0, i + off)
    tprev = lambda bi, gi, i: (bi, gi, 0, jnp.maximum(i - 1, 0) + off)
    tnxt = lambda bi, gi, i: (bi, gi, 0, jnp.minimum(i + 1, nb - 1) + off)
    blk = (None, None, tq, dh)
    tblk = (None, None, dh, tq)
    kern = functools.partial(_window_attn_kernel, n_blocks=nb)
    return pl.pallas_call(
        kern,
        grid=(b, g, nb),
        in_specs=[pl.BlockSpec(memory_space=pltpu.SMEM),
                  pl.BlockSpec((None, r, tq, dh), cur),
                  pl.BlockSpec((None, None, ctx_len, dh), ctx),
                  pl.BlockSpec(blk, prev), pl.BlockSpec(blk, cur), pl.BlockSpec(blk, nxt),
                  pl.BlockSpec((None, None, dh, ctx_len), ctx),
                  pl.BlockSpec(tblk, tprev), pl.BlockSpec(tblk, tcur), pl.BlockSpec(tblk, tnxt)],
        out_specs=pl.BlockSpec((None, r, dh, tq), lambda bi, gi, i: (bi, gi, 0, i)),
        out_shape=jax.ShapeDtypeStruct((b, hq, dh, t), F32),
        compiler_params=_cparams(("parallel", "parallel", "parallel")),
        name="window_attention",
    )(sink, qh, kh, kh, kh, kh, vth, vth, vth, vth)


def _rwkv_prep_kernel(p_ref, hp_ref, hn_ref, mu_ref, w0_ref, w2_ref, a0_ref, a2_ref,
                      kk_ref, ka_ref, rk_ref, ones_ref,
                      m_ref, n_ref, q_ref, y0_ref, bonus_ref, *, ctx_chunks, n_chunks):
    j = pl.program_id(1)
    c = CHUNK
    bw = kk_ref.shape[1]
    n_pairs = bw // LANES

    p = p_ref[...]
    has_prev = jnp.logical_and(j != 0, j != ctx_chunks)
    has_next = jnp.logical_and(j != ctx_chunks - 1, j != n_chunks - 1)
    prev_row = jnp.where(has_prev, hp_ref[7:8, :], 0.0)
    next_row = jnp.where(has_next, hn_ref[0:1, :], 0.0)
    rowi = lax.broadcasted_iota(jnp.int32, (c, 1), 0)
    p_prev = jnp.where(rowi == 0, prev_row, pltpu.roll(p, 1, 0))
    p_next = jnp.where(rowi == c - 1, next_row, pltpu.roll(p, c - 1, 0))
    ps = p + mu_ref[...] * (0.5 * (p_prev + p_next) - p)

    r = ps[:, 0:bw]
    k = ps[:, bw:2 * bw]
    v = ps[:, 2 * bw:3 * bw]
    zw = ps[:, 3 * bw:3 * bw + 2 * B_LORA]
    za = ps[:, 3 * bw + 2 * B_LORA:3 * bw + 4 * B_LORA]

    wl = w0_ref[...] + jnp.dot(jnp.tanh(zw), w2_ref[...], precision=HIGHEST,
                               preferred_element_type=F32)
    z = -wl
    softplus = jnp.maximum(z, 0.0) + jnp.log(1.0 + jnp.exp(-jnp.abs(z)))
    logw = -jnp.exp(-softplus - 0.5)
    a = _sigmoid(a0_ref[...] + jnp.dot(za, a2_ref[...], precision=HIGHEST,
                                       preferred_element_type=F32))

    ones_seg = ones_ref[...]
    kf = k * kk_ref[...]
    kf2 = kf * kf
    ss = jnp.concatenate([_split_dot(kf2[:, g * LANES:(g + 1) * LANES], ones_seg)
                          for g in range(n_pairs)], axis=1)
    kk = kf / jnp.maximum(jnp.sqrt(ss), 1e-12)

    ti = lax.broadcasted_iota(jnp.int32, (c, c), 0)
    tj = lax.broadcasted_iota(jnp.int32, (c, c), 1)
    eye = (ti == tj).astype(F32)
    lane = lax.broadcasted_iota(jnp.int32, (1, LANES), 1)
    head_mask = [lane < HEAD_DIM, lane >= HEAD_DIM]
    lane2 = lax.broadcasted_iota(jnp.int32, (1, 2 * LANES), 1)
    head0_wide = (lane2 % LANES) < HEAD_DIM
    bi = lax.broadcasted_iota(jnp.int32, (LANES, LANES), 0)
    bj = lax.broadcasted_iota(jnp.int32, (LANES, LANES), 1)
    same_head = (bi // HEAD_DIM) == (bj // HEAD_DIM)
    diag = bi == bj

    bonus = jnp.zeros((c, bw), F32)
    pairs = []
    for d in range(2):
        lw = logw[:, d * bw:(d + 1) * bw]
        a_d = a[:, d * bw:(d + 1) * bw]
        b_d = kk * a_d
        kmod = k * (1.0 + (a_d - 1.0) * ka_ref[...])
        rkk = r * kmod * rk_ref[...]
        bsum = jnp.concatenate([_split_dot(rkk[:, g * LANES:(g + 1) * LANES], ones_seg)
                                for g in range(n_pairs)], axis=1)
        bonus = bonus + bsum * v

        if d == 0:
            tri = (tj <= ti).astype(F32)
            strict = tj < ti
            incl = tj <= ti
            end_row = c - 1
        else:
            tri = (tj >= ti).astype(F32)
            strict = tj > ti
            incl = tj >= ti
            end_row = 0
        cum = jnp.dot(tri, lw, precision=HIGHEST, preferred_element_type=F32)
        cum_end = cum[end_row:end_row + 1, :]
        e_in = jnp.exp(cum)
        e_out = jnp.exp(-cum)
        r_t = r * e_in
        kk_t = kk * jnp.exp(cum - lw)
        k_h = kmod * e_out
        b_h = b_d * e_out
        tail = jnp.exp(cum_end - cum)
        k_bar = kmod * tail
        b_bar = b_d * tail
        w_end = jnp.exp(cum_end)

        for pr in range(n_pairs):
            sl = slice(pr * LANES, (pr + 1) * LANES)
            x_full = jnp.concatenate([kk_t[:, sl], r_t[:, sl]], axis=0)
            pairs.append(dict(d=d, sl=sl, strict=strict, incl=incl, x_full=x_full,
                              v=v[:, sl].astype(BF16), kkt=kk_t[:, sl], rt=r_t[:, sl],
                              kh=k_h[:, sl].astype(BF16), bh=b_h[:, sl].astype(BF16),
                              kbar=k_bar[:, sl].astype(BF16), bbar=b_bar[:, sl].astype(BF16),
                              w_end=w_end[:, sl]))
    bonus_ref[...] = bonus

    contract_lanes = (((1,), (1,)), ((), ()))
    contract_rows = (((0,), (0,)), ((), ()))
    heads = []
    for pi, pp in enumerate(pairs):
        for hh in range(2):
            xm = jnp.where(head_mask[hh], pp["x_full"], 0.0).astype(BF16)
            ak = lax.dot_general(xm, pp["kh"], contract_lanes, preferred_element_type=F32)
            ab = lax.dot_general(xm, pp["bh"], contract_lanes, preferred_element_type=F32)
            heads.append(dict(
                pi=pi, hh=hh,
                a_k=jnp.concatenate([jnp.where(pp["strict"], ak[0:c], 0.0),
                                     jnp.where(pp["incl"], ak[c:2 * c], 0.0)], axis=0).astype(BF16),
                a_kb=jnp.where(pp["strict"], ab[0:c], 0.0),
                a_qb=jnp.where(pp["incl"], ab[c:2 * c], 0.0).astype(BF16)))

    pws = [-hd["a_kb"] for hd in heads]
    tinvs = [eye + pw for pw in pws]
    for _ in range(5):
        pws = [jnp.dot(pw.astype(BF16), pw.astype(BF16), preferred_element_type=F32) for pw in pws]
        tinvs = [t + jnp.dot(t.astype(BF16), pw.astype(BF16), preferred_element_type=F32)
                 for t, pw in zip(tinvs, pws)]

    avs = [jnp.dot(hd["a_k"], pairs[hd["pi"]]["v"], preferred_element_type=F32) for hd in heads]
    tps = [jnp.dot(t.astype(BF16),
                   jnp.concatenate([pairs[hd["pi"]]["kkt"], av[0:c]], axis=1).astype(BF16),
                   preferred_element_type=F32)
           for hd, t, av in zip(heads, tinvs, avs)]
    qps = [jnp.dot(hd["a_qb"], tp.astype(BF16), preferred_element_type=F32) for hd, tp in zip(heads, tps)]

    for pi, pp in enumerate(pairs):
        h0, h1 = 2 * pi, 2 * pi + 1
        p12_pair = jnp.where(head0_wide, tps[h0], tps[h1]).astype(BF16)
        qp_pair = jnp.where(head0_wide, qps[h0], qps[h1])
        yloc_pair = jnp.where(head_mask[0], avs[h0][c:2 * c], avs[h1][c:2 * c])
        bp = lax.dot_general(pp["bbar"], p12_pair, contract_rows, preferred_element_type=F32)
        kv = lax.dot_general(pp["kbar"], pp["v"], contract_rows, preferred_element_type=F32)
        d, sl = pp["d"], pp["sl"]
        pr = pi % n_pairs
        w_diag = jnp.where(diag, pp["w_end"], 0.0)
        m_ref[d, pr] = w_diag - jnp.where(same_head, bp[:, 0:LANES], 0.0)
        n_ref[d, pr] = jnp.where(same_head, kv - bp[:, LANES:2 * LANES], 0.0)
        q_ref[d, :, sl] = pp["rt"] - qp_pair[:, 0:LANES]
        y0_ref[d, :, sl] = yloc_pair - qp_pair[:, LANES:2 * LANES]


def _rwkv_prep(bp, mu, w0, w2cat, a0, a2cat, kkw, kaw, rkw, ones_seg, *, ctx_len):
    b, s, pw = bp.shape
    bw = kkw.shape[1]
    n_pairs = bw // LANES
    nc = s // CHUNK
    ctx_chunks = ctx_len // CHUNK
    rb = CHUNK // 8
    kern = functools.partial(_rwkv_prep_kernel, ctx_chunks=ctx_chunks, n_chunks=nc)
    const = lambda bi, j: (0, 0)
    return pl.pallas_call(
        kern,
        grid=(b, nc),
        in_specs=[pl.BlockSpec((None, CHUNK, pw), lambda bi, j: (bi, j, 0)),
                  pl.BlockSpec((None, 8, pw), lambda bi, j: (bi, jnp.maximum(j * rb - 1, 0), 0)),
                  pl.BlockSpec((None, 8, pw), lambda bi, j: (bi, jnp.minimum((j + 1) * rb, s // 8 - 1), 0)),
                  pl.BlockSpec((1, pw), const),
                  pl.BlockSpec((1, 2 * bw), const),
                  pl.BlockSpec((2 * B_LORA, 2 * bw), const),
                  pl.BlockSpec((1, 2 * bw), const),
                  pl.BlockSpec((2 * B_LORA, 2 * bw), const),
                  pl.BlockSpec((1, bw), const),
                  pl.BlockSpec((1, bw), const),
                  pl.BlockSpec((1, bw), const),
                  pl.BlockSpec((LANES, LANES), const)],
        out_specs=[pl.BlockSpec((None, None, 2, n_pairs, LANES, LANES), lambda bi, j: (bi, j, 0, 0, 0, 0)),
                   pl.BlockSpec((None, None, 2, n_pairs, LANES, LANES), lambda bi, j: (bi, j, 0, 0, 0, 0)),
                   pl.BlockSpec((None, 2, CHUNK, bw), lambda bi, j: (bi, 0, j, 0)),
                   pl.BlockSpec((None, 2, CHUNK, bw), lambda bi, j: (bi, 0, j, 0)),
                   pl.BlockSpec((None, CHUNK, bw), lambda bi, j: (bi, j, 0))],
        out_shape=[jax.ShapeDtypeStruct((b, nc, 2, n_pairs, LANES, LANES), F32),
                   jax.ShapeDtypeStruct((b, nc, 2, n_pairs, LANES, LANES), F32),
                   jax.ShapeDtypeStruct((b, 2, s, bw), F32),
                   jax.ShapeDtypeStruct((b, 2, s, bw), F32),
                   jax.ShapeDtypeStruct((b, s, bw), F32)],
        compiler_params=_cparams(("parallel", "parallel")),
        name="rwkv_prep",
    )(bp, bp, bp, mu, w0, w2cat, a0, a2cat, kkw, kaw, rkw, ones_seg)


def _rwkv_scan_kernel(m0_ref, m1_ref, n0_ref, n1_ref, q0_ref, q1_ref, y00_ref, y01_ref,
                      o0_ref, o1_ref, h_ref):
    j = pl.program_id(0)

    @pl.when(j == 0)
    def _():
        h_ref[...] = jnp.zeros_like(h_ref)

    nb, n_pairs = m0_ref.shape[0], m0_ref.shape[1]
    dirs = ((m0_ref, n0_ref, q0_ref, y00_ref, o0_ref), (m1_ref, n1_ref, q1_ref, y01_ref, o1_ref))
    for d, (m_ref, n_ref, q_ref, y0_ref, o_ref) in enumerate(dirs):
        for bi in range(nb):
            for pr in range(n_pairs):
                sl = slice(pr * LANES, (pr + 1) * LANES)
                h = h_ref[d, bi, pr]
                hb = h.astype(BF16)
                o_ref[bi, :, sl] = y0_ref[bi, :, sl] + jnp.dot(
                    q_ref[bi, :, sl].astype(BF16), hb, preferred_element_type=F32)
                h_ref[d, bi, pr] = n_ref[bi, pr] + jnp.dot(
                    m_ref[bi, pr].astype(BF16), hb, preferred_element_type=F32)


def _rwkv_scan(m, n, q, y0, *, ctx_len):
    b, nc, _, n_pairs, _, _ = m.shape
    s, bw = q.shape[2], q.shape[3]
    cc = ctx_len // CHUNK

    def mem_chunk(d, j):
        if d == 0:
            return j
        return jnp.where(j < cc, cc - 1 - j, nc - 1 + cc - j)

    def mn_spec(d):
        return pl.BlockSpec((b, None, None, n_pairs, LANES, LANES),
                            lambda j: (0, mem_chunk(d, j), d, 0, 0, 0))

    def row_spec(d):
        return pl.BlockSpec((b, None, CHUNK, bw), lambda j: (0, d, mem_chunk(d, j), 0))

    return pl.pallas_call(
        _rwkv_scan_kernel,
        grid=(nc,),
        in_specs=[mn_spec(0), mn_spec(1), mn_spec(0), mn_spec(1),
                  row_spec(0), row_spec(1), row_spec(0), row_spec(1)],
        out_specs=[pl.BlockSpec((b, CHUNK, bw), lambda j: (0, mem_chunk(0, j), 0)),
                   pl.BlockSpec((b, CHUNK, bw), lambda j: (0, mem_chunk(1, j), 0))],
        out_shape=[jax.ShapeDtypeStruct((b, s, bw), F32), jax.ShapeDtypeStruct((b, s, bw), F32)],
        scratch_shapes=[pltpu.VMEM((2, b, n_pairs, LANES, LANES), F32)],
        compiler_params=_cparams(("arbitrary",)),
        name="rwkv_scan",
    )(m, m, n, n, q, q, y0, y0)


def _outproj_kernel(*refs, n_parts, gn):
    o_refs = refs[:n_parts]
    (g_ref, w_ref, x_ref, mod_ref, gpost_ref, seg_ref, gnw_ref, gnb_ref, bonus_ref,
     out_ref) = refs[n_parts:]
    parts = [r[...] for r in o_refs]
    if gn:
        seg = seg_ref[...]
        y_sum = bonus_ref[...]
        for y in parts[-2:]:
            cols = []
            for g in range(y.shape[1] // LANES):
                yg = y[:, g * LANES:(g + 1) * LANES]
                mean = _split_dot(yg, seg)
                yc = yg - mean
                var = _split_dot(yc * yc, seg)
                cols.append(yc * lax.rsqrt(var + GN_EPS))
            yn = jnp.concatenate(cols, axis=1)
            y_sum = y_sum + yn * gnw_ref[...] + gnb_ref[...]
        parts = parts[:-2] + [y_sum]
    o = jnp.concatenate(parts, axis=1) if len(parts) > 1 else parts[0]
    u = (o * _silu(g_ref[...])).astype(BF16)
    y = jnp.dot(u, w_ref[...], preferred_element_type=F32)
    ms = jnp.mean(y * y, axis=-1, keepdims=True)
    yn = y * lax.rsqrt(ms + RMS_EPS) * gpost_ref[...]
    out_ref[...] = x_ref[...] + mod_ref[2:3, :] * yn


def _outproj(parts, gate, w_bf16, xc, modsel, gpost, seg, gnw, gnb, bonus, *, gn, ctx_tiles,
             latent_only):
    b, s, d = xc.shape
    tm = ROW_TILE
    off = ctx_tiles if latent_only else 0
    n_tiles = s // tm - off
    row = lambda bi, i: (bi, i + off, 0)
    const = lambda bi, i: (0, 0)
    in_specs = [pl.BlockSpec((None, tm, p.shape[2]), lambda bi, i: (bi, i, 0)) for p in parts]
    in_specs += [pl.BlockSpec((None, tm, gate.shape[2]), row),
                 pl.BlockSpec(w_bf16.shape, const),
                 pl.BlockSpec((None, tm, d), row),
                 pl.BlockSpec((None, None, 3, d),
                              lambda bi, i: (bi, jnp.minimum((i + off) // ctx_tiles, 1), 0, 0)),
                 pl.BlockSpec((1, d), const),
                 pl.BlockSpec((LANES, LANES), const),
                 pl.BlockSpec(gnw.shape, const),
                 pl.BlockSpec(gnb.shape, const),
                 pl.BlockSpec((None, tm, bonus.shape[2]), row)]
    kern = functools.partial(_outproj_kernel, n_parts=len(parts), gn=gn)
    return pl.pallas_call(
        kern,
        grid=(b, n_tiles),
        in_specs=in_specs,
        out_specs=pl.BlockSpec((None, tm, d), lambda bi, i: (bi, i, 0)),
        out_shape=jax.ShapeDtypeStruct((b, n_tiles * tm, d), F32),
        compiler_params=_cparams(("parallel", "parallel")),
        name="outproj",
    )(*parts, gate, w_bf16, xc, modsel, gpost, seg, gnw, gnb, bonus)


def _deinterleave_perm(n_heads):
    within = np.concatenate([np.arange(0, HEAD_DIM, 2), np.arange(1, HEAD_DIM, 2)])
    return np.concatenate([h * HEAD_DIM + within for h in range(n_heads)])


def _rope_tables(n_latent, ctx_len):
    t = jnp.arange(n_latent)
    rowp = (t // GRID_W).astype(F32)
    colp = (t % GRID_W).astype(F32)
    axis_dim = HEAD_DIM // 2
    inv = ROPE_THETA ** (-jnp.arange(0, axis_dim, 2, dtype=F32) / axis_dim)
    ang = jnp.concatenate([rowp[:, None] * inv, colp[:, None] * inv], axis=-1)
    cos, sin = jnp.cos(ang), jnp.sin(ang)
    cos = jnp.concatenate([jnp.ones((ctx_len, axis_dim), F32), cos], axis=0)
    sin = jnp.concatenate([jnp.zeros((ctx_len, axis_dim), F32), sin], axis=0)
    cos_t = jnp.concatenate([cos, cos, cos, cos], axis=1)
    sin_t = jnp.concatenate([-sin, sin, -sin, sin], axis=1)
    return cos_t, sin_t


def _to_heads(a, n_heads):
    b, s, _ = a.shape
    return a.reshape(b, s, n_heads, HEAD_DIM).transpose(0, 2, 1, 3)


def _from_heads(a):
    b, h, s, dh = a.shape
    return a.transpose(0, 2, 1, 3).reshape(b, s, h * dh)


def _block_diag_lora(w2):
    r, w = w2.shape[1], w2.shape[2]
    z = jnp.zeros((r, w), w2.dtype)
    return jnp.concatenate([jnp.concatenate([w2[0], z], axis=1),
                            jnp.concatenate([z, w2[1]], axis=1)], axis=0)


def kernel(x, c, ctx, c_ctx, w_mod, b_mod, g_pre, g_post, w_in_even, w_out_even, qn_a, kn_a, mu_b, w0_b, w2_b, a0_b, a2_b, kk_b, ka_b, rk_b, gn_w_b, gn_b_b, w_in_odd, w_out_odd, qn_c, kn_c, sink_c):
    b, t, d = x.shape
    ctx_len = ctx.shape[1]
    s = ctx_len + t
    assert ctx_len % ROW_TILE == 0 and t % ROW_TILE == 0 and b + 1 <= 8
    ctx_tiles = ctx_len // ROW_TILE
    depth = w_mod.shape[0]

    xc = jnp.concatenate([ctx, x], axis=1)

    cc = jnp.concatenate([c, c_ctx[None, :], jnp.zeros((8 - b - 1, d), F32)], axis=0)
    mod = _modulation(cc, w_mod, b_mod)
    mod = mod.reshape(depth, 8, 3, d)
    modsel = jnp.stack([jnp.broadcast_to(mod[:, b][:, None], (depth, b, 3, d)), mod[:, :b]], axis=2)

    cos_t, sin_t = _rope_tables(t, ctx_len)
    seg_mean = jnp.asarray(np.kron(np.eye(2), np.full((HEAD_DIM, HEAD_DIM), 1.0 / HEAD_DIM)), BF16)
    seg_ones = jnp.asarray(np.kron(np.eye(2), np.ones((HEAD_DIM, HEAD_DIM))), BF16)

    bw = kk_b.shape[1]
    a_width = w_out_even.shape[1] - bw
    a_heads = a_width // HEAD_DIM
    n_in = w_in_even.shape[2]
    b_proj = 3 * bw + 4 * B_LORA
    kv_width = (n_in - 2 * a_width - b_proj - bw) // 2
    a_kv_heads = kv_width // HEAD_DIM
    qk_width = a_width + kv_width
    perm = np.arange(n_in)
    perm[:a_width] = _deinterleave_perm(a_heads)
    perm[a_width:qk_width] = a_width + _deinterleave_perm(a_kv_heads)
    w_in0 = w_in_even[0][:, perm].astype(BF16)
    within = _deinterleave_perm(1)
    gain0 = jnp.concatenate([jnp.tile(qn_a[0][within], a_heads), jnp.tile(kn_a[0][within], a_kv_heads)])[None, :]
    v_lo = qk_width
    bp_lo = v_lo + kv_width
    g_lo = bp_lo + b_proj
    qk0, v0, bproj, gate0 = _inproj(
        xc, modsel[0], g_pre[0][None, :], w_in0, cos_t, sin_t, gain0, seg_mean,
        q_width=a_width, qk_width=qk_width,
        splits=((v_lo, bp_lo), (bp_lo, g_lo), (g_lo, n_in)), out_dtypes=(BF16, F32, F32),
        ctx_tiles=ctx_tiles)
    qh = _to_heads(qk0[..., :a_width], a_heads)
    kh = _to_heads(qk0[..., a_width:], a_kv_heads)
    vth = v0.reshape(b, s, a_kv_heads, HEAD_DIM).transpose(0, 2, 3, 1)
    oa = _dense_attention(qh, kh, vth, ctx_len=ctx_len)
    oa = oa.transpose(0, 3, 1, 2).reshape(b, s, a_width)

    m_c, n_c, q_c, y0_c, bonus = _rwkv_prep(
        bproj, mu_b[0][None, :], w0_b[0].reshape(1, 2 * bw), _block_diag_lora(w2_b[0]),
        a0_b[0].reshape(1, 2 * bw), _block_diag_lora(a2_b[0]), kk_b[0][None, :], ka_b[0][None, :],
        rk_b[0].reshape(1, bw), seg_ones, ctx_len=ctx_len)
    y_f, y_b = _rwkv_scan(m_c, n_c, q_c, y0_c, ctx_len=ctx_len)

    xc = _outproj([oa, y_f, y_b], gate0, w_out_even[0].astype(BF16), xc, modsel[0], g_post[0][None, :],
                  seg_mean, gn_w_b[0][None, :], gn_b_b[0][None, :], bonus,
                  gn=True, ctx_tiles=ctx_tiles, latent_only=False)

    c_heads = sink_c.shape[1]
    c_width = c_heads * HEAD_DIM
    n_in1 = w_in_odd.shape[2]
    ckv_width = (n_in1 - 2 * c_width) // 2
    c_kv_heads = ckv_width // HEAD_DIM
    qk_width1 = c_width + ckv_width
    perm1 = np.arange(n_in1)
    perm1[:c_width] = _deinterleave_perm(c_heads)
    perm1[c_width:qk_width1] = c_width + _deinterleave_perm(c_kv_heads)
    w_in1 = w_in_odd[0][:, perm1].astype(BF16)
    gain1 = jnp.concatenate([jnp.tile(qn_c[0][within], c_heads), jnp.tile(kn_c[0][within], c_kv_heads)])[None, :]
    qk1, v1, gate1 = _inproj(
        xc, modsel[1], g_pre[1][None, :], w_in1, cos_t, sin_t, gain1, seg_mean,
        q_width=c_width, qk_width=qk_width1,
        splits=((qk_width1, qk_width1 + ckv_width), (qk_width1 + ckv_width, n_in1)),
        out_dtypes=(BF16, F32), ctx_tiles=ctx_tiles)
    qh1 = _to_heads(qk1[..., :c_width], c_heads)
    kh1 = _to_heads(qk1[..., c_width:], c_kv_heads)
    vth1 = v1.reshape(b, s, c_kv_heads, HEAD_DIM).transpose(0, 2, 3, 1)
    ow = _window_attention(qh1, kh1, vth1, sink_c[0], ctx_len=ctx_len)
    ow = ow.transpose(0, 3, 1, 2).reshape(b, t, c_width)
    dummy = jnp.zeros((b, s, LANES), F32)
    return _outproj([ow], gate1, w_out_odd[0].astype(BF16), xc, modsel[1], g_post[1][None, :],
                    seg_mean, jnp.zeros((1, LANES), F32), jnp.zeros((1, LANES), F32), dummy,
                    gn=False, ctx_tiles=ctx_tiles, latent_only=True)
```

```python
import functools

import numpy as np
import jax
import jax.numpy as jnp
from jax import lax
from jax.experimental import pallas as pl
from jax.experimental.pallas import tpu as pltpu

F32 = jnp.float32
BF16 = jnp.bfloat16
HIGHEST = lax.Precision.HIGHEST

HEAD_DIM = 64
LANES = 128
GRID_W = 64
Q_BLOCK = 128
WINDOW = 128
ROPE_THETA = 10000.0
RMS_EPS = 1e-6
GN_EPS = 64e-5
LOG2_E = float(np.log2(np.e))
Q_SCALE = HEAD_DIM ** -0.5 * LOG2_E
B_LORA = 64
CHUNK = 64
ROW_TILE = 256
DENSE_KV_TILE = 1024
VMEM_LIMIT = 56 * 1024 * 1024


def _cparams(sem):
    return pltpu.CompilerParams(dimension_semantics=sem, vmem_limit_bytes=VMEM_LIMIT)


def _split_dot(a, g_bf16):
    hi = a.astype(BF16)
    lo = (a - hi.astype(F32)).astype(BF16)
    return (jnp.dot(hi, g_bf16, preferred_element_type=F32)
            + jnp.dot(lo, g_bf16, preferred_element_type=F32))


def _sigmoid(z):
    return 1.0 / (1.0 + jnp.exp(-z))


def _silu(z):
    return z * _sigmoid(z)


def _mod_kernel(c_ref, w_ref, b_ref, o_ref):
    o_ref[...] = jnp.dot(_silu(c_ref[...]), w_ref[...], precision=HIGHEST,
                         preferred_element_type=F32) + b_ref[...]


def _modulation(cc, w_mod, b_mod):
    depth, d, d3 = w_mod.shape
    nj = d3 // d
    return pl.pallas_call(
        _mod_kernel,
        grid=(depth, nj),
        in_specs=[pl.BlockSpec((8, d), lambda l, j: (0, 0)),
                  pl.BlockSpec((None, d, d), lambda l, j: (l, 0, j)),
                  pl.BlockSpec((None, 1, d), lambda l, j: (l, 0, j))],
        out_specs=pl.BlockSpec((None, 8, d), lambda l, j: (l, 0, j)),
        out_shape=jax.ShapeDtypeStruct((depth, 8, d3), F32),
        compiler_params=_cparams(("arbitrary", "arbitrary")),
        name="modulation",
    )(cc, w_mod, b_mod.reshape(depth, 1, d3))


def _stream_rows(ctx_ref, x_ref, ctx_tiles):
    if ctx_ref is None:
        return x_ref[...]
    return jnp.where(pl.program_id(1) < ctx_tiles, ctx_ref[...], x_ref[...])


def _stream_specs(two_sources, tm, d, ctx_tiles):
    if not two_sources:
        return [pl.BlockSpec((None, tm, d), lambda bi, i: (bi, i, 0))]
    return [pl.BlockSpec((None, tm, d), lambda bi, i: (bi, jnp.minimum(i, ctx_tiles - 1), 0)),
            pl.BlockSpec((None, tm, d), lambda bi, i: (bi, jnp.maximum(i - ctx_tiles, 0), 0))]


def _inproj_kernel(*refs, two_sources, ctx_tiles, q_width, qk_width, v_width, splits):
    n_src = 2 if two_sources else 1
    src = refs[:n_src]
    mod_ref, gpre_ref, w_ref, cos_ref, sin_ref, gain_ref, seg_ref = refs[n_src:n_src + 7]
    q_ref, k_ref, vt_ref = refs[n_src + 7:n_src + 10]
    extra_refs = refs[n_src + 10:]
    x = _stream_rows(src[0] if two_sources else None, src[-1], ctx_tiles)
    ms = jnp.mean(x * x, axis=-1, keepdims=True)
    h = x * lax.rsqrt(ms + RMS_EPS) * gpre_ref[...]
    h = h * (1.0 + mod_ref[1:2, :]) + mod_ref[0:1, :]
    acc = jnp.dot(h.astype(BF16), w_ref[...], preferred_element_type=F32)

    cos = cos_ref[...]
    sin = sin_ref[...]
    seg = seg_ref[...]
    lane = lax.broadcasted_iota(jnp.int32, (1, LANES), 1)
    even_lane = (lane % 2) == 0
    for g in range(qk_width // LANES):
        xg = acc[:, g * LANES:(g + 1) * LANES]
        msq = _split_dot(xg * xg, seg)
        xn = xg * lax.rsqrt(msq + RMS_EPS) * gain_ref[:, g * LANES:(g + 1) * LANES]
        partner = jnp.where(even_lane, pltpu.roll(xn, LANES - 1, 1), pltpu.roll(xn, 1, 1))
        y = xn * cos + partner * sin
        is_q = g * LANES < q_width
        if is_q:
            y = y * Q_SCALE
        ref = q_ref if is_q else k_ref
        h0 = 2 * g if is_q else 2 * (g - q_width // LANES)
        ref[h0] = y[:, 0:HEAD_DIM].astype(ref.dtype)
        ref[h0 + 1] = y[:, HEAD_DIM:LANES].astype(ref.dtype)
    for g in range(v_width // LANES):
        vt = acc[:, qk_width + g * LANES:qk_width + (g + 1) * LANES].T
        vt_ref[2 * g] = vt[0:HEAD_DIM].astype(vt_ref.dtype)
        vt_ref[2 * g + 1] = vt[HEAD_DIM:LANES].astype(vt_ref.dtype)
    for ref, (lo, hi) in zip(extra_refs, splits):
        ref[...] = acc[:, lo:hi].astype(ref.dtype)


def _inproj(sources, modsel, gpre, w_bf16, cos_t, sin_t, gain, seg, *, q_width, qk_width, v_width,
            splits, out_dtypes, ctx_tiles):
    two_sources = len(sources) == 2
    b, d = sources[0].shape[0], sources[0].shape[2]
    s = sum(a.shape[1] for a in sources)
    n = w_bf16.shape[1]
    tm = ROW_TILE
    hq = q_width // HEAD_DIM
    hk = (qk_width - q_width) // HEAD_DIM
    hv = v_width // HEAD_DIM
    row = lambda bi, i: (bi, i, 0)
    const = lambda bi, i: (0, 0)
    out_shapes = [jax.ShapeDtypeStruct((b, hq, s, HEAD_DIM), BF16),
                  jax.ShapeDtypeStruct((b, hk, s, HEAD_DIM), BF16),
                  jax.ShapeDtypeStruct((b, hv, HEAD_DIM, s), BF16)]
    out_specs = [pl.BlockSpec((None, hq, tm, HEAD_DIM), lambda bi, i: (bi, 0, i, 0)),
                 pl.BlockSpec((None, hk, tm, HEAD_DIM), lambda bi, i: (bi, 0, i, 0)),
                 pl.BlockSpec((None, hv, HEAD_DIM, tm), lambda bi, i: (bi, 0, 0, i))]
    for (lo, hi), dt in zip(splits, out_dtypes):
        out_shapes.append(jax.ShapeDtypeStruct((b, s, hi - lo), dt))
        out_specs.append(pl.BlockSpec((None, tm, hi - lo), row))
    kern = functools.partial(_inproj_kernel, two_sources=two_sources, ctx_tiles=ctx_tiles,
                             q_width=q_width, qk_width=qk_width, v_width=v_width, splits=splits)
    return pl.pallas_call(
        kern,
        grid=(b, s // tm),
        in_specs=_stream_specs(two_sources, tm, d, ctx_tiles) + [
            pl.BlockSpec((None, None, 3, d),
                         lambda bi, i: (bi, jnp.minimum(i // ctx_tiles, 1), 0, 0)),
            pl.BlockSpec((1, d), const),
            pl.BlockSpec((d, n), const),
            pl.BlockSpec((tm, LANES), lambda bi, i: (i, 0)),
            pl.BlockSpec((tm, LANES), lambda bi, i: (i, 0)),
            pl.BlockSpec((1, qk_width), const),
            pl.BlockSpec((LANES, LANES), const)],
        out_specs=out_specs,
        out_shape=out_shapes,
        compiler_params=_cparams(("parallel", "parallel")),
        name="inproj",
    )(*sources, modsel, gpre, w_bf16, cos_t, sin_t, gain, seg)


def _dense_attn_kernel(q_ref, k_ref, vt_ref, o_ref, *, tk, ctx_len):
    i = pl.program_id(2)
    r, tq, dh = q_ref.shape
    qs = [q_ref[hh] for hh in range(r)]
    contract_lanes = (((1,), (1,)), ((), ()))

    def step(carries, ks, vts):
        sts = [lax.dot_general(ks, q, contract_lanes, preferred_element_type=F32) for q in qs]
        out = []
        for (m, l, acc), st in zip(carries, sts):
            m_new = jnp.maximum(m, jnp.max(st, axis=0, keepdims=True))
            alpha = jnp.exp2(m - m_new)
            p = jnp.exp2(st - m_new)
            l = alpha * l + jnp.sum(p, axis=0, keepdims=True)
            acc = alpha * acc + jnp.dot(vts, p.astype(BF16), preferred_element_type=F32)
            out.append((m_new, l, acc))
        return out

    def body(j, carries):
        start = pl.multiple_of(ctx_len + j * tk, LANES)
        return step(carries, k_ref[pl.ds(start, tk), :], vt_ref[:, pl.ds(start, tk)])

    carries = [(jnp.full((1, tq), -jnp.inf, F32), jnp.zeros((1, tq), F32), jnp.zeros((dh, tq), F32))
               for _ in range(r)]
    carries = step(carries, k_ref[0:ctx_len, :], vt_ref[:, 0:ctx_len])
    n_latent_steps = jnp.where(i < ctx_len // tq, 0, (k_ref.shape[0] - ctx_len) // tk)
    carries = lax.fori_loop(0, n_latent_steps, body, carries)
    for hh, (_, l, acc) in enumerate(carries):
        o_ref[hh] = (acc / l).astype(o_ref.dtype)


def _dense_attention(qh, kh, vth, *, ctx_len):
    b, hq, s, dh = qh.shape
    g = kh.shape[1]
    r = hq // g
    tq = Q_BLOCK
    assert (s - ctx_len) % DENSE_KV_TILE == 0 and ctx_len % tq == 0
    kern = functools.partial(_dense_attn_kernel, tk=DENSE_KV_TILE, ctx_len=ctx_len)
    return pl.pallas_call(
        kern,
        grid=(b, g, s // tq),
        in_specs=[pl.BlockSpec((None, r, tq, dh), lambda bi, gi, i: (bi, gi, i, 0)),
                  pl.BlockSpec((None, None, s, dh), lambda bi, gi, i: (bi, gi, 0, 0)),
                  pl.BlockSpec((None, None, dh, s), lambda bi, gi, i: (bi, gi, 0, 0))],
        out_specs=pl.BlockSpec((None, r, dh, tq), lambda bi, gi, i: (bi, gi, 0, i)),
        out_shape=jax.ShapeDtypeStruct((b, hq, dh, s), F32),
        compiler_params=_cparams(("parallel", "parallel", "parallel")),
        name="dense_attention",
    )(qh, kh, vth)


def _window_attn_kernel(sink_ref, q_ref, bias0_ref, bias1_ref, kc_ref, kp_ref, km_ref, kn_ref,
                        vc_ref, vp_ref, vm_ref, vn_ref, o_ref):
    gi = pl.program_id(1)
    r, tq2, dh = q_ref.shape
    tq = tq2 // 2
    contract_lanes = (((1,), (1,)), ((), ()))
    k_mid, vt_mid = km_ref[...], vm_ref[...]
    k_loc = [jnp.concatenate([kp_ref[...], k_mid], axis=0), jnp.concatenate([k_mid, kn_ref[...]], axis=0)]
    vt_loc = [jnp.concatenate([vp_ref[...], vt_mid], axis=1), jnp.concatenate([vt_mid, vn_ref[...]], axis=1)]
    bias = [bias0_ref[...], bias1_ref[...]]
    k_ctx, vt_ctx = kc_ref[...], vc_ref[...]
    lane = lax.broadcasted_iota(jnp.int32, (1, r * tq), 1)
    sink = jnp.zeros((1, r * tq), F32)
    for hh in range(r):
        sink = jnp.where(lane // tq == hh, sink_ref[gi * r + hh] * LOG2_E, sink)
    blocks = range(2)
    qs = [jnp.concatenate([q_ref[hh, qb * tq:(qb + 1) * tq, :] for hh in range(r)], axis=0)
          for qb in blocks]
    s_loc = [lax.dot_general(k_loc[qb], qs[qb], contract_lanes, preferred_element_type=F32) + bias[qb]
             for qb in blocks]
    s_ctx = [lax.dot_general(k_ctx, qs[qb], contract_lanes, preferred_element_type=F32)
             for qb in blocks]
    m = [jnp.maximum(jnp.maximum(jnp.max(s_loc[qb], axis=0, keepdims=True),
                                 jnp.max(s_ctx[qb], axis=0, keepdims=True)), sink) for qb in blocks]
    p_loc = [jnp.exp2(s_loc[qb] - m[qb]) for qb in blocks]
    p_ctx = [jnp.exp2(s_ctx[qb] - m[qb]) for qb in blocks]
    for qb in blocks:
        l = (jnp.sum(p_loc[qb], axis=0, keepdims=True) + jnp.sum(p_ctx[qb], axis=0, keepdims=True)
             + jnp.exp2(sink - m[qb]))
        o = (jnp.dot(vt_loc[qb], p_loc[qb].astype(BF16), preferred_element_type=F32)
             + jnp.dot(vt_ctx, p_ctx[qb].astype(BF16), preferred_element_type=F32)) / l
        for hh in range(r):
            o_ref[hh, :, qb * tq:(qb + 1) * tq] = o[:, hh * tq:(hh + 1) * tq].astype(o_ref.dtype)


def _window_bias(tq, r):
    key = np.arange(3 * tq)[:, None] - tq
    qpos = np.arange(tq)[None, :]
    band = np.abs(key - qpos) <= WINDOW
    variants = [band & (key >= 0), band, band & (key < tq)]
    table = np.stack([np.where(np.tile(v, (1, r)), 0.0, -np.inf) for v in variants])
    return jnp.asarray(table, F32)


def _window_attention(qh, kh, vth, sink, *, ctx_len):
    b, hq, s, dh = qh.shape
    g = kh.shape[1]
    r = hq // g
    tq = Q_BLOCK
    t = s - ctx_len
    assert t % (2 * tq) == 0 and ctx_len % (2 * tq) == 0
    ns = t // (2 * tq)
    off = ctx_len // tq
    last = t // tq - 1
    bias = _window_bias(tq, r)
    kern = _window_attn_kernel
    ctx = lambda bi, gi, i: (bi, gi, 0, 0)
    blk = lambda rows: (None, None, rows, dh)
    tblk = lambda cols: (None, None, dh, cols)
    prev_blk = lambda i: jnp.maximum(2 * i - 1, 0) + off
    next_blk = lambda i: jnp.minimum(2 * i + 2, last) + off
    return pl.pallas_call(
        kern,
        grid=(b, g, ns),
        in_specs=[pl.BlockSpec(memory_space=pltpu.SMEM),
                  pl.BlockSpec((None, r, 2 * tq, dh), lambda bi, gi, i: (bi, gi, i + off // 2, 0)),
                  pl.BlockSpec((None, 3 * tq, r * tq), lambda bi, gi, i: (jnp.minimum(i, 1), 0, 0)),
                  pl.BlockSpec((None, 3 * tq, r * tq),
                               lambda bi, gi, i: (jnp.where(i == ns - 1, 2, 1), 0, 0)),
                  pl.BlockSpec(blk(ctx_len), ctx),
                  pl.BlockSpec(blk(tq), lambda bi, gi, i: (bi, gi, prev_blk(i), 0)),
                  pl.BlockSpec(blk(2 * tq), lambda bi, gi, i: (bi, gi, i + off // 2, 0)),
                  pl.BlockSpec(blk(tq), lambda bi, gi, i: (bi, gi, next_blk(i), 0)),
                  pl.BlockSpec(tblk(ctx_len), ctx),
                  pl.BlockSpec(tblk(tq), lambda bi, gi, i: (bi, gi, 0, prev_blk(i))),
                  pl.BlockSpec(tblk(2 * tq), lambda bi, gi, i: (bi, gi, 0, i + off // 2)),
                  pl.BlockSpec(tblk(tq), lambda bi, gi, i: (bi, gi, 0, next_blk(i)))],
        out_specs=pl.BlockSpec((None, r, dh, 2 * tq), lambda bi, gi, i: (bi, gi, 0, i)),
        out_shape=jax.ShapeDtypeStruct((b, hq, dh, t), F32),
        compiler_params=_cparams(("parallel", "parallel", "parallel")),
        name="window_attention",
    )(sink, qh, bias, bias, kh, kh, kh, kh, vth, vth, vth, vth)


def _rwkv_prep_kernel(p_ref, hp_ref, hn_ref, mu_ref, w0_ref, w2_ref, a0_ref, a2_ref,
                      kk_ref, ka_ref, rk_ref, ones_ref,
                      m_ref, n_ref, q_ref, y0_ref, bonus_ref, *, ctx_chunks, n_chunks):
    j = pl.program_id(1)
    c = CHUNK
    bw = kk_ref.shape[1]
    n_pairs = bw // LANES

    p = p_ref[...]
    has_prev = jnp.logical_and(j != 0, j != ctx_chunks)
    has_next = jnp.logical_and(j != ctx_chunks - 1, j != n_chunks - 1)
    prev_row = jnp.where(has_prev, hp_ref[7:8, :], 0.0)
    next_row = jnp.where(has_next, hn_ref[0:1, :], 0.0)
    rowi = lax.broadcasted_iota(jnp.int32, (c, 1), 0)
    p_prev = jnp.where(rowi == 0, prev_row, pltpu.roll(p, 1, 0))
    p_next = jnp.where(rowi == c - 1, next_row, pltpu.roll(p, c - 1, 0))
    ps = p + mu_ref[...] * (0.5 * (p_prev + p_next) - p)

    r = ps[:, 0:bw]
    k = ps[:, bw:2 * bw]
    v = ps[:, 2 * bw:3 * bw]
    zw = ps[:, 3 * bw:3 * bw + 2 * B_LORA]
    za = ps[:, 3 * bw + 2 * B_LORA:3 * bw + 4 * B_LORA]

    wl = w0_ref[...] + jnp.dot(jnp.tanh(zw), w2_ref[...], precision=HIGHEST,
                               preferred_element_type=F32)
    z = -wl
    softplus = jnp.maximum(z, 0.0) + jnp.log(1.0 + jnp.exp(-jnp.abs(z)))
    logw = -jnp.exp(-softplus - 0.5)
    a = _sigmoid(a0_ref[...] + jnp.dot(za, a2_ref[...], precision=HIGHEST,
                                       preferred_element_type=F32))

    ones_seg = ones_ref[...]
    kf = k * kk_ref[...]
    kf2 = kf * kf
    ss = jnp.concatenate([_split_dot(kf2[:, g * LANES:(g + 1) * LANES], ones_seg)
                          for g in range(n_pairs)], axis=1)
    kk = kf / jnp.maximum(jnp.sqrt(ss), 1e-12)

    ti = lax.broadcasted_iota(jnp.int32, (c, c), 0)
    tj = lax.broadcasted_iota(jnp.int32, (c, c), 1)
    eye = (ti == tj).astype(F32)
    lane = lax.broadcasted_iota(jnp.int32, (1, LANES), 1)
    head_mask = [lane < HEAD_DIM, lane >= HEAD_DIM]
    lane2 = lax.broadcasted_iota(jnp.int32, (1, 2 * LANES), 1)
    head0_wide = (lane2 % LANES) < HEAD_DIM
    bi = lax.broadcasted_iota(jnp.int32, (LANES, LANES), 0)
    bj = lax.broadcasted_iota(jnp.int32, (LANES, LANES), 1)
    same_head = (bi // HEAD_DIM) == (bj // HEAD_DIM)
    diag = bi == bj

    bonus = jnp.zeros((c, bw), F32)
    pairs = []
    for d in range(2):
        lw = logw[:, d * bw:(d + 1) * bw]
        a_d = a[:, d * bw:(d + 1) * bw]
        b_d = kk * a_d
        kmod = k * (1.0 + (a_d - 1.0) * ka_ref[...])
        rkk = r * kmod * rk_ref[...]
        bsum = jnp.concatenate([_split_dot(rkk[:, g * LANES:(g + 1) * LANES], ones_seg)
                                for g in range(n_pairs)], axis=1)
        bonus = bonus + bsum * v

        if d == 0:
            tri = (tj <= ti).astype(F32)
            strict = tj < ti
            incl = tj <= ti
            end_row = c - 1
        else:
            tri = (tj >= ti).astype(F32)
            strict = tj > ti
            incl = tj >= ti
            end_row = 0
        cum = jnp.dot(tri, lw, precision=HIGHEST, preferred_element_type=F32)
        cum_end = cum[end_row:end_row + 1, :]
        e_in = jnp.exp(cum)
        e_out = jnp.exp(-cum)
        r_t = r * e_in
        kk_t = kk * jnp.exp(cum - lw)
        k_h = kmod * e_out
        b_h = b_d * e_out
        tail = jnp.exp(cum_end - cum)
        k_bar = kmod * tail
        b_bar = b_d * tail
        w_end = jnp.exp(cum_end)

        for pr in range(n_pairs):
            sl = slice(pr * LANES, (pr + 1) * LANES)
            x_full = jnp.concatenate([kk_t[:, sl], r_t[:, sl]], axis=0)
            pairs.append(dict(d=d, sl=sl, strict=strict, incl=incl, x_full=x_full,
                              v=v[:, sl].astype(BF16), kkt=kk_t[:, sl], rt=r_t[:, sl],
                              kh=k_h[:, sl].astype(BF16), bh=b_h[:, sl].astype(BF16),
                              kbar=k_bar[:, sl].astype(BF16), bbar=b_bar[:, sl].astype(BF16),
                              w_end=w_end[:, sl]))
    bonus_ref[...] = bonus

    contract_lanes = (((1,), (1,)), ((), ()))
    contract_rows = (((0,), (0,)), ((), ()))
    heads = []
    for pi, pp in enumerate(pairs):
        for hh in range(2):
            xm = jnp.where(head_mask[hh], pp["x_full"], 0.0).astype(BF16)
            ak = lax.dot_general(xm, pp["kh"], contract_lanes, preferred_element_type=F32)
            ab = lax.dot_general(xm, pp["bh"], contract_lanes, preferred_element_type=F32)
            heads.append(dict(
                pi=pi, hh=hh,
                a_k=jnp.concatenate([jnp.where(pp["strict"], ak[0:c], 0.0),
                                     jnp.where(pp["incl"], ak[c:2 * c], 0.0)], axis=0).astype(BF16),
                a_kb=jnp.where(pp["strict"], ab[0:c], 0.0),
                a_qb=jnp.where(pp["incl"], ab[c:2 * c], 0.0).astype(BF16)))

    pws = [-hd["a_kb"] for hd in heads]
    tinvs = [eye + pw for pw in pws]
    for _ in range(5):
        pws = [jnp.dot(pw.astype(BF16), pw.astype(BF16), preferred_element_type=F32) for pw in pws]
        tinvs = [t + jnp.dot(t.astype(BF16), pw.astype(BF16), preferred_element_type=F32)
                 for t, pw in zip(tinvs, pws)]

    avs = [jnp.dot(hd["a_k"], pairs[hd["pi"]]["v"], preferred_element_type=F32) for hd in heads]
    tps = [jnp.dot(t.astype(BF16),
                   jnp.concatenate([pairs[hd["pi"]]["kkt"], av[0:c]], axis=1).astype(BF16),
                   preferred_element_type=F32)
           for hd, t, av in zip(heads, tinvs, avs)]
    qps = [jnp.dot(hd["a_qb"], tp.astype(BF16), preferred_element_type=F32) for hd, tp in zip(heads, tps)]

    for pi, pp in enumerate(pairs):
        h0, h1 = 2 * pi, 2 * pi + 1
        p12_pair = jnp.where(head0_wide, tps[h0], tps[h1]).astype(BF16)
        qp_pair = jnp.where(head0_wide, qps[h0], qps[h1])
        yloc_pair = jnp.where(head_mask[0], avs[h0][c:2 * c], avs[h1][c:2 * c])
        bp = lax.dot_general(pp["bbar"], p12_pair, contract_rows, preferred_element_type=F32)
        kv = lax.dot_general(pp["kbar"], pp["v"], contract_rows, preferred_element_type=F32)
        d, sl = pp["d"], pp["sl"]
        pr = pi % n_pairs
        w_diag = jnp.where(diag, pp["w_end"], 0.0)
        m_ref[d, pr] = w_diag - jnp.where(same_head, bp[:, 0:LANES], 0.0)
        n_ref[d, pr] = jnp.where(same_head, kv - bp[:, LANES:2 * LANES], 0.0)
        q_ref[d, :, sl] = pp["rt"] - qp_pair[:, 0:LANES]
        y0_ref[d, :, sl] = yloc_pair - qp_pair[:, LANES:2 * LANES]


def _rwkv_prep(bp, mu, w0, w2cat, a0, a2cat, kkw, kaw, rkw, ones_seg, *, ctx_len):
    b, s, pw = bp.shape
    bw = kkw.shape[1]
    n_pairs = bw // LANES
    nc = s // CHUNK
    ctx_chunks = ctx_len // CHUNK
    rb = CHUNK // 8
    kern = functools.partial(_rwkv_prep_kernel, ctx_chunks=ctx_chunks, n_chunks=nc)
    const = lambda bi, j: (0, 0)
    return pl.pallas_call(
        kern,
        grid=(b, nc),
        in_specs=[pl.BlockSpec((None, CHUNK, pw), lambda bi, j: (bi, j, 0)),
                  pl.BlockSpec((None, 8, pw), lambda bi, j: (bi, jnp.maximum(j * rb - 1, 0), 0)),
                  pl.BlockSpec((None, 8, pw), lambda bi, j: (bi, jnp.minimum((j + 1) * rb, s // 8 - 1), 0)),
                  pl.BlockSpec((1, pw), const),
                  pl.BlockSpec((1, 2 * bw), const),
                  pl.BlockSpec((2 * B_LORA, 2 * bw), const),
                  pl.BlockSpec((1, 2 * bw), const),
                  pl.BlockSpec((2 * B_LORA, 2 * bw), const),
                  pl.BlockSpec((1, bw), const),
                  pl.BlockSpec((1, bw), const),
                  pl.BlockSpec((1, bw), const),
                  pl.BlockSpec((LANES, LANES), const)],
        out_specs=[pl.BlockSpec((None, None, 2, n_pairs, LANES, LANES), lambda bi, j: (bi, j, 0, 0, 0, 0)),
                   pl.BlockSpec((None, None, 2, n_pairs, LANES, LANES), lambda bi, j: (bi, j, 0, 0, 0, 0)),
                   pl.BlockSpec((None, 2, CHUNK, bw), lambda bi, j: (bi, 0, j, 0)),
                   pl.BlockSpec((None, 2, CHUNK, bw), lambda bi, j: (bi, 0, j, 0)),
                   pl.BlockSpec((None, CHUNK, bw), lambda bi, j: (bi, j, 0))],
        out_shape=[jax.ShapeDtypeStruct((b, nc, 2, n_pairs, LANES, LANES), F32),
                   jax.ShapeDtypeStruct((b, nc, 2, n_pairs, LANES, LANES), F32),
                   jax.ShapeDtypeStruct((b, 2, s, bw), F32),
                   jax.ShapeDtypeStruct((b, 2, s, bw), F32),
                   jax.ShapeDtypeStruct((b, s, bw), F32)],
        compiler_params=_cparams(("parallel", "parallel")),
        name="rwkv_prep",
    )(bp, bp, bp, mu, w0, w2cat, a0, a2cat, kkw, kaw, rkw, ones_seg)


def _rwkv_scan_kernel(m0_ref, m1_ref, n0_ref, n1_ref, q0_ref, q1_ref, y00_ref, y01_ref,
                      o0_ref, o1_ref, h_ref):
    j = pl.program_id(0)

    @pl.when(j == 0)
    def _():
        h_ref[...] = jnp.zeros_like(h_ref)

    nb, n_pairs = m0_ref.shape[0], m0_ref.shape[1]
    dirs = ((m0_ref, n0_ref, q0_ref, y00_ref, o0_ref), (m1_ref, n1_ref, q1_ref, y01_ref, o1_ref))
    for d, (m_ref, n_ref, q_ref, y0_ref, o_ref) in enumerate(dirs):
        for bi in range(nb):
            for pr in range(n_pairs):
                sl = slice(pr * LANES, (pr + 1) * LANES)
                h = h_ref[d, bi, pr]
                hb = h.astype(BF16)
                o_ref[bi, :, sl] = y0_ref[bi, :, sl] + jnp.dot(
                    q_ref[bi, :, sl].astype(BF16), hb, preferred_element_type=F32)
                h_ref[d, bi, pr] = n_ref[bi, pr] + jnp.dot(
                    m_ref[bi, pr].astype(BF16), hb, preferred_element_type=F32)


def _rwkv_scan(m, n, q, y0, *, ctx_len):
    b, nc, _, n_pairs, _, _ = m.shape
    s, bw = q.shape[2], q.shape[3]
    cc = ctx_len // CHUNK

    def mem_chunk(d, j):
        if d == 0:
            return j
        return jnp.where(j < cc, cc - 1 - j, nc - 1 + cc - j)

    def mn_spec(d):
        return pl.BlockSpec((b, None, None, n_pairs, LANES, LANES),
                            lambda j: (0, mem_chunk(d, j), d, 0, 0, 0))

    def row_spec(d):
        return pl.BlockSpec((b, None, CHUNK, bw), lambda j: (0, d, mem_chunk(d, j), 0))

    return pl.pallas_call(
        _rwkv_scan_kernel,
        grid=(nc,),
        in_specs=[mn_spec(0), mn_spec(1), mn_spec(0), mn_spec(1),
                  row_spec(0), row_spec(1), row_spec(0), row_spec(1)],
        out_specs=[pl.BlockSpec((b, CHUNK, bw), lambda j: (0, mem_chunk(0, j), 0)),
                   pl.BlockSpec((b, CHUNK, bw), lambda j: (0, mem_chunk(1, j), 0))],
        out_shape=[jax.ShapeDtypeStruct((b, s, bw), F32), jax.ShapeDtypeStruct((b, s, bw), F32)],
        scratch_shapes=[pltpu.VMEM((2, b, n_pairs, LANES, LANES), F32)],
        compiler_params=_cparams(("arbitrary",)),
        name="rwkv_scan",
    )(m, m, n, n, q, q, y0, y0)


def _outproj_kernel(*refs, two_sources, ctx_tiles, rwkv):
    n_src = 2 if two_sources else 1
    src = refs[:n_src]
    at_ref, g_ref, w_ref, mod_ref, gpost_ref = refs[n_src:n_src + 5]
    rest = refs[n_src + 5:]
    out_ref = rest[-1]
    hq, dh, tm = at_ref.shape
    parts = [at_ref[...].reshape(hq * dh, tm).T]
    if rwkv:
        yf_ref, yb_ref, bonus_ref, seg_ref, gnw_ref, gnb_ref = rest[:6]
        seg = seg_ref[...]
        y_sum = bonus_ref[...]
        for y_ref in (yf_ref, yb_ref):
            y = y_ref[...]
            cols = []
            for g in range(y.shape[1] // LANES):
                yg = y[:, g * LANES:(g + 1) * LANES]
                mean = _split_dot(yg, seg)
                yc = yg - mean
                var = _split_dot(yc * yc, seg)
                cols.append(yc * lax.rsqrt(var + GN_EPS))
            y_sum = y_sum + jnp.concatenate(cols, axis=1) * gnw_ref[...] + gnb_ref[...]
        parts.append(y_sum)
    o = jnp.concatenate(parts, axis=1) if len(parts) > 1 else parts[0]
    u = (o * _silu(g_ref[...])).astype(BF16)
    y = jnp.dot(u, w_ref[...], preferred_element_type=F32)
    ms = jnp.mean(y * y, axis=-1, keepdims=True)
    yn = y * lax.rsqrt(ms + RMS_EPS) * gpost_ref[...]
    x = _stream_rows(src[0] if two_sources else None, src[-1], ctx_tiles)
    out_ref[...] = x + mod_ref[2:3, :] * yn


def _outproj(sources, attn_t, gate, w_bf16, modsel, gpost, rwkv_parts, *, ctx_tiles, latent_only):
    two_sources = len(sources) == 2
    b, d = sources[0].shape[0], sources[0].shape[2]
    s = sum(a.shape[1] for a in sources)
    tm = ROW_TILE
    off = ctx_tiles if latent_only else 0
    n_tiles = s // tm - off
    assert not (two_sources and latent_only)
    row = lambda bi, i: (bi, i + off, 0)
    const = lambda bi, i: (0, 0)
    hq, dh = attn_t.shape[1], attn_t.shape[2]
    if two_sources:
        src_specs = _stream_specs(True, tm, d, ctx_tiles)
    else:
        src_specs = [pl.BlockSpec((None, tm, d), row)]
    in_specs = src_specs + [
        pl.BlockSpec((None, hq, dh, tm), lambda bi, i: (bi, 0, 0, i)),
        pl.BlockSpec((None, tm, gate.shape[2]), row),
        pl.BlockSpec(w_bf16.shape, const),
        pl.BlockSpec((None, None, 3, d),
                     lambda bi, i: (bi, jnp.minimum((i + off) // ctx_tiles, 1), 0, 0)),
        pl.BlockSpec((1, d), const)]
    args = [*sources, attn_t, gate, w_bf16, modsel, gpost]
    if rwkv_parts is not None:
        y_f, y_b, bonus, seg, gnw, gnb = rwkv_parts
        in_specs += [pl.BlockSpec((None, tm, y_f.shape[2]), row),
                     pl.BlockSpec((None, tm, y_b.shape[2]), row),
                     pl.BlockSpec((None, tm, bonus.shape[2]), row),
                     pl.BlockSpec((LANES, LANES), const),
                     pl.BlockSpec(gnw.shape, const),
                     pl.BlockSpec(gnb.shape, const)]
        args += [y_f, y_b, bonus, seg, gnw, gnb]
    kern = functools.partial(_outproj_kernel, two_sources=two_sources, ctx_tiles=ctx_tiles,
                             rwkv=rwkv_parts is not None)
    return pl.pallas_call(
        kern,
        grid=(b, n_tiles),
        in_specs=in_specs,
        out_specs=pl.BlockSpec((None, tm, d), lambda bi, i: (bi, i, 0)),
        out_shape=jax.ShapeDtypeStruct((b, n_tiles * tm, d), F32),
        compiler_params=_cparams(("parallel", "parallel")),
        name="outproj",
    )(*args)


def _rope_tables(n_latent, ctx_len):
    t = jnp.arange(n_latent)
    rowp = (t // GRID_W).astype(F32)
    colp = (t % GRID_W).astype(F32)
    axis_dim = HEAD_DIM // 2
    inv = ROPE_THETA ** (-jnp.arange(0, axis_dim, 2, dtype=F32) / axis_dim)
    ang = jnp.concatenate([rowp[:, None] * inv, colp[:, None] * inv], axis=-1)
    cos, sin = jnp.cos(ang), jnp.sin(ang)
    cos = jnp.concatenate([jnp.ones((ctx_len, axis_dim), F32), cos], axis=0)
    sin = jnp.concatenate([jnp.zeros((ctx_len, axis_dim), F32), sin], axis=0)
    cos_h = jnp.repeat(cos, 2, axis=1)
    sin_h = jnp.stack([-sin, sin], axis=-1).reshape(sin.shape[0], HEAD_DIM)
    return jnp.tile(cos_h, (1, LANES // HEAD_DIM)), jnp.tile(sin_h, (1, LANES // HEAD_DIM))


def _block_diag_lora(w2):
    r, w = w2.shape[1], w2.shape[2]
    z = jnp.zeros((r, w), w2.dtype)
    return jnp.concatenate([jnp.concatenate([w2[0], z], axis=1),
                            jnp.concatenate([z, w2[1]], axis=1)], axis=0)


def kernel(x, c, ctx, c_ctx, w_mod, b_mod, g_pre, g_post, w_in_even, w_out_even, qn_a, kn_a, mu_b, w0_b, w2_b, a0_b, a2_b, kk_b, ka_b, rk_b, gn_w_b, gn_b_b, w_in_odd, w_out_odd, qn_c, kn_c, sink_c):
    b, t, d = x.shape
    ctx_len = ctx.shape[1]
    s = ctx_len + t
    assert ctx_len % ROW_TILE == 0 and t % ROW_TILE == 0 and b + 1 <= 8
    ctx_tiles = ctx_len // ROW_TILE
    depth = w_mod.shape[0]

    cc = jnp.concatenate([c, c_ctx[None, :], jnp.zeros((8 - b - 1, d), F32)], axis=0)
    mod = _modulation(cc, w_mod, b_mod)
    mod = mod.reshape(depth, 8, 3, d)
    modsel = jnp.stack([jnp.broadcast_to(mod[:, b][:, None], (depth, b, 3, d)), mod[:, :b]], axis=2)

    cos_t, sin_t = _rope_tables(t, ctx_len)
    seg_mean = jnp.asarray(np.kron(np.eye(2), np.full((HEAD_DIM, HEAD_DIM), 1.0 / HEAD_DIM)), BF16)
    seg_ones = jnp.asarray(np.kron(np.eye(2), np.ones((HEAD_DIM, HEAD_DIM))), BF16)

    bw = kk_b.shape[1]
    a_width = w_out_even.shape[1] - bw
    a_heads = a_width // HEAD_DIM
    n_in = w_in_even.shape[2]
    b_proj = 3 * bw + 4 * B_LORA
    kv_width = (n_in - 2 * a_width - b_proj - bw) // 2
    a_kv_heads = kv_width // HEAD_DIM
    qk_width = a_width + kv_width
    gain0 = jnp.concatenate([jnp.tile(qn_a[0], a_heads), jnp.tile(kn_a[0], a_kv_heads)])[None, :]
    bp_lo = qk_width + kv_width
    g_lo = bp_lo + b_proj
    qh, kh, vth, bproj, gate0 = _inproj(
        [ctx, x], modsel[0], g_pre[0][None, :], w_in_even[0].astype(BF16), cos_t, sin_t, gain0, seg_mean,
        q_width=a_width, qk_width=qk_width, v_width=kv_width,
        splits=((bp_lo, g_lo), (g_lo, n_in)), out_dtypes=(F32, F32), ctx_tiles=ctx_tiles)
    oa_t = _dense_attention(qh, kh, vth, ctx_len=ctx_len)

    m_c, n_c, q_c, y0_c, bonus = _rwkv_prep(
        bproj, mu_b[0][None, :], w0_b[0].reshape(1, 2 * bw), _block_diag_lora(w2_b[0]),
        a0_b[0].reshape(1, 2 * bw), _block_diag_lora(a2_b[0]), kk_b[0][None, :], ka_b[0][None, :],
        rk_b[0].reshape(1, bw), seg_ones, ctx_len=ctx_len)
    y_f, y_b = _rwkv_scan(m_c, n_c, q_c, y0_c, ctx_len=ctx_len)

    xc = _outproj([ctx, x], oa_t, gate0, w_out_even[0].astype(BF16), modsel[0], g_post[0][None, :],
                  (y_f, y_b, bonus, seg_mean, gn_w_b[0][None, :], gn_b_b[0][None, :]),
                  ctx_tiles=ctx_tiles, latent_only=False)

    c_heads = sink_c.shape[1]
    c_width = c_heads * HEAD_DIM
    n_in1 = w_in_odd.shape[2]
    ckv_width = (n_in1 - 2 * c_width) // 2
    c_kv_heads = ckv_width // HEAD_DIM
    qk_width1 = c_width + ckv_width
    gain1 = jnp.concatenate([jnp.tile(qn_c[0], c_heads), jnp.tile(kn_c[0], c_kv_heads)])[None, :]
    qh1, kh1, vth1, gate1 = _inproj(
        [xc], modsel[1], g_pre[1][None, :], w_in_odd[0].astype(BF16), cos_t, sin_t, gain1, seg_mean,
        q_width=c_width, qk_width=qk_width1, v_width=ckv_width,
        splits=((qk_width1 + ckv_width, n_in1),), out_dtypes=(F32,), ctx_tiles=ctx_tiles)
    ow_t = _window_attention(qh1, kh1, vth1, sink_c[0], ctx_len=ctx_len)
    return _outproj([xc], ow_t, gate1, w_out_odd[0].astype(BF16), modsel[1], g_post[1][None, :], None,
                    ctx_tiles=ctx_tiles, latent_only=True)
```

```python
import functools

import numpy as np
import jax
import jax.numpy as jnp
from jax import lax
from jax.experimental import pallas as pl
from jax.experimental.pallas import tpu as pltpu

F32 = jnp.float32
BF16 = jnp.bfloat16
HIGHEST = lax.Precision.HIGHEST

HEAD_DIM = 64
LANES = 128
RWKV_GROUP_LANES = 128
GRID_W = 64
Q_BLOCK = 128
WINDOW = 128
ROPE_THETA = 10000.0
RMS_EPS = 1e-6
GN_EPS = 64e-5
LOG2_E = float(np.log2(np.e))
Q_SCALE = HEAD_DIM ** -0.5 * LOG2_E
B_LORA = 64
CHUNK = 64
ROW_TILE = 256
DENSE_KV_TILE = 1024
VMEM_LIMIT = 56 * 1024 * 1024


def _cparams(sem):
    return pltpu.CompilerParams(dimension_semantics=sem, vmem_limit_bytes=VMEM_LIMIT)


def _split_dot(a, g_bf16):
    hi = a.astype(BF16)
    lo = (a - hi.astype(F32)).astype(BF16)
    return (jnp.dot(hi, g_bf16, preferred_element_type=F32)
            + jnp.dot(lo, g_bf16, preferred_element_type=F32))


def _sigmoid(z):
    return 1.0 / (1.0 + jnp.exp(-z))


def _silu(z):
    return z * _sigmoid(z)


def _mod_kernel(c_ref, w_ref, b_ref, o_ref):
    o_ref[...] = jnp.dot(_silu(c_ref[...]), w_ref[...], precision=HIGHEST,
                         preferred_element_type=F32) + b_ref[...]


def _modulation(cc, w_mod, b_mod):
    depth, d, d3 = w_mod.shape
    nj = d3 // d
    return pl.pallas_call(
        _mod_kernel,
        grid=(depth, nj),
        in_specs=[pl.BlockSpec((8, d), lambda l, j: (0, 0)),
                  pl.BlockSpec((None, d, d), lambda l, j: (l, 0, j)),
                  pl.BlockSpec((None, 1, d), lambda l, j: (l, 0, j))],
        out_specs=pl.BlockSpec((None, 8, d), lambda l, j: (l, 0, j)),
        out_shape=jax.ShapeDtypeStruct((depth, 8, d3), F32),
        compiler_params=_cparams(("arbitrary", "arbitrary")),
        name="modulation",
    )(cc, w_mod, b_mod.reshape(depth, 1, d3))


def _stream_rows(ctx_ref, x_ref, ctx_tiles):
    if ctx_ref is None:
        return x_ref[...]
    return jnp.where(pl.program_id(1) < ctx_tiles, ctx_ref[...], x_ref[...])


def _stream_specs(two_sources, tm, d, ctx_tiles):
    if not two_sources:
        return [pl.BlockSpec((None, tm, d), lambda bi, i: (bi, i, 0))]
    return [pl.BlockSpec((None, tm, d), lambda bi, i: (bi, jnp.minimum(i, ctx_tiles - 1), 0)),
            pl.BlockSpec((None, tm, d), lambda bi, i: (bi, jnp.maximum(i - ctx_tiles, 0), 0))]


def _inproj_kernel(*refs, two_sources, ctx_tiles, q_width, qk_width, v_width, splits):
    n_src = 2 if two_sources else 1
    src = refs[:n_src]
    mod_ref, gpre_ref, w_ref, cos_ref, sin_ref, gain_ref, seg_ref = refs[n_src:n_src + 7]
    q_ref, k_ref, vt_ref = refs[n_src + 7:n_src + 10]
    extra_refs = refs[n_src + 10:]
    x = _stream_rows(src[0] if two_sources else None, src[-1], ctx_tiles)
    ms = jnp.mean(x * x, axis=-1, keepdims=True)
    h = x * lax.rsqrt(ms + RMS_EPS) * gpre_ref[...]
    h = h * (1.0 + mod_ref[1:2, :]) + mod_ref[0:1, :]
    acc = jnp.dot(h.astype(BF16), w_ref[...], preferred_element_type=F32)

    cos = cos_ref[...]
    sin = sin_ref[...]
    seg = seg_ref[...]
    lane = lax.broadcasted_iota(jnp.int32, (1, LANES), 1)
    even_lane = (lane % 2) == 0
    for g in range(qk_width // LANES):
        xg = acc[:, g * LANES:(g + 1) * LANES]
        msq = _split_dot(xg * xg, seg)
        xn = xg * lax.rsqrt(msq + RMS_EPS) * gain_ref[:, g * LANES:(g + 1) * LANES]
        partner = jnp.where(even_lane, pltpu.roll(xn, LANES - 1, 1), pltpu.roll(xn, 1, 1))
        y = xn * cos + partner * sin
        is_q = g * LANES < q_width
        if is_q:
            y = y * Q_SCALE
        ref = q_ref if is_q else k_ref
        h0 = 2 * g if is_q else 2 * (g - q_width // LANES)
        ref[h0] = y[:, 0:HEAD_DIM].astype(ref.dtype)
        ref[h0 + 1] = y[:, HEAD_DIM:LANES].astype(ref.dtype)
    for g in range(v_width // LANES):
        vt = acc[:, qk_width + g * LANES:qk_width + (g + 1) * LANES].T
        vt_ref[2 * g] = vt[0:HEAD_DIM].astype(vt_ref.dtype)
        vt_ref[2 * g + 1] = vt[HEAD_DIM:LANES].astype(vt_ref.dtype)
    for ref, (lo, hi) in zip(extra_refs, splits):
        ref[...] = acc[:, lo:hi].astype(ref.dtype)


def _inproj(sources, modsel, gpre, w_bf16, cos_t, sin_t, gain, seg, *, q_width, qk_width, v_width,
            splits, out_dtypes, ctx_tiles):
    two_sources = len(sources) == 2
    b, d = sources[0].shape[0], sources[0].shape[2]
    s = sum(a.shape[1] for a in sources)
    n = w_bf16.shape[1]
    tm = ROW_TILE
    hq = q_width // HEAD_DIM
    hk = (qk_width - q_width) // HEAD_DIM
    hv = v_width // HEAD_DIM
    row = lambda bi, i: (bi, i, 0)
    const = lambda bi, i: (0, 0)
    out_shapes = [jax.ShapeDtypeStruct((b, hq, s, HEAD_DIM), BF16),
                  jax.ShapeDtypeStruct((b, hk, s, HEAD_DIM), BF16),
                  jax.ShapeDtypeStruct((b, hv, HEAD_DIM, s), BF16)]
    out_specs = [pl.BlockSpec((None, hq, tm, HEAD_DIM), lambda bi, i: (bi, 0, i, 0)),
                 pl.BlockSpec((None, hk, tm, HEAD_DIM), lambda bi, i: (bi, 0, i, 0)),
                 pl.BlockSpec((None, hv, HEAD_DIM, tm), lambda bi, i: (bi, 0, 0, i))]
    for (lo, hi), dt in zip(splits, out_dtypes):
        out_shapes.append(jax.ShapeDtypeStruct((b, s, hi - lo), dt))
        out_specs.append(pl.BlockSpec((None, tm, hi - lo), row))
    kern = functools.partial(_inproj_kernel, two_sources=two_sources, ctx_tiles=ctx_tiles,
                             q_width=q_width, qk_width=qk_width, v_width=v_width, splits=splits)
    return pl.pallas_call(
        kern,
        grid=(b, s // tm),
        in_specs=_stream_specs(two_sources, tm, d, ctx_tiles) + [
            pl.BlockSpec((None, None, 3, d),
                         lambda bi, i: (bi, jnp.minimum(i // ctx_tiles, 1), 0, 0)),
            pl.BlockSpec((1, d), const),
            pl.BlockSpec((d, n), const),
            pl.BlockSpec((tm, LANES), lambda bi, i: (i, 0)),
            pl.BlockSpec((tm, LANES), lambda bi, i: (i, 0)),
            pl.BlockSpec((1, qk_width), const),
            pl.BlockSpec((LANES, LANES), const)],
        out_specs=out_specs,
        out_shape=out_shapes,
        compiler_params=_cparams(("parallel", "parallel")),
        name="inproj",
    )(*sources, modsel, gpre, w_bf16, cos_t, sin_t, gain, seg)


def _dense_attn_kernel(q_ref, k_ref, vt_ref, o_ref, *, tk, ctx_len):
    i = pl.program_id(2)
    r, tq, dh = q_ref.shape
    m_rows = r * tq
    n_keys = k_ref.shape[0]
    q = q_ref[...].reshape(m_rows, dh)
    contract_lanes = (((1,), (1,)), ((), ()))

    def scores(lo, hi):
        return lax.dot_general(k_ref[lo:hi, :], q, contract_lanes, preferred_element_type=F32)

    def attend(bounds):
        m = jnp.full((1, m_rows), -jnp.inf, F32)
        l = jnp.zeros((1, m_rows), F32)
        acc = jnp.zeros((dh, m_rows), F32)
        st = scores(*bounds[0])
        pending = None
        for j in range(len(bounds)):
            if pending is not None:
                (plo, phi), p_prev = pending
                acc = acc + jnp.dot(vt_ref[:, plo:phi], p_prev, preferred_element_type=F32)
            st_next = scores(*bounds[j + 1]) if j + 1 < len(bounds) else None
            m_new = jnp.maximum(m, jnp.max(st, axis=0, keepdims=True))
            alpha = jnp.exp2(m - m_new)
            p = jnp.exp2(st - m_new)
            l = alpha * l + jnp.sum(p, axis=0, keepdims=True)
            acc = acc * alpha
            pending = (bounds[j], p.astype(BF16))
            m, st = m_new, st_next
        (plo, phi), p_prev = pending
        acc = acc + jnp.dot(vt_ref[:, plo:phi], p_prev, preferred_element_type=F32)
        o = acc / l
        for hh in range(r):
            o_ref[hh] = o[:, hh * tq:(hh + 1) * tq].astype(o_ref.dtype)

    ctx_bounds = [(0, ctx_len)]
    all_bounds = ctx_bounds + [(lo, lo + tk) for lo in range(ctx_len, n_keys, tk)]

    @pl.when(i < ctx_len // tq)
    def _():
        attend(ctx_bounds)

    @pl.when(i >= ctx_len // tq)
    def _():
        attend(all_bounds)


def _dense_attention(qh, kh, vth, *, ctx_len):
    b, hq, s, dh = qh.shape
    g = kh.shape[1]
    r = hq // g
    tq = Q_BLOCK
    assert (s - ctx_len) % DENSE_KV_TILE == 0 and ctx_len % tq == 0
    kern = functools.partial(_dense_attn_kernel, tk=DENSE_KV_TILE, ctx_len=ctx_len)
    return pl.pallas_call(
        kern,
        grid=(b, g, s // tq),
        in_specs=[pl.BlockSpec((None, r, tq, dh), lambda bi, gi, i: (bi, gi, i, 0)),
                  pl.BlockSpec((None, None, s, dh), lambda bi, gi, i: (bi, gi, 0, 0)),
                  pl.BlockSpec((None, None, dh, s), lambda bi, gi, i: (bi, gi, 0, 0))],
        out_specs=pl.BlockSpec((None, r, dh, tq), lambda bi, gi, i: (bi, gi, 0, i)),
        out_shape=jax.ShapeDtypeStruct((b, hq, dh, s), F32),
        compiler_params=_cparams(("parallel", "parallel", "parallel")),
        name="dense_attention",
    )(qh, kh, vth)


def _window_attn_kernel(sink_ref, q_ref, bias0_ref, bias1_ref, kc_ref, kp_ref, km_ref, kn_ref,
                        vc_ref, vp_ref, vm_ref, vn_ref, o_ref):
    gi = pl.program_id(1)
    r, tq2, dh = q_ref.shape
    tq = tq2 // 2
    contract_lanes = (((1,), (1,)), ((), ()))
    k_mid, vt_mid = km_ref[...], vm_ref[...]
    k_loc = [jnp.concatenate([kp_ref[...], k_mid], axis=0), jnp.concatenate([k_mid, kn_ref[...]], axis=0)]
    vt_loc = [jnp.concatenate([vp_ref[...], vt_mid], axis=1), jnp.concatenate([vt_mid, vn_ref[...]], axis=1)]
    bias = [bias0_ref[...], bias1_ref[...]]
    k_ctx, vt_ctx = kc_ref[...], vc_ref[...]
    lane = lax.broadcasted_iota(jnp.int32, (1, r * tq), 1)
    sink = jnp.zeros((1, r * tq), F32)
    for hh in range(r):
        sink = jnp.where(lane // tq == hh, sink_ref[gi * r + hh] * LOG2_E, sink)
    blocks = range(2)
    qs = [jnp.concatenate([q_ref[hh, qb * tq:(qb + 1) * tq, :] for hh in range(r)], axis=0)
          for qb in blocks]
    s_loc = [lax.dot_general(k_loc[qb], qs[qb], contract_lanes, preferred_element_type=F32) + bias[qb]
             for qb in blocks]
    s_ctx = [lax.dot_general(k_ctx, qs[qb], contract_lanes, preferred_element_type=F32)
             for qb in blocks]
    m = [jnp.maximum(jnp.maximum(jnp.max(s_loc[qb], axis=0, keepdims=True),
                                 jnp.max(s_ctx[qb], axis=0, keepdims=True)), sink) for qb in blocks]
    p_loc = [jnp.exp2(s_loc[qb] - m[qb]) for qb in blocks]
    p_ctx = [jnp.exp2(s_ctx[qb] - m[qb]) for qb in blocks]
    for qb in blocks:
        l = (jnp.sum(p_loc[qb], axis=0, keepdims=True) + jnp.sum(p_ctx[qb], axis=0, keepdims=True)
             + jnp.exp2(sink - m[qb]))
        o = (jnp.dot(vt_loc[qb], p_loc[qb].astype(BF16), preferred_element_type=F32)
             + jnp.dot(vt_ctx, p_ctx[qb].astype(BF16), preferred_element_type=F32)) / l
        for hh in range(r):
            o_ref[hh, :, qb * tq:(qb + 1) * tq] = o[:, hh * tq:(hh + 1) * tq].astype(o_ref.dtype)


def _window_bias(tq, r):
    key = np.arange(3 * tq)[:, None] - tq
    qpos = np.arange(tq)[None, :]
    band = np.abs(key - qpos) <= WINDOW
    variants = [band & (key >= 0), band, band & (key < tq)]
    table = np.stack([np.where(np.tile(v, (1, r)), 0.0, -np.inf) for v in variants])
    return jnp.asarray(table, F32)


def _window_attention(qh, kh, vth, sink, *, ctx_len):
    b, hq, s, dh = qh.shape
    g = kh.shape[1]
    r = hq // g
    tq = Q_BLOCK
    t = s - ctx_len
    assert t % (2 * tq) == 0 and ctx_len % (2 * tq) == 0
    ns = t // (2 * tq)
    off = ctx_len // tq
    last = t // tq - 1
    bias = _window_bias(tq, r)
    kern = _window_attn_kernel
    ctx = lambda bi, gi, i: (bi, gi, 0, 0)
    blk = lambda rows: (None, None, rows, dh)
    tblk = lambda cols: (None, None, dh, cols)
    prev_blk = lambda i: jnp.maximum(2 * i - 1, 0) + off
    next_blk = lambda i: jnp.minimum(2 * i + 2, last) + off
    return pl.pallas_call(
        kern,
        grid=(b, g, ns),
        in_specs=[pl.BlockSpec(memory_space=pltpu.SMEM),
                  pl.BlockSpec((None, r, 2 * tq, dh), lambda bi, gi, i: (bi, gi, i + off // 2, 0)),
                  pl.BlockSpec((None, 3 * tq, r * tq), lambda bi, gi, i: (jnp.minimum(i, 1), 0, 0)),
                  pl.BlockSpec((None, 3 * tq, r * tq),
                               lambda bi, gi, i: (jnp.where(i == ns - 1, 2, 1), 0, 0)),
                  pl.BlockSpec(blk(ctx_len), ctx),
                  pl.BlockSpec(blk(tq), lambda bi, gi, i: (bi, gi, prev_blk(i), 0)),
                  pl.BlockSpec(blk(2 * tq), lambda bi, gi, i: (bi, gi, i + off // 2, 0)),
                  pl.BlockSpec(blk(tq), lambda bi, gi, i: (bi, gi, next_blk(i), 0)),
                  pl.BlockSpec(tblk(ctx_len), ctx),
                  pl.BlockSpec(tblk(tq), lambda bi, gi, i: (bi, gi, 0, prev_blk(i))),
                  pl.BlockSpec(tblk(2 * tq), lambda bi, gi, i: (bi, gi, 0, i + off // 2)),
                  pl.BlockSpec(tblk(tq), lambda bi, gi, i: (bi, gi, 0, next_blk(i)))],
        out_specs=pl.BlockSpec((None, r, dh, 2 * tq), lambda bi, gi, i: (bi, gi, 0, i)),
        out_shape=jax.ShapeDtypeStruct((b, hq, dh, t), F32),
        compiler_params=_cparams(("parallel", "parallel", "parallel")),
        name="window_attention",
    )(sink, qh, bias, bias, kh, kh, kh, kh, vth, vth, vth, vth)


def _rwkv_prep_kernel(p_ref, hp_ref, hn_ref, mu_ref, w0_ref, w2_ref, a0_ref, a2_ref,
                      kk_ref, ka_ref, rk_ref, ones_ref,
                      m_ref, n_ref, q_ref, y0_ref, bonus_ref, *, ctx_chunks, n_chunks):
    j = pl.program_id(1)
    c = CHUNK
    bw = kk_ref.shape[1]
    n_pairs = bw // LANES

    p = p_ref[...]
    has_prev = jnp.logical_and(j != 0, j != ctx_chunks)
    has_next = jnp.logical_and(j != ctx_chunks - 1, j != n_chunks - 1)
    prev_row = jnp.where(has_prev, hp_ref[7:8, :], 0.0)
    next_row = jnp.where(has_next, hn_ref[0:1, :], 0.0)
    rowi = lax.broadcasted_iota(jnp.int32, (c, 1), 0)
    p_prev = jnp.where(rowi == 0, prev_row, pltpu.roll(p, 1, 0))
    p_next = jnp.where(rowi == c - 1, next_row, pltpu.roll(p, c - 1, 0))
    ps = p + mu_ref[...] * (0.5 * (p_prev + p_next) - p)

    r = ps[:, 0:bw]
    k = ps[:, bw:2 * bw]
    v = ps[:, 2 * bw:3 * bw]
    zw = ps[:, 3 * bw:3 * bw + 2 * B_LORA]
    za = ps[:, 3 * bw + 2 * B_LORA:3 * bw + 4 * B_LORA]

    wl = w0_ref[...] + jnp.dot(jnp.tanh(zw).astype(BF16), w2_ref[...],
                               preferred_element_type=F32)
    z = -wl
    softplus = jnp.maximum(z, 0.0) + jnp.log(1.0 + jnp.exp(-jnp.abs(z)))
    logw = -jnp.exp(-softplus - 0.5)
    a = _sigmoid(a0_ref[...] + jnp.dot(za.astype(BF16), a2_ref[...],
                                       preferred_element_type=F32))

    ones_seg = ones_ref[...]
    kf = k * kk_ref[...]
    kf2 = kf * kf
    ss = jnp.concatenate([_split_dot(kf2[:, g * LANES:(g + 1) * LANES], ones_seg)
                          for g in range(n_pairs)], axis=1)
    kk = kf / jnp.maximum(jnp.sqrt(ss), 1e-12)

    gw = RWKV_GROUP_LANES
    hpg = gw // HEAD_DIM
    n_groups = bw // gw
    ti = lax.broadcasted_iota(jnp.int32, (c, c), 0)
    tj = lax.broadcasted_iota(jnp.int32, (c, c), 1)
    ti_g = lax.broadcasted_iota(jnp.int32, (c, gw), 0)
    tj_g = lax.broadcasted_iota(jnp.int32, (c, gw), 1) % c
    eye_g = (ti_g == tj_g).astype(F32)
    gi_r = lax.broadcasted_iota(jnp.int32, (gw, gw), 0)
    gi_c = lax.broadcasted_iota(jnp.int32, (gw, gw), 1)
    group_diag = (gi_r // HEAD_DIM) == (gi_c // HEAD_DIM)
    bi = lax.broadcasted_iota(jnp.int32, (LANES, LANES), 0)
    bj = lax.broadcasted_iota(jnp.int32, (LANES, LANES), 1)
    same_head = (bi // HEAD_DIM) == (bj // HEAD_DIM)
    diag = bi == bj

    def blockdiag(xs):
        xb = xs.astype(BF16)
        return jnp.where(group_diag, jnp.concatenate([xb] * hpg, axis=0), jnp.zeros((), BF16))

    v_bd = [blockdiag(v[:, g * gw:(g + 1) * gw]) for g in range(n_groups)]

    bonus = jnp.zeros((c, bw), F32)
    pairs = []
    groups = []
    for d in range(2):
        lw = logw[:, d * bw:(d + 1) * bw]
        a_d = a[:, d * bw:(d + 1) * bw]
        b_d = kk * a_d
        kmod = k * (1.0 + (a_d - 1.0) * ka_ref[...])
        rkk = r * kmod * rk_ref[...]
        bsum = jnp.concatenate([_split_dot(rkk[:, g * LANES:(g + 1) * LANES], ones_seg)
                                for g in range(n_pairs)], axis=1)
        bonus = bonus + bsum * v

        if d == 0:
            tri = (tj <= ti).astype(BF16)
            strict = tj_g < ti_g
            incl = tj_g <= ti_g
            end_row = c - 1
        else:
            tri = (tj >= ti).astype(BF16)
            strict = tj_g > ti_g
            incl = tj_g >= ti_g
            end_row = 0
        lw_hi = lw.astype(BF16)
        lw_lo = (lw - lw_hi.astype(F32)).astype(BF16)
        cum = (jnp.dot(tri, lw_hi, preferred_element_type=F32)
               + jnp.dot(tri, lw_lo, preferred_element_type=F32))
        cum_end = cum[end_row:end_row + 1, :]
        e_in = jnp.exp(cum)
        e_out = jnp.exp(-cum)
        r_t = r * e_in
        kk_t = kk * jnp.exp(cum - lw)
        k_h = kmod * e_out
        b_h = b_d * e_out
        tail = jnp.exp(cum_end - cum)
        k_bar = kmod * tail
        b_bar = b_d * tail
        w_end = jnp.exp(cum_end)

        for g in range(n_groups):
            sl = slice(g * gw, (g + 1) * gw)
            groups.append(dict(
                d=d, g=g, sl=sl, strict=strict, incl=incl, rt=r_t[:, sl],
                x_full=jnp.concatenate([kk_t[:, sl], r_t[:, sl]], axis=0).astype(BF16),
                kh_bd=blockdiag(k_h[:, sl]), bh_bd=blockdiag(b_h[:, sl]), kkt_bd=blockdiag(kk_t[:, sl])))
        for pr in range(n_pairs):
            sl = slice(pr * LANES, (pr + 1) * LANES)
            pairs.append(dict(d=d, pr=pr, sl=sl, v=v[:, sl].astype(BF16), kbar=k_bar[:, sl].astype(BF16),
                              bbar=b_bar[:, sl].astype(BF16), w_end=w_end[:, sl]))
    bonus_ref[...] = bonus

    contract_lanes = (((1,), (1,)), ((), ()))
    contract_rows = (((0,), (0,)), ((), ()))
    aks = [lax.dot_general(gp["x_full"], gp["kh_bd"], contract_lanes, preferred_element_type=F32)
           for gp in groups]
    abs_ = [lax.dot_general(gp["x_full"], gp["bh_bd"], contract_lanes, preferred_element_type=F32)
            for gp in groups]
    a_ks = [jnp.concatenate([jnp.where(gp["strict"], ak[0:c], 0.0),
                             jnp.where(gp["incl"], ak[c:2 * c], 0.0)], axis=0).astype(BF16)
            for gp, ak in zip(groups, aks)]
    a_qbs = [jnp.where(gp["incl"], ab[c:2 * c], 0.0).astype(BF16) for gp, ab in zip(groups, abs_)]

    pws = [jnp.where(gp["strict"], -ab[0:c], 0.0) for gp, ab in zip(groups, abs_)]
    tinvs = [eye_g + pw for pw in pws]
    pws = [jnp.dot(pw.astype(BF16), blockdiag(pw), preferred_element_type=F32) for pw in pws]
    for step in range(4):
        stacked = [jnp.dot(jnp.concatenate([t, pw], axis=0).astype(BF16), blockdiag(pw),
                           preferred_element_type=F32) for t, pw in zip(tinvs, pws)]
        tinvs = [t + st[0:c] for t, st in zip(tinvs, stacked)]
        pws = [st[c:2 * c] for st in stacked]
    tinvs = [(t + jnp.dot(t.astype(BF16), blockdiag(pw), preferred_element_type=F32)).astype(BF16)
             for t, pw in zip(tinvs, pws)]

    avs = [jnp.dot(a_k, v_bd[gp["g"]], preferred_element_type=F32) for gp, a_k in zip(groups, a_ks)]
    p1s = [jnp.dot(t, gp["kkt_bd"], preferred_element_type=F32) for gp, t in zip(groups, tinvs)]
    p2s = [jnp.dot(t, blockdiag(av[0:c]), preferred_element_type=F32) for t, av in zip(tinvs, avs)]
    qp1s = [jnp.dot(a_qb, blockdiag(p1), preferred_element_type=F32) for a_qb, p1 in zip(a_qbs, p1s)]
    qp2s = [jnp.dot(a_qb, blockdiag(p2), preferred_element_type=F32) for a_qb, p2 in zip(a_qbs, p2s)]
    for gp, av, qp1, qp2 in zip(groups, avs, qp1s, qp2s):
        q_ref[gp["d"], :, gp["sl"]] = gp["rt"] - qp1
        y0_ref[gp["d"], :, gp["sl"]] = av[c:2 * c] - qp2

    for pp in pairs:
        gidx = pp["d"] * n_groups + pp["pr"] * LANES // gw
        lo = pp["pr"] * LANES % gw
        p12 = jnp.concatenate([p1s[gidx][:, lo:lo + LANES], p2s[gidx][:, lo:lo + LANES]], axis=1).astype(BF16)
        bp = lax.dot_general(pp["bbar"], p12, contract_rows, preferred_element_type=F32)
        kv = lax.dot_general(pp["kbar"], pp["v"], contract_rows, preferred_element_type=F32)
        w_diag = jnp.where(diag, pp["w_end"], 0.0)
        m_ref[pp["d"], pp["pr"]] = w_diag - jnp.where(same_head, bp[:, 0:LANES], 0.0)
        n_ref[pp["d"], pp["pr"]] = jnp.where(same_head, kv - bp[:, LANES:2 * LANES], 0.0)


def _rwkv_prep(bp, mu, w0, w2cat, a0, a2cat, kkw, kaw, rkw, ones_seg, *, ctx_len):
    b, s, pw = bp.shape
    bw = kkw.shape[1]
    n_pairs = bw // LANES
    nc = s // CHUNK
    ctx_chunks = ctx_len // CHUNK
    rb = CHUNK // 8
    kern = functools.partial(_rwkv_prep_kernel, ctx_chunks=ctx_chunks, n_chunks=nc)
    const = lambda bi, j: (0, 0)
    return pl.pallas_call(
        kern,
        grid=(b, nc),
        in_specs=[pl.BlockSpec((None, CHUNK, pw), lambda bi, j: (bi, j, 0)),
                  pl.BlockSpec((None, 8, pw), lambda bi, j: (bi, jnp.maximum(j * rb - 1, 0), 0)),
                  pl.BlockSpec((None, 8, pw), lambda bi, j: (bi, jnp.minimum((j + 1) * rb, s // 8 - 1), 0)),
                  pl.BlockSpec((1, pw), const),
                  pl.BlockSpec((1, 2 * bw), const),
                  pl.BlockSpec((2 * B_LORA, 2 * bw), const),
                  pl.BlockSpec((1, 2 * bw), const),
                  pl.BlockSpec((2 * B_LORA, 2 * bw), const),
                  pl.BlockSpec((1, bw), const),
                  pl.BlockSpec((1, bw), const),
                  pl.BlockSpec((1, bw), const),
                  pl.BlockSpec((LANES, LANES), const)],
        out_specs=[pl.BlockSpec((None, None, 2, n_pairs, LANES, LANES), lambda bi, j: (bi, j, 0, 0, 0, 0)),
                   pl.BlockSpec((None, None, 2, n_pairs, LANES, LANES), lambda bi, j: (bi, j, 0, 0, 0, 0)),
                   pl.BlockSpec((None, 2, CHUNK, bw), lambda bi, j: (bi, 0, j, 0)),
                   pl.BlockSpec((None, 2, CHUNK, bw), lambda bi, j: (bi, 0, j, 0)),
                   pl.BlockSpec((None, CHUNK, bw), lambda bi, j: (bi, j, 0))],
        out_shape=[jax.ShapeDtypeStruct((b, nc, 2, n_pairs, LANES, LANES), F32),
                   jax.ShapeDtypeStruct((b, nc, 2, n_pairs, LANES, LANES), F32),
                   jax.ShapeDtypeStruct((b, 2, s, bw), F32),
                   jax.ShapeDtypeStruct((b, 2, s, bw), F32),
                   jax.ShapeDtypeStruct((b, s, bw), F32)],
        compiler_params=_cparams(("parallel", "parallel")),
        name="rwkv_prep",
    )(bp, bp, bp, mu, w0, w2cat, a0, a2cat, kkw, kaw, rkw, ones_seg)


def _rwkv_scan_kernel(m0_ref, m1_ref, n0_ref, n1_ref, q0_ref, q1_ref, y00_ref, y01_ref,
                      o0_ref, o1_ref, h_ref):
    j = pl.program_id(0)

    @pl.when(j == 0)
    def _():
        h_ref[...] = jnp.zeros_like(h_ref)

    nb, n_pairs = m0_ref.shape[0], m0_ref.shape[1]
    dirs = ((m0_ref, n0_ref, q0_ref, y00_ref, o0_ref), (m1_ref, n1_ref, q1_ref, y01_ref, o1_ref))
    for d, (m_ref, n_ref, q_ref, y0_ref, o_ref) in enumerate(dirs):
        for bi in range(nb):
            for pr in range(n_pairs):
                sl = slice(pr * LANES, (pr + 1) * LANES)
                h = h_ref[d, bi, pr]
                hb = h.astype(BF16)
                o_ref[bi, :, sl] = y0_ref[bi, :, sl] + jnp.dot(
                    q_ref[bi, :, sl].astype(BF16), hb, preferred_element_type=F32)
                h_ref[d, bi, pr] = n_ref[bi, pr] + jnp.dot(
                    m_ref[bi, pr].astype(BF16), hb, preferred_element_type=F32)


def _rwkv_scan(m, n, q, y0, *, ctx_len):
    b, nc, _, n_pairs, _, _ = m.shape
    s, bw = q.shape[2], q.shape[3]
    cc = ctx_len // CHUNK

    def mem_chunk(d, j):
        if d == 0:
            return j
        return jnp.where(j < cc, cc - 1 - j, nc - 1 + cc - j)

    def mn_spec(d):
        return pl.BlockSpec((b, None, None, n_pairs, LANES, LANES),
                            lambda j: (0, mem_chunk(d, j), d, 0, 0, 0))

    def row_spec(d):
        return pl.BlockSpec((b, None, CHUNK, bw), lambda j: (0, d, mem_chunk(d, j), 0))

    return pl.pallas_call(
        _rwkv_scan_kernel,
        grid=(nc,),
        in_specs=[mn_spec(0), mn_spec(1), mn_spec(0), mn_spec(1),
                  row_spec(0), row_spec(1), row_spec(0), row_spec(1)],
        out_specs=[pl.BlockSpec((b, CHUNK, bw), lambda j: (0, mem_chunk(0, j), 0)),
                   pl.BlockSpec((b, CHUNK, bw), lambda j: (0, mem_chunk(1, j), 0))],
        out_shape=[jax.ShapeDtypeStruct((b, s, bw), F32), jax.ShapeDtypeStruct((b, s, bw), F32)],
        scratch_shapes=[pltpu.VMEM((2, b, n_pairs, LANES, LANES), F32)],
        compiler_params=_cparams(("arbitrary",)),
        name="rwkv_scan",
    )(m, m, n, n, q, q, y0, y0)


def _outproj_kernel(*refs, two_sources, ctx_tiles, rwkv):
    n_src = 2 if two_sources else 1
    src = refs[:n_src]
    at_ref, g_ref, w_ref, mod_ref, gpost_ref = refs[n_src:n_src + 5]
    rest = refs[n_src + 5:]
    out_ref = rest[-1]
    hq, dh, tm = at_ref.shape
    parts = [at_ref[...].reshape(hq * dh, tm).T]
    if rwkv:
        yf_ref, yb_ref, bonus_ref, seg_ref, gnw_ref, gnb_ref = rest[:6]
        seg = seg_ref[...]
        y_sum = bonus_ref[...]
        for y_ref in (yf_ref, yb_ref):
            y = y_ref[...]
            cols = []
            for g in range(y.shape[1] // LANES):
                yg = y[:, g * LANES:(g + 1) * LANES]
                mean = _split_dot(yg, seg)
                yc = yg - mean
                var = _split_dot(yc * yc, seg)
                cols.append(yc * lax.rsqrt(var + GN_EPS))
            y_sum = y_sum + jnp.concatenate(cols, axis=1) * gnw_ref[...] + gnb_ref[...]
        parts.append(y_sum)
    o = jnp.concatenate(parts, axis=1) if len(parts) > 1 else parts[0]
    u = (o * _silu(g_ref[...])).astype(BF16)
    y = jnp.dot(u, w_ref[...], preferred_element_type=F32)
    ms = jnp.mean(y * y, axis=-1, keepdims=True)
    yn = y * lax.rsqrt(ms + RMS_EPS) * gpost_ref[...]
    x = _stream_rows(src[0] if two_sources else None, src[-1], ctx_tiles)
    out_ref[...] = x + mod_ref[2:3, :] * yn


def _outproj(sources, attn_t, gate, w_bf16, modsel, gpost, rwkv_parts, *, ctx_tiles, latent_only):
    two_sources = len(sources) == 2
    b, d = sources[0].shape[0], sources[0].shape[2]
    s = sum(a.shape[1] for a in sources)
    tm = ROW_TILE
    off = ctx_tiles if latent_only else 0
    n_tiles = s // tm - off
    assert not (two_sources and latent_only)
    row = lambda bi, i: (bi, i + off, 0)
    const = lambda bi, i: (0, 0)
    hq, dh = attn_t.shape[1], attn_t.shape[2]
    if two_sources:
        src_specs = _stream_specs(True, tm, d, ctx_tiles)
    else:
        src_specs = [pl.BlockSpec((None, tm, d), row)]
    in_specs = src_specs + [
        pl.BlockSpec((None, hq, dh, tm), lambda bi, i: (bi, 0, 0, i)),
        pl.BlockSpec((None, tm, gate.shape[2]), row),
        pl.BlockSpec(w_bf16.shape, const),
        pl.BlockSpec((None, None, 3, d),
                     lambda bi, i: (bi, jnp.minimum((i + off) // ctx_tiles, 1), 0, 0)),
        pl.BlockSpec((1, d), const)]
    args = [*sources, attn_t, gate, w_bf16, modsel, gpost]
    if rwkv_parts is not None:
        y_f, y_b, bonus, seg, gnw, gnb = rwkv_parts
        in_specs += [pl.BlockSpec((None, tm, y_f.shape[2]), row),
                     pl.BlockSpec((None, tm, y_b.shape[2]), row),
                     pl.BlockSpec((None, tm, bonus.shape[2]), row),
                     pl.BlockSpec((LANES, LANES), const),
                     pl.BlockSpec(gnw.shape, const),
                     pl.BlockSpec(gnb.shape, const)]
        args += [y_f, y_b, bonus, seg, gnw, gnb]
    kern = functools.partial(_outproj_kernel, two_sources=two_sources, ctx_tiles=ctx_tiles,
                             rwkv=rwkv_parts is not None)
    return pl.pallas_call(
        kern,
        grid=(b, n_tiles),
        in_specs=in_specs,
        out_specs=pl.BlockSpec((None, tm, d), lambda bi, i: (bi, i, 0)),
        out_shape=jax.ShapeDtypeStruct((b, n_tiles * tm, d), F32),
        compiler_params=_cparams(("parallel", "parallel")),
        name="outproj",
    )(*args)


def _rope_tables(n_latent, ctx_len):
    t = jnp.arange(n_latent)
    rowp = (t // GRID_W).astype(F32)
    colp = (t % GRID_W).astype(F32)
    axis_dim = HEAD_DIM // 2
    inv = ROPE_THETA ** (-jnp.arange(0, axis_dim, 2, dtype=F32) / axis_dim)
    ang = jnp.concatenate([rowp[:, None] * inv, colp[:, None] * inv], axis=-1)
    cos, sin = jnp.cos(ang), jnp.sin(ang)
    cos = jnp.concatenate([jnp.ones((ctx_len, axis_dim), F32), cos], axis=0)
    sin = jnp.concatenate([jnp.zeros((ctx_len, axis_dim), F32), sin], axis=0)
    cos_h = jnp.repeat(cos, 2, axis=1)
    sin_h = jnp.stack([-sin, sin], axis=-1).reshape(sin.shape[0], HEAD_DIM)
    return jnp.tile(cos_h, (1, LANES // HEAD_DIM)), jnp.tile(sin_h, (1, LANES // HEAD_DIM))


def _block_diag_lora(w2):
    r, w = w2.shape[1], w2.shape[2]
    z = jnp.zeros((r, w), w2.dtype)
    return jnp.concatenate([jnp.concatenate([w2[0], z], axis=1),
                            jnp.concatenate([z, w2[1]], axis=1)], axis=0)


def kernel(x, c, ctx, c_ctx, w_mod, b_mod, g_pre, g_post, w_in_even, w_out_even, qn_a, kn_a, mu_b, w0_b, w2_b, a0_b, a2_b, kk_b, ka_b, rk_b, gn_w_b, gn_b_b, w_in_odd, w_out_odd, qn_c, kn_c, sink_c):
    b, t, d = x.shape
    ctx_len = ctx.shape[1]
    s = ctx_len + t
    assert ctx_len % ROW_TILE == 0 and t % ROW_TILE == 0 and b + 1 <= 8
    ctx_tiles = ctx_len // ROW_TILE
    depth = w_mod.shape[0]

    cc = jnp.concatenate([c, c_ctx[None, :], jnp.zeros((8 - b - 1, d), F32)], axis=0)
    mod = _modulation(cc, w_mod, b_mod)
    mod = mod.reshape(depth, 8, 3, d)
    modsel = jnp.stack([jnp.broadcast_to(mod[:, b][:, None], (depth, b, 3, d)), mod[:, :b]], axis=2)

    cos_t, sin_t = _rope_tables(t, ctx_len)
    seg_mean = jnp.asarray(np.kron(np.eye(2), np.full((HEAD_DIM, HEAD_DIM), 1.0 / HEAD_DIM)), BF16)
    seg_ones = jnp.asarray(np.kron(np.eye(2), np.ones((HEAD_DIM, HEAD_DIM))), BF16)

    bw = kk_b.shape[1]
    a_width = w_out_even.shape[1] - bw
    a_heads = a_width // HEAD_DIM
    n_in = w_in_even.shape[2]
    b_proj = 3 * bw + 4 * B_LORA
    kv_width = (n_in - 2 * a_width - b_proj - bw) // 2
    a_kv_heads = kv_width // HEAD_DIM
    qk_width = a_width + kv_width
    gain0 = jnp.concatenate([jnp.tile(qn_a[0], a_heads), jnp.tile(kn_a[0], a_kv_heads)])[None, :]
    bp_lo = qk_width + kv_width
    g_lo = bp_lo + b_proj
    qh, kh, vth, bproj, gate0 = _inproj(
        [ctx, x], modsel[0], g_pre[0][None, :], w_in_even[0].astype(BF16), cos_t, sin_t, gain0, seg_mean,
        q_width=a_width, qk_width=qk_width, v_width=kv_width,
        splits=((bp_lo, g_lo), (g_lo, n_in)), out_dtypes=(F32, F32), ctx_tiles=ctx_tiles)
    oa_t = _dense_attention(qh, kh, vth, ctx_len=ctx_len)

    m_c, n_c, q_c, y0_c, bonus = _rwkv_prep(
        bproj, mu_b[0][None, :], w0_b[0].reshape(1, 2 * bw), _block_diag_lora(w2_b[0]).astype(BF16),
        a0_b[0].reshape(1, 2 * bw), _block_diag_lora(a2_b[0]).astype(BF16), kk_b[0][None, :], ka_b[0][None, :],
        rk_b[0].reshape(1, bw), seg_ones, ctx_len=ctx_len)
    y_f, y_b = _rwkv_scan(m_c, n_c, q_c, y0_c, ctx_len=ctx_len)

    xc = _outproj([ctx, x], oa_t, gate0, w_out_even[0].astype(BF16), modsel[0], g_post[0][None, :],
                  (y_f, y_b, bonus, seg_mean, gn_w_b[0][None, :], gn_b_b[0][None, :]),
                  ctx_tiles=ctx_tiles, latent_only=False)

    c_heads = sink_c.shape[1]
    c_width = c_heads * HEAD_DIM
    n_in1 = w_in_odd.shape[2]
    ckv_width = (n_in1 - 2 * c_width) // 2
    c_kv_heads = ckv_width // HEAD_DIM
    qk_width1 = c_width + ckv_width
    gain1 = jnp.concatenate([jnp.tile(qn_c[0], c_heads), jnp.tile(kn_c[0], c_kv_heads)])[None, :]
    qh1, kh1, vth1, gate1 = _inproj(
        [xc], modsel[1], g_pre[1][None, :], w_in_odd[0].astype(BF16), cos_t, sin_t, gain1, seg_mean,
        q_width=c_width, qk_width=qk_width1, v_width=ckv_width,
        splits=((qk_width1 + ckv_width, n_in1),), out_dtypes=(F32,), ctx_tiles=ctx_tiles)
    ow_t = _window_attention(qh1, kh1, vth1, sink_c[0], ctx_len=ctx_len)
    return _outproj([xc], ow_t, gate1, w_out_odd[0].astype(BF16), modsel[1], g_post[1][None, :], None,
                    ctx_tiles=ctx_tiles, latent_only=True)
```

```python
import functools

import numpy as np
import jax
import jax.numpy as jnp
from jax import lax
from jax.experimental import pallas as pl
from jax.experimental.pallas import tpu as pltpu

F32 = jnp.float32
BF16 = jnp.bfloat16
HIGHEST = lax.Precision.HIGHEST

HEAD_DIM = 64
LANES = 128
MXU_WIDTH = 256
RWKV_GROUP_LANES = 128
GRID_W = 64
Q_BLOCK = 128
WINDOW = 128
ROPE_THETA = 10000.0
RMS_EPS = 1e-6
GN_EPS = 64e-5
LOG2_E = float(np.log2(np.e))
Q_SCALE = HEAD_DIM ** -0.5 * LOG2_E
B_LORA = 64
CHUNK = 64
ROW_TILE = 256
DENSE_KV_TILE = 1024
WINDOW_BLOCKS = 8
VMEM_LIMIT = 56 * 1024 * 1024


def _cparams(sem):
    return pltpu.CompilerParams(dimension_semantics=sem, vmem_limit_bytes=VMEM_LIMIT)


def _split_dot(a, g_bf16):
    hi = a.astype(BF16)
    lo = (a - hi.astype(F32)).astype(BF16)
    return (jnp.dot(hi, g_bf16, preferred_element_type=F32)
            + jnp.dot(lo, g_bf16, preferred_element_type=F32))


def _head_mean(x, seg):
    width = x.shape[1]
    cols = []
    for lo in range(0, width, MXU_WIDTH):
        n = min(MXU_WIDTH, width - lo)
        cols.append(jnp.dot(x[:, lo:lo + n].astype(BF16), seg[0:n, 0:n], preferred_element_type=F32))
    return cols[0] if len(cols) == 1 else jnp.concatenate(cols, axis=1)


def _sigmoid(z):
    return 1.0 / (1.0 + jnp.exp(-z))


def _silu(z):
    return z * _sigmoid(z)


def _mod_kernel(c_ref, w_ref, b_ref, o_ref):
    o_ref[...] = jnp.dot(_silu(c_ref[...]), w_ref[...], precision=HIGHEST,
                         preferred_element_type=F32) + b_ref[...]


def _modulation(cc, w_mod, b_mod):
    depth, d, d3 = w_mod.shape
    nj = d3 // d
    return pl.pallas_call(
        _mod_kernel,
        grid=(depth, nj),
        in_specs=[pl.BlockSpec((8, d), lambda l, j: (0, 0)),
                  pl.BlockSpec((None, d, d), lambda l, j: (l, 0, j)),
                  pl.BlockSpec((None, 1, d), lambda l, j: (l, 0, j))],
        out_specs=pl.BlockSpec((None, 8, d), lambda l, j: (l, 0, j)),
        out_shape=jax.ShapeDtypeStruct((depth, 8, d3), F32),
        compiler_params=_cparams(("arbitrary", "arbitrary")),
        name="modulation",
    )(cc, w_mod, b_mod.reshape(depth, 1, d3))


def _stream_rows(ctx_ref, x_ref, ctx_tiles):
    if ctx_ref is None:
        return x_ref[...]
    return jnp.where(pl.program_id(1) < ctx_tiles, ctx_ref[...], x_ref[...])


def _stream_specs(two_sources, tm, d, ctx_tiles):
    if not two_sources:
        return [pl.BlockSpec((None, tm, d), lambda bi, i: (bi, i, 0))]
    return [pl.BlockSpec((None, tm, d), lambda bi, i: (bi, jnp.minimum(i, ctx_tiles - 1), 0)),
            pl.BlockSpec((None, tm, d), lambda bi, i: (bi, jnp.maximum(i - ctx_tiles, 0), 0))]


def _inproj_kernel(*refs, two_sources, ctx_tiles, q_width, qk_width, v_width, splits):
    n_src = 2 if two_sources else 1
    src = refs[:n_src]
    mod_ref, gpre_ref, w_ref, cos_ref, sin_ref, gain_ref, seg_ref = refs[n_src:n_src + 7]
    q_ref, k_ref, vt_ref = refs[n_src + 7:n_src + 10]
    extra_refs = refs[n_src + 10:]
    x = _stream_rows(src[0] if two_sources else None, src[-1], ctx_tiles)
    ms = jnp.mean(x * x, axis=-1, keepdims=True)
    h = x * lax.rsqrt(ms + RMS_EPS) * gpre_ref[...]
    h = h * (1.0 + mod_ref[1:2, :]) + mod_ref[0:1, :]
    acc = jnp.dot(h.astype(BF16), w_ref[...], preferred_element_type=F32)

    cos = cos_ref[...]
    sin = sin_ref[...]
    lane = lax.broadcasted_iota(jnp.int32, (1, LANES), 1)
    even_lane = (lane % 2) == 0
    qk = acc[:, 0:qk_width]
    msq_all = _head_mean(qk * qk, seg_ref[...])
    for g in range(qk_width // LANES):
        xg = acc[:, g * LANES:(g + 1) * LANES]
        msq = msq_all[:, g * LANES:(g + 1) * LANES]
        xn = xg * lax.rsqrt(msq + RMS_EPS) * gain_ref[:, g * LANES:(g + 1) * LANES]
        partner = jnp.where(even_lane, pltpu.roll(xn, LANES - 1, 1), pltpu.roll(xn, 1, 1))
        y = xn * cos + partner * sin
        is_q = g * LANES < q_width
        if is_q:
            y = y * Q_SCALE
        ref = q_ref if is_q else k_ref
        h0 = 2 * g if is_q else 2 * (g - q_width // LANES)
        ref[h0] = y[:, 0:HEAD_DIM].astype(ref.dtype)
        ref[h0 + 1] = y[:, HEAD_DIM:LANES].astype(ref.dtype)
    for g in range(v_width // LANES):
        vt = acc[:, qk_width + g * LANES:qk_width + (g + 1) * LANES].T
        vt_ref[2 * g] = vt[0:HEAD_DIM].astype(vt_ref.dtype)
        vt_ref[2 * g + 1] = vt[HEAD_DIM:LANES].astype(vt_ref.dtype)
    for ref, (lo, hi) in zip(extra_refs, splits):
        ref[...] = acc[:, lo:hi].astype(ref.dtype)


def _inproj(sources, modsel, gpre, w_bf16, cos_t, sin_t, gain, seg, *, q_width, qk_width, v_width,
            splits, out_dtypes, ctx_tiles):
    two_sources = len(sources) == 2
    b, d = sources[0].shape[0], sources[0].shape[2]
    s = sum(a.shape[1] for a in sources)
    n = w_bf16.shape[1]
    tm = ROW_TILE
    hq = q_width // HEAD_DIM
    hk = (qk_width - q_width) // HEAD_DIM
    hv = v_width // HEAD_DIM
    row = lambda bi, i: (bi, i, 0)
    const = lambda bi, i: (0, 0)
    out_shapes = [jax.ShapeDtypeStruct((b, hq, s, HEAD_DIM), BF16),
                  jax.ShapeDtypeStruct((b, hk, s, HEAD_DIM), BF16),
                  jax.ShapeDtypeStruct((b, hv, HEAD_DIM, s), BF16)]
    out_specs = [pl.BlockSpec((None, hq, tm, HEAD_DIM), lambda bi, i: (bi, 0, i, 0)),
                 pl.BlockSpec((None, hk, tm, HEAD_DIM), lambda bi, i: (bi, 0, i, 0)),
                 pl.BlockSpec((None, hv, HEAD_DIM, tm), lambda bi, i: (bi, 0, 0, i))]
    for (lo, hi), dt in zip(splits, out_dtypes):
        out_shapes.append(jax.ShapeDtypeStruct((b, s, hi - lo), dt))
        out_specs.append(pl.BlockSpec((None, tm, hi - lo), row))
    kern = functools.partial(_inproj_kernel, two_sources=two_sources, ctx_tiles=ctx_tiles,
                             q_width=q_width, qk_width=qk_width, v_width=v_width, splits=splits)
    return pl.pallas_call(
        kern,
        grid=(b, s // tm),
        in_specs=_stream_specs(two_sources, tm, d, ctx_tiles) + [
            pl.BlockSpec((None, None, 3, d),
                         lambda bi, i: (bi, jnp.minimum(i // ctx_tiles, 1), 0, 0)),
            pl.BlockSpec((1, d), const),
            pl.BlockSpec((d, n), const),
            pl.BlockSpec((tm, LANES), lambda bi, i: (i, 0)),
            pl.BlockSpec((tm, LANES), lambda bi, i: (i, 0)),
            pl.BlockSpec((1, qk_width), const),
            pl.BlockSpec((MXU_WIDTH, MXU_WIDTH), const)],
        out_specs=out_specs,
        out_shape=out_shapes,
        compiler_params=_cparams(("parallel", "parallel")),
        name="inproj",
    )(*sources, modsel, gpre, w_bf16, cos_t, sin_t, gain, seg)


def _dense_attn_kernel(q_ref, k_ref, vt_ref, o_ref, *, tk, ctx_len):
    i = pl.program_id(2)
    r, tq, dh = q_ref.shape
    m_rows = r * tq
    n_keys = k_ref.shape[0]
    q = q_ref[...].reshape(m_rows, dh)
    contract_lanes = (((1,), (1,)), ((), ()))

    def scores(lo, hi):
        return lax.dot_general(k_ref[lo:hi, :], q, contract_lanes, preferred_element_type=F32)

    def attend(bounds):
        m = jnp.full((1, m_rows), -jnp.inf, F32)
        l = jnp.zeros((1, m_rows), F32)
        acc = jnp.zeros((dh, m_rows), F32)
        st = scores(*bounds[0])
        pending = None
        for j in range(len(bounds)):
            if pending is not None:
                (plo, phi), p_prev = pending
                acc = acc + jnp.dot(vt_ref[:, plo:phi], p_prev, preferred_element_type=F32)
            st_next = scores(*bounds[j + 1]) if j + 1 < len(bounds) else None
            m_new = jnp.maximum(m, jnp.max(st, axis=0, keepdims=True))
            alpha = jnp.exp2(m - m_new)
            p = jnp.exp2(st - m_new)
            l = alpha * l + jnp.sum(p, axis=0, keepdims=True)
            acc = acc * alpha
            pending = (bounds[j], p.astype(BF16))
            m, st = m_new, st_next
        (plo, phi), p_prev = pending
        acc = acc + jnp.dot(vt_ref[:, plo:phi], p_prev, preferred_element_type=F32)
        o = acc / l
        for hh in range(r):
            o_ref[hh] = o[:, hh * tq:(hh + 1) * tq].astype(o_ref.dtype)

    ctx_bounds = [(0, ctx_len)]
    all_bounds = ctx_bounds + [(lo, lo + tk) for lo in range(ctx_len, n_keys, tk)]

    @pl.when(i < ctx_len // tq)
    def _():
        attend(ctx_bounds)

    @pl.when(i >= ctx_len // tq)
    def _():
        attend(all_bounds)


def _dense_attention(qh, kh, vth, *, ctx_len):
    b, hq, s, dh = qh.shape
    g = kh.shape[1]
    r = hq // g
    tq = Q_BLOCK
    assert (s - ctx_len) % DENSE_KV_TILE == 0 and ctx_len % tq == 0
    kern = functools.partial(_dense_attn_kernel, tk=DENSE_KV_TILE, ctx_len=ctx_len)
    return pl.pallas_call(
        kern,
        grid=(b, g, s // tq),
        in_specs=[pl.BlockSpec((None, r, tq, dh), lambda bi, gi, i: (bi, gi, i, 0)),
                  pl.BlockSpec((None, None, s, dh), lambda bi, gi, i: (bi, gi, 0, 0)),
                  pl.BlockSpec((None, None, dh, s), lambda bi, gi, i: (bi, gi, 0, 0))],
        out_specs=pl.BlockSpec((None, r, dh, tq), lambda bi, gi, i: (bi, gi, 0, i)),
        out_shape=jax.ShapeDtypeStruct((b, hq, dh, s), F32),
        compiler_params=_cparams(("parallel", "parallel", "parallel")),
        name="dense_attention",
    )(qh, kh, vth)


def _window_attn_kernel(sink_ref, q_ref, bias_first_ref, bias_mid_ref, bias_last_ref,
                        kc_ref, kp_ref, km_ref, kn_ref, vc_ref, vp_ref, vm_ref, vn_ref, o_ref):
    gi = pl.program_id(1)
    _, r, rows, dh = q_ref.shape
    tq = Q_BLOCK
    nb = rows // tq
    contract_lanes = (((1,), (1,)), ((), ()))
    k_all = jnp.concatenate([kp_ref[...], km_ref[0, 0], kn_ref[...]], axis=0)
    vt_all = jnp.concatenate([vp_ref[...], vm_ref[0, 0], vn_ref[...]], axis=1)
    k_ctx, vt_ctx = kc_ref[...], vc_ref[...]
    lane = lax.broadcasted_iota(jnp.int32, (1, r * tq), 1)
    sink = jnp.zeros((1, r * tq), F32)
    for hh in range(r):
        sink = jnp.where(lane // tq == hh, sink_ref[gi * r + hh] * LOG2_E, sink)

    def scores(qb):
        q = jnp.concatenate([q_ref[0, hh, qb * tq:(qb + 1) * tq, :] for hh in range(r)], axis=0)
        bias_ref = bias_first_ref if qb == 0 else (bias_last_ref if qb == nb - 1 else bias_mid_ref)
        s_loc = lax.dot_general(k_all[qb * tq:(qb + 3) * tq], q, contract_lanes,
                                preferred_element_type=F32) + bias_ref[...]
        s_ctx = lax.dot_general(k_ctx, q, contract_lanes, preferred_element_type=F32)
        return s_loc, s_ctx

    def finish(qb, p_loc, p_ctx, l):
        o = (jnp.dot(vt_all[:, qb * tq:(qb + 3) * tq], p_loc, preferred_element_type=F32)
             + jnp.dot(vt_ctx, p_ctx, preferred_element_type=F32)) / l
        for hh in range(r):
            o_ref[hh, :, qb * tq:(qb + 1) * tq] = o[:, hh * tq:(hh + 1) * tq].astype(o_ref.dtype)

    s_cur = scores(0)
    pending = None
    for qb in range(nb):
        if pending is not None:
            finish(*pending)
        s_next = scores(qb + 1) if qb + 1 < nb else None
        s_loc, s_ctx = s_cur
        m = jnp.maximum(jnp.maximum(jnp.max(s_loc, axis=0, keepdims=True),
                                    jnp.max(s_ctx, axis=0, keepdims=True)), sink)
        p_loc = jnp.exp2(s_loc - m)
        p_ctx = jnp.exp2(s_ctx - m)
        l = (jnp.sum(p_loc, axis=0, keepdims=True) + jnp.sum(p_ctx, axis=0, keepdims=True)
             + jnp.exp2(sink - m))
        pending = (qb, p_loc.astype(BF16), p_ctx.astype(BF16), l)
        s_cur = s_next
    finish(*pending)


def _window_bias(tq, r):
    key = np.arange(3 * tq)[:, None] - tq
    qpos = np.arange(tq)[None, :]
    band = np.abs(key - qpos) <= WINDOW
    variants = [band & (key >= 0), band, band & (key < tq)]
    table = np.stack([np.where(np.tile(v, (1, r)), 0.0, -np.inf) for v in variants])
    return jnp.asarray(table, F32)


def _window_attention(qh, kh, vth, sink, *, ctx_len):
    b, hq, s, dh = qh.shape
    g = kh.shape[1]
    r = hq // g
    tq = Q_BLOCK
    t = s - ctx_len
    nb = WINDOW_BLOCKS
    rows = nb * tq
    assert t % rows == 0 and ctx_len % tq == 0
    ns = t // rows
    off = ctx_len // tq
    last = t // tq - 1
    bias = _window_bias(tq, r)
    ctx = lambda bi, gi, i: (bi, gi, 0, 0)
    blk = lambda n_rows: (None, None, n_rows, dh)
    tblk = lambda n_cols: (None, None, dh, n_cols)
    prev_blk = lambda i: jnp.maximum(nb * i - 1, 0) + off
    next_blk = lambda i: jnp.minimum(nb * i + nb, last) + off
    el = pl.Element
    mid = lambda i: pl.multiple_of(ctx_len + i * rows, tq)
    return pl.pallas_call(
        _window_attn_kernel,
        grid=(b, g, ns),
        in_specs=[pl.BlockSpec(memory_space=pltpu.SMEM),
                  pl.BlockSpec((el(1), el(r), el(rows), el(dh)), lambda bi, gi, i: (bi, gi * r, mid(i), 0)),
                  pl.BlockSpec((None, 3 * tq, r * tq), lambda bi, gi, i: (jnp.minimum(i, 1), 0, 0)),
                  pl.BlockSpec((None, 3 * tq, r * tq), lambda bi, gi, i: (1, 0, 0)),
                  pl.BlockSpec((None, 3 * tq, r * tq),
                               lambda bi, gi, i: (jnp.where(i == ns - 1, 2, 1), 0, 0)),
                  pl.BlockSpec(blk(ctx_len), ctx),
                  pl.BlockSpec(blk(tq), lambda bi, gi, i: (bi, gi, prev_blk(i), 0)),
                  pl.BlockSpec((el(1), el(1), el(rows), el(dh)), lambda bi, gi, i: (bi, gi, mid(i), 0)),
                  pl.BlockSpec(blk(tq), lambda bi, gi, i: (bi, gi, next_blk(i), 0)),
                  pl.BlockSpec(tblk(ctx_len), ctx),
                  pl.BlockSpec(tblk(tq), lambda bi, gi, i: (bi, gi, 0, prev_blk(i))),
                  pl.BlockSpec((el(1), el(1), el(dh), el(rows)), lambda bi, gi, i: (bi, gi, 0, mid(i))),
                  pl.BlockSpec(tblk(tq), lambda bi, gi, i: (bi, gi, 0, next_blk(i)))],
        out_specs=pl.BlockSpec((None, r, dh, rows), lambda bi, gi, i: (bi, gi, 0, i)),
        out_shape=jax.ShapeDtypeStruct((b, hq, dh, t), F32),
        compiler_params=_cparams(("parallel", "parallel", "parallel")),
        name="window_attention",
    )(sink, qh, bias, bias, bias, kh, kh, kh, kh, vth, vth, vth, vth)


def _rwkv_prep_kernel(p_ref, hp_ref, hn_ref, mu_ref, w0_ref, w2_ref, a0_ref, a2_ref,
                      kk_ref, ka_ref, rk_ref, ones_ref,
                      m_ref, n_ref, q_ref, y0_ref, bonus_ref, *, ctx_chunks, n_chunks):
    j = pl.program_id(1)
    c = CHUNK
    bw = kk_ref.shape[1]
    n_pairs = bw // LANES

    p = p_ref[...]
    has_prev = jnp.logical_and(j != 0, j != ctx_chunks)
    has_next = jnp.logical_and(j != ctx_chunks - 1, j != n_chunks - 1)
    prev_row = jnp.where(has_prev, hp_ref[7:8, :], 0.0)
    next_row = jnp.where(has_next, hn_ref[0:1, :], 0.0)
    rowi = lax.broadcasted_iota(jnp.int32, (c, 1), 0)
    p_prev = jnp.where(rowi == 0, prev_row, pltpu.roll(p, 1, 0))
    p_next = jnp.where(rowi == c - 1, next_row, pltpu.roll(p, c - 1, 0))
    ps = p + mu_ref[...] * (0.5 * (p_prev + p_next) - p)

    r = ps[:, 0:bw]
    k = ps[:, bw:2 * bw]
    v = ps[:, 2 * bw:3 * bw]
    zw = ps[:, 3 * bw:3 * bw + 2 * B_LORA]
    za = ps[:, 3 * bw + 2 * B_LORA:3 * bw + 4 * B_LORA]

    wl = w0_ref[...] + jnp.dot(jnp.tanh(zw).astype(BF16), w2_ref[...],
                               preferred_element_type=F32)
    z = -wl
    softplus = jnp.maximum(z, 0.0) + jnp.log(1.0 + jnp.exp(-jnp.abs(z)))
    logw = -jnp.exp(-softplus - 0.5)
    a = _sigmoid(a0_ref[...] + jnp.dot(za.astype(BF16), a2_ref[...],
                                       preferred_element_type=F32))

    ones_seg = ones_ref[...]
    kf = k * kk_ref[...]
    kf2 = kf * kf
    ss = jnp.concatenate([_split_dot(kf2[:, g * LANES:(g + 1) * LANES], ones_seg)
                          for g in range(n_pairs)], axis=1)
    kk = kf / jnp.maximum(jnp.sqrt(ss), 1e-12)

    gw = RWKV_GROUP_LANES
    hpg = gw // HEAD_DIM
    n_groups = bw // gw
    ti = lax.broadcasted_iota(jnp.int32, (c, c), 0)
    tj = lax.broadcasted_iota(jnp.int32, (c, c), 1)
    ti_g = lax.broadcasted_iota(jnp.int32, (c, gw), 0)
    tj_g = lax.broadcasted_iota(jnp.int32, (c, gw), 1) % c
    eye_g = (ti_g == tj_g).astype(F32)
    gi_r = lax.broadcasted_iota(jnp.int32, (gw, gw), 0)
    gi_c = lax.broadcasted_iota(jnp.int32, (gw, gw), 1)
    group_diag = (gi_r // HEAD_DIM) == (gi_c // HEAD_DIM)
    bi = lax.broadcasted_iota(jnp.int32, (LANES, LANES), 0)
    bj = lax.broadcasted_iota(jnp.int32, (LANES, LANES), 1)
    same_head = (bi // HEAD_DIM) == (bj // HEAD_DIM)
    diag = bi == bj

    def blockdiag(xs):
        xb = xs.astype(BF16)
        return jnp.where(group_diag, jnp.concatenate([xb] * hpg, axis=0), jnp.zeros((), BF16))

    v_bd = [blockdiag(v[:, g * gw:(g + 1) * gw]) for g in range(n_groups)]

    bonus = jnp.zeros((c, bw), F32)
    pairs = []
    groups = []
    for d in range(2):
        lw = logw[:, d * bw:(d + 1) * bw]
        a_d = a[:, d * bw:(d + 1) * bw]
        b_d = kk * a_d
        kmod = k * (1.0 + (a_d - 1.0) * ka_ref[...])
        rkk = r * kmod * rk_ref[...]
        bsum = jnp.concatenate([_split_dot(rkk[:, g * LANES:(g + 1) * LANES], ones_seg)
                                for g in range(n_pairs)], axis=1)
        bonus = bonus + bsum * v

        if d == 0:
            tri = (tj <= ti).astype(BF16)
            strict = tj_g < ti_g
            incl = tj_g <= ti_g
            end_row = c - 1
        else:
            tri = (tj >= ti).astype(BF16)
            strict = tj_g > ti_g
            incl = tj_g >= ti_g
            end_row = 0
        lw_hi = lw.astype(BF16)
        lw_lo = (lw - lw_hi.astype(F32)).astype(BF16)
        cum = (jnp.dot(tri, lw_hi, preferred_element_type=F32)
               + jnp.dot(tri, lw_lo, preferred_element_type=F32))
        cum_end = cum[end_row:end_row + 1, :]
        e_in = jnp.exp(cum)
        e_out = jnp.exp(-cum)
        r_t = r * e_in
        kk_t = kk * jnp.exp(cum - lw)
        k_h = kmod * e_out
        b_h = b_d * e_out
        tail = jnp.exp(cum_end - cum)
        k_bar = kmod * tail
        b_bar = b_d * tail
        w_end = jnp.exp(cum_end)

        for g in range(n_groups):
            sl = slice(g * gw, (g + 1) * gw)
            groups.append(dict(
                d=d, g=g, sl=sl, strict=strict, incl=incl, rt=r_t[:, sl],
                x_full=jnp.concatenate([kk_t[:, sl], r_t[:, sl]], axis=0).astype(BF16),
                kh_bd=blockdiag(k_h[:, sl]), bh_bd=blockdiag(b_h[:, sl]), kkt_bd=blockdiag(kk_t[:, sl])))
        for pr in range(n_pairs):
            sl = slice(pr * LANES, (pr + 1) * LANES)
            pairs.append(dict(d=d, pr=pr, sl=sl, v=v[:, sl].astype(BF16), kbar=k_bar[:, sl].astype(BF16),
                              bbar=b_bar[:, sl].astype(BF16), w_end=w_end[:, sl]))
    bonus_ref[...] = bonus

    contract_lanes = (((1,), (1,)), ((), ()))
    contract_rows = (((0,), (0,)), ((), ()))
    aks = [lax.dot_general(gp["x_full"], gp["kh_bd"], contract_lanes, preferred_element_type=F32)
           for gp in groups]
    abs_ = [lax.dot_general(gp["x_full"], gp["bh_bd"], contract_lanes, preferred_element_type=F32)
            for gp in groups]
    a_ks = [jnp.concatenate([jnp.where(gp["strict"], ak[0:c], 0.0),
                             jnp.where(gp["incl"], ak[c:2 * c], 0.0)], axis=0).astype(BF16)
            for gp, ak in zip(groups, aks)]
    a_qbs = [jnp.where(gp["incl"], ab[c:2 * c], 0.0).astype(BF16) for gp, ab in zip(groups, abs_)]

    pws = [jnp.where(gp["strict"], -ab[0:c], 0.0) for gp, ab in zip(groups, abs_)]
    tinvs = [eye_g + pw for pw in pws]
    pws = [jnp.dot(pw.astype(BF16), blockdiag(pw), preferred_element_type=F32) for pw in pws]
    for step in range(4):
        stacked = [jnp.dot(jnp.concatenate([t, pw], axis=0).astype(BF16), blockdiag(pw),
                           preferred_element_type=F32) for t, pw in zip(tinvs, pws)]
        tinvs = [t + st[0:c] for t, st in zip(tinvs, stacked)]
        pws = [st[c:2 * c] for st in stacked]
    tinvs = [(t + jnp.dot(t.astype(BF16), blockdiag(pw), preferred_element_type=F32)).astype(BF16)
             for t, pw in zip(tinvs, pws)]

    avs = [jnp.dot(a_k, v_bd[gp["g"]], preferred_element_type=F32) for gp, a_k in zip(groups, a_ks)]
    p1s = [jnp.dot(t, gp["kkt_bd"], preferred_element_type=F32) for gp, t in zip(groups, tinvs)]
    p2s = [jnp.dot(t, blockdiag(av[0:c]), preferred_element_type=F32) for t, av in zip(tinvs, avs)]
    qp1s = [jnp.dot(a_qb, blockdiag(p1), preferred_element_type=F32) for a_qb, p1 in zip(a_qbs, p1s)]
    qp2s = [jnp.dot(a_qb, blockdiag(p2), preferred_element_type=F32) for a_qb, p2 in zip(a_qbs, p2s)]
    for gp, av, qp1, qp2 in zip(groups, avs, qp1s, qp2s):
        q_ref[gp["d"], :, gp["sl"]] = (gp["rt"] - qp1).astype(q_ref.dtype)
        y0_ref[gp["d"], :, gp["sl"]] = av[c:2 * c] - qp2

    for pp in pairs:
        gidx = pp["d"] * n_groups + pp["pr"] * LANES // gw
        lo = pp["pr"] * LANES % gw
        p12 = jnp.concatenate([p1s[gidx][:, lo:lo + LANES], p2s[gidx][:, lo:lo + LANES]], axis=1).astype(BF16)
        bp = lax.dot_general(pp["bbar"], p12, contract_rows, preferred_element_type=F32)
        kv = lax.dot_general(pp["kbar"], pp["v"], contract_rows, preferred_element_type=F32)
        w_diag = jnp.where(diag, pp["w_end"], 0.0)
        m_ref[pp["d"], pp["pr"]] = (w_diag - jnp.where(same_head, bp[:, 0:LANES], 0.0)).astype(m_ref.dtype)
        n_ref[pp["d"], pp["pr"]] = jnp.where(same_head, kv - bp[:, LANES:2 * LANES], 0.0)


def _rwkv_prep(bp, mu, w0, w2cat, a0, a2cat, kkw, kaw, rkw, ones_seg, *, ctx_len):
    b, s, pw = bp.shape
    bw = kkw.shape[1]
    n_pairs = bw // LANES
    nc = s // CHUNK
    ctx_chunks = ctx_len // CHUNK
    rb = CHUNK // 8
    kern = functools.partial(_rwkv_prep_kernel, ctx_chunks=ctx_chunks, n_chunks=nc)
    const = lambda bi, j: (0, 0)
    return pl.pallas_call(
        kern,
        grid=(b, nc),
        in_specs=[pl.BlockSpec((None, CHUNK, pw), lambda bi, j: (bi, j, 0)),
                  pl.BlockSpec((None, 8, pw), lambda bi, j: (bi, jnp.maximum(j * rb - 1, 0), 0)),
                  pl.BlockSpec((None, 8, pw), lambda bi, j: (bi, jnp.minimum((j + 1) * rb, s // 8 - 1), 0)),
                  pl.BlockSpec((1, pw), const),
                  pl.BlockSpec((1, 2 * bw), const),
                  pl.BlockSpec((2 * B_LORA, 2 * bw), const),
                  pl.BlockSpec((1, 2 * bw), const),
                  pl.BlockSpec((2 * B_LORA, 2 * bw), const),
                  pl.BlockSpec((1, bw), const),
                  pl.BlockSpec((1, bw), const),
                  pl.BlockSpec((1, bw), const),
                  pl.BlockSpec((LANES, LANES), const)],
        out_specs=[pl.BlockSpec((None, None, 2, n_pairs, LANES, LANES), lambda bi, j: (bi, j, 0, 0, 0, 0)),
                   pl.BlockSpec((None, None, 2, n_pairs, LANES, LANES), lambda bi, j: (bi, j, 0, 0, 0, 0)),
                   pl.BlockSpec((None, 2, CHUNK, bw), lambda bi, j: (bi, 0, j, 0)),
                   pl.BlockSpec((None, 2, CHUNK, bw), lambda bi, j: (bi, 0, j, 0)),
                   pl.BlockSpec((None, CHUNK, bw), lambda bi, j: (bi, j, 0))],
        out_shape=[jax.ShapeDtypeStruct((b, nc, 2, n_pairs, LANES, LANES), BF16),
                   jax.ShapeDtypeStruct((b, nc, 2, n_pairs, LANES, LANES), F32),
                   jax.ShapeDtypeStruct((b, 2, s, bw), BF16),
                   jax.ShapeDtypeStruct((b, 2, s, bw), F32),
                   jax.ShapeDtypeStruct((b, s, bw), F32)],
        compiler_params=_cparams(("parallel", "parallel")),
        name="rwkv_prep",
    )(bp, bp, bp, mu, w0, w2cat, a0, a2cat, kkw, kaw, rkw, ones_seg)


def _rwkv_scan_kernel(m0_ref, m1_ref, n0_ref, n1_ref, q0_ref, q1_ref, y00_ref, y01_ref,
                      o0_ref, o1_ref, h_ref):
    j = pl.program_id(0)

    @pl.when(j == 0)
    def _():
        h_ref[...] = jnp.zeros_like(h_ref)

    nb, n_pairs = m0_ref.shape[0], m0_ref.shape[1]
    dirs = ((m0_ref, n0_ref, q0_ref, y00_ref, o0_ref), (m1_ref, n1_ref, q1_ref, y01_ref, o1_ref))
    for d, (m_ref, n_ref, q_ref, y0_ref, o_ref) in enumerate(dirs):
        for bi in range(nb):
            for pr in range(n_pairs):
                sl = slice(pr * LANES, (pr + 1) * LANES)
                h = h_ref[d, bi, pr]
                hb = h.astype(BF16)
                o_ref[bi, :, sl] = y0_ref[bi, :, sl] + jnp.dot(
                    q_ref[bi, :, sl], hb, preferred_element_type=F32)
                h_ref[d, bi, pr] = n_ref[bi, pr] + jnp.dot(
                    m_ref[bi, pr], hb, preferred_element_type=F32)


def _rwkv_scan(m, n, q, y0, *, ctx_len):
    b, nc, _, n_pairs, _, _ = m.shape
    s, bw = q.shape[2], q.shape[3]
    cc = ctx_len // CHUNK

    def mem_chunk(d, j):
        if d == 0:
            return j
        return jnp.where(j < cc, cc - 1 - j, nc - 1 + cc - j)

    def mn_spec(d):
        return pl.BlockSpec((b, None, None, n_pairs, LANES, LANES),
                            lambda j: (0, mem_chunk(d, j), d, 0, 0, 0))

    def row_spec(d):
        return pl.BlockSpec((b, None, CHUNK, bw), lambda j: (0, d, mem_chunk(d, j), 0))

    return pl.pallas_call(
        _rwkv_scan_kernel,
        grid=(nc,),
        in_specs=[mn_spec(0), mn_spec(1), mn_spec(0), mn_spec(1),
                  row_spec(0), row_spec(1), row_spec(0), row_spec(1)],
        out_specs=[pl.BlockSpec((b, CHUNK, bw), lambda j: (0, mem_chunk(0, j), 0)),
                   pl.BlockSpec((b, CHUNK, bw), lambda j: (0, mem_chunk(1, j), 0))],
        out_shape=[jax.ShapeDtypeStruct((b, s, bw), F32), jax.ShapeDtypeStruct((b, s, bw), F32)],
        scratch_shapes=[pltpu.VMEM((2, b, n_pairs, LANES, LANES), F32)],
        compiler_params=_cparams(("arbitrary",)),
        name="rwkv_scan",
    )(m, m, n, n, q, q, y0, y0)


def _outproj_kernel(*refs, two_sources, ctx_tiles, rwkv):
    n_src = 2 if two_sources else 1
    src = refs[:n_src]
    at_ref, g_ref, w_ref, mod_ref, gpost_ref = refs[n_src:n_src + 5]
    rest = refs[n_src + 5:]
    out_ref = rest[-1]
    hq, dh, tm = at_ref.shape
    parts = [at_ref[...].reshape(hq * dh, tm).T]
    if rwkv:
        yf_ref, yb_ref, bonus_ref, seg_ref, gnw_ref, gnb_ref = rest[:6]
        seg = seg_ref[...]
        y_sum = bonus_ref[...]
        for y_ref in (yf_ref, yb_ref):
            y = y_ref[...]
            yc = y - _head_mean(y, seg)
            var = _head_mean(yc * yc, seg)
            y_sum = y_sum + yc * lax.rsqrt(var + GN_EPS) * gnw_ref[...] + gnb_ref[...]
        parts.append(y_sum)
    o = jnp.concatenate(parts, axis=1) if len(parts) > 1 else parts[0]
    u = (o * _silu(g_ref[...])).astype(BF16)
    y = jnp.dot(u, w_ref[...], preferred_element_type=F32)
    ms = jnp.mean(y * y, axis=-1, keepdims=True)
    yn = y * lax.rsqrt(ms + RMS_EPS) * gpost_ref[...]
    x = _stream_rows(src[0] if two_sources else None, src[-1], ctx_tiles)
    out_ref[...] = x + mod_ref[2:3, :] * yn


def _outproj(sources, attn_t, gate, w_bf16, modsel, gpost, rwkv_parts, *, ctx_tiles, latent_only):
    two_sources = len(sources) == 2
    b, d = sources[0].shape[0], sources[0].shape[2]
    s = sum(a.shape[1] for a in sources)
    tm = ROW_TILE
    off = ctx_tiles if latent_only else 0
    n_tiles = s // tm - off
    assert not (two_sources and latent_only)
    row = lambda bi, i: (bi, i + off, 0)
    const = lambda bi, i: (0, 0)
    hq, dh = attn_t.shape[1], attn_t.shape[2]
    if two_sources:
        src_specs = _stream_specs(True, tm, d, ctx_tiles)
    else:
        src_specs = [pl.BlockSpec((None, tm, d), row)]
    in_specs = src_specs + [
        pl.BlockSpec((None, hq, dh, tm), lambda bi, i: (bi, 0, 0, i)),
        pl.BlockSpec((None, tm, gate.shape[2]), row),
        pl.BlockSpec(w_bf16.shape, const),
        pl.BlockSpec((None, None, 3, d),
                     lambda bi, i: (bi, jnp.minimum((i + off) // ctx_tiles, 1), 0, 0)),
        pl.BlockSpec((1, d), const)]
    args = [*sources, attn_t, gate, w_bf16, modsel, gpost]
    if rwkv_parts is not None:
        y_f, y_b, bonus, seg, gnw, gnb = rwkv_parts
        in_specs += [pl.BlockSpec((None, tm, y_f.shape[2]), row),
                     pl.BlockSpec((None, tm, y_b.shape[2]), row),
                     pl.BlockSpec((None, tm, bonus.shape[2]), row),
                     pl.BlockSpec((MXU_WIDTH, MXU_WIDTH), const),
                     pl.BlockSpec(gnw.shape, const),
                     pl.BlockSpec(gnb.shape, const)]
        args += [y_f, y_b, bonus, seg, gnw, gnb]
    kern = functools.partial(_outproj_kernel, two_sources=two_sources, ctx_tiles=ctx_tiles,
                             rwkv=rwkv_parts is not None)
    return pl.pallas_call(
        kern,
        grid=(b, n_tiles),
        in_specs=in_specs,
        out_specs=pl.BlockSpec((None, tm, d), lambda bi, i: (bi, i, 0)),
        out_shape=jax.ShapeDtypeStruct((b, n_tiles * tm, d), F32),
        compiler_params=_cparams(("parallel", "parallel")),
        name="outproj",
    )(*args)


def _rope_tables(n_latent, ctx_len):
    t = jnp.arange(n_latent)
    rowp = (t // GRID_W).astype(F32)
    colp = (t % GRID_W).astype(F32)
    axis_dim = HEAD_DIM // 2
    inv = ROPE_THETA ** (-jnp.arange(0, axis_dim, 2, dtype=F32) / axis_dim)
    ang = jnp.concatenate([rowp[:, None] * inv, colp[:, None] * inv], axis=-1)
    cos, sin = jnp.cos(ang), jnp.sin(ang)
    cos = jnp.concatenate([jnp.ones((ctx_len, axis_dim), F32), cos], axis=0)
    sin = jnp.concatenate([jnp.zeros((ctx_len, axis_dim), F32), sin], axis=0)
    cos_h = jnp.repeat(cos, 2, axis=1)
    sin_h = jnp.stack([-sin, sin], axis=-1).reshape(sin.shape[0], HEAD_DIM)
    return jnp.tile(cos_h, (1, LANES // HEAD_DIM)), jnp.tile(sin_h, (1, LANES // HEAD_DIM))


def _block_diag_lora(w2):
    r, w = w2.shape[1], w2.shape[2]
    z = jnp.zeros((r, w), w2.dtype)
    return jnp.concatenate([jnp.concatenate([w2[0], z], axis=1),
                            jnp.concatenate([z, w2[1]], axis=1)], axis=0)


def kernel(x, c, ctx, c_ctx, w_mod, b_mod, g_pre, g_post, w_in_even, w_out_even, qn_a, kn_a, mu_b, w0_b, w2_b, a0_b, a2_b, kk_b, ka_b, rk_b, gn_w_b, gn_b_b, w_in_odd, w_out_odd, qn_c, kn_c, sink_c):
    b, t, d = x.shape
    ctx_len = ctx.shape[1]
    s = ctx_len + t
    assert ctx_len % ROW_TILE == 0 and t % ROW_TILE == 0 and b + 1 <= 8
    ctx_tiles = ctx_len // ROW_TILE
    depth = w_mod.shape[0]

    cc = jnp.concatenate([c, c_ctx[None, :], jnp.zeros((8 - b - 1, d), F32)], axis=0)
    mod = _modulation(cc, w_mod, b_mod)
    mod = mod.reshape(depth, 8, 3, d)
    modsel = jnp.stack([jnp.broadcast_to(mod[:, b][:, None], (depth, b, 3, d)), mod[:, :b]], axis=2)

    cos_t, sin_t = _rope_tables(t, ctx_len)
    seg_mean = jnp.asarray(np.kron(np.eye(MXU_WIDTH // HEAD_DIM),
                                   np.full((HEAD_DIM, HEAD_DIM), 1.0 / HEAD_DIM)), BF16)
    seg_ones = jnp.asarray(np.kron(np.eye(2), np.ones((HEAD_DIM, HEAD_DIM))), BF16)

    bw = kk_b.shape[1]
    a_width = w_out_even.shape[1] - bw
    a_heads = a_width // HEAD_DIM
    n_in = w_in_even.shape[2]
    b_proj = 3 * bw + 4 * B_LORA
    kv_width = (n_in - 2 * a_width - b_proj - bw) // 2
    a_kv_heads = kv_width // HEAD_DIM
    qk_width = a_width + kv_width
    gain0 = jnp.concatenate([jnp.tile(qn_a[0], a_heads), jnp.tile(kn_a[0], a_kv_heads)])[None, :]
    bp_lo = qk_width + kv_width
    g_lo = bp_lo + b_proj
    qh, kh, vth, bproj, gate0 = _inproj(
        [ctx, x], modsel[0], g_pre[0][None, :], w_in_even[0].astype(BF16), cos_t, sin_t, gain0, seg_mean,
        q_width=a_width, qk_width=qk_width, v_width=kv_width,
        splits=((bp_lo, g_lo), (g_lo, n_in)), out_dtypes=(F32, F32), ctx_tiles=ctx_tiles)
    oa_t = _dense_attention(qh, kh, vth, ctx_len=ctx_len)

    m_c, n_c, q_c, y0_c, bonus = _rwkv_prep(
        bproj, mu_b[0][None, :], w0_b[0].reshape(1, 2 * bw), _block_diag_lora(w2_b[0]).astype(BF16),
        a0_b[0].reshape(1, 2 * bw), _block_diag_lora(a2_b[0]).astype(BF16), kk_b[0][None, :], ka_b[0][None, :],
        rk_b[0].reshape(1, bw), seg_ones, ctx_len=ctx_len)
    y_f, y_b = _rwkv_scan(m_c, n_c, q_c, y0_c, ctx_len=ctx_len)

    xc = _outproj([ctx, x], oa_t, gate0, w_out_even[0].astype(BF16), modsel[0], g_post[0][None, :],
                  (y_f, y_b, bonus, seg_mean, gn_w_b[0][None, :], gn_b_b[0][None, :]),
                  ctx_tiles=ctx_tiles, latent_only=False)

    c_heads = sink_c.shape[1]
    c_width = c_heads * HEAD_DIM
    n_in1 = w_in_odd.shape[2]
    ckv_width = (n_in1 - 2 * c_width) // 2
    c_kv_heads = ckv_width // HEAD_DIM
    qk_width1 = c_width + ckv_width
    gain1 = jnp.concatenate([jnp.tile(qn_c[0], c_heads), jnp.tile(kn_c[0], c_kv_heads)])[None, :]
    qh1, kh1, vth1, gate1 = _inproj(
        [xc], modsel[1], g_pre[1][None, :], w_in_odd[0].astype(BF16), cos_t, sin_t, gain1, seg_mean,
        q_width=c_width, qk_width=qk_width1, v_width=ckv_width,
        splits=((qk_width1 + ckv_width, n_in1),), out_dtypes=(F32,), ctx_tiles=ctx_tiles)
    ow_t = _window_attention(qh1, kh1, vth1, sink_c[0], ctx_len=ctx_len)
    return _outproj([xc], ow_t, gate1, w_out_odd[0].astype(BF16), modsel[1], g_post[1][None, :], None,
                    ctx_tiles=ctx_tiles, latent_only=True)
```

```python
import functools

import numpy as np
import jax
import jax.numpy as jnp
from jax import lax
from jax.experimental import pallas as pl
from jax.experimental.pallas import tpu as pltpu

F32 = jnp.float32
BF16 = jnp.bfloat16
HIGHEST = lax.Precision.HIGHEST

HEAD_DIM = 64
LANES = 128
MXU_WIDTH = 256
RWKV_GROUP_LANES = 128
GRID_W = 64
Q_BLOCK = 128
WINDOW = 128
ROPE_THETA = 10000.0
RMS_EPS = 1e-6
GN_EPS = 64e-5
LOG2_E = float(np.log2(np.e))
Q_SCALE = HEAD_DIM ** -0.5 * LOG2_E
B_LORA = 64
CHUNK = 64
ROW_TILE = 256
DENSE_KV_TILE = 1024
WINDOW_BLOCKS = 8
VMEM_LIMIT = 56 * 1024 * 1024


def _cparams(sem):
    return pltpu.CompilerParams(dimension_semantics=sem, vmem_limit_bytes=VMEM_LIMIT)


def _split_dot(a, g_bf16):
    hi = a.astype(BF16)
    lo = (a - hi.astype(F32)).astype(BF16)
    return (jnp.dot(hi, g_bf16, preferred_element_type=F32)
            + jnp.dot(lo, g_bf16, preferred_element_type=F32))


def _head_mean(x, seg):
    width = x.shape[1]
    cols = []
    for lo in range(0, width, MXU_WIDTH):
        n = min(MXU_WIDTH, width - lo)
        cols.append(jnp.dot(x[:, lo:lo + n].astype(BF16), seg[0:n, 0:n], preferred_element_type=F32))
    return cols[0] if len(cols) == 1 else jnp.concatenate(cols, axis=1)


def _sigmoid(z):
    return 1.0 / (1.0 + jnp.exp(-z))


def _silu(z):
    return z * _sigmoid(z)


def _mod_kernel(c_ref, w_ref, b_ref, o_ref):
    o_ref[...] = jnp.dot(_silu(c_ref[...]), w_ref[...], precision=HIGHEST,
                         preferred_element_type=F32) + b_ref[...]


def _modulation(cc, w_mod, b_mod):
    depth, d, d3 = w_mod.shape
    nj = d3 // d
    return pl.pallas_call(
        _mod_kernel,
        grid=(depth, nj),
        in_specs=[pl.BlockSpec((8, d), lambda l, j: (0, 0)),
                  pl.BlockSpec((None, d, d), lambda l, j: (l, 0, j)),
                  pl.BlockSpec((None, 1, d), lambda l, j: (l, 0, j))],
        out_specs=pl.BlockSpec((None, 8, d), lambda l, j: (l, 0, j)),
        out_shape=jax.ShapeDtypeStruct((depth, 8, d3), F32),
        compiler_params=_cparams(("arbitrary", "arbitrary")),
        name="modulation",
    )(cc, w_mod, b_mod.reshape(depth, 1, d3))


def _stream_rows(ctx_ref, x_ref, ctx_tiles):
    if ctx_ref is None:
        return x_ref[...]
    return jnp.where(pl.program_id(1) < ctx_tiles, ctx_ref[...], x_ref[...])


def _stream_specs(two_sources, tm, d, ctx_tiles):
    if not two_sources:
        return [pl.BlockSpec((None, tm, d), lambda bi, i: (bi, i, 0))]
    return [pl.BlockSpec((None, tm, d), lambda bi, i: (bi, jnp.minimum(i, ctx_tiles - 1), 0)),
            pl.BlockSpec((None, tm, d), lambda bi, i: (bi, jnp.maximum(i - ctx_tiles, 0), 0))]


def _inproj_kernel(*refs, two_sources, ctx_tiles, q_width, qk_width, v_width, splits):
    n_src = 2 if two_sources else 1
    src = refs[:n_src]
    mod_ref, gpre_ref, w_ref, cos_ref, sin_ref, gain_ref, seg_ref = refs[n_src:n_src + 7]
    q_ref, k_ref, vt_ref = refs[n_src + 7:n_src + 10]
    extra_refs = refs[n_src + 10:]
    x = _stream_rows(src[0] if two_sources else None, src[-1], ctx_tiles)
    ms = jnp.mean(x * x, axis=-1, keepdims=True)
    h = x * lax.rsqrt(ms + RMS_EPS) * gpre_ref[...]
    h = h * (1.0 + mod_ref[1:2, :]) + mod_ref[0:1, :]
    acc = jnp.dot(h.astype(BF16), w_ref[...], preferred_element_type=F32)

    cos = cos_ref[...]
    sin = sin_ref[...]
    lane = lax.broadcasted_iota(jnp.int32, (1, LANES), 1)
    even_lane = (lane % 2) == 0
    qk = acc[:, 0:qk_width]
    msq_all = _head_mean(qk * qk, seg_ref[...])
    for g in range(qk_width // LANES):
        xg = acc[:, g * LANES:(g + 1) * LANES]
        msq = msq_all[:, g * LANES:(g + 1) * LANES]
        xn = xg * lax.rsqrt(msq + RMS_EPS) * gain_ref[:, g * LANES:(g + 1) * LANES]
        partner = jnp.where(even_lane, pltpu.roll(xn, LANES - 1, 1), pltpu.roll(xn, 1, 1))
        y = xn * cos + partner * sin
        is_q = g * LANES < q_width
        if is_q:
            y = y * Q_SCALE
        ref = q_ref if is_q else k_ref
        h0 = 2 * g if is_q else 2 * (g - q_width // LANES)
        ref[h0] = y[:, 0:HEAD_DIM].astype(ref.dtype)
        ref[h0 + 1] = y[:, HEAD_DIM:LANES].astype(ref.dtype)
    for g in range(v_width // LANES):
        vt = acc[:, qk_width + g * LANES:qk_width + (g + 1) * LANES].T
        vt_ref[2 * g] = vt[0:HEAD_DIM].astype(vt_ref.dtype)
        vt_ref[2 * g + 1] = vt[HEAD_DIM:LANES].astype(vt_ref.dtype)
    for ref, (lo, hi) in zip(extra_refs, splits):
        ref[...] = acc[:, lo:hi].astype(ref.dtype)


def _inproj(sources, modsel, gpre, w_bf16, cos_t, sin_t, gain, seg, *, q_width, qk_width, v_width,
            splits, out_dtypes, ctx_tiles):
    two_sources = len(sources) == 2
    b, d = sources[0].shape[0], sources[0].shape[2]
    s = sum(a.shape[1] for a in sources)
    n = w_bf16.shape[1]
    tm = ROW_TILE
    hq = q_width // HEAD_DIM
    hk = (qk_width - q_width) // HEAD_DIM
    hv = v_width // HEAD_DIM
    row = lambda bi, i: (bi, i, 0)
    const = lambda bi, i: (0, 0)
    out_shapes = [jax.ShapeDtypeStruct((b, hq, s, HEAD_DIM), BF16),
                  jax.ShapeDtypeStruct((b, hk, s, HEAD_DIM), BF16),
                  jax.ShapeDtypeStruct((b, hv, HEAD_DIM, s), BF16)]
    out_specs = [pl.BlockSpec((None, hq, tm, HEAD_DIM), lambda bi, i: (bi, 0, i, 0)),
                 pl.BlockSpec((None, hk, tm, HEAD_DIM), lambda bi, i: (bi, 0, i, 0)),
                 pl.BlockSpec((None, hv, HEAD_DIM, tm), lambda bi, i: (bi, 0, 0, i))]
    for (lo, hi), dt in zip(splits, out_dtypes):
        out_shapes.append(jax.ShapeDtypeStruct((b, s, hi - lo), dt))
        out_specs.append(pl.BlockSpec((None, tm, hi - lo), row))
    kern = functools.partial(_inproj_kernel, two_sources=two_sources, ctx_tiles=ctx_tiles,
                             q_width=q_width, qk_width=qk_width, v_width=v_width, splits=splits)
    return pl.pallas_call(
        kern,
        grid=(b, s // tm),
        in_specs=_stream_specs(two_sources, tm, d, ctx_tiles) + [
            pl.BlockSpec((None, None, 3, d),
                         lambda bi, i: (bi, jnp.minimum(i // ctx_tiles, 1), 0, 0)),
            pl.BlockSpec((1, d), const),
            pl.BlockSpec((d, n), const),
            pl.BlockSpec((tm, LANES), lambda bi, i: (i, 0)),
            pl.BlockSpec((tm, LANES), lambda bi, i: (i, 0)),
            pl.BlockSpec((1, qk_width), const),
            pl.BlockSpec((MXU_WIDTH, MXU_WIDTH), const)],
        out_specs=out_specs,
        out_shape=out_shapes,
        compiler_params=_cparams(("parallel", "parallel")),
        name="inproj",
    )(*sources, modsel, gpre, w_bf16, cos_t, sin_t, gain, seg)


def _dense_attn_kernel(q_ref, k_ref, vt_ref, o_ref, *, tk, ctx_len):
    i = pl.program_id(2)
    r, tq, dh = q_ref.shape
    m_rows = r * tq
    n_keys = k_ref.shape[0]
    q = q_ref[...].reshape(m_rows, dh)
    contract_lanes = (((1,), (1,)), ((), ()))

    def scores(lo, hi):
        return lax.dot_general(k_ref[lo:hi, :], q, contract_lanes, preferred_element_type=F32)

    def attend(bounds):
        m = jnp.full((1, m_rows), -jnp.inf, F32)
        l = jnp.zeros((1, m_rows), F32)
        acc = jnp.zeros((dh, m_rows), F32)
        st = scores(*bounds[0])
        pending = None
        for j in range(len(bounds)):
            if pending is not None:
                (plo, phi), p_prev = pending
                acc = acc + jnp.dot(vt_ref[:, plo:phi], p_prev, preferred_element_type=F32)
            st_next = scores(*bounds[j + 1]) if j + 1 < len(bounds) else None
            m_new = jnp.maximum(m, jnp.max(st, axis=0, keepdims=True))
            alpha = jnp.exp2(m - m_new)
            p = jnp.exp2(st - m_new)
            l = alpha * l + jnp.sum(p, axis=0, keepdims=True)
            acc = acc * alpha
            pending = (bounds[j], p.astype(BF16))
            m, st = m_new, st_next
        (plo, phi), p_prev = pending
        acc = acc + jnp.dot(vt_ref[:, plo:phi], p_prev, preferred_element_type=F32)
        o = acc / l
        for hh in range(r):
            o_ref[hh] = o[:, hh * tq:(hh + 1) * tq].astype(o_ref.dtype)

    ctx_bounds = [(0, ctx_len)]
    all_bounds = ctx_bounds + [(lo, lo + tk) for lo in range(ctx_len, n_keys, tk)]

    @pl.when(i < ctx_len // tq)
    def _():
        attend(ctx_bounds)

    @pl.when(i >= ctx_len // tq)
    def _():
        attend(all_bounds)


def _dense_attention(qh, kh, vth, *, ctx_len):
    b, hq, s, dh = qh.shape
    g = kh.shape[1]
    r = hq // g
    tq = Q_BLOCK
    assert (s - ctx_len) % DENSE_KV_TILE == 0 and ctx_len % tq == 0
    kern = functools.partial(_dense_attn_kernel, tk=DENSE_KV_TILE, ctx_len=ctx_len)
    return pl.pallas_call(
        kern,
        grid=(b, g, s // tq),
        in_specs=[pl.BlockSpec((None, r, tq, dh), lambda bi, gi, i: (bi, gi, i, 0)),
                  pl.BlockSpec((None, None, s, dh), lambda bi, gi, i: (bi, gi, 0, 0)),
                  pl.BlockSpec((None, None, dh, s), lambda bi, gi, i: (bi, gi, 0, 0))],
        out_specs=pl.BlockSpec((None, r, dh, tq), lambda bi, gi, i: (bi, gi, 0, i)),
        out_shape=jax.ShapeDtypeStruct((b, hq, dh, s), F32),
        compiler_params=_cparams(("parallel", "parallel", "parallel")),
        name="dense_attention",
    )(qh, kh, vth)


def _window_attn_kernel(sink_ref, q_ref, bias_first_ref, bias_mid_ref, bias_last_ref,
                        kc_ref, kp_ref, km_ref, kn_ref, vc_ref, vp_ref, vm_ref, vn_ref, o_ref):
    gi = pl.program_id(1)
    _, r, rows, dh = q_ref.shape
    tq = Q_BLOCK
    nb = rows // tq
    contract_lanes = (((1,), (1,)), ((), ()))
    k_all = jnp.concatenate([kp_ref[...], km_ref[0, 0], kn_ref[...]], axis=0)
    vt_all = jnp.concatenate([vp_ref[...], vm_ref[0, 0], vn_ref[...]], axis=1)
    k_ctx, vt_ctx = kc_ref[...], vc_ref[...]
    lane = lax.broadcasted_iota(jnp.int32, (1, r * tq), 1)
    sink = jnp.zeros((1, r * tq), F32)
    for hh in range(r):
        sink = jnp.where(lane // tq == hh, sink_ref[gi * r + hh] * LOG2_E, sink)

    def scores(qb):
        q = jnp.concatenate([q_ref[0, hh, qb * tq:(qb + 1) * tq, :] for hh in range(r)], axis=0)
        bias_ref = bias_first_ref if qb == 0 else (bias_last_ref if qb == nb - 1 else bias_mid_ref)
        s_loc = lax.dot_general(k_all[qb * tq:(qb + 3) * tq], q, contract_lanes,
                                preferred_element_type=F32) + bias_ref[...]
        s_ctx = lax.dot_general(k_ctx, q, contract_lanes, preferred_element_type=F32)
        return s_loc, s_ctx

    def finish(qb, p_loc, p_ctx, l):
        o = (jnp.dot(vt_all[:, qb * tq:(qb + 3) * tq], p_loc, preferred_element_type=F32)
             + jnp.dot(vt_ctx, p_ctx, preferred_element_type=F32)) / l
        for hh in range(r):
            o_ref[hh, :, qb * tq:(qb + 1) * tq] = o[:, hh * tq:(hh + 1) * tq].astype(o_ref.dtype)

    s_cur = scores(0)
    pending = None
    for qb in range(nb):
        if pending is not None:
            finish(*pending)
        s_next = scores(qb + 1) if qb + 1 < nb else None
        s_loc, s_ctx = s_cur
        m = jnp.maximum(jnp.maximum(jnp.max(s_loc, axis=0, keepdims=True),
                                    jnp.max(s_ctx, axis=0, keepdims=True)), sink)
        p_loc = jnp.exp2(s_loc - m)
        p_ctx = jnp.exp2(s_ctx - m)
        l = (jnp.sum(p_loc, axis=0, keepdims=True) + jnp.sum(p_ctx, axis=0, keepdims=True)
             + jnp.exp2(sink - m))
        pending = (qb, p_loc.astype(BF16), p_ctx.astype(BF16), l)
        s_cur = s_next
    finish(*pending)


def _window_bias(tq, r):
    key = np.arange(3 * tq)[:, None] - tq
    qpos = np.arange(tq)[None, :]
    band = np.abs(key - qpos) <= WINDOW
    variants = [band & (key >= 0), band, band & (key < tq)]
    table = np.stack([np.where(np.tile(v, (1, r)), 0.0, -np.inf) for v in variants])
    return jnp.asarray(table, F32)


def _window_attention(qh, kh, vth, sink, *, ctx_len):
    b, hq, s, dh = qh.shape
    g = kh.shape[1]
    r = hq // g
    tq = Q_BLOCK
    t = s - ctx_len
    nb = WINDOW_BLOCKS
    rows = nb * tq
    assert t % rows == 0 and ctx_len % tq == 0
    ns = t // rows
    off = ctx_len // tq
    last = t // tq - 1
    bias = _window_bias(tq, r)
    ctx = lambda bi, gi, i: (bi, gi, 0, 0)
    blk = lambda n_rows: (None, None, n_rows, dh)
    tblk = lambda n_cols: (None, None, dh, n_cols)
    prev_blk = lambda i: jnp.maximum(nb * i - 1, 0) + off
    next_blk = lambda i: jnp.minimum(nb * i + nb, last) + off
    el = pl.Element
    mid = lambda i: pl.multiple_of(ctx_len + i * rows, tq)
    return pl.pallas_call(
        _window_attn_kernel,
        grid=(b, g, ns),
        in_specs=[pl.BlockSpec(memory_space=pltpu.SMEM),
                  pl.BlockSpec((el(1), el(r), el(rows), el(dh)), lambda bi, gi, i: (bi, gi * r, mid(i), 0)),
                  pl.BlockSpec((None, 3 * tq, r * tq), lambda bi, gi, i: (jnp.minimum(i, 1), 0, 0)),
                  pl.BlockSpec((None, 3 * tq, r * tq), lambda bi, gi, i: (1, 0, 0)),
                  pl.BlockSpec((None, 3 * tq, r * tq),
                               lambda bi, gi, i: (jnp.where(i == ns - 1, 2, 1), 0, 0)),
                  pl.BlockSpec(blk(ctx_len), ctx),
                  pl.BlockSpec(blk(tq), lambda bi, gi, i: (bi, gi, prev_blk(i), 0)),
                  pl.BlockSpec((el(1), el(1), el(rows), el(dh)), lambda bi, gi, i: (bi, gi, mid(i), 0)),
                  pl.BlockSpec(blk(tq), lambda bi, gi, i: (bi, gi, next_blk(i), 0)),
                  pl.BlockSpec(tblk(ctx_len), ctx),
                  pl.BlockSpec(tblk(tq), lambda bi, gi, i: (bi, gi, 0, prev_blk(i))),
                  pl.BlockSpec((el(1), el(1), el(dh), el(rows)), lambda bi, gi, i: (bi, gi, 0, mid(i))),
                  pl.BlockSpec(tblk(tq), lambda bi, gi, i: (bi, gi, 0, next_blk(i)))],
        out_specs=pl.BlockSpec((None, r, dh, rows), lambda bi, gi, i: (bi, gi, 0, i)),
        out_shape=jax.ShapeDtypeStruct((b, hq, dh, t), F32),
        compiler_params=_cparams(("parallel", "parallel", "parallel")),
        name="window_attention",
    )(sink, qh, bias, bias, bias, kh, kh, kh, kh, vth, vth, vth, vth)


def _rwkv_prep_kernel(p_ref, hp_ref, hn_ref, mu_ref, w0_ref, w2_ref, a0_ref, a2_ref,
                      kk_ref, ka_ref, rk_ref, ones_ref,
                      m_ref, n_ref, q_ref, y0_ref, bonus_ref,
                      xf_s, kh_s, bh_s, kbar_s, bbar_s, rt_s, v_s, wend_s, *, ctx_chunks, n_chunks):
    step_id = pl.program_id(1)
    j = jnp.minimum(step_id, n_chunks - 1)
    c = CHUNK
    bw = kk_ref.shape[1]
    n_pairs = bw // LANES
    staged = (xf_s, kh_s, bh_s, kbar_s, bbar_s, rt_s, v_s, wend_s)

    @pl.when(step_id == 0)
    def _():
        for ref in staged:
            ref[...] = jnp.zeros_like(ref)

    gw = RWKV_GROUP_LANES
    hpg = gw // HEAD_DIM
    n_groups = bw // gw
    ti = lax.broadcasted_iota(jnp.int32, (c, c), 0)
    tj = lax.broadcasted_iota(jnp.int32, (c, c), 1)
    ti_g = lax.broadcasted_iota(jnp.int32, (c, gw), 0)
    tj_g = lax.broadcasted_iota(jnp.int32, (c, gw), 1) % c
    eye_g = (ti_g == tj_g).astype(F32)
    gi_r = lax.broadcasted_iota(jnp.int32, (gw, gw), 0)
    gi_c = lax.broadcasted_iota(jnp.int32, (gw, gw), 1)
    group_diag = (gi_r // HEAD_DIM) == (gi_c // HEAD_DIM)
    bi = lax.broadcasted_iota(jnp.int32, (LANES, LANES), 0)
    bj = lax.broadcasted_iota(jnp.int32, (LANES, LANES), 1)
    same_head = (bi // HEAD_DIM) == (bj // HEAD_DIM)
    diag = bi == bj
    scan_masks = [(tj_g < ti_g, tj_g <= ti_g), (tj_g > ti_g, tj_g >= ti_g)]

    def blockdiag(xs):
        xb = xs.astype(BF16)
        return jnp.where(group_diag, jnp.concatenate([xb] * hpg, axis=0), jnp.zeros((), BF16))


    def elementwise_stage():
        p = p_ref[...]
        has_prev = jnp.logical_and(j != 0, j != ctx_chunks)
        has_next = jnp.logical_and(j != ctx_chunks - 1, j != n_chunks - 1)
        prev_row = jnp.where(has_prev, hp_ref[7:8, :], 0.0)
        next_row = jnp.where(has_next, hn_ref[0:1, :], 0.0)
        rowi = lax.broadcasted_iota(jnp.int32, (c, 1), 0)
        p_prev = jnp.where(rowi == 0, prev_row, pltpu.roll(p, 1, 0))
        p_next = jnp.where(rowi == c - 1, next_row, pltpu.roll(p, c - 1, 0))
        ps = p + mu_ref[...] * (0.5 * (p_prev + p_next) - p)
        r = ps[:, 0:bw]
        k = ps[:, bw:2 * bw]
        v = ps[:, 2 * bw:3 * bw]
        zw = ps[:, 3 * bw:3 * bw + 2 * B_LORA]
        za = ps[:, 3 * bw + 2 * B_LORA:3 * bw + 4 * B_LORA]
        yield

        wl = w0_ref[...] + jnp.dot(jnp.tanh(zw).astype(BF16), w2_ref[...],
                                   preferred_element_type=F32)
        z = -wl
        softplus = jnp.maximum(z, 0.0) + jnp.log(1.0 + jnp.exp(-jnp.abs(z)))
        logw = -jnp.exp(-softplus - 0.5)
        yield
        a = _sigmoid(a0_ref[...] + jnp.dot(za.astype(BF16), a2_ref[...],
                                           preferred_element_type=F32))
        ones_seg = ones_ref[...]
        kf = k * kk_ref[...]
        kf2 = kf * kf
        ss = jnp.concatenate([_split_dot(kf2[:, g * LANES:(g + 1) * LANES], ones_seg)
                              for g in range(n_pairs)], axis=1)
        kk = kf / jnp.maximum(jnp.sqrt(ss), 1e-12)
        yield

        bonus = jnp.zeros((c, bw), F32)
        out = dict(xf=[], kh=[], bh=[], kbar=[], bbar=[], rt=[], wend=[])
        for d in range(2):
            lw = logw[:, d * bw:(d + 1) * bw]
            a_d = a[:, d * bw:(d + 1) * bw]
            b_d = kk * a_d
            kmod = k * (1.0 + (a_d - 1.0) * ka_ref[...])
            rkk = r * kmod * rk_ref[...]
            bsum = jnp.concatenate([_split_dot(rkk[:, g * LANES:(g + 1) * LANES], ones_seg)
                                    for g in range(n_pairs)], axis=1)
            bonus = bonus + bsum * v
            yield

            tri = ((tj <= ti) if d == 0 else (tj >= ti)).astype(BF16)
            end_row = c - 1 if d == 0 else 0
            lw_hi = lw.astype(BF16)
            lw_lo = (lw - lw_hi.astype(F32)).astype(BF16)
            cum = (jnp.dot(tri, lw_hi, preferred_element_type=F32)
                   + jnp.dot(tri, lw_lo, preferred_element_type=F32))
            cum_end = cum[end_row:end_row + 1, :]
            r_t = r * jnp.exp(cum)
            kk_t = kk * jnp.exp(cum - lw)
            yield
            e_out = jnp.exp(-cum)
            tail = jnp.exp(cum_end - cum)
            out["xf"].append(jnp.concatenate([kk_t, r_t], axis=0).astype(BF16))
            out["rt"].append(r_t)
            out["kh"].append((kmod * e_out).astype(BF16))
            out["bh"].append((b_d * e_out).astype(BF16))
            out["kbar"].append((kmod * tail).astype(BF16))
            out["bbar"].append((b_d * tail).astype(BF16))
            out["wend"].append(jnp.broadcast_to(jnp.exp(cum_end), (8, bw)))
            yield
        bonus_ref[...] = bonus
        out["v"] = v.astype(BF16)
        staged_next.update(out)

    def matmul_stage():
        contract_lanes = (((1,), (1,)), ((), ()))
        contract_rows = (((0,), (0,)), ((), ()))
        probs = [(d, g, slice(g * gw, (g + 1) * gw)) for d in range(2) for g in range(n_groups)]
        v_bd = [blockdiag(v_s[:, g * gw:(g + 1) * gw]) for g in range(n_groups)]
        aks = [lax.dot_general(xf_s[d, :, sl], blockdiag(kh_s[d, :, sl]), contract_lanes,
                               preferred_element_type=F32) for d, g, sl in probs]
        yield
        abs_ = [lax.dot_general(xf_s[d, :, sl], blockdiag(bh_s[d, :, sl]), contract_lanes,
                                preferred_element_type=F32) for d, g, sl in probs]
        yield
        a_ks = [jnp.concatenate([jnp.where(scan_masks[d][0], ak[0:c], 0.0),
                                 jnp.where(scan_masks[d][1], ak[c:2 * c], 0.0)], axis=0).astype(BF16)
                for (d, g, sl), ak in zip(probs, aks)]
        a_qbs = [jnp.where(scan_masks[d][1], ab[c:2 * c], 0.0).astype(BF16)
                 for (d, g, sl), ab in zip(probs, abs_)]
        pws = [jnp.where(scan_masks[d][0], -ab[0:c], 0.0) for (d, g, sl), ab in zip(probs, abs_)]
        tinvs = [eye_g + pw for pw in pws]
        pws = [jnp.dot(pw.astype(BF16), blockdiag(pw), preferred_element_type=F32) for pw in pws]
        yield
        for _ in range(4):
            stacked = [jnp.dot(jnp.concatenate([t, pw], axis=0).astype(BF16), blockdiag(pw),
                               preferred_element_type=F32) for t, pw in zip(tinvs, pws)]
            tinvs = [t + st[0:c] for t, st in zip(tinvs, stacked)]
            pws = [st[c:2 * c] for st in stacked]
            yield
        tinvs = [(t + jnp.dot(t.astype(BF16), blockdiag(pw), preferred_element_type=F32)).astype(BF16)
                 for t, pw in zip(tinvs, pws)]
        avs = [jnp.dot(a_k, v_bd[g], preferred_element_type=F32) for (d, g, sl), a_k in zip(probs, a_ks)]
        yield
        p1s = [jnp.dot(t, blockdiag(xf_s[d, 0:c, sl]), preferred_element_type=F32)
               for (d, g, sl), t in zip(probs, tinvs)]
        p2s = [jnp.dot(t, blockdiag(av[0:c]), preferred_element_type=F32) for t, av in zip(tinvs, avs)]
        yield
        qp1s = [jnp.dot(a_qb, blockdiag(p1), preferred_element_type=F32) for a_qb, p1 in zip(a_qbs, p1s)]
        qp2s = [jnp.dot(a_qb, blockdiag(p2), preferred_element_type=F32) for a_qb, p2 in zip(a_qbs, p2s)]
        for (d, g, sl), av, qp1, qp2 in zip(probs, avs, qp1s, qp2s):
            q_ref[d, :, sl] = (rt_s[d, :, sl] - qp1).astype(q_ref.dtype)
            y0_ref[d, :, sl] = av[c:2 * c] - qp2
        yield
        for d in range(2):
            for pr in range(n_pairs):
                sl = slice(pr * LANES, (pr + 1) * LANES)
                gidx = d * n_groups + pr * LANES // gw
                lo = pr * LANES % gw
                p12 = jnp.concatenate([p1s[gidx][:, lo:lo + LANES], p2s[gidx][:, lo:lo + LANES]],
                                      axis=1).astype(BF16)
                bp = lax.dot_general(bbar_s[d, :, sl], p12, contract_rows, preferred_element_type=F32)
                kv = lax.dot_general(kbar_s[d, :, sl], v_s[:, sl], contract_rows, preferred_element_type=F32)
                w_diag = jnp.where(diag, wend_s[d, 0:1, sl], 0.0)
                m_ref[d, pr] = (w_diag - jnp.where(same_head, bp[:, 0:LANES], 0.0)).astype(m_ref.dtype)
                n_ref[d, pr] = jnp.where(same_head, kv - bp[:, LANES:2 * LANES], 0.0)
            yield

    staged_next = {}
    stages = [matmul_stage(), elementwise_stage()]
    while stages:
        for stage in list(stages):
            if next(stage, "done") == "done":
                stages.remove(stage)
    for ref, key in ((xf_s, "xf"), (kh_s, "kh"), (bh_s, "bh"), (kbar_s, "kbar"), (bbar_s, "bbar"),
                     (rt_s, "rt"), (wend_s, "wend")):
        for d in range(2):
            ref[d] = staged_next[key][d]
    v_s[...] = staged_next["v"]


def _rwkv_prep(bp, mu, w0, w2cat, a0, a2cat, kkw, kaw, rkw, ones_seg, *, ctx_len):
    b, s, pw = bp.shape
    bw = kkw.shape[1]
    n_pairs = bw // LANES
    nc = s // CHUNK
    ctx_chunks = ctx_len // CHUNK
    rb = CHUNK // 8
    kern = functools.partial(_rwkv_prep_kernel, ctx_chunks=ctx_chunks, n_chunks=nc)
    const = lambda bi, j: (0, 0)
    cur = lambda j: jnp.minimum(j, nc - 1)
    done = lambda j: jnp.maximum(j - 1, 0)
    return pl.pallas_call(
        kern,
        grid=(b, nc + 1),
        in_specs=[pl.BlockSpec((None, CHUNK, pw), lambda bi, j: (bi, cur(j), 0)),
                  pl.BlockSpec((None, 8, pw), lambda bi, j: (bi, jnp.maximum(cur(j) * rb - 1, 0), 0)),
                  pl.BlockSpec((None, 8, pw),
                               lambda bi, j: (bi, jnp.minimum((cur(j) + 1) * rb, s // 8 - 1), 0)),
                  pl.BlockSpec((1, pw), const),
                  pl.BlockSpec((1, 2 * bw), const),
                  pl.BlockSpec((2 * B_LORA, 2 * bw), const),
                  pl.BlockSpec((1, 2 * bw), const),
                  pl.BlockSpec((2 * B_LORA, 2 * bw), const),
                  pl.BlockSpec((1, bw), const),
                  pl.BlockSpec((1, bw), const),
                  pl.BlockSpec((1, bw), const),
                  pl.BlockSpec((LANES, LANES), const)],
        out_specs=[pl.BlockSpec((None, None, 2, n_pairs, LANES, LANES), lambda bi, j: (bi, done(j), 0, 0, 0, 0)),
                   pl.BlockSpec((None, None, 2, n_pairs, LANES, LANES), lambda bi, j: (bi, done(j), 0, 0, 0, 0)),
                   pl.BlockSpec((None, 2, CHUNK, bw), lambda bi, j: (bi, 0, done(j), 0)),
                   pl.BlockSpec((None, 2, CHUNK, bw), lambda bi, j: (bi, 0, done(j), 0)),
                   pl.BlockSpec((None, CHUNK, bw), lambda bi, j: (bi, cur(j), 0))],
        out_shape=[jax.ShapeDtypeStruct((b, nc, 2, n_pairs, LANES, LANES), BF16),
                   jax.ShapeDtypeStruct((b, nc, 2, n_pairs, LANES, LANES), F32),
                   jax.ShapeDtypeStruct((b, 2, s, bw), BF16),
                   jax.ShapeDtypeStruct((b, 2, s, bw), F32),
                   jax.ShapeDtypeStruct((b, s, bw), F32)],
        scratch_shapes=[pltpu.VMEM((2, 2 * CHUNK, bw), BF16),
                        pltpu.VMEM((2, CHUNK, bw), BF16),
                        pltpu.VMEM((2, CHUNK, bw), BF16),
                        pltpu.VMEM((2, CHUNK, bw), BF16),
                        pltpu.VMEM((2, CHUNK, bw), BF16),
                        pltpu.VMEM((2, CHUNK, bw), F32),
                        pltpu.VMEM((CHUNK, bw), BF16),
                        pltpu.VMEM((2, 8, bw), F32)],
        compiler_params=_cparams(("parallel", "arbitrary")),
        name="rwkv_prep",
    )(bp, bp, bp, mu, w0, w2cat, a0, a2cat, kkw, kaw, rkw, ones_seg)


def _rwkv_scan_kernel(m0_ref, m1_ref, n0_ref, n1_ref, q0_ref, q1_ref, y00_ref, y01_ref,
                      o0_ref, o1_ref, h_ref):
    j = pl.program_id(0)

    @pl.when(j == 0)
    def _():
        h_ref[...] = jnp.zeros_like(h_ref)

    nb, n_pairs = m0_ref.shape[0], m0_ref.shape[1]
    dirs = ((m0_ref, n0_ref, q0_ref, y00_ref, o0_ref), (m1_ref, n1_ref, q1_ref, y01_ref, o1_ref))
    for d, (m_ref, n_ref, q_ref, y0_ref, o_ref) in enumerate(dirs):
        for bi in range(nb):
            for pr in range(n_pairs):
                sl = slice(pr * LANES, (pr + 1) * LANES)
                h = h_ref[d, bi, pr]
                hb = h.astype(BF16)
                o_ref[bi, :, sl] = y0_ref[bi, :, sl] + jnp.dot(
                    q_ref[bi, :, sl], hb, preferred_element_type=F32)
                h_ref[d, bi, pr] = n_ref[bi, pr] + jnp.dot(
                    m_ref[bi, pr], hb, preferred_element_type=F32)


def _rwkv_scan(m, n, q, y0, *, ctx_len):
    b, nc, _, n_pairs, _, _ = m.shape
    s, bw = q.shape[2], q.shape[3]
    cc = ctx_len // CHUNK

    def mem_chunk(d, j):
        if d == 0:
            return j
        return jnp.where(j < cc, cc - 1 - j, nc - 1 + cc - j)

    def mn_spec(d):
        return pl.BlockSpec((b, None, None, n_pairs, LANES, LANES),
                            lambda j: (0, mem_chunk(d, j), d, 0, 0, 0))

    def row_spec(d):
        return pl.BlockSpec((b, None, CHUNK, bw), lambda j: (0, d, mem_chunk(d, j), 0))

    return pl.pallas_call(
        _rwkv_scan_kernel,
        grid=(nc,),
        in_specs=[mn_spec(0), mn_spec(1), mn_spec(0), mn_spec(1),
                  row_spec(0), row_spec(1), row_spec(0), row_spec(1)],
        out_specs=[pl.BlockSpec((b, CHUNK, bw), lambda j: (0, mem_chunk(0, j), 0)),
                   pl.BlockSpec((b, CHUNK, bw), lambda j: (0, mem_chunk(1, j), 0))],
        out_shape=[jax.ShapeDtypeStruct((b, s, bw), F32), jax.ShapeDtypeStruct((b, s, bw), F32)],
        scratch_shapes=[pltpu.VMEM((2, b, n_pairs, LANES, LANES), F32)],
        compiler_params=_cparams(("arbitrary",)),
        name="rwkv_scan",
    )(m, m, n, n, q, q, y0, y0)


def _outproj_kernel(*refs, two_sources, ctx_tiles, rwkv):
    n_src = 2 if two_sources else 1
    src = refs[:n_src]
    at_ref, g_ref, w_ref, mod_ref, gpost_ref = refs[n_src:n_src + 5]
    rest = refs[n_src + 5:]
    out_ref = rest[-1]
    hq, dh, tm = at_ref.shape
    parts = [at_ref[...].reshape(hq * dh, tm).T]
    if rwkv:
        yf_ref, yb_ref, bonus_ref, seg_ref, gnw_ref, gnb_ref = rest[:6]
        seg = seg_ref[...]
        y_sum = bonus_ref[...]
        for y_ref in (yf_ref, yb_ref):
            y = y_ref[...]
            yc = y - _head_mean(y, seg)
            var = _head_mean(yc * yc, seg)
            y_sum = y_sum + yc * lax.rsqrt(var + GN_EPS) * gnw_ref[...] + gnb_ref[...]
        parts.append(y_sum)
    o = jnp.concatenate(parts, axis=1) if len(parts) > 1 else parts[0]
    u = (o * _silu(g_ref[...])).astype(BF16)
    y = jnp.dot(u, w_ref[...], preferred_element_type=F32)
    ms = jnp.mean(y * y, axis=-1, keepdims=True)
    yn = y * lax.rsqrt(ms + RMS_EPS) * gpost_ref[...]
    x = _stream_rows(src[0] if two_sources else None, src[-1], ctx_tiles)
    out_ref[...] = x + mod_ref[2:3, :] * yn


def _outproj(sources, attn_t, gate, w_bf16, modsel, gpost, rwkv_parts, *, ctx_tiles, latent_only):
    two_sources = len(sources) == 2
    b, d = sources[0].shape[0], sources[0].shape[2]
    s = sum(a.shape[1] for a in sources)
    tm = ROW_TILE
    off = ctx_tiles if latent_only else 0
    n_tiles = s // tm - off
    assert not (two_sources and latent_only)
    row = lambda bi, i: (bi, i + off, 0)
    const = lambda bi, i: (0, 0)
    hq, dh = attn_t.shape[1], attn_t.shape[2]
    if two_sources:
        src_specs = _stream_specs(True, tm, d, ctx_tiles)
    else:
        src_specs = [pl.BlockSpec((None, tm, d), row)]
    in_specs = src_specs + [
        pl.BlockSpec((None, hq, dh, tm), lambda bi, i: (bi, 0, 0, i)),
        pl.BlockSpec((None, tm, gate.shape[2]), row),
        pl.BlockSpec(w_bf16.shape, const),
        pl.BlockSpec((None, None, 3, d),
                     lambda bi, i: (bi, jnp.minimum((i + off) // ctx_tiles, 1), 0, 0)),
        pl.BlockSpec((1, d), const)]
    args = [*sources, attn_t, gate, w_bf16, modsel, gpost]
    if rwkv_parts is not None:
        y_f, y_b, bonus, seg, gnw, gnb = rwkv_parts
        in_specs += [pl.BlockSpec((None, tm, y_f.shape[2]), row),
                     pl.BlockSpec((None, tm, y_b.shape[2]), row),
                     pl.BlockSpec((None, tm, bonus.shape[2]), row),
                     pl.BlockSpec((MXU_WIDTH, MXU_WIDTH), const),
                     pl.BlockSpec(gnw.shape, const),
                     pl.BlockSpec(gnb.shape, const)]
        args += [y_f, y_b, bonus, seg, gnw, gnb]
    kern = functools.partial(_outproj_kernel, two_sources=two_sources, ctx_tiles=ctx_tiles,
                             rwkv=rwkv_parts is not None)
    return pl.pallas_call(
        kern,
        grid=(b, n_tiles),
        in_specs=in_specs,
        out_specs=pl.BlockSpec((None, tm, d), lambda bi, i: (bi, i, 0)),
        out_shape=jax.ShapeDtypeStruct((b, n_tiles * tm, d), F32),
        compiler_params=_cparams(("parallel", "parallel")),
        name="outproj",
    )(*args)


def _rope_tables(n_latent, ctx_len):
    t = jnp.arange(n_latent)
    rowp = (t // GRID_W).astype(F32)
    colp = (t % GRID_W).astype(F32)
    axis_dim = HEAD_DIM // 2
    inv = ROPE_THETA ** (-jnp.arange(0, axis_dim, 2, dtype=F32) / axis_dim)
    ang = jnp.concatenate([rowp[:, None] * inv, colp[:, None] * inv], axis=-1)
    cos, sin = jnp.cos(ang), jnp.sin(ang)
    cos = jnp.concatenate([jnp.ones((ctx_len, axis_dim), F32), cos], axis=0)
    sin = jnp.concatenate([jnp.zeros((ctx_len, axis_dim), F32), sin], axis=0)
    cos_h = jnp.repeat(cos, 2, axis=1)
    sin_h = jnp.stack([-sin, sin], axis=-1).reshape(sin.shape[0], HEAD_DIM)
    return jnp.tile(cos_h, (1, LANES // HEAD_DIM)), jnp.tile(sin_h, (1, LANES // HEAD_DIM))


def _block_diag_lora(w2):
    r, w = w2.shape[1], w2.shape[2]
    z = jnp.zeros((r, w), w2.dtype)
    return jnp.concatenate([jnp.concatenate([w2[0], z], axis=1),
                            jnp.concatenate([z, w2[1]], axis=1)], axis=0)


def kernel(x, c, ctx, c_ctx, w_mod, b_mod, g_pre, g_post, w_in_even, w_out_even, qn_a, kn_a, mu_b, w0_b, w2_b, a0_b, a2_b, kk_b, ka_b, rk_b, gn_w_b, gn_b_b, w_in_odd, w_out_odd, qn_c, kn_c, sink_c):
    b, t, d = x.shape
    ctx_len = ctx.shape[1]
    s = ctx_len + t
    assert ctx_len % ROW_TILE == 0 and t % ROW_TILE == 0 and b + 1 <= 8
    ctx_tiles = ctx_len // ROW_TILE
    depth = w_mod.shape[0]

    cc = jnp.concatenate([c, c_ctx[None, :], jnp.zeros((8 - b - 1, d), F32)], axis=0)
    mod = _modulation(cc, w_mod, b_mod)
    mod = mod.reshape(depth, 8, 3, d)
    modsel = jnp.stack([jnp.broadcast_to(mod[:, b][:, None], (depth, b, 3, d)), mod[:, :b]], axis=2)

    cos_t, sin_t = _rope_tables(t, ctx_len)
    seg_mean = jnp.asarray(np.kron(np.eye(MXU_WIDTH // HEAD_DIM),
                                   np.full((HEAD_DIM, HEAD_DIM), 1.0 / HEAD_DIM)), BF16)
    seg_ones = jnp.asarray(np.kron(np.eye(2), np.ones((HEAD_DIM, HEAD_DIM))), BF16)

    bw = kk_b.shape[1]
    a_width = w_out_even.shape[1] - bw
    a_heads = a_width // HEAD_DIM
    n_in = w_in_even.shape[2]
    b_proj = 3 * bw + 4 * B_LORA
    kv_width = (n_in - 2 * a_width - b_proj - bw) // 2
    a_kv_heads = kv_width // HEAD_DIM
    qk_width = a_width + kv_width
    gain0 = jnp.concatenate([jnp.tile(qn_a[0], a_heads), jnp.tile(kn_a[0], a_kv_heads)])[None, :]
    bp_lo = qk_width + kv_width
    g_lo = bp_lo + b_proj
    qh, kh, vth, bproj, gate0 = _inproj(
        [ctx, x], modsel[0], g_pre[0][None, :], w_in_even[0].astype(BF16), cos_t, sin_t, gain0, seg_mean,
        q_width=a_width, qk_width=qk_width, v_width=kv_width,
        splits=((bp_lo, g_lo), (g_lo, n_in)), out_dtypes=(F32, F32), ctx_tiles=ctx_tiles)
    oa_t = _dense_attention(qh, kh, vth, ctx_len=ctx_len)

    m_c, n_c, q_c, y0_c, bonus = _rwkv_prep(
        bproj, mu_b[0][None, :], w0_b[0].reshape(1, 2 * bw), _block_diag_lora(w2_b[0]).astype(BF16),
        a0_b[0].reshape(1, 2 * bw), _block_diag_lora(a2_b[0]).astype(BF16), kk_b[0][None, :], ka_b[0][None, :],
        rk_b[0].reshape(1, bw), seg_ones, ctx_len=ctx_len)
    y_f, y_b = _rwkv_scan(m_c, n_c, q_c, y0_c, ctx_len=ctx_len)

    xc = _outproj([ctx, x], oa_t, gate0, w_out_even[0].astype(BF16), modsel[0], g_post[0][None, :],
                  (y_f, y_b, bonus, seg_mean, gn_w_b[0][None, :], gn_b_b[0][None, :]),
                  ctx_tiles=ctx_tiles, latent_only=False)

    c_heads = sink_c.shape[1]
    c_width = c_heads * HEAD_DIM
    n_in1 = w_in_odd.shape[2]
    ckv_width = (n_in1 - 2 * c_width) // 2
    c_kv_heads = ckv_width // HEAD_DIM
    qk_width1 = c_width + ckv_width
    gain1 = jnp.concatenate([jnp.tile(qn_c[0], c_heads), jnp.tile(kn_c[0], c_kv_heads)])[None, :]
    qh1, kh1, vth1, gate1 = _inproj(
        [xc], modsel[1], g_pre[1][None, :], w_in_odd[0].astype(BF16), cos_t, sin_t, gain1, seg_mean,
        q_width=c_width, qk_width=qk_width1, v_width=ckv_width,
        splits=((qk_width1 + ckv_width, n_in1),), out_dtypes=(F32,), ctx_tiles=ctx_tiles)
    ow_t = _window_attention(qh1, kh1, vth1, sink_c[0], ctx_len=ctx_len)
    return _outproj([xc], ow_t, gate1, w_out_odd[0].astype(BF16), modsel[1], g_post[1][None, :], None,
                    ctx_tiles=ctx_tiles, latent_only=True)
```

```python
import functools

import numpy as np
import jax
import jax.numpy as jnp
from jax import lax
from jax.experimental import pallas as pl
from jax.experimental.pallas import tpu as pltpu

F32 = jnp.float32
BF16 = jnp.bfloat16
HIGHEST = lax.Precision.HIGHEST

HEAD_DIM = 64
LANES = 128
MXU_WIDTH = 256
RWKV_GROUP_LANES = 128
GRID_W = 64
Q_BLOCK = 128
WINDOW = 128
ROPE_THETA = 10000.0
RMS_EPS = 1e-6
GN_EPS = 64e-5
LOG2_E = float(np.log2(np.e))
Q_SCALE = HEAD_DIM ** -0.5 * LOG2_E
B_LORA = 64
CHUNK = 64
ROW_TILE = 256
DENSE_Q_TILE = 128
DENSE_KV_TILE = 1024
WINDOW_BLOCKS = 8
VMEM_LIMIT = 56 * 1024 * 1024


def _cparams(sem):
    return pltpu.CompilerParams(dimension_semantics=sem, vmem_limit_bytes=VMEM_LIMIT)


def _split_dot(a, g_bf16):
    hi = a.astype(BF16)
    lo = (a - hi.astype(F32)).astype(BF16)
    return (jnp.dot(hi, g_bf16, preferred_element_type=F32)
            + jnp.dot(lo, g_bf16, preferred_element_type=F32))


def _head_mean(x, seg):
    width = x.shape[1]
    cols = []
    for lo in range(0, width, MXU_WIDTH):
        n = min(MXU_WIDTH, width - lo)
        cols.append(jnp.dot(x[:, lo:lo + n].astype(BF16), seg[0:n, 0:n], preferred_element_type=F32))
    return cols[0] if len(cols) == 1 else jnp.concatenate(cols, axis=1)


def _sigmoid(z):
    return 1.0 / (1.0 + jnp.exp(-z))


def _silu(z):
    return z * _sigmoid(z)


def _mod_kernel(c_ref, w_ref, b_ref, o_ref):
    o_ref[...] = jnp.dot(_silu(c_ref[...]), w_ref[...], precision=HIGHEST,
                         preferred_element_type=F32) + b_ref[...]


def _modulation(cc, w_mod, b_mod):
    depth, d, d3 = w_mod.shape
    nj = d3 // d
    return pl.pallas_call(
        _mod_kernel,
        grid=(depth, nj),
        in_specs=[pl.BlockSpec((8, d), lambda l, j: (0, 0)),
                  pl.BlockSpec((None, d, d), lambda l, j: (l, 0, j)),
                  pl.BlockSpec((None, 1, d), lambda l, j: (l, 0, j))],
        out_specs=pl.BlockSpec((None, 8, d), lambda l, j: (l, 0, j)),
        out_shape=jax.ShapeDtypeStruct((depth, 8, d3), F32),
        compiler_params=_cparams(("arbitrary", "arbitrary")),
        name="modulation",
    )(cc, w_mod, b_mod.reshape(depth, 1, d3))


def _stream_rows(ctx_ref, x_ref, ctx_tiles):
    if ctx_ref is None:
        return x_ref[...]
    return jnp.where(pl.program_id(1) < ctx_tiles, ctx_ref[...], x_ref[...])


def _stream_specs(two_sources, tm, d, ctx_tiles):
    if not two_sources:
        return [pl.BlockSpec((None, tm, d), lambda bi, i: (bi, i, 0))]
    return [pl.BlockSpec((None, tm, d), lambda bi, i: (bi, jnp.minimum(i, ctx_tiles - 1), 0)),
            pl.BlockSpec((None, tm, d), lambda bi, i: (bi, jnp.maximum(i - ctx_tiles, 0), 0))]


def _inproj_kernel(*refs, two_sources, ctx_tiles, q_width, qk_width, v_width, splits):
    n_src = 2 if two_sources else 1
    src = refs[:n_src]
    mod_ref, gpre_ref, w_ref, cos_ref, sin_ref, gain_ref, seg_ref = refs[n_src:n_src + 7]
    q_ref, k_ref, vt_ref = refs[n_src + 7:n_src + 10]
    extra_refs = refs[n_src + 10:]
    x = _stream_rows(src[0] if two_sources else None, src[-1], ctx_tiles)
    ms = jnp.mean(x * x, axis=-1, keepdims=True)
    h = x * lax.rsqrt(ms + RMS_EPS) * gpre_ref[...]
    h = h * (1.0 + mod_ref[1:2, :]) + mod_ref[0:1, :]
    acc = jnp.dot(h.astype(BF16), w_ref[...], preferred_element_type=F32)

    cos = cos_ref[...]
    sin = sin_ref[...]
    lane = lax.broadcasted_iota(jnp.int32, (1, LANES), 1)
    even_lane = (lane % 2) == 0
    qk = acc[:, 0:qk_width]
    msq_all = _head_mean(qk * qk, seg_ref[...])
    for g in range(qk_width // LANES):
        xg = acc[:, g * LANES:(g + 1) * LANES]
        msq = msq_all[:, g * LANES:(g + 1) * LANES]
        xn = xg * lax.rsqrt(msq + RMS_EPS) * gain_ref[:, g * LANES:(g + 1) * LANES]
        partner = jnp.where(even_lane, pltpu.roll(xn, LANES - 1, 1), pltpu.roll(xn, 1, 1))
        y = xn * cos + partner * sin
        is_q = g * LANES < q_width
        if is_q:
            y = y * Q_SCALE
        ref = q_ref if is_q else k_ref
        h0 = 2 * g if is_q else 2 * (g - q_width // LANES)
        ref[h0] = y[:, 0:HEAD_DIM].astype(ref.dtype)
        ref[h0 + 1] = y[:, HEAD_DIM:LANES].astype(ref.dtype)
    for g in range(v_width // LANES):
        vt = acc[:, qk_width + g * LANES:qk_width + (g + 1) * LANES].T
        vt_ref[2 * g] = vt[0:HEAD_DIM].astype(vt_ref.dtype)
        vt_ref[2 * g + 1] = vt[HEAD_DIM:LANES].astype(vt_ref.dtype)
    for ref, (lo, hi) in zip(extra_refs, splits):
        ref[...] = acc[:, lo:hi].astype(ref.dtype)


def _inproj(sources, modsel, gpre, w_bf16, cos_t, sin_t, gain, seg, *, q_width, qk_width, v_width,
            splits, out_dtypes, ctx_tiles):
    two_sources = len(sources) == 2
    b, d = sources[0].shape[0], sources[0].shape[2]
    s = sum(a.shape[1] for a in sources)
    n = w_bf16.shape[1]
    tm = ROW_TILE
    hq = q_width // HEAD_DIM
    hk = (qk_width - q_width) // HEAD_DIM
    hv = v_width // HEAD_DIM
    row = lambda bi, i: (bi, i, 0)
    const = lambda bi, i: (0, 0)
    out_shapes = [jax.ShapeDtypeStruct((b, hq, s, HEAD_DIM), BF16),
                  jax.ShapeDtypeStruct((b, hk, s, HEAD_DIM), BF16),
                  jax.ShapeDtypeStruct((b, hv, HEAD_DIM, s), BF16)]
    out_specs = [pl.BlockSpec((None, hq, tm, HEAD_DIM), lambda bi, i: (bi, 0, i, 0)),
                 pl.BlockSpec((None, hk, tm, HEAD_DIM), lambda bi, i: (bi, 0, i, 0)),
                 pl.BlockSpec((None, hv, HEAD_DIM, tm), lambda bi, i: (bi, 0, 0, i))]
    for (lo, hi), dt in zip(splits, out_dtypes):
        out_shapes.append(jax.ShapeDtypeStruct((b, s, hi - lo), dt))
        out_specs.append(pl.BlockSpec((None, tm, hi - lo), row))
    kern = functools.partial(_inproj_kernel, two_sources=two_sources, ctx_tiles=ctx_tiles,
                             q_width=q_width, qk_width=qk_width, v_width=v_width, splits=splits)
    return pl.pallas_call(
        kern,
        grid=(b, s // tm),
        in_specs=_stream_specs(two_sources, tm, d, ctx_tiles) + [
            pl.BlockSpec((None, None, 3, d),
                         lambda bi, i: (bi, jnp.minimum(i // ctx_tiles, 1), 0, 0)),
            pl.BlockSpec((1, d), const),
            pl.BlockSpec((d, n), const),
            pl.BlockSpec((tm, LANES), lambda bi, i: (i, 0)),
            pl.BlockSpec((tm, LANES), lambda bi, i: (i, 0)),
            pl.BlockSpec((1, qk_width), const),
            pl.BlockSpec((MXU_WIDTH, MXU_WIDTH), const)],
        out_specs=out_specs,
        out_shape=out_shapes,
        compiler_params=_cparams(("parallel", "parallel")),
        name="inproj",
    )(*sources, modsel, gpre, w_bf16, cos_t, sin_t, gain, seg)


def _dense_attn_kernel(q_ref, k_ref, vt_ref, o_ref, *, tk, ctx_len):
    i = pl.program_id(2)
    r, tq, dh = q_ref.shape
    m_rows = r * tq
    n_keys = k_ref.shape[0]
    q = q_ref[...].reshape(m_rows, dh)
    contract_lanes = (((1,), (1,)), ((), ()))

    def scores(lo, hi):
        return lax.dot_general(k_ref[lo:hi, :], q, contract_lanes, preferred_element_type=F32)

    def attend(bounds):
        m = jnp.full((1, m_rows), -jnp.inf, F32)
        l = jnp.zeros((1, m_rows), F32)
        acc = jnp.zeros((dh, m_rows), F32)
        st = scores(*bounds[0])
        pending = None
        for j in range(len(bounds)):
            if pending is not None:
                (plo, phi), p_prev = pending
                acc = acc + jnp.dot(vt_ref[:, plo:phi], p_prev, preferred_element_type=F32)
            st_next = scores(*bounds[j + 1]) if j + 1 < len(bounds) else None
            m_new = jnp.maximum(m, jnp.max(st, axis=0, keepdims=True))
            alpha = jnp.exp2(m - m_new)
            p = jnp.exp2(st - m_new)
            l = alpha * l + jnp.sum(p, axis=0, keepdims=True)
            acc = acc * alpha
            pending = (bounds[j], p.astype(BF16))
            m, st = m_new, st_next
        (plo, phi), p_prev = pending
        acc = acc + jnp.dot(vt_ref[:, plo:phi], p_prev, preferred_element_type=F32)
        o = acc / l
        for hh in range(r):
            o_ref[hh] = o[:, hh * tq:(hh + 1) * tq].astype(o_ref.dtype)

    ctx_bounds = [(0, ctx_len)]
    all_bounds = ctx_bounds + [(lo, lo + tk) for lo in range(ctx_len, n_keys, tk)]

    @pl.when(i < ctx_len // tq)
    def _():
        attend(ctx_bounds)

    @pl.when(i >= ctx_len // tq)
    def _():
        attend(all_bounds)


def _dense_attention(qh, kh, vth, *, ctx_len):
    b, hq, s, dh = qh.shape
    g = kh.shape[1]
    r = hq // g
    tq = DENSE_Q_TILE
    assert (s - ctx_len) % DENSE_KV_TILE == 0 and ctx_len % tq == 0
    kern = functools.partial(_dense_attn_kernel, tk=DENSE_KV_TILE, ctx_len=ctx_len)
    return pl.pallas_call(
        kern,
        grid=(b, g, s // tq),
        in_specs=[pl.BlockSpec((None, r, tq, dh), lambda bi, gi, i: (bi, gi, i, 0)),
                  pl.BlockSpec((None, None, s, dh), lambda bi, gi, i: (bi, gi, 0, 0)),
                  pl.BlockSpec((None, None, dh, s), lambda bi, gi, i: (bi, gi, 0, 0))],
        out_specs=pl.BlockSpec((None, r, dh, tq), lambda bi, gi, i: (bi, gi, 0, i)),
        out_shape=jax.ShapeDtypeStruct((b, hq, dh, s), BF16),
        compiler_params=_cparams(("parallel", "parallel", "parallel")),
        name="dense_attention",
    )(qh, kh, vth)


def _window_attn_kernel(sink_ref, q_ref, bias_first_ref, bias_mid_ref, bias_last_ref,
                        kc_ref, kp_ref, km_ref, kn_ref, vc_ref, vp_ref, vm_ref, vn_ref, o_ref):
    gi = pl.program_id(1)
    _, r, rows, dh = q_ref.shape
    tq = Q_BLOCK
    nb = rows // tq
    contract_lanes = (((1,), (1,)), ((), ()))
    k_all = jnp.concatenate([kp_ref[...], km_ref[0, 0], kn_ref[...]], axis=0)
    vt_all = jnp.concatenate([vp_ref[...], vm_ref[0, 0], vn_ref[...]], axis=1)
    k_ctx, vt_ctx = kc_ref[...], vc_ref[...]
    lane = lax.broadcasted_iota(jnp.int32, (1, r * tq), 1)
    sink = jnp.zeros((1, r * tq), F32)
    for hh in range(r):
        sink = jnp.where(lane // tq == hh, sink_ref[gi * r + hh] * LOG2_E, sink)

    def scores(qb):
        q = jnp.concatenate([q_ref[0, hh, qb * tq:(qb + 1) * tq, :] for hh in range(r)], axis=0)
        bias_ref = bias_first_ref if qb == 0 else (bias_last_ref if qb == nb - 1 else bias_mid_ref)
        s_loc = lax.dot_general(k_all[qb * tq:(qb + 3) * tq], q, contract_lanes,
                                preferred_element_type=F32) + bias_ref[...]
        s_ctx = lax.dot_general(k_ctx, q, contract_lanes, preferred_element_type=F32)
        return s_loc, s_ctx

    def finish(qb, p_loc, p_ctx, l):
        o = (jnp.dot(vt_all[:, qb * tq:(qb + 3) * tq], p_loc, preferred_element_type=F32)
             + jnp.dot(vt_ctx, p_ctx, preferred_element_type=F32)) / l
        for hh in range(r):
            o_ref[hh, :, qb * tq:(qb + 1) * tq] = o[:, hh * tq:(hh + 1) * tq].astype(o_ref.dtype)

    s_cur = scores(0)
    pending = None
    for qb in range(nb):
        if pending is not None:
            finish(*pending)
        s_next = scores(qb + 1) if qb + 1 < nb else None
        s_loc, s_ctx = s_cur
        m = jnp.maximum(jnp.maximum(jnp.max(s_loc, axis=0, keepdims=True),
                                    jnp.max(s_ctx, axis=0, keepdims=True)), sink)
        p_loc = jnp.exp2(s_loc - m)
        p_ctx = jnp.exp2(s_ctx - m)
        l = (jnp.sum(p_loc, axis=0, keepdims=True) + jnp.sum(p_ctx, axis=0, keepdims=True)
             + jnp.exp2(sink - m))
        pending = (qb, p_loc.astype(BF16), p_ctx.astype(BF16), l)
        s_cur = s_next
    finish(*pending)


def _window_bias(tq, r):
    key = np.arange(3 * tq)[:, None] - tq
    qpos = np.arange(tq)[None, :]
    band = np.abs(key - qpos) <= WINDOW
    variants = [band & (key >= 0), band, band & (key < tq)]
    table = np.stack([np.where(np.tile(v, (1, r)), 0.0, -np.inf) for v in variants])
    return jnp.asarray(table, F32)


def _window_attention(qh, kh, vth, sink, *, ctx_len):
    b, hq, s, dh = qh.shape
    g = kh.shape[1]
    r = hq // g
    tq = Q_BLOCK
    t = s - ctx_len
    nb = WINDOW_BLOCKS
    rows = nb * tq
    assert t % rows == 0 and ctx_len % tq == 0
    ns = t // rows
    off = ctx_len // tq
    last = t // tq - 1
    bias = _window_bias(tq, r)
    ctx = lambda bi, gi, i: (bi, gi, 0, 0)
    blk = lambda n_rows: (None, None, n_rows, dh)
    tblk = lambda n_cols: (None, None, dh, n_cols)
    prev_blk = lambda i: jnp.maximum(nb * i - 1, 0) + off
    next_blk = lambda i: jnp.minimum(nb * i + nb, last) + off
    el = pl.Element
    mid = lambda i: pl.multiple_of(ctx_len + i * rows, tq)
    return pl.pallas_call(
        _window_attn_kernel,
        grid=(b, g, ns),
        in_specs=[pl.BlockSpec(memory_space=pltpu.SMEM),
                  pl.BlockSpec((el(1), el(r), el(rows), el(dh)), lambda bi, gi, i: (bi, gi * r, mid(i), 0)),
                  pl.BlockSpec((None, 3 * tq, r * tq), lambda bi, gi, i: (jnp.minimum(i, 1), 0, 0)),
                  pl.BlockSpec((None, 3 * tq, r * tq), lambda bi, gi, i: (1, 0, 0)),
                  pl.BlockSpec((None, 3 * tq, r * tq),
                               lambda bi, gi, i: (jnp.where(i == ns - 1, 2, 1), 0, 0)),
                  pl.BlockSpec(blk(ctx_len), ctx),
                  pl.BlockSpec(blk(tq), lambda bi, gi, i: (bi, gi, prev_blk(i), 0)),
                  pl.BlockSpec((el(1), el(1), el(rows), el(dh)), lambda bi, gi, i: (bi, gi, mid(i), 0)),
                  pl.BlockSpec(blk(tq), lambda bi, gi, i: (bi, gi, next_blk(i), 0)),
                  pl.BlockSpec(tblk(ctx_len), ctx),
                  pl.BlockSpec(tblk(tq), lambda bi, gi, i: (bi, gi, 0, prev_blk(i))),
                  pl.BlockSpec((el(1), el(1), el(dh), el(rows)), lambda bi, gi, i: (bi, gi, 0, mid(i))),
                  pl.BlockSpec(tblk(tq), lambda bi, gi, i: (bi, gi, 0, next_blk(i)))],
        out_specs=pl.BlockSpec((None, r, dh, rows), lambda bi, gi, i: (bi, gi, 0, i)),
        out_shape=jax.ShapeDtypeStruct((b, hq, dh, t), BF16),
        compiler_params=_cparams(("parallel", "parallel", "parallel")),
        name="window_attention",
    )(sink, qh, bias, bias, bias, kh, kh, kh, kh, vth, vth, vth, vth)


def _rwkv_prep_kernel(p_ref, hp_ref, hn_ref, mu_ref, w0_ref, w2_ref, a0_ref, a2_ref,
                      kk_ref, ka_ref, rk_ref, ones_ref,
                      m_ref, n_ref, q_ref, y0_ref, bonus_ref,
                      xf_s, kh_s, bh_s, kbar_s, bbar_s, rt_s, v_s, wend_s, *, ctx_chunks, n_chunks):
    step_id = pl.program_id(1)
    j = jnp.minimum(step_id, n_chunks - 1)
    c = CHUNK
    bw = kk_ref.shape[1]
    n_pairs = bw // LANES
    staged = (xf_s, kh_s, bh_s, kbar_s, bbar_s, rt_s, v_s, wend_s)

    @pl.when(step_id == 0)
    def _():
        for ref in staged:
            ref[...] = jnp.zeros_like(ref)

    gw = RWKV_GROUP_LANES
    hpg = gw // HEAD_DIM
    n_groups = bw // gw
    ti = lax.broadcasted_iota(jnp.int32, (c, c), 0)
    tj = lax.broadcasted_iota(jnp.int32, (c, c), 1)
    ti_g = lax.broadcasted_iota(jnp.int32, (c, gw), 0)
    tj_g = lax.broadcasted_iota(jnp.int32, (c, gw), 1) % c
    eye_g = (ti_g == tj_g).astype(F32)
    gi_r = lax.broadcasted_iota(jnp.int32, (gw, gw), 0)
    gi_c = lax.broadcasted_iota(jnp.int32, (gw, gw), 1)
    group_diag = (gi_r // HEAD_DIM) == (gi_c // HEAD_DIM)
    bi = lax.broadcasted_iota(jnp.int32, (LANES, LANES), 0)
    bj = lax.broadcasted_iota(jnp.int32, (LANES, LANES), 1)
    same_head = (bi // HEAD_DIM) == (bj // HEAD_DIM)
    diag = bi == bj
    scan_masks = [(tj_g < ti_g, tj_g <= ti_g), (tj_g > ti_g, tj_g >= ti_g)]

    def blockdiag(xs):
        xb = xs.astype(BF16)
        return jnp.where(group_diag, jnp.concatenate([xb] * hpg, axis=0), jnp.zeros((), BF16))


    def elementwise_stage():
        p = p_ref[...]
        has_prev = jnp.logical_and(j != 0, j != ctx_chunks)
        has_next = jnp.logical_and(j != ctx_chunks - 1, j != n_chunks - 1)
        prev_row = jnp.where(has_prev, hp_ref[7:8, :], 0.0)
        next_row = jnp.where(has_next, hn_ref[0:1, :], 0.0)
        rowi = lax.broadcasted_iota(jnp.int32, (c, 1), 0)
        p_prev = jnp.where(rowi == 0, prev_row, pltpu.roll(p, 1, 0))
        p_next = jnp.where(rowi == c - 1, next_row, pltpu.roll(p, c - 1, 0))
        ps = p + mu_ref[...] * (0.5 * (p_prev + p_next) - p)
        r = ps[:, 0:bw]
        k = ps[:, bw:2 * bw]
        v = ps[:, 2 * bw:3 * bw]
        zw = ps[:, 3 * bw:3 * bw + 2 * B_LORA]
        za = ps[:, 3 * bw + 2 * B_LORA:3 * bw + 4 * B_LORA]
        yield

        wl = w0_ref[...] + jnp.dot(jnp.tanh(zw).astype(BF16), w2_ref[...],
                                   preferred_element_type=F32)
        z = -wl
        softplus = jnp.maximum(z, 0.0) + jnp.log(1.0 + jnp.exp(-jnp.abs(z)))
        logw = -jnp.exp(-softplus - 0.5)
        yield
        a = _sigmoid(a0_ref[...] + jnp.dot(za.astype(BF16), a2_ref[...],
                                           preferred_element_type=F32))
        ones_seg = ones_ref[...]
        kf = k * kk_ref[...]
        kf2 = kf * kf
        ss = jnp.concatenate([_split_dot(kf2[:, g * LANES:(g + 1) * LANES], ones_seg)
                              for g in range(n_pairs)], axis=1)
        kk = kf / jnp.maximum(jnp.sqrt(ss), 1e-12)
        yield

        bonus = jnp.zeros((c, bw), F32)
        out = dict(xf=[], kh=[], bh=[], kbar=[], bbar=[], rt=[], wend=[])
        for d in range(2):
            lw = logw[:, d * bw:(d + 1) * bw]
            a_d = a[:, d * bw:(d + 1) * bw]
            b_d = kk * a_d
            kmod = k * (1.0 + (a_d - 1.0) * ka_ref[...])
            rkk = r * kmod * rk_ref[...]
            bsum = jnp.concatenate([_split_dot(rkk[:, g * LANES:(g + 1) * LANES], ones_seg)
                                    for g in range(n_pairs)], axis=1)
            bonus = bonus + bsum * v
            yield

            tri = ((tj <= ti) if d == 0 else (tj >= ti)).astype(BF16)
            end_row = c - 1 if d == 0 else 0
            lw_hi = lw.astype(BF16)
            lw_lo = (lw - lw_hi.astype(F32)).astype(BF16)
            cum = (jnp.dot(tri, lw_hi, preferred_element_type=F32)
                   + jnp.dot(tri, lw_lo, preferred_element_type=F32))
            cum_end = cum[end_row:end_row + 1, :]
            r_t = r * jnp.exp(cum)
            kk_t = kk * jnp.exp(cum - lw)
            yield
            e_out = jnp.exp(-cum)
            tail = jnp.exp(cum_end - cum)
            out["xf"].append(jnp.concatenate([kk_t, r_t], axis=0).astype(BF16))
            out["rt"].append(r_t)
            out["kh"].append((kmod * e_out).astype(BF16))
            out["bh"].append((b_d * e_out).astype(BF16))
            out["kbar"].append((kmod * tail).astype(BF16))
            out["bbar"].append((b_d * tail).astype(BF16))
            out["wend"].append(jnp.broadcast_to(jnp.exp(cum_end), (8, bw)))
            yield
        bonus_ref[...] = bonus
        out["v"] = v.astype(BF16)
        staged_next.update(out)

    def matmul_stage():
        contract_lanes = (((1,), (1,)), ((), ()))
        contract_rows = (((0,), (0,)), ((), ()))
        probs = [(d, g, slice(g * gw, (g + 1) * gw)) for d in range(2) for g in range(n_groups)]
        v_bd = [blockdiag(v_s[:, g * gw:(g + 1) * gw]) for g in range(n_groups)]
        aks = [lax.dot_general(xf_s[d, :, sl], blockdiag(kh_s[d, :, sl]), contract_lanes,
                               preferred_element_type=F32) for d, g, sl in probs]
        yield
        abs_ = [lax.dot_general(xf_s[d, :, sl], blockdiag(bh_s[d, :, sl]), contract_lanes,
                                preferred_element_type=F32) for d, g, sl in probs]
        yield
        a_ks = [jnp.concatenate([jnp.where(scan_masks[d][0], ak[0:c], 0.0),
                                 jnp.where(scan_masks[d][1], ak[c:2 * c], 0.0)], axis=0).astype(BF16)
                for (d, g, sl), ak in zip(probs, aks)]
        a_qbs = [jnp.where(scan_masks[d][1], ab[c:2 * c], 0.0).astype(BF16)
                 for (d, g, sl), ab in zip(probs, abs_)]
        pws = [jnp.where(scan_masks[d][0], -ab[0:c], 0.0) for (d, g, sl), ab in zip(probs, abs_)]
        tinvs = [eye_g + pw for pw in pws]
        pws = [jnp.dot(pw.astype(BF16), blockdiag(pw), preferred_element_type=F32) for pw in pws]
        yield
        for _ in range(4):
            stacked = [jnp.dot(jnp.concatenate([t, pw], axis=0).astype(BF16), blockdiag(pw),
                               preferred_element_type=F32) for t, pw in zip(tinvs, pws)]
            tinvs = [t + st[0:c] for t, st in zip(tinvs, stacked)]
            pws = [st[c:2 * c] for st in stacked]
            yield
        tinvs = [(t + jnp.dot(t.astype(BF16), blockdiag(pw), preferred_element_type=F32)).astype(BF16)
                 for t, pw in zip(tinvs, pws)]
        avs = [jnp.dot(a_k, v_bd[g], preferred_element_type=F32) for (d, g, sl), a_k in zip(probs, a_ks)]
        yield
        p1s = [jnp.dot(t, blockdiag(xf_s[d, 0:c, sl]), preferred_element_type=F32)
               for (d, g, sl), t in zip(probs, tinvs)]
        p2s = [jnp.dot(t, blockdiag(av[0:c]), preferred_element_type=F32) for t, av in zip(tinvs, avs)]
        yield
        qp1s = [jnp.dot(a_qb, blockdiag(p1), preferred_element_type=F32) for a_qb, p1 in zip(a_qbs, p1s)]
        qp2s = [jnp.dot(a_qb, blockdiag(p2), preferred_element_type=F32) for a_qb, p2 in zip(a_qbs, p2s)]
        for (d, g, sl), av, qp1, qp2 in zip(probs, avs, qp1s, qp2s):
            q_ref[d, :, sl] = (rt_s[d, :, sl] - qp1).astype(q_ref.dtype)
            y0_ref[d, :, sl] = av[c:2 * c] - qp2
        yield
        for d in range(2):
            for pr in range(n_pairs):
                sl = slice(pr * LANES, (pr + 1) * LANES)
                gidx = d * n_groups + pr * LANES // gw
                lo = pr * LANES % gw
                p12 = jnp.concatenate([p1s[gidx][:, lo:lo + LANES], p2s[gidx][:, lo:lo + LANES]],
                                      axis=1).astype(BF16)
                bp = lax.dot_general(bbar_s[d, :, sl], p12, contract_rows, preferred_element_type=F32)
                kv = lax.dot_general(kbar_s[d, :, sl], v_s[:, sl], contract_rows, preferred_element_type=F32)
                w_diag = jnp.where(diag, wend_s[d, 0:1, sl], 0.0)
                m_ref[d, pr] = (w_diag - jnp.where(same_head, bp[:, 0:LANES], 0.0)).astype(m_ref.dtype)
                n_ref[d, pr] = jnp.where(same_head, kv - bp[:, LANES:2 * LANES], 0.0)
            yield

    staged_next = {}
    stages = [matmul_stage(), elementwise_stage()]
    while stages:
        for stage in list(stages):
            if next(stage, "done") == "done":
                stages.remove(stage)
    for ref, key in ((xf_s, "xf"), (kh_s, "kh"), (bh_s, "bh"), (kbar_s, "kbar"), (bbar_s, "bbar"),
                     (rt_s, "rt"), (wend_s, "wend")):
        for d in range(2):
            ref[d] = staged_next[key][d]
    v_s[...] = staged_next["v"]


def _rwkv_prep(bp, mu, w0, w2cat, a0, a2cat, kkw, kaw, rkw, ones_seg, *, ctx_len):
    b, s, pw = bp.shape
    bw = kkw.shape[1]
    n_pairs = bw // LANES
    nc = s // CHUNK
    ctx_chunks = ctx_len // CHUNK
    rb = CHUNK // 8
    kern = functools.partial(_rwkv_prep_kernel, ctx_chunks=ctx_chunks, n_chunks=nc)
    const = lambda bi, j: (0, 0)
    cur = lambda j: jnp.minimum(j, nc - 1)
    done = lambda j: jnp.maximum(j - 1, 0)
    return pl.pallas_call(
        kern,
        grid=(b, nc + 1),
        in_specs=[pl.BlockSpec((None, CHUNK, pw), lambda bi, j: (bi, cur(j), 0)),
                  pl.BlockSpec((None, 8, pw), lambda bi, j: (bi, jnp.maximum(cur(j) * rb - 1, 0), 0)),
                  pl.BlockSpec((None, 8, pw),
                               lambda bi, j: (bi, jnp.minimum((cur(j) + 1) * rb, s // 8 - 1), 0)),
                  pl.BlockSpec((1, pw), const),
                  pl.BlockSpec((1, 2 * bw), const),
                  pl.BlockSpec((2 * B_LORA, 2 * bw), const),
                  pl.BlockSpec((1, 2 * bw), const),
                  pl.BlockSpec((2 * B_LORA, 2 * bw), const),
                  pl.BlockSpec((1, bw), const),
                  pl.BlockSpec((1, bw), const),
                  pl.BlockSpec((1, bw), const),
                  pl.BlockSpec((LANES, LANES), const)],
        out_specs=[pl.BlockSpec((None, None, 2, n_pairs, LANES, LANES), lambda bi, j: (bi, done(j), 0, 0, 0, 0)),
                   pl.BlockSpec((None, None, 2, n_pairs, LANES, LANES), lambda bi, j: (bi, done(j), 0, 0, 0, 0)),
                   pl.BlockSpec((None, 2, CHUNK, bw), lambda bi, j: (bi, 0, done(j), 0)),
                   pl.BlockSpec((None, 2, CHUNK, bw), lambda bi, j: (bi, 0, done(j), 0)),
                   pl.BlockSpec((None, CHUNK, bw), lambda bi, j: (bi, cur(j), 0))],
        out_shape=[jax.ShapeDtypeStruct((b, nc, 2, n_pairs, LANES, LANES), BF16),
                   jax.ShapeDtypeStruct((b, nc, 2, n_pairs, LANES, LANES), F32),
                   jax.ShapeDtypeStruct((b, 2, s, bw), BF16),
                   jax.ShapeDtypeStruct((b, 2, s, bw), F32),
                   jax.ShapeDtypeStruct((b, s, bw), F32)],
        scratch_shapes=[pltpu.VMEM((2, 2 * CHUNK, bw), BF16),
                        pltpu.VMEM((2, CHUNK, bw), BF16),
                        pltpu.VMEM((2, CHUNK, bw), BF16),
                        pltpu.VMEM((2, CHUNK, bw), BF16),
                        pltpu.VMEM((2, CHUNK, bw), BF16),
                        pltpu.VMEM((2, CHUNK, bw), F32),
                        pltpu.VMEM((CHUNK, bw), BF16),
                        pltpu.VMEM((2, 8, bw), F32)],
        compiler_params=_cparams(("parallel", "arbitrary")),
        name="rwkv_prep",
    )(bp, bp, bp, mu, w0, w2cat, a0, a2cat, kkw, kaw, rkw, ones_seg)


def _rwkv_scan_kernel(m0_ref, m1_ref, n0_ref, n1_ref, q0_ref, q1_ref, y00_ref, y01_ref,
                      o0_ref, o1_ref, h_ref):
    j = pl.program_id(0)

    @pl.when(j == 0)
    def _():
        h_ref[...] = jnp.zeros_like(h_ref)

    nb, n_pairs = m0_ref.shape[0], m0_ref.shape[1]
    dirs = ((m0_ref, n0_ref, q0_ref, y00_ref, o0_ref), (m1_ref, n1_ref, q1_ref, y01_ref, o1_ref))
    for d, (m_ref, n_ref, q_ref, y0_ref, o_ref) in enumerate(dirs):
        for bi in range(nb):
            for pr in range(n_pairs):
                sl = slice(pr * LANES, (pr + 1) * LANES)
                h = h_ref[d, bi, pr]
                hb = h.astype(BF16)
                o_ref[bi, :, sl] = y0_ref[bi, :, sl] + jnp.dot(
                    q_ref[bi, :, sl], hb, preferred_element_type=F32)
                h_ref[d, bi, pr] = n_ref[bi, pr] + jnp.dot(
                    m_ref[bi, pr], hb, preferred_element_type=F32)


def _rwkv_scan(m, n, q, y0, *, ctx_len):
    b, nc, _, n_pairs, _, _ = m.shape
    s, bw = q.shape[2], q.shape[3]
    cc = ctx_len // CHUNK

    def mem_chunk(d, j):
        if d == 0:
            return j
        return jnp.where(j < cc, cc - 1 - j, nc - 1 + cc - j)

    def mn_spec(d):
        return pl.BlockSpec((b, None, None, n_pairs, LANES, LANES),
                            lambda j: (0, mem_chunk(d, j), d, 0, 0, 0))

    def row_spec(d):
        return pl.BlockSpec((b, None, CHUNK, bw), lambda j: (0, d, mem_chunk(d, j), 0))

    return pl.pallas_call(
        _rwkv_scan_kernel,
        grid=(nc,),
        in_specs=[mn_spec(0), mn_spec(1), mn_spec(0), mn_spec(1),
                  row_spec(0), row_spec(1), row_spec(0), row_spec(1)],
        out_specs=[pl.BlockSpec((b, CHUNK, bw), lambda j: (0, mem_chunk(0, j), 0)),
                   pl.BlockSpec((b, CHUNK, bw), lambda j: (0, mem_chunk(1, j), 0))],
        out_shape=[jax.ShapeDtypeStruct((b, s, bw), F32), jax.ShapeDtypeStruct((b, s, bw), F32)],
        scratch_shapes=[pltpu.VMEM((2, b, n_pairs, LANES, LANES), F32)],
        compiler_params=_cparams(("arbitrary",)),
        name="rwkv_scan",
    )(m, m, n, n, q, q, y0, y0)


def _outproj_kernel(*refs, two_sources, ctx_tiles, rwkv):
    n_src = 2 if two_sources else 1
    src = refs[:n_src]
    at_ref, g_ref, w_ref, mod_ref, gpost_ref = refs[n_src:n_src + 5]
    rest = refs[n_src + 5:]
    out_ref = rest[-1]
    hq, dh, tm = at_ref.shape
    parts = [at_ref[...].astype(F32).reshape(hq * dh, tm).T]
    if rwkv:
        yf_ref, yb_ref, bonus_ref, seg_ref, gnw_ref, gnb_ref = rest[:6]
        seg = seg_ref[...]
        y_sum = bonus_ref[...]
        for y_ref in (yf_ref, yb_ref):
            y = y_ref[...]
            yc = y - _head_mean(y, seg)
            var = _head_mean(yc * yc, seg)
            y_sum = y_sum + yc * lax.rsqrt(var + GN_EPS) * gnw_ref[...] + gnb_ref[...]
        parts.append(y_sum)
    o = jnp.concatenate(parts, axis=1) if len(parts) > 1 else parts[0]
    u = (o * _silu(g_ref[...].astype(F32))).astype(BF16)
    y = jnp.dot(u, w_ref[...], preferred_element_type=F32)
    ms = jnp.mean(y * y, axis=-1, keepdims=True)
    yn = y * lax.rsqrt(ms + RMS_EPS) * gpost_ref[...]
    x = _stream_rows(src[0] if two_sources else None, src[-1], ctx_tiles)
    out_ref[...] = x + mod_ref[2:3, :] * yn


def _outproj(sources, attn_t, gate, w_bf16, modsel, gpost, rwkv_parts, *, ctx_tiles, latent_only):
    two_sources = len(sources) == 2
    b, d = sources[0].shape[0], sources[0].shape[2]
    s = sum(a.shape[1] for a in sources)
    tm = ROW_TILE
    off = ctx_tiles if latent_only else 0
    n_tiles = s // tm - off
    assert not (two_sources and latent_only)
    row = lambda bi, i: (bi, i + off, 0)
    const = lambda bi, i: (0, 0)
    hq, dh = attn_t.shape[1], attn_t.shape[2]
    if two_sources:
        src_specs = _stream_specs(True, tm, d, ctx_tiles)
    else:
        src_specs = [pl.BlockSpec((None, tm, d), row)]
    in_specs = src_specs + [
        pl.BlockSpec((None, hq, dh, tm), lambda bi, i: (bi, 0, 0, i)),
        pl.BlockSpec((None, tm, gate.shape[2]), row),
        pl.BlockSpec(w_bf16.shape, const),
        pl.BlockSpec((None, None, 3, d),
                     lambda bi, i: (bi, jnp.minimum((i + off) // ctx_tiles, 1), 0, 0)),
        pl.BlockSpec((1, d), const)]
    args = [*sources, attn_t, gate, w_bf16, modsel, gpost]
    if rwkv_parts is not None:
        y_f, y_b, bonus, seg, gnw, gnb = rwkv_parts
        in_specs += [pl.BlockSpec((None, tm, y_f.shape[2]), row),
                     pl.BlockSpec((None, tm, y_b.shape[2]), row),
                     pl.BlockSpec((None, tm, bonus.shape[2]), row),
                     pl.BlockSpec((MXU_WIDTH, MXU_WIDTH), const),
                     pl.BlockSpec(gnw.shape, const),
                     pl.BlockSpec(gnb.shape, const)]
        args += [y_f, y_b, bonus, seg, gnw, gnb]
    kern = functools.partial(_outproj_kernel, two_sources=two_sources, ctx_tiles=ctx_tiles,
                             rwkv=rwkv_parts is not None)
    return pl.pallas_call(
        kern,
        grid=(b, n_tiles),
        in_specs=in_specs,
        out_specs=pl.BlockSpec((None, tm, d), lambda bi, i: (bi, i, 0)),
        out_shape=jax.ShapeDtypeStruct((b, n_tiles * tm, d), F32),
        compiler_params=_cparams(("parallel", "parallel")),
        name="outproj",
    )(*args)


def _rope_tables(n_latent, ctx_len):
    t = jnp.arange(n_latent)
    rowp = (t // GRID_W).astype(F32)
    colp = (t % GRID_W).astype(F32)
    axis_dim = HEAD_DIM // 2
    inv = ROPE_THETA ** (-jnp.arange(0, axis_dim, 2, dtype=F32) / axis_dim)
    ang = jnp.concatenate([rowp[:, None] * inv, colp[:, None] * inv], axis=-1)
    cos, sin = jnp.cos(ang), jnp.sin(ang)
    cos = jnp.concatenate([jnp.ones((ctx_len, axis_dim), F32), cos], axis=0)
    sin = jnp.concatenate([jnp.zeros((ctx_len, axis_dim), F32), sin], axis=0)
    cos_h = jnp.repeat(cos, 2, axis=1)
    sin_h = jnp.stack([-sin, sin], axis=-1).reshape(sin.shape[0], HEAD_DIM)
    return jnp.tile(cos_h, (1, LANES // HEAD_DIM)), jnp.tile(sin_h, (1, LANES // HEAD_DIM))


def _block_diag_lora(w2):
    r, w = w2.shape[1], w2.shape[2]
    z = jnp.zeros((r, w), w2.dtype)
    return jnp.concatenate([jnp.concatenate([w2[0], z], axis=1),
                            jnp.concatenate([z, w2[1]], axis=1)], axis=0)


def kernel(x, c, ctx, c_ctx, w_mod, b_mod, g_pre, g_post, w_in_even, w_out_even, qn_a, kn_a, mu_b, w0_b, w2_b, a0_b, a2_b, kk_b, ka_b, rk_b, gn_w_b, gn_b_b, w_in_odd, w_out_odd, qn_c, kn_c, sink_c):
    b, t, d = x.shape
    ctx_len = ctx.shape[1]
    s = ctx_len + t
    assert ctx_len % ROW_TILE == 0 and t % ROW_TILE == 0 and b + 1 <= 8
    ctx_tiles = ctx_len // ROW_TILE
    depth = w_mod.shape[0]

    cc = jnp.concatenate([c, c_ctx[None, :], jnp.zeros((8 - b - 1, d), F32)], axis=0)
    mod = _modulation(cc, w_mod, b_mod)
    mod = mod.reshape(depth, 8, 3, d)
    modsel = jnp.stack([jnp.broadcast_to(mod[:, b][:, None], (depth, b, 3, d)), mod[:, :b]], axis=2)

    cos_t, sin_t = _rope_tables(t, ctx_len)
    seg_mean = jnp.asarray(np.kron(np.eye(MXU_WIDTH // HEAD_DIM),
                                   np.full((HEAD_DIM, HEAD_DIM), 1.0 / HEAD_DIM)), BF16)
    seg_ones = jnp.asarray(np.kron(np.eye(2), np.ones((HEAD_DIM, HEAD_DIM))), BF16)

    bw = kk_b.shape[1]
    a_width = w_out_even.shape[1] - bw
    a_heads = a_width // HEAD_DIM
    n_in = w_in_even.shape[2]
    b_proj = 3 * bw + 4 * B_LORA
    kv_width = (n_in - 2 * a_width - b_proj - bw) // 2
    a_kv_heads = kv_width // HEAD_DIM
    qk_width = a_width + kv_width
    gain0 = jnp.concatenate([jnp.tile(qn_a[0], a_heads), jnp.tile(kn_a[0], a_kv_heads)])[None, :]
    bp_lo = qk_width + kv_width
    g_lo = bp_lo + b_proj
    qh, kh, vth, bproj, gate0 = _inproj(
        [ctx, x], modsel[0], g_pre[0][None, :], w_in_even[0].astype(BF16), cos_t, sin_t, gain0, seg_mean,
        q_width=a_width, qk_width=qk_width, v_width=kv_width,
        splits=((bp_lo, g_lo), (g_lo, n_in)), out_dtypes=(F32, BF16), ctx_tiles=ctx_tiles)
    oa_t = _dense_attention(qh, kh, vth, ctx_len=ctx_len)

    m_c, n_c, q_c, y0_c, bonus = _rwkv_prep(
        bproj, mu_b[0][None, :], w0_b[0].reshape(1, 2 * bw), _block_diag_lora(w2_b[0]).astype(BF16),
        a0_b[0].reshape(1, 2 * bw), _block_diag_lora(a2_b[0]).astype(BF16), kk_b[0][None, :], ka_b[0][None, :],
        rk_b[0].reshape(1, bw), seg_ones, ctx_len=ctx_len)
    y_f, y_b = _rwkv_scan(m_c, n_c, q_c, y0_c, ctx_len=ctx_len)

    xc = _outproj([ctx, x], oa_t, gate0, w_out_even[0].astype(BF16), modsel[0], g_post[0][None, :],
                  (y_f, y_b, bonus, seg_mean, gn_w_b[0][None, :], gn_b_b[0][None, :]),
                  ctx_tiles=ctx_tiles, latent_only=False)

    c_heads = sink_c.shape[1]
    c_width = c_heads * HEAD_DIM
    n_in1 = w_in_odd.shape[2]
    ckv_width = (n_in1 - 2 * c_width) // 2
    c_kv_heads = ckv_width // HEAD_DIM
    qk_width1 = c_width + ckv_width
    gain1 = jnp.concatenate([jnp.tile(qn_c[0], c_heads), jnp.tile(kn_c[0], c_kv_heads)])[None, :]
    qh1, kh1, vth1, gate1 = _inproj(
        [xc], modsel[1], g_pre[1][None, :], w_in_odd[0].astype(BF16), cos_t, sin_t, gain1, seg_mean,
        q_width=c_width, qk_width=qk_width1, v_width=ckv_width,
        splits=((qk_width1 + ckv_width, n_in1),), out_dtypes=(BF16,), ctx_tiles=ctx_tiles)
    ow_t = _window_attention(qh1, kh1, vth1, sink_c[0], ctx_len=ctx_len)
    return _outproj([xc], ow_t, gate1, w_out_odd[0].astype(BF16), modsel[1], g_post[1][None, :], None,
                    ctx_tiles=ctx_tiles, latent_only=True)
```

```python
import functools

import numpy as np
import jax
import jax.numpy as jnp
from jax import lax
from jax.experimental import pallas as pl
from jax.experimental.pallas import tpu as pltpu

F32 = jnp.float32
BF16 = jnp.bfloat16
HIGHEST = lax.Precision.HIGHEST

HEAD_DIM = 64
LANES = 128
MXU_WIDTH = 256
RWKV_GROUP_LANES = 128
GRID_W = 64
Q_BLOCK = 128
WINDOW = 128
ROPE_THETA = 10000.0
RMS_EPS = 1e-6
GN_EPS = 64e-5
LOG2_E = float(np.log2(np.e))
Q_SCALE = HEAD_DIM ** -0.5 * LOG2_E
B_LORA = 64
CHUNK = 64
ROW_TILE = 256
PROJ_COL_CHUNK = 512
DENSE_Q_TILE = 128
DENSE_KV_TILE = 1024
WINDOW_BLOCKS = 8
VMEM_LIMIT = 56 * 1024 * 1024


def _cparams(sem):
    return pltpu.CompilerParams(dimension_semantics=sem, vmem_limit_bytes=VMEM_LIMIT)


def _split_dot(a, g_bf16):
    hi = a.astype(BF16)
    lo = (a - hi.astype(F32)).astype(BF16)
    return (jnp.dot(hi, g_bf16, preferred_element_type=F32)
            + jnp.dot(lo, g_bf16, preferred_element_type=F32))


def _head_mean(x, seg):
    width = x.shape[1]
    cols = []
    for lo in range(0, width, MXU_WIDTH):
        n = min(MXU_WIDTH, width - lo)
        cols.append(jnp.dot(x[:, lo:lo + n].astype(BF16), seg[0:n, 0:n], preferred_element_type=F32))
    return cols[0] if len(cols) == 1 else jnp.concatenate(cols, axis=1)


def _sigmoid(z):
    return 1.0 / (1.0 + jnp.exp(-z))


def _silu(z):
    return z * _sigmoid(z)


def _mod_kernel(c_ref, w_ref, b_ref, o_ref):
    o_ref[...] = jnp.dot(_silu(c_ref[...]), w_ref[...], precision=HIGHEST,
                         preferred_element_type=F32) + b_ref[...]


def _modulation(cc, w_mod, b_mod):
    depth, d, d3 = w_mod.shape
    nj = d3 // d
    return pl.pallas_call(
        _mod_kernel,
        grid=(depth, nj),
        in_specs=[pl.BlockSpec((8, d), lambda l, j: (0, 0)),
                  pl.BlockSpec((None, d, d), lambda l, j: (l, 0, j)),
                  pl.BlockSpec((None, 1, d), lambda l, j: (l, 0, j))],
        out_specs=pl.BlockSpec((None, 8, d), lambda l, j: (l, 0, j)),
        out_shape=jax.ShapeDtypeStruct((depth, 8, d3), F32),
        compiler_params=_cparams(("arbitrary", "arbitrary")),
        name="modulation",
    )(cc, w_mod, b_mod.reshape(depth, 1, d3))


def _stream_rows(ctx_ref, x_ref, ctx_tiles, tile):
    if ctx_ref is None:
        return x_ref[...]
    return jnp.where(tile < ctx_tiles, ctx_ref[...], x_ref[...])


def _stream_specs(two_sources, tm, d, ctx_tiles, tile_of=lambda i: i):
    if not two_sources:
        return [pl.BlockSpec((None, tm, d), lambda bi, i: (bi, tile_of(i), 0))]
    return [pl.BlockSpec((None, tm, d), lambda bi, i: (bi, jnp.minimum(tile_of(i), ctx_tiles - 1), 0)),
            pl.BlockSpec((None, tm, d), lambda bi, i: (bi, jnp.maximum(tile_of(i) - ctx_tiles, 0), 0))]


def _interleave(*stages):
    stages = list(stages)
    while stages:
        for stage in list(stages):
            if next(stage, "done") == "done":
                stages.remove(stage)


def _inproj_kernel(*refs, two_sources, ctx_tiles, n_tiles, q_width, qk_width, v_width, splits):
    n_src = 2 if two_sources else 1
    src = refs[:n_src]
    mod_ref, gpre_ref, w_ref, cos_ref, sin_ref, gain_ref, seg_ref = refs[n_src:n_src + 7]
    q_ref, k_ref, vt_ref = refs[n_src + 7:n_src + 10]
    extra_refs = refs[n_src + 10:-1]
    attn_s = refs[-1]
    step = pl.program_id(1)
    tile = jnp.minimum(step, n_tiles - 1)
    n_attn = qk_width + v_width

    @pl.when(step == 0)
    def _():
        attn_s[...] = jnp.zeros_like(attn_s)

    staged = []

    def project():
        x = _stream_rows(src[0] if two_sources else None, src[-1], ctx_tiles, tile)
        ms = jnp.mean(x * x, axis=-1, keepdims=True)
        h = x * lax.rsqrt(ms + RMS_EPS) * gpre_ref[...]
        hb = (h * (1.0 + mod_ref[1:2, :]) + mod_ref[0:1, :]).astype(BF16)
        yield
        for lo in range(0, n_attn, PROJ_COL_CHUNK):
            hi = min(lo + PROJ_COL_CHUNK, n_attn)
            staged.append((lo, hi, jnp.dot(hb, w_ref[:, lo:hi], preferred_element_type=F32)))
            yield
        for ref, (s_lo, s_hi) in zip(extra_refs, splits):
            for lo in range(s_lo, s_hi, PROJ_COL_CHUNK):
                hi = min(lo + PROJ_COL_CHUNK, s_hi)
                ref[:, lo - s_lo:hi - s_lo] = jnp.dot(hb, w_ref[:, lo:hi],
                                                      preferred_element_type=F32).astype(ref.dtype)
                yield

    def finish_previous():
        cos = cos_ref[...]
        sin = sin_ref[...]
        lane = lax.broadcasted_iota(jnp.int32, (1, LANES), 1)
        even_lane = (lane % 2) == 0
        qk = attn_s[:, 0:qk_width]
        msq_all = _head_mean(qk * qk, seg_ref[...])
        yield
        for g in range(qk_width // LANES):
            xg = attn_s[:, g * LANES:(g + 1) * LANES]
            msq = msq_all[:, g * LANES:(g + 1) * LANES]
            xn = xg * lax.rsqrt(msq + RMS_EPS) * gain_ref[:, g * LANES:(g + 1) * LANES]
            partner = jnp.where(even_lane, pltpu.roll(xn, LANES - 1, 1), pltpu.roll(xn, 1, 1))
            y = xn * cos + partner * sin
            is_q = g * LANES < q_width
            if is_q:
                y = y * Q_SCALE
            ref = q_ref if is_q else k_ref
            h0 = 2 * g if is_q else 2 * (g - q_width // LANES)
            ref[h0] = y[:, 0:HEAD_DIM].astype(ref.dtype)
            ref[h0 + 1] = y[:, HEAD_DIM:LANES].astype(ref.dtype)
            yield
        for g in range(v_width // LANES):
            vt = attn_s[:, qk_width + g * LANES:qk_width + (g + 1) * LANES].T
            vt_ref[2 * g] = vt[0:HEAD_DIM].astype(vt_ref.dtype)
            vt_ref[2 * g + 1] = vt[HEAD_DIM:LANES].astype(vt_ref.dtype)
            yield

    _interleave(project(), finish_previous())
    for lo, hi, acc in staged:
        attn_s[:, lo:hi] = acc


def _inproj(sources, modsel, gpre, w_bf16, cos_t, sin_t, gain, seg, *, q_width, qk_width, v_width,
            splits, out_dtypes, ctx_tiles):
    two_sources = len(sources) == 2
    b, d = sources[0].shape[0], sources[0].shape[2]
    s = sum(a.shape[1] for a in sources)
    n = w_bf16.shape[1]
    tm = ROW_TILE
    hq = q_width // HEAD_DIM
    hk = (qk_width - q_width) // HEAD_DIM
    hv = v_width // HEAD_DIM
    n_tiles = s // tm
    cur = lambda i: jnp.minimum(i, n_tiles - 1)
    done = lambda i: jnp.maximum(i - 1, 0)
    const = lambda bi, i: (0, 0)
    out_shapes = [jax.ShapeDtypeStruct((b, hq, s, HEAD_DIM), BF16),
                  jax.ShapeDtypeStruct((b, hk, s, HEAD_DIM), BF16),
                  jax.ShapeDtypeStruct((b, hv, HEAD_DIM, s), BF16)]
    out_specs = [pl.BlockSpec((None, hq, tm, HEAD_DIM), lambda bi, i: (bi, 0, done(i), 0)),
                 pl.BlockSpec((None, hk, tm, HEAD_DIM), lambda bi, i: (bi, 0, done(i), 0)),
                 pl.BlockSpec((None, hv, HEAD_DIM, tm), lambda bi, i: (bi, 0, 0, done(i)))]
    for (lo, hi), dt in zip(splits, out_dtypes):
        out_shapes.append(jax.ShapeDtypeStruct((b, s, hi - lo), dt))
        out_specs.append(pl.BlockSpec((None, tm, hi - lo), lambda bi, i: (bi, cur(i), 0)))
    kern = functools.partial(_inproj_kernel, two_sources=two_sources, ctx_tiles=ctx_tiles, n_tiles=n_tiles,
                             q_width=q_width, qk_width=qk_width, v_width=v_width, splits=splits)
    return pl.pallas_call(
        kern,
        grid=(b, n_tiles + 1),
        in_specs=_stream_specs(two_sources, tm, d, ctx_tiles, cur) + [
            pl.BlockSpec((None, None, 3, d),
                         lambda bi, i: (bi, jnp.minimum(cur(i) // ctx_tiles, 1), 0, 0)),
            pl.BlockSpec((1, d), const),
            pl.BlockSpec((d, n), const),
            pl.BlockSpec((tm, LANES), lambda bi, i: (done(i), 0)),
            pl.BlockSpec((tm, LANES), lambda bi, i: (done(i), 0)),
            pl.BlockSpec((1, qk_width), const),
            pl.BlockSpec((MXU_WIDTH, MXU_WIDTH), const)],
        out_specs=out_specs,
        out_shape=out_shapes,
        scratch_shapes=[pltpu.VMEM((tm, qk_width + v_width), F32)],
        compiler_params=_cparams(("parallel", "arbitrary")),
        name="inproj",
    )(*sources, modsel, gpre, w_bf16, cos_t, sin_t, gain, seg)


def _dense_attn_kernel(q_ref, k_ref, vt_ref, o_ref, *, tk, ctx_len):
    i = pl.program_id(2)
    r, tq, dh = q_ref.shape
    m_rows = r * tq
    n_keys = k_ref.shape[0]
    q = q_ref[...].reshape(m_rows, dh)
    contract_lanes = (((1,), (1,)), ((), ()))

    def scores(lo, hi):
        return lax.dot_general(k_ref[lo:hi, :], q, contract_lanes, preferred_element_type=F32)

    def attend(bounds):
        m = jnp.full((1, m_rows), -jnp.inf, F32)
        l = jnp.zeros((1, m_rows), F32)
        acc = jnp.zeros((dh, m_rows), F32)
        st = scores(*bounds[0])
        pending = None
        for j in range(len(bounds)):
            if pending is not None:
                (plo, phi), p_prev = pending
                acc = acc + jnp.dot(vt_ref[:, plo:phi], p_prev, preferred_element_type=F32)
            st_next = scores(*bounds[j + 1]) if j + 1 < len(bounds) else None
            m_new = jnp.maximum(m, jnp.max(st, axis=0, keepdims=True))
            alpha = jnp.exp2(m - m_new)
            p = jnp.exp2(st - m_new)
            l = alpha * l + jnp.sum(p, axis=0, keepdims=True)
            acc = acc * alpha
            pending = (bounds[j], p.astype(BF16))
            m, st = m_new, st_next
        (plo, phi), p_prev = pending
        acc = acc + jnp.dot(vt_ref[:, plo:phi], p_prev, preferred_element_type=F32)
        o = acc / l
        for hh in range(r):
            o_ref[hh] = o[:, hh * tq:(hh + 1) * tq].astype(o_ref.dtype)

    ctx_bounds = [(0, ctx_len)]
    all_bounds = ctx_bounds + [(lo, lo + tk) for lo in range(ctx_len, n_keys, tk)]

    @pl.when(i < ctx_len // tq)
    def _():
        attend(ctx_bounds)

    @pl.when(i >= ctx_len // tq)
    def _():
        attend(all_bounds)


def _dense_attention(qh, kh, vth, *, ctx_len):
    b, hq, s, dh = qh.shape
    g = kh.shape[1]
    r = hq // g
    tq = DENSE_Q_TILE
    assert (s - ctx_len) % DENSE_KV_TILE == 0 and ctx_len % tq == 0
    kern = functools.partial(_dense_attn_kernel, tk=DENSE_KV_TILE, ctx_len=ctx_len)
    return pl.pallas_call(
        kern,
        grid=(b, g, s // tq),
        in_specs=[pl.BlockSpec((None, r, tq, dh), lambda bi, gi, i: (bi, gi, i, 0)),
                  pl.BlockSpec((None, None, s, dh), lambda bi, gi, i: (bi, gi, 0, 0)),
                  pl.BlockSpec((None, None, dh, s), lambda bi, gi, i: (bi, gi, 0, 0))],
        out_specs=pl.BlockSpec((None, r, dh, tq), lambda bi, gi, i: (bi, gi, 0, i)),
        out_shape=jax.ShapeDtypeStruct((b, hq, dh, s), F32),
        compiler_params=_cparams(("parallel", "parallel", "parallel")),
        name="dense_attention",
    )(qh, kh, vth)


def _window_attn_kernel(sink_ref, q_ref, bias_first_ref, bias_mid_ref, bias_last_ref,
                        kc_ref, kp_ref, km_ref, kn_ref, vc_ref, vp_ref, vm_ref, vn_ref, o_ref):
    gi = pl.program_id(1)
    _, r, rows, dh = q_ref.shape
    tq = Q_BLOCK
    nb = rows // tq
    contract_lanes = (((1,), (1,)), ((), ()))
    k_all = jnp.concatenate([kp_ref[...], km_ref[0, 0], kn_ref[...]], axis=0)
    vt_all = jnp.concatenate([vp_ref[...], vm_ref[0, 0], vn_ref[...]], axis=1)
    k_ctx, vt_ctx = kc_ref[...], vc_ref[...]
    lane = lax.broadcasted_iota(jnp.int32, (1, r * tq), 1)
    sink = jnp.zeros((1, r * tq), F32)
    for hh in range(r):
        sink = jnp.where(lane // tq == hh, sink_ref[gi * r + hh] * LOG2_E, sink)

    def scores(qb):
        q = jnp.concatenate([q_ref[0, hh, qb * tq:(qb + 1) * tq, :] for hh in range(r)], axis=0)
        bias_ref = bias_first_ref if qb == 0 else (bias_last_ref if qb == nb - 1 else bias_mid_ref)
        s_loc = lax.dot_general(k_all[qb * tq:(qb + 3) * tq], q, contract_lanes,
                                preferred_element_type=F32) + bias_ref[...]
        s_ctx = lax.dot_general(k_ctx, q, contract_lanes, preferred_element_type=F32)
        return s_loc, s_ctx

    def finish(qb, p_loc, p_ctx, l):
        o = (jnp.dot(vt_all[:, qb * tq:(qb + 3) * tq], p_loc, preferred_element_type=F32)
             + jnp.dot(vt_ctx, p_ctx, preferred_element_type=F32)) / l
        for hh in range(r):
            o_ref[hh, :, qb * tq:(qb + 1) * tq] = o[:, hh * tq:(hh + 1) * tq].astype(o_ref.dtype)

    s_cur = scores(0)
    pending = None
    for qb in range(nb):
        if pending is not None:
            finish(*pending)
        s_next = scores(qb + 1) if qb + 1 < nb else None
        s_loc, s_ctx = s_cur
        m = jnp.maximum(jnp.maximum(jnp.max(s_loc, axis=0, keepdims=True),
                                    jnp.max(s_ctx, axis=0, keepdims=True)), sink)
        p_loc = jnp.exp2(s_loc - m)
        p_ctx = jnp.exp2(s_ctx - m)
        l = (jnp.sum(p_loc, axis=0, keepdims=True) + jnp.sum(p_ctx, axis=0, keepdims=True)
             + jnp.exp2(sink - m))
        pending = (qb, p_loc.astype(BF16), p_ctx.astype(BF16), l)
        s_cur = s_next
    finish(*pending)


def _window_bias(tq, r):
    key = np.arange(3 * tq)[:, None] - tq
    qpos = np.arange(tq)[None, :]
    band = np.abs(key - qpos) <= WINDOW
    variants = [band & (key >= 0), band, band & (key < tq)]
    table = np.stack([np.where(np.tile(v, (1, r)), 0.0, -np.inf) for v in variants])
    return jnp.asarray(table, F32)


def _window_attention(qh, kh, vth, sink, *, ctx_len):
    b, hq, s, dh = qh.shape
    g = kh.shape[1]
    r = hq // g
    tq = Q_BLOCK
    t = s - ctx_len
    nb = WINDOW_BLOCKS
    rows = nb * tq
    assert t % rows == 0 and ctx_len % tq == 0
    ns = t // rows
    off = ctx_len // tq
    last = t // tq - 1
    bias = _window_bias(tq, r)
    ctx = lambda bi, gi, i: (bi, gi, 0, 0)
    blk = lambda n_rows: (None, None, n_rows, dh)
    tblk = lambda n_cols: (None, None, dh, n_cols)
    prev_blk = lambda i: jnp.maximum(nb * i - 1, 0) + off
    next_blk = lambda i: jnp.minimum(nb * i + nb, last) + off
    el = pl.Element
    mid = lambda i: pl.multiple_of(ctx_len + i * rows, tq)
    return pl.pallas_call(
        _window_attn_kernel,
        grid=(b, g, ns),
        in_specs=[pl.BlockSpec(memory_space=pltpu.SMEM),
                  pl.BlockSpec((el(1), el(r), el(rows), el(dh)), lambda bi, gi, i: (bi, gi * r, mid(i), 0)),
                  pl.BlockSpec((None, 3 * tq, r * tq), lambda bi, gi, i: (jnp.minimum(i, 1), 0, 0)),
                  pl.BlockSpec((None, 3 * tq, r * tq), lambda bi, gi, i: (1, 0, 0)),
                  pl.BlockSpec((None, 3 * tq, r * tq),
                               lambda bi, gi, i: (jnp.where(i == ns - 1, 2, 1), 0, 0)),
                  pl.BlockSpec(blk(ctx_len), ctx),
                  pl.BlockSpec(blk(tq), lambda bi, gi, i: (bi, gi, prev_blk(i), 0)),
                  pl.BlockSpec((el(1), el(1), el(rows), el(dh)), lambda bi, gi, i: (bi, gi, mid(i), 0)),
                  pl.BlockSpec(blk(tq), lambda bi, gi, i: (bi, gi, next_blk(i), 0)),
                  pl.BlockSpec(tblk(ctx_len), ctx),
                  pl.BlockSpec(tblk(tq), lambda bi, gi, i: (bi, gi, 0, prev_blk(i))),
                  pl.BlockSpec((el(1), el(1), el(dh), el(rows)), lambda bi, gi, i: (bi, gi, 0, mid(i))),
                  pl.BlockSpec(tblk(tq), lambda bi, gi, i: (bi, gi, 0, next_blk(i)))],
        out_specs=pl.BlockSpec((None, r, dh, rows), lambda bi, gi, i: (bi, gi, 0, i)),
        out_shape=jax.ShapeDtypeStruct((b, hq, dh, t), F32),
        compiler_params=_cparams(("parallel", "parallel", "parallel")),
        name="window_attention",
    )(sink, qh, bias, bias, bias, kh, kh, kh, kh, vth, vth, vth, vth)


def _rwkv_prep_kernel(p_ref, hp_ref, hn_ref, mu_ref, w0_ref, w2_ref, a0_ref, a2_ref,
                      kk_ref, ka_ref, rk_ref, ones_ref,
                      m_ref, n_ref, q_ref, y0_ref, bonus_ref,
                      xf_s, kh_s, bh_s, kbar_s, bbar_s, rt_s, v_s, wend_s, *, ctx_chunks, n_chunks):
    step_id = pl.program_id(1)
    j = jnp.minimum(step_id, n_chunks - 1)
    c = CHUNK
    bw = kk_ref.shape[1]
    n_pairs = bw // LANES
    staged = (xf_s, kh_s, bh_s, kbar_s, bbar_s, rt_s, v_s, wend_s)

    @pl.when(step_id == 0)
    def _():
        for ref in staged:
            ref[...] = jnp.zeros_like(ref)

    gw = RWKV_GROUP_LANES
    hpg = gw // HEAD_DIM
    n_groups = bw // gw
    ti = lax.broadcasted_iota(jnp.int32, (c, c), 0)
    tj = lax.broadcasted_iota(jnp.int32, (c, c), 1)
    ti_g = lax.broadcasted_iota(jnp.int32, (c, gw), 0)
    tj_g = lax.broadcasted_iota(jnp.int32, (c, gw), 1) % c
    eye_g = (ti_g == tj_g).astype(F32)
    gi_r = lax.broadcasted_iota(jnp.int32, (gw, gw), 0)
    gi_c = lax.broadcasted_iota(jnp.int32, (gw, gw), 1)
    group_diag = (gi_r // HEAD_DIM) == (gi_c // HEAD_DIM)
    bi = lax.broadcasted_iota(jnp.int32, (LANES, LANES), 0)
    bj = lax.broadcasted_iota(jnp.int32, (LANES, LANES), 1)
    same_head = (bi // HEAD_DIM) == (bj // HEAD_DIM)
    diag = bi == bj
    scan_masks = [(tj_g < ti_g, tj_g <= ti_g), (tj_g > ti_g, tj_g >= ti_g)]

    def blockdiag(xs):
        xb = xs.astype(BF16)
        return jnp.where(group_diag, jnp.concatenate([xb] * hpg, axis=0), jnp.zeros((), BF16))


    def elementwise_stage():
        p = p_ref[...]
        has_prev = jnp.logical_and(j != 0, j != ctx_chunks)
        has_next = jnp.logical_and(j != ctx_chunks - 1, j != n_chunks - 1)
        prev_row = jnp.where(has_prev, hp_ref[7:8, :], 0.0)
        next_row = jnp.where(has_next, hn_ref[0:1, :], 0.0)
        rowi = lax.broadcasted_iota(jnp.int32, (c, 1), 0)
        p_prev = jnp.where(rowi == 0, prev_row, pltpu.roll(p, 1, 0))
        p_next = jnp.where(rowi == c - 1, next_row, pltpu.roll(p, c - 1, 0))
        ps = p + mu_ref[...] * (0.5 * (p_prev + p_next) - p)
        r = ps[:, 0:bw]
        k = ps[:, bw:2 * bw]
        v = ps[:, 2 * bw:3 * bw]
        zw = ps[:, 3 * bw:3 * bw + 2 * B_LORA]
        za = ps[:, 3 * bw + 2 * B_LORA:3 * bw + 4 * B_LORA]
        yield

        wl = w0_ref[...] + jnp.dot(jnp.tanh(zw).astype(BF16), w2_ref[...],
                                   preferred_element_type=F32)
        z = -wl
        softplus = jnp.maximum(z, 0.0) + jnp.log(1.0 + jnp.exp(-jnp.abs(z)))
        logw = -jnp.exp(-softplus - 0.5)
        yield
        a = _sigmoid(a0_ref[...] + jnp.dot(za.astype(BF16), a2_ref[...],
                                           preferred_element_type=F32))
        ones_seg = ones_ref[...]
        kf = k * kk_ref[...]
        kf2 = kf * kf
        ss = jnp.concatenate([_split_dot(kf2[:, g * LANES:(g + 1) * LANES], ones_seg)
                              for g in range(n_pairs)], axis=1)
        kk = kf / jnp.maximum(jnp.sqrt(ss), 1e-12)
        yield

        bonus = jnp.zeros((c, bw), F32)
        out = dict(xf=[], kh=[], bh=[], kbar=[], bbar=[], rt=[], wend=[])
        for d in range(2):
            lw = logw[:, d * bw:(d + 1) * bw]
            a_d = a[:, d * bw:(d + 1) * bw]
            b_d = kk * a_d
            kmod = k * (1.0 + (a_d - 1.0) * ka_ref[...])
            rkk = r * kmod * rk_ref[...]
            bsum = jnp.concatenate([_split_dot(rkk[:, g * LANES:(g + 1) * LANES], ones_seg)
                                    for g in range(n_pairs)], axis=1)
            bonus = bonus + bsum * v
            yield

            tri = ((tj <= ti) if d == 0 else (tj >= ti)).astype(BF16)
            end_row = c - 1 if d == 0 else 0
            lw_hi = lw.astype(BF16)
            lw_lo = (lw - lw_hi.astype(F32)).astype(BF16)
            cum = (jnp.dot(tri, lw_hi, preferred_element_type=F32)
                   + jnp.dot(tri, lw_lo, preferred_element_type=F32))
            cum_end = cum[end_row:end_row + 1, :]
            r_t = r * jnp.exp(cum)
            kk_t = kk * jnp.exp(cum - lw)
            yield
            e_out = jnp.exp(-cum)
            tail = jnp.exp(cum_end - cum)
            out["xf"].append(jnp.concatenate([kk_t, r_t], axis=0).astype(BF16))
            out["rt"].append(r_t)
            out["kh"].append((kmod * e_out).astype(BF16))
            out["bh"].append((b_d * e_out).astype(BF16))
            out["kbar"].append((kmod * tail).astype(BF16))
            out["bbar"].append((b_d * tail).astype(BF16))
            out["wend"].append(jnp.broadcast_to(jnp.exp(cum_end), (8, bw)))
            yield
        bonus_ref[...] = bonus
        out["v"] = v.astype(BF16)
        staged_next.update(out)

    def matmul_stage():
        contract_lanes = (((1,), (1,)), ((), ()))
        contract_rows = (((0,), (0,)), ((), ()))
        probs = [(d, g, slice(g * gw, (g + 1) * gw)) for d in range(2) for g in range(n_groups)]
        v_bd = [blockdiag(v_s[:, g * gw:(g + 1) * gw]) for g in range(n_groups)]
        aks = [lax.dot_general(xf_s[d, :, sl], blockdiag(kh_s[d, :, sl]), contract_lanes,
                               preferred_element_type=F32) for d, g, sl in probs]
        yield
        abs_ = [lax.dot_general(xf_s[d, :, sl], blockdiag(bh_s[d, :, sl]), contract_lanes,
                                preferred_element_type=F32) for d, g, sl in probs]
        yield
        a_ks = [jnp.concatenate([jnp.where(scan_masks[d][0], ak[0:c], 0.0),
                                 jnp.where(scan_masks[d][1], ak[c:2 * c], 0.0)], axis=0).astype(BF16)
                for (d, g, sl), ak in zip(probs, aks)]
        a_qbs = [jnp.where(scan_masks[d][1], ab[c:2 * c], 0.0).astype(BF16)
                 for (d, g, sl), ab in zip(probs, abs_)]
        pws = [jnp.where(scan_masks[d][0], -ab[0:c], 0.0) for (d, g, sl), ab in zip(probs, abs_)]
        tinvs = [eye_g + pw for pw in pws]
        pws = [jnp.dot(pw.astype(BF16), blockdiag(pw), preferred_element_type=F32) for pw in pws]
        yield
        for _ in range(4):
            stacked = [jnp.dot(jnp.concatenate([t, pw], axis=0).astype(BF16), blockdiag(pw),
                               preferred_element_type=F32) for t, pw in zip(tinvs, pws)]
            tinvs = [t + st[0:c] for t, st in zip(tinvs, stacked)]
            pws = [st[c:2 * c] for st in stacked]
            yield
        tinvs = [(t + jnp.dot(t.astype(BF16), blockdiag(pw), preferred_element_type=F32)).astype(BF16)
                 for t, pw in zip(tinvs, pws)]
        avs = [jnp.dot(a_k, v_bd[g], preferred_element_type=F32) for (d, g, sl), a_k in zip(probs, a_ks)]
        yield
        p1s = [jnp.dot(t, blockdiag(xf_s[d, 0:c, sl]), preferred_element_type=F32)
               for (d, g, sl), t in zip(probs, tinvs)]
        p2s = [jnp.dot(t, blockdiag(av[0:c]), preferred_element_type=F32) for t, av in zip(tinvs, avs)]
        yield
        qp1s = [jnp.dot(a_qb, blockdiag(p1), preferred_element_type=F32) for a_qb, p1 in zip(a_qbs, p1s)]
        qp2s = [jnp.dot(a_qb, blockdiag(p2), preferred_element_type=F32) for a_qb, p2 in zip(a_qbs, p2s)]
        for (d, g, sl), av, qp1, qp2 in zip(probs, avs, qp1s, qp2s):
            q_ref[d, :, sl] = (rt_s[d, :, sl] - qp1).astype(q_ref.dtype)
            y0_ref[d, :, sl] = av[c:2 * c] - qp2
        yield
        for d in range(2):
            for pr in range(n_pairs):
                sl = slice(pr * LANES, (pr + 1) * LANES)
                gidx = d * n_groups + pr * LANES // gw
                lo = pr * LANES % gw
                p12 = jnp.concatenate([p1s[gidx][:, lo:lo + LANES], p2s[gidx][:, lo:lo + LANES]],
                                      axis=1).astype(BF16)
                bp = lax.dot_general(bbar_s[d, :, sl], p12, contract_rows, preferred_element_type=F32)
                kv = lax.dot_general(kbar_s[d, :, sl], v_s[:, sl], contract_rows, preferred_element_type=F32)
                w_diag = jnp.where(diag, wend_s[d, 0:1, sl], 0.0)
                m_ref[d, pr] = (w_diag - jnp.where(same_head, bp[:, 0:LANES], 0.0)).astype(m_ref.dtype)
                n_ref[d, pr] = jnp.where(same_head, kv - bp[:, LANES:2 * LANES], 0.0)
            yield

    staged_next = {}
    _interleave(matmul_stage(), elementwise_stage())
    for ref, key in ((xf_s, "xf"), (kh_s, "kh"), (bh_s, "bh"), (kbar_s, "kbar"), (bbar_s, "bbar"),
                     (rt_s, "rt"), (wend_s, "wend")):
        for d in range(2):
            ref[d] = staged_next[key][d]
    v_s[...] = staged_next["v"]


def _rwkv_prep(bp, mu, w0, w2cat, a0, a2cat, kkw, kaw, rkw, ones_seg, *, ctx_len):
    b, s, pw = bp.shape
    bw = kkw.shape[1]
    n_pairs = bw // LANES
    nc = s // CHUNK
    ctx_chunks = ctx_len // CHUNK
    rb = CHUNK // 8
    kern = functools.partial(_rwkv_prep_kernel, ctx_chunks=ctx_chunks, n_chunks=nc)
    const = lambda bi, j: (0, 0)
    cur = lambda j: jnp.minimum(j, nc - 1)
    done = lambda j: jnp.maximum(j - 1, 0)
    return pl.pallas_call(
        kern,
        grid=(b, nc + 1),
        in_specs=[pl.BlockSpec((None, CHUNK, pw), lambda bi, j: (bi, cur(j), 0)),
                  pl.BlockSpec((None, 8, pw), lambda bi, j: (bi, jnp.maximum(cur(j) * rb - 1, 0), 0)),
                  pl.BlockSpec((None, 8, pw),
                               lambda bi, j: (bi, jnp.minimum((cur(j) + 1) * rb, s // 8 - 1), 0)),
                  pl.BlockSpec((1, pw), const),
                  pl.BlockSpec((1, 2 * bw), const),
                  pl.BlockSpec((2 * B_LORA, 2 * bw), const),
                  pl.BlockSpec((1, 2 * bw), const),
                  pl.BlockSpec((2 * B_LORA, 2 * bw), const),
                  pl.BlockSpec((1, bw), const),
                  pl.BlockSpec((1, bw), const),
                  pl.BlockSpec((1, bw), const),
                  pl.BlockSpec((LANES, LANES), const)],
        out_specs=[pl.BlockSpec((None, None, 2, n_pairs, LANES, LANES), lambda bi, j: (bi, done(j), 0, 0, 0, 0)),
                   pl.BlockSpec((None, None, 2, n_pairs, LANES, LANES), lambda bi, j: (bi, done(j), 0, 0, 0, 0)),
                   pl.BlockSpec((None, 2, CHUNK, bw), lambda bi, j: (bi, 0, done(j), 0)),
                   pl.BlockSpec((None, 2, CHUNK, bw), lambda bi, j: (bi, 0, done(j), 0)),
                   pl.BlockSpec((None, CHUNK, bw), lambda bi, j: (bi, cur(j), 0))],
        out_shape=[jax.ShapeDtypeStruct((b, nc, 2, n_pairs, LANES, LANES), BF16),
                   jax.ShapeDtypeStruct((b, nc, 2, n_pairs, LANES, LANES), F32),
                   jax.ShapeDtypeStruct((b, 2, s, bw), BF16),
                   jax.ShapeDtypeStruct((b, 2, s, bw), F32),
                   jax.ShapeDtypeStruct((b, s, bw), F32)],
        scratch_shapes=[pltpu.VMEM((2, 2 * CHUNK, bw), BF16),
                        pltpu.VMEM((2, CHUNK, bw), BF16),
                        pltpu.VMEM((2, CHUNK, bw), BF16),
                        pltpu.VMEM((2, CHUNK, bw), BF16),
                        pltpu.VMEM((2, CHUNK, bw), BF16),
                        pltpu.VMEM((2, CHUNK, bw), F32),
                        pltpu.VMEM((CHUNK, bw), BF16),
                        pltpu.VMEM((2, 8, bw), F32)],
        compiler_params=_cparams(("parallel", "arbitrary")),
        name="rwkv_prep",
    )(bp, bp, bp, mu, w0, w2cat, a0, a2cat, kkw, kaw, rkw, ones_seg)


def _rwkv_scan_kernel(m0_ref, m1_ref, n0_ref, n1_ref, q0_ref, q1_ref, y00_ref, y01_ref,
                      o0_ref, o1_ref, h_ref):
    j = pl.program_id(0)

    @pl.when(j == 0)
    def _():
        h_ref[...] = jnp.zeros_like(h_ref)

    nb, n_pairs = m0_ref.shape[0], m0_ref.shape[1]
    dirs = ((m0_ref, n0_ref, q0_ref, y00_ref, o0_ref), (m1_ref, n1_ref, q1_ref, y01_ref, o1_ref))
    for d, (m_ref, n_ref, q_ref, y0_ref, o_ref) in enumerate(dirs):
        for bi in range(nb):
            for pr in range(n_pairs):
                sl = slice(pr * LANES, (pr + 1) * LANES)
                h = h_ref[d, bi, pr]
                hb = h.astype(BF16)
                o_ref[bi, :, sl] = y0_ref[bi, :, sl] + jnp.dot(
                    q_ref[bi, :, sl], hb, preferred_element_type=F32)
                h_ref[d, bi, pr] = n_ref[bi, pr] + jnp.dot(
                    m_ref[bi, pr], hb, preferred_element_type=F32)


def _rwkv_scan(m, n, q, y0, *, ctx_len):
    b, nc, _, n_pairs, _, _ = m.shape
    s, bw = q.shape[2], q.shape[3]
    cc = ctx_len // CHUNK

    def mem_chunk(d, j):
        if d == 0:
            return j
        return jnp.where(j < cc, cc - 1 - j, nc - 1 + cc - j)

    def mn_spec(d):
        return pl.BlockSpec((b, None, None, n_pairs, LANES, LANES),
                            lambda j: (0, mem_chunk(d, j), d, 0, 0, 0))

    def row_spec(d):
        return pl.BlockSpec((b, None, CHUNK, bw), lambda j: (0, d, mem_chunk(d, j), 0))

    return pl.pallas_call(
        _rwkv_scan_kernel,
        grid=(nc,),
        in_specs=[mn_spec(0), mn_spec(1), mn_spec(0), mn_spec(1),
                  row_spec(0), row_spec(1), row_spec(0), row_spec(1)],
        out_specs=[pl.BlockSpec((b, CHUNK, bw), lambda j: (0, mem_chunk(0, j), 0)),
                   pl.BlockSpec((b, CHUNK, bw), lambda j: (0, mem_chunk(1, j), 0))],
        out_shape=[jax.ShapeDtypeStruct((b, s, bw), F32), jax.ShapeDtypeStruct((b, s, bw), F32)],
        scratch_shapes=[pltpu.VMEM((2, b, n_pairs, LANES, LANES), F32)],
        compiler_params=_cparams(("arbitrary",)),
        name="rwkv_scan",
    )(m, m, n, n, q, q, y0, y0)


def _outproj_kernel(*refs, two_sources, ctx_tiles, rwkv):
    n_src = 2 if two_sources else 1
    src = refs[:n_src]
    at_ref, g_ref, w_ref, mod_ref, gpost_ref = refs[n_src:n_src + 5]
    rest = refs[n_src + 5:]
    out_ref = rest[-1]
    hq, dh, tm = at_ref.shape
    parts = [at_ref[...].reshape(hq * dh, tm).T]
    if rwkv:
        yf_ref, yb_ref, bonus_ref, seg_ref, gnw_ref, gnb_ref = rest[:6]
        seg = seg_ref[...]
        y_sum = bonus_ref[...]
        for y_ref in (yf_ref, yb_ref):
            y = y_ref[...]
            yc = y - _head_mean(y, seg)
            var = _head_mean(yc * yc, seg)
            y_sum = y_sum + yc * lax.rsqrt(var + GN_EPS) * gnw_ref[...] + gnb_ref[...]
        parts.append(y_sum)
    o = jnp.concatenate(parts, axis=1) if len(parts) > 1 else parts[0]
    u = (o * _silu(g_ref[...])).astype(BF16)
    y = jnp.dot(u, w_ref[...], preferred_element_type=F32)
    ms = jnp.mean(y * y, axis=-1, keepdims=True)
    yn = y * lax.rsqrt(ms + RMS_EPS) * gpost_ref[...]
    x = _stream_rows(src[0] if two_sources else None, src[-1], ctx_tiles, pl.program_id(1))
    out_ref[...] = x + mod_ref[2:3, :] * yn


def _outproj(sources, attn_t, gate, w_bf16, modsel, gpost, rwkv_parts, *, ctx_tiles, latent_only):
    two_sources = len(sources) == 2
    b, d = sources[0].shape[0], sources[0].shape[2]
    s = sum(a.shape[1] for a in sources)
    tm = ROW_TILE
    off = ctx_tiles if latent_only else 0
    n_tiles = s // tm - off
    assert not (two_sources and latent_only)
    row = lambda bi, i: (bi, i + off, 0)
    const = lambda bi, i: (0, 0)
    hq, dh = attn_t.shape[1], attn_t.shape[2]
    if two_sources:
        src_specs = _stream_specs(True, tm, d, ctx_tiles)
    else:
        src_specs = [pl.BlockSpec((None, tm, d), row)]
    in_specs = src_specs + [
        pl.BlockSpec((None, hq, dh, tm), lambda bi, i: (bi, 0, 0, i)),
        pl.BlockSpec((None, tm, gate.shape[2]), row),
        pl.BlockSpec(w_bf16.shape, const),
        pl.BlockSpec((None, None, 3, d),
                     lambda bi, i: (bi, jnp.minimum((i + off) // ctx_tiles, 1), 0, 0)),
        pl.BlockSpec((1, d), const)]
    args = [*sources, attn_t, gate, w_bf16, modsel, gpost]
    if rwkv_parts is not None:
        y_f, y_b, bonus, seg, gnw, gnb = rwkv_parts
        in_specs += [pl.BlockSpec((None, tm, y_f.shape[2]), row),
                     pl.BlockSpec((None, tm, y_b.shape[2]), row),
                     pl.BlockSpec((None, tm, bonus.shape[2]), row),
                     pl.BlockSpec((MXU_WIDTH, MXU_WIDTH), const),
                     pl.BlockSpec(gnw.shape, const),
                     pl.BlockSpec(gnb.shape, const)]
        args += [y_f, y_b, bonus, seg, gnw, gnb]
    kern = functools.partial(_outproj_kernel, two_sources=two_sources, ctx_tiles=ctx_tiles,
                             rwkv=rwkv_parts is not None)
    return pl.pallas_call(
        kern,
        grid=(b, n_tiles),
        in_specs=in_specs,
        out_specs=pl.BlockSpec((None, tm, d), lambda bi, i: (bi, i, 0)),
        out_shape=jax.ShapeDtypeStruct((b, n_tiles * tm, d), F32),
        compiler_params=_cparams(("parallel", "parallel")),
        name="outproj",
    )(*args)


def _rope_tables(n_latent, ctx_len):
    t = jnp.arange(n_latent)
    rowp = (t // GRID_W).astype(F32)
    colp = (t % GRID_W).astype(F32)
    axis_dim = HEAD_DIM // 2
    inv = ROPE_THETA ** (-jnp.arange(0, axis_dim, 2, dtype=F32) / axis_dim)
    ang = jnp.concatenate([rowp[:, None] * inv, colp[:, None] * inv], axis=-1)
    cos, sin = jnp.cos(ang), jnp.sin(ang)
    cos = jnp.concatenate([jnp.ones((ctx_len, axis_dim), F32), cos], axis=0)
    sin = jnp.concatenate([jnp.zeros((ctx_len, axis_dim), F32), sin], axis=0)
    cos_h = jnp.repeat(cos, 2, axis=1)
    sin_h = jnp.stack([-sin, sin], axis=-1).reshape(sin.shape[0], HEAD_DIM)
    return jnp.tile(cos_h, (1, LANES // HEAD_DIM)), jnp.tile(sin_h, (1, LANES // HEAD_DIM))


def _block_diag_lora(w2):
    r, w = w2.shape[1], w2.shape[2]
    z = jnp.zeros((r, w), w2.dtype)
    return jnp.concatenate([jnp.concatenate([w2[0], z], axis=1),
                            jnp.concatenate([z, w2[1]], axis=1)], axis=0)


def kernel(x, c, ctx, c_ctx, w_mod, b_mod, g_pre, g_post, w_in_even, w_out_even, qn_a, kn_a, mu_b, w0_b, w2_b, a0_b, a2_b, kk_b, ka_b, rk_b, gn_w_b, gn_b_b, w_in_odd, w_out_odd, qn_c, kn_c, sink_c):
    b, t, d = x.shape
    ctx_len = ctx.shape[1]
    s = ctx_len + t
    assert ctx_len % ROW_TILE == 0 and t % ROW_TILE == 0 and b + 1 <= 8
    ctx_tiles = ctx_len // ROW_TILE
    depth = w_mod.shape[0]

    cc = jnp.concatenate([c, c_ctx[None, :], jnp.zeros((8 - b - 1, d), F32)], axis=0)
    mod = _modulation(cc, w_mod, b_mod)
    mod = mod.reshape(depth, 8, 3, d)
    modsel = jnp.stack([jnp.broadcast_to(mod[:, b][:, None], (depth, b, 3, d)), mod[:, :b]], axis=2)

    cos_t, sin_t = _rope_tables(t, ctx_len)
    seg_mean = jnp.asarray(np.kron(np.eye(MXU_WIDTH // HEAD_DIM),
                                   np.full((HEAD_DIM, HEAD_DIM), 1.0 / HEAD_DIM)), BF16)
    seg_ones = jnp.asarray(np.kron(np.eye(2), np.ones((HEAD_DIM, HEAD_DIM))), BF16)

    bw = kk_b.shape[1]
    a_width = w_out_even.shape[1] - bw
    a_heads = a_width // HEAD_DIM
    n_in = w_in_even.shape[2]
    b_proj = 3 * bw + 4 * B_LORA
    kv_width = (n_in - 2 * a_width - b_proj - bw) // 2
    a_kv_heads = kv_width // HEAD_DIM
    qk_width = a_width + kv_width
    gain0 = jnp.concatenate([jnp.tile(qn_a[0], a_heads), jnp.tile(kn_a[0], a_kv_heads)])[None, :]
    bp_lo = qk_width + kv_width
    g_lo = bp_lo + b_proj
    qh, kh, vth, bproj, gate0 = _inproj(
        [ctx, x], modsel[0], g_pre[0][None, :], w_in_even[0].astype(BF16), cos_t, sin_t, gain0, seg_mean,
        q_width=a_width, qk_width=qk_width, v_width=kv_width,
        splits=((bp_lo, g_lo), (g_lo, n_in)), out_dtypes=(F32, F32), ctx_tiles=ctx_tiles)
    oa_t = _dense_attention(qh, kh, vth, ctx_len=ctx_len)

    m_c, n_c, q_c, y0_c, bonus = _rwkv_prep(
        bproj, mu_b[0][None, :], w0_b[0].reshape(1, 2 * bw), _block_diag_lora(w2_b[0]).astype(BF16),
        a0_b[0].reshape(1, 2 * bw), _block_diag_lora(a2_b[0]).astype(BF16), kk_b[0][None, :], ka_b[0][None, :],
        rk_b[0].reshape(1, bw), seg_ones, ctx_len=ctx_len)
    y_f, y_b = _rwkv_scan(m_c, n_c, q_c, y0_c, ctx_len=ctx_len)

    xc = _outproj([ctx, x], oa_t, gate0, w_out_even[0].astype(BF16), modsel[0], g_post[0][None, :],
                  (y_f, y_b, bonus, seg_mean, gn_w_b[0][None, :], gn_b_b[0][None, :]),
                  ctx_tiles=ctx_tiles, latent_only=False)

    c_heads = sink_c.shape[1]
    c_width = c_heads * HEAD_DIM
    n_in1 = w_in_odd.shape[2]
    ckv_width = (n_in1 - 2 * c_width) // 2
    c_kv_heads = ckv_width // HEAD_DIM
    qk_width1 = c_width + ckv_width
    gain1 = jnp.concatenate([jnp.tile(qn_c[0], c_heads), jnp.tile(kn_c[0], c_kv_heads)])[None, :]
    qh1, kh1, vth1, gate1 = _inproj(
        [xc], modsel[1], g_pre[1][None, :], w_in_odd[0].astype(BF16), cos_t, sin_t, gain1, seg_mean,
        q_width=c_width, qk_width=qk_width1, v_width=ckv_width,
        splits=((qk_width1 + ckv_width, n_in1),), out_dtypes=(F32,), ctx_tiles=ctx_tiles)
    ow_t = _window_attention(qh1, kh1, vth1, sink_c[0], ctx_len=ctx_len)
    return _outproj([xc], ow_t, gate1, w_out_odd[0].astype(BF16), modsel[1], g_post[1][None, :], None,
                    ctx_tiles=ctx_tiles, latent_only=True)
```

```python
import functools

import numpy as np
import jax
import jax.numpy as jnp
from jax import lax
from jax.experimental import pallas as pl
from jax.experimental.pallas import tpu as pltpu

F32 = jnp.float32
BF16 = jnp.bfloat16
HIGHEST = lax.Precision.HIGHEST

HEAD_DIM = 64
LANES = 128
MXU_WIDTH = 256
RWKV_GROUP_LANES = 128
GRID_W = 64
Q_BLOCK = 128
WINDOW = 128
ROPE_THETA = 10000.0
RMS_EPS = 1e-6
GN_EPS = 64e-5
LOG2_E = float(np.log2(np.e))
Q_SCALE = HEAD_DIM ** -0.5 * LOG2_E
B_LORA = 64
CHUNK = 64
PREP_CHUNKS = 2
ROW_TILE = 256
PROJ_COL_CHUNK = 512
DENSE_Q_TILE = 128
DENSE_KV_TILE = 1024
WINDOW_BLOCKS = 8
VMEM_LIMIT = 56 * 1024 * 1024


def _cparams(sem):
    return pltpu.CompilerParams(dimension_semantics=sem, vmem_limit_bytes=VMEM_LIMIT)


def _split_dot(a, g_bf16):
    hi = a.astype(BF16)
    lo = (a - hi.astype(F32)).astype(BF16)
    return (jnp.dot(hi, g_bf16, preferred_element_type=F32)
            + jnp.dot(lo, g_bf16, preferred_element_type=F32))


def _head_mean(x, seg):
    width = x.shape[1]
    cols = []
    for lo in range(0, width, MXU_WIDTH):
        n = min(MXU_WIDTH, width - lo)
        cols.append(jnp.dot(x[:, lo:lo + n].astype(BF16), seg[0:n, 0:n], preferred_element_type=F32))
    return cols[0] if len(cols) == 1 else jnp.concatenate(cols, axis=1)


def _sigmoid(z):
    return 1.0 / (1.0 + jnp.exp(-z))


def _silu(z):
    return z * _sigmoid(z)


def _mod_kernel(c_ref, w_ref, b_ref, o_ref):
    o_ref[...] = jnp.dot(_silu(c_ref[...]), w_ref[...], precision=HIGHEST,
                         preferred_element_type=F32) + b_ref[...]


def _modulation(cc, w_mod, b_mod):
    depth, d, d3 = w_mod.shape
    nj = d3 // d
    return pl.pallas_call(
        _mod_kernel,
        grid=(depth, nj),
        in_specs=[pl.BlockSpec((8, d), lambda l, j: (0, 0)),
                  pl.BlockSpec((None, d, d), lambda l, j: (l, 0, j)),
                  pl.BlockSpec((None, 1, d), lambda l, j: (l, 0, j))],
        out_specs=pl.BlockSpec((None, 8, d), lambda l, j: (l, 0, j)),
        out_shape=jax.ShapeDtypeStruct((depth, 8, d3), F32),
        compiler_params=_cparams(("arbitrary", "arbitrary")),
        name="modulation",
    )(cc, w_mod, b_mod.reshape(depth, 1, d3))


def _stream_rows(ctx_ref, x_ref, ctx_tiles, tile):
    if ctx_ref is None:
        return x_ref[...]
    return jnp.where(tile < ctx_tiles, ctx_ref[...], x_ref[...])


def _stream_specs(two_sources, tm, d, ctx_tiles, tile_of=lambda i: i):
    if not two_sources:
        return [pl.BlockSpec((None, tm, d), lambda bi, i: (bi, tile_of(i), 0))]
    return [pl.BlockSpec((None, tm, d), lambda bi, i: (bi, jnp.minimum(tile_of(i), ctx_tiles - 1), 0)),
            pl.BlockSpec((None, tm, d), lambda bi, i: (bi, jnp.maximum(tile_of(i) - ctx_tiles, 0), 0))]


def _interleave(*stages):
    stages = list(stages)
    while stages:
        for stage in list(stages):
            if next(stage, "done") == "done":
                stages.remove(stage)


def _inproj_kernel(*refs, two_sources, ctx_tiles, n_tiles, q_width, qk_width, v_width, splits):
    n_src = 2 if two_sources else 1
    src = refs[:n_src]
    mod_ref, gpre_ref, w_ref, cos_ref, sin_ref, gain_ref, seg_ref = refs[n_src:n_src + 7]
    q_ref, k_ref, vt_ref = refs[n_src + 7:n_src + 10]
    extra_refs = refs[n_src + 10:-1]
    attn_s = refs[-1]
    step = pl.program_id(1)
    tile = jnp.minimum(step, n_tiles - 1)
    n_attn = qk_width + v_width

    @pl.when(step == 0)
    def _():
        attn_s[...] = jnp.zeros_like(attn_s)

    staged = []

    def project():
        x = _stream_rows(src[0] if two_sources else None, src[-1], ctx_tiles, tile)
        ms = jnp.mean(x * x, axis=-1, keepdims=True)
        h = x * lax.rsqrt(ms + RMS_EPS) * gpre_ref[...]
        hb = (h * (1.0 + mod_ref[1:2, :]) + mod_ref[0:1, :]).astype(BF16)
        yield
        for lo in range(0, n_attn, PROJ_COL_CHUNK):
            hi = min(lo + PROJ_COL_CHUNK, n_attn)
            staged.append((lo, hi, jnp.dot(hb, w_ref[:, lo:hi], preferred_element_type=F32)))
            yield
        for ref, (s_lo, s_hi) in zip(extra_refs, splits):
            for lo in range(s_lo, s_hi, PROJ_COL_CHUNK):
                hi = min(lo + PROJ_COL_CHUNK, s_hi)
                ref[:, lo - s_lo:hi - s_lo] = jnp.dot(hb, w_ref[:, lo:hi],
                                                      preferred_element_type=F32).astype(ref.dtype)
                yield

    def finish_previous():
        cos = cos_ref[...]
        sin = sin_ref[...]
        lane = lax.broadcasted_iota(jnp.int32, (1, LANES), 1)
        even_lane = (lane % 2) == 0
        qk = attn_s[:, 0:qk_width]
        msq_all = _head_mean(qk * qk, seg_ref[...])
        yield
        for g in range(qk_width // LANES):
            xg = attn_s[:, g * LANES:(g + 1) * LANES]
            msq = msq_all[:, g * LANES:(g + 1) * LANES]
            xn = xg * lax.rsqrt(msq + RMS_EPS) * gain_ref[:, g * LANES:(g + 1) * LANES]
            partner = jnp.where(even_lane, pltpu.roll(xn, LANES - 1, 1), pltpu.roll(xn, 1, 1))
            y = xn * cos + partner * sin
            is_q = g * LANES < q_width
            if is_q:
                y = y * Q_SCALE
            ref = q_ref if is_q else k_ref
            h0 = 2 * g if is_q else 2 * (g - q_width // LANES)
            ref[h0] = y[:, 0:HEAD_DIM].astype(ref.dtype)
            ref[h0 + 1] = y[:, HEAD_DIM:LANES].astype(ref.dtype)
            yield
        for g in range(v_width // LANES):
            vt = attn_s[:, qk_width + g * LANES:qk_width + (g + 1) * LANES].T
            vt_ref[2 * g] = vt[0:HEAD_DIM].astype(vt_ref.dtype)
            vt_ref[2 * g + 1] = vt[HEAD_DIM:LANES].astype(vt_ref.dtype)
            yield

    _interleave(project(), finish_previous())
    for lo, hi, acc in staged:
        attn_s[:, lo:hi] = acc


def _inproj(sources, modsel, gpre, w_bf16, cos_t, sin_t, gain, seg, *, q_width, qk_width, v_width,
            splits, out_dtypes, ctx_tiles):
    two_sources = len(sources) == 2
    b, d = sources[0].shape[0], sources[0].shape[2]
    s = sum(a.shape[1] for a in sources)
    n = w_bf16.shape[1]
    tm = ROW_TILE
    hq = q_width // HEAD_DIM
    hk = (qk_width - q_width) // HEAD_DIM
    hv = v_width // HEAD_DIM
    n_tiles = s // tm
    cur = lambda i: jnp.minimum(i, n_tiles - 1)
    done = lambda i: jnp.maximum(i - 1, 0)
    const = lambda bi, i: (0, 0)
    out_shapes = [jax.ShapeDtypeStruct((b, hq, s, HEAD_DIM), BF16),
                  jax.ShapeDtypeStruct((b, hk, s, HEAD_DIM), BF16),
                  jax.ShapeDtypeStruct((b, hv, HEAD_DIM, s), BF16)]
    out_specs = [pl.BlockSpec((None, hq, tm, HEAD_DIM), lambda bi, i: (bi, 0, done(i), 0)),
                 pl.BlockSpec((None, hk, tm, HEAD_DIM), lambda bi, i: (bi, 0, done(i), 0)),
                 pl.BlockSpec((None, hv, HEAD_DIM, tm), lambda bi, i: (bi, 0, 0, done(i)))]
    for (lo, hi), dt in zip(splits, out_dtypes):
        out_shapes.append(jax.ShapeDtypeStruct((b, s, hi - lo), dt))
        out_specs.append(pl.BlockSpec((None, tm, hi - lo), lambda bi, i: (bi, cur(i), 0)))
    kern = functools.partial(_inproj_kernel, two_sources=two_sources, ctx_tiles=ctx_tiles, n_tiles=n_tiles,
                             q_width=q_width, qk_width=qk_width, v_width=v_width, splits=splits)
    return pl.pallas_call(
        kern,
        grid=(b, n_tiles + 1),
        in_specs=_stream_specs(two_sources, tm, d, ctx_tiles, cur) + [
            pl.BlockSpec((None, None, 3, d),
                         lambda bi, i: (bi, jnp.minimum(cur(i) // ctx_tiles, 1), 0, 0)),
            pl.BlockSpec((1, d), const),
            pl.BlockSpec((d, n), const),
            pl.BlockSpec((tm, LANES), lambda bi, i: (done(i), 0)),
            pl.BlockSpec((tm, LANES), lambda bi, i: (done(i), 0)),
            pl.BlockSpec((1, qk_width), const),
            pl.BlockSpec((MXU_WIDTH, MXU_WIDTH), const)],
        out_specs=out_specs,
        out_shape=out_shapes,
        scratch_shapes=[pltpu.VMEM((tm, qk_width + v_width), F32)],
        compiler_params=_cparams(("parallel", "arbitrary")),
        name="inproj",
    )(*sources, modsel, gpre, w_bf16, cos_t, sin_t, gain, seg)


def _dense_attn_kernel(q_ref, k_ref, vt_ref, o_ref, *, tk, ctx_len):
    i = pl.program_id(2)
    r, tq, dh = q_ref.shape
    m_rows = r * tq
    n_keys = k_ref.shape[0]
    q = q_ref[...].reshape(m_rows, dh)
    contract_lanes = (((1,), (1,)), ((), ()))

    def scores(lo, hi):
        return lax.dot_general(k_ref[lo:hi, :], q, contract_lanes, preferred_element_type=F32)

    def attend(bounds):
        m = jnp.full((1, m_rows), -jnp.inf, F32)
        l = jnp.zeros((1, m_rows), F32)
        acc = jnp.zeros((dh, m_rows), F32)
        st = scores(*bounds[0])
        pending = None
        for j in range(len(bounds)):
            if pending is not None:
                (plo, phi), p_prev = pending
                acc = acc + jnp.dot(vt_ref[:, plo:phi], p_prev, preferred_element_type=F32)
            st_next = scores(*bounds[j + 1]) if j + 1 < len(bounds) else None
            m_new = jnp.maximum(m, jnp.max(st, axis=0, keepdims=True))
            alpha = jnp.exp2(m - m_new)
            p = jnp.exp2(st - m_new)
            l = alpha * l + jnp.sum(p, axis=0, keepdims=True)
            acc = acc * alpha
            pending = (bounds[j], p.astype(BF16))
            m, st = m_new, st_next
        (plo, phi), p_prev = pending
        acc = acc + jnp.dot(vt_ref[:, plo:phi], p_prev, preferred_element_type=F32)
        o = acc / l
        for hh in range(r):
            o_ref[hh] = o[:, hh * tq:(hh + 1) * tq].astype(o_ref.dtype)

    ctx_bounds = [(0, ctx_len)]
    all_bounds = ctx_bounds + [(lo, lo + tk) for lo in range(ctx_len, n_keys, tk)]

    @pl.when(i < ctx_len // tq)
    def _():
        attend(ctx_bounds)

    @pl.when(i >= ctx_len // tq)
    def _():
        attend(all_bounds)


def _dense_attention(qh, kh, vth, *, ctx_len):
    b, hq, s, dh = qh.shape
    g = kh.shape[1]
    r = hq // g
    tq = DENSE_Q_TILE
    assert (s - ctx_len) % DENSE_KV_TILE == 0 and ctx_len % tq == 0
    kern = functools.partial(_dense_attn_kernel, tk=DENSE_KV_TILE, ctx_len=ctx_len)
    return pl.pallas_call(
        kern,
        grid=(b, g, s // tq),
        in_specs=[pl.BlockSpec((None, r, tq, dh), lambda bi, gi, i: (bi, gi, i, 0)),
                  pl.BlockSpec((None, None, s, dh), lambda bi, gi, i: (bi, gi, 0, 0)),
                  pl.BlockSpec((None, None, dh, s), lambda bi, gi, i: (bi, gi, 0, 0))],
        out_specs=pl.BlockSpec((None, r, dh, tq), lambda bi, gi, i: (bi, gi, 0, i)),
        out_shape=jax.ShapeDtypeStruct((b, hq, dh, s), F32),
        compiler_params=_cparams(("parallel", "parallel", "parallel")),
        name="dense_attention",
    )(qh, kh, vth)


def _window_attn_kernel(sink_ref, q_ref, bias_first_ref, bias_mid_ref, bias_last_ref,
                        kc_ref, kp_ref, km_ref, kn_ref, vc_ref, vp_ref, vm_ref, vn_ref, o_ref):
    gi = pl.program_id(1)
    _, r, rows, dh = q_ref.shape
    tq = Q_BLOCK
    nb = rows // tq
    contract_lanes = (((1,), (1,)), ((), ()))
    k_all = jnp.concatenate([kp_ref[...], km_ref[0, 0], kn_ref[...]], axis=0)
    vt_all = jnp.concatenate([vp_ref[...], vm_ref[0, 0], vn_ref[...]], axis=1)
    k_ctx, vt_ctx = kc_ref[...], vc_ref[...]
    lane = lax.broadcasted_iota(jnp.int32, (1, r * tq), 1)
    sink = jnp.zeros((1, r * tq), F32)
    for hh in range(r):
        sink = jnp.where(lane // tq == hh, sink_ref[gi * r + hh] * LOG2_E, sink)

    def scores(qb):
        q = jnp.concatenate([q_ref[0, hh, qb * tq:(qb + 1) * tq, :] for hh in range(r)], axis=0)
        bias_ref = bias_first_ref if qb == 0 else (bias_last_ref if qb == nb - 1 else bias_mid_ref)
        s_loc = lax.dot_general(k_all[qb * tq:(qb + 3) * tq], q, contract_lanes,
                                preferred_element_type=F32) + bias_ref[...]
        s_ctx = lax.dot_general(k_ctx, q, contract_lanes, preferred_element_type=F32)
        return s_loc, s_ctx

    def finish(qb, p_loc, p_ctx, l):
        o = (jnp.dot(vt_all[:, qb * tq:(qb + 3) * tq], p_loc, preferred_element_type=F32)
             + jnp.dot(vt_ctx, p_ctx, preferred_element_type=F32)) / l
        for hh in range(r):
            o_ref[hh, :, qb * tq:(qb + 1) * tq] = o[:, hh * tq:(hh + 1) * tq].astype(o_ref.dtype)

    s_cur = scores(0)
    pending = None
    for qb in range(nb):
        if pending is not None:
            finish(*pending)
        s_next = scores(qb + 1) if qb + 1 < nb else None
        s_loc, s_ctx = s_cur
        m = jnp.maximum(jnp.maximum(jnp.max(s_loc, axis=0, keepdims=True),
                                    jnp.max(s_ctx, axis=0, keepdims=True)), sink)
        p_loc = jnp.exp2(s_loc - m)
        p_ctx = jnp.exp2(s_ctx - m)
        l = (jnp.sum(p_loc, axis=0, keepdims=True) + jnp.sum(p_ctx, axis=0, keepdims=True)
             + jnp.exp2(sink - m))
        pending = (qb, p_loc.astype(BF16), p_ctx.astype(BF16), l)
        s_cur = s_next
    finish(*pending)


def _window_bias(tq, r):
    key = np.arange(3 * tq)[:, None] - tq
    qpos = np.arange(tq)[None, :]
    band = np.abs(key - qpos) <= WINDOW
    variants = [band & (key >= 0), band, band & (key < tq)]
    table = np.stack([np.where(np.tile(v, (1, r)), 0.0, -np.inf) for v in variants])
    return jnp.asarray(table, F32)


def _window_attention(qh, kh, vth, sink, *, ctx_len):
    b, hq, s, dh = qh.shape
    g = kh.shape[1]
    r = hq // g
    tq = Q_BLOCK
    t = s - ctx_len
    nb = WINDOW_BLOCKS
    rows = nb * tq
    assert t % rows == 0 and ctx_len % tq == 0
    ns = t // rows
    off = ctx_len // tq
    last = t // tq - 1
    bias = _window_bias(tq, r)
    ctx = lambda bi, gi, i: (bi, gi, 0, 0)
    blk = lambda n_rows: (None, None, n_rows, dh)
    tblk = lambda n_cols: (None, None, dh, n_cols)
    prev_blk = lambda i: jnp.maximum(nb * i - 1, 0) + off
    next_blk = lambda i: jnp.minimum(nb * i + nb, last) + off
    el = pl.Element
    mid = lambda i: pl.multiple_of(ctx_len + i * rows, tq)
    return pl.pallas_call(
        _window_attn_kernel,
        grid=(b, g, ns),
        in_specs=[pl.BlockSpec(memory_space=pltpu.SMEM),
                  pl.BlockSpec((el(1), el(r), el(rows), el(dh)), lambda bi, gi, i: (bi, gi * r, mid(i), 0)),
                  pl.BlockSpec((None, 3 * tq, r * tq), lambda bi, gi, i: (jnp.minimum(i, 1), 0, 0)),
                  pl.BlockSpec((None, 3 * tq, r * tq), lambda bi, gi, i: (1, 0, 0)),
                  pl.BlockSpec((None, 3 * tq, r * tq),
                               lambda bi, gi, i: (jnp.where(i == ns - 1, 2, 1), 0, 0)),
                  pl.BlockSpec(blk(ctx_len), ctx),
                  pl.BlockSpec(blk(tq), lambda bi, gi, i: (bi, gi, prev_blk(i), 0)),
                  pl.BlockSpec((el(1), el(1), el(rows), el(dh)), lambda bi, gi, i: (bi, gi, mid(i), 0)),
                  pl.BlockSpec(blk(tq), lambda bi, gi, i: (bi, gi, next_blk(i), 0)),
                  pl.BlockSpec(tblk(ctx_len), ctx),
                  pl.BlockSpec(tblk(tq), lambda bi, gi, i: (bi, gi, 0, prev_blk(i))),
                  pl.BlockSpec((el(1), el(1), el(dh), el(rows)), lambda bi, gi, i: (bi, gi, 0, mid(i))),
                  pl.BlockSpec(tblk(tq), lambda bi, gi, i: (bi, gi, 0, next_blk(i)))],
        out_specs=pl.BlockSpec((None, r, dh, rows), lambda bi, gi, i: (bi, gi, 0, i)),
        out_shape=jax.ShapeDtypeStruct((b, hq, dh, t), F32),
        compiler_params=_cparams(("parallel", "parallel", "parallel")),
        name="window_attention",
    )(sink, qh, bias, bias, bias, kh, kh, kh, kh, vth, vth, vth, vth)


def _rwkv_prep_kernel(p_ref, hp_ref, hn_ref, mu_ref, w0_ref, w2_ref, a0_ref, a2_ref,
                      kk_ref, ka_ref, rk_ref, ones_ref,
                      m_ref, n_ref, q_ref, y0_ref, bonus_ref,
                      xf_s, kh_s, bh_s, kbar_s, bbar_s, rt_s, v_s, wend_s, *, ctx_chunks, n_chunks):
    step_id = pl.program_id(1)
    sub = PREP_CHUNKS
    c = CHUNK
    rows = sub * c
    jb = jnp.minimum(step_id, n_chunks // sub - 1)
    first_chunk = jb * sub
    last_chunk = first_chunk + sub - 1
    bw = kk_ref.shape[1]
    n_pairs = bw // LANES
    staged = (xf_s, kh_s, bh_s, kbar_s, bbar_s, rt_s, v_s, wend_s)

    @pl.when(step_id == 0)
    def _():
        for ref in staged:
            ref[...] = jnp.zeros_like(ref)

    gw = RWKV_GROUP_LANES
    hpg = gw // HEAD_DIM
    n_groups = bw // gw
    ti = lax.broadcasted_iota(jnp.int32, (rows, rows), 0)
    tj = lax.broadcasted_iota(jnp.int32, (rows, rows), 1)
    same_chunk = (ti // c) == (tj // c)
    ti_g = lax.broadcasted_iota(jnp.int32, (c, gw), 0)
    tj_g = lax.broadcasted_iota(jnp.int32, (c, gw), 1) % c
    eye_g = (ti_g == tj_g).astype(F32)
    gi_r = lax.broadcasted_iota(jnp.int32, (gw, gw), 0)
    gi_c = lax.broadcasted_iota(jnp.int32, (gw, gw), 1)
    group_diag = (gi_r // HEAD_DIM) == (gi_c // HEAD_DIM)
    bi = lax.broadcasted_iota(jnp.int32, (LANES, LANES), 0)
    bj = lax.broadcasted_iota(jnp.int32, (LANES, LANES), 1)
    same_head = (bi // HEAD_DIM) == (bj // HEAD_DIM)
    diag = bi == bj
    scan_masks = [(tj_g < ti_g, tj_g <= ti_g), (tj_g > ti_g, tj_g >= ti_g)]

    def blockdiag(xs):
        xb = xs.astype(BF16)
        return jnp.where(group_diag, jnp.concatenate([xb] * hpg, axis=0), jnp.zeros((), BF16))


    def elementwise_stage():
        p = p_ref[...]
        has_prev = jnp.logical_and(first_chunk != 0, first_chunk != ctx_chunks)
        has_next = jnp.logical_and(last_chunk != ctx_chunks - 1, last_chunk != n_chunks - 1)
        prev_row = jnp.where(has_prev, hp_ref[7:8, :], 0.0)
        next_row = jnp.where(has_next, hn_ref[0:1, :], 0.0)
        rowi = lax.broadcasted_iota(jnp.int32, (rows, 1), 0)
        p_prev = jnp.where(rowi == 0, prev_row, pltpu.roll(p, 1, 0))
        p_next = jnp.where(rowi == rows - 1, next_row, pltpu.roll(p, rows - 1, 0))
        ps = p + mu_ref[...] * (0.5 * (p_prev + p_next) - p)
        r = ps[:, 0:bw]
        k = ps[:, bw:2 * bw]
        v = ps[:, 2 * bw:3 * bw]
        zw = ps[:, 3 * bw:3 * bw + 2 * B_LORA]
        za = ps[:, 3 * bw + 2 * B_LORA:3 * bw + 4 * B_LORA]
        yield

        wl = w0_ref[...] + jnp.dot(jnp.tanh(zw).astype(BF16), w2_ref[...],
                                   preferred_element_type=F32)
        z = -wl
        softplus = jnp.maximum(z, 0.0) + jnp.log(1.0 + jnp.exp(-jnp.abs(z)))
        logw = -jnp.exp(-softplus - 0.5)
        yield
        a = _sigmoid(a0_ref[...] + jnp.dot(za.astype(BF16), a2_ref[...],
                                           preferred_element_type=F32))
        ones_seg = ones_ref[...]
        kf = k * kk_ref[...]
        kf2 = kf * kf
        ss = jnp.concatenate([_split_dot(kf2[:, g * LANES:(g + 1) * LANES], ones_seg)
                              for g in range(n_pairs)], axis=1)
        kk = kf / jnp.maximum(jnp.sqrt(ss), 1e-12)
        yield

        bonus = jnp.zeros((rows, bw), F32)
        chunk_rows = [slice(ch * c, (ch + 1) * c) for ch in range(sub)]
        out = dict(xf=[], kh=[], bh=[], kbar=[], bbar=[], rt=[], wend=[])
        for d in range(2):
            lw = logw[:, d * bw:(d + 1) * bw]
            a_d = a[:, d * bw:(d + 1) * bw]
            b_d = kk * a_d
            kmod = k * (1.0 + (a_d - 1.0) * ka_ref[...])
            rkk = r * kmod * rk_ref[...]
            bsum = jnp.concatenate([_split_dot(rkk[:, g * LANES:(g + 1) * LANES], ones_seg)
                                    for g in range(n_pairs)], axis=1)
            bonus = bonus + bsum * v
            yield

            tri = (same_chunk & ((tj <= ti) if d == 0 else (tj >= ti))).astype(BF16)
            lw_hi = lw.astype(BF16)
            lw_lo = (lw - lw_hi.astype(F32)).astype(BF16)
            cum = (jnp.dot(tri, lw_hi, preferred_element_type=F32)
                   + jnp.dot(tri, lw_lo, preferred_element_type=F32))
            ends = [cum[rs.stop - 1:rs.stop] if d == 0 else cum[rs.start:rs.start + 1] for rs in chunk_rows]
            cum_end = jnp.concatenate([jnp.broadcast_to(e, (c, bw)) for e in ends], axis=0)
            r_t = r * jnp.exp(cum)
            kk_t = kk * jnp.exp(cum - lw)
            yield
            e_out = jnp.exp(-cum)
            tail = jnp.exp(cum_end - cum)
            out["xf"].append(jnp.concatenate([part[rs] for rs in chunk_rows for part in (kk_t, r_t)],
                                             axis=0).astype(BF16))
            out["rt"].append(r_t)
            out["kh"].append((kmod * e_out).astype(BF16))
            out["bh"].append((b_d * e_out).astype(BF16))
            out["kbar"].append((kmod * tail).astype(BF16))
            out["bbar"].append((b_d * tail).astype(BF16))
            out["wend"].append(jnp.concatenate([jnp.broadcast_to(jnp.exp(e), (8, bw)) for e in ends], axis=0))
            yield
        bonus_ref[...] = bonus
        out["v"] = v.astype(BF16)
        staged_next.update(out)

    def matmul_stage():
        contract_lanes = (((1,), (1,)), ((), ()))
        contract_rows = (((0,), (0,)), ((), ()))
        probs = [(d, g, ch) for d in range(2) for g in range(n_groups) for ch in range(sub)]
        lanes_of = lambda g: slice(g * gw, (g + 1) * gw)
        rows_of = lambda ch: slice(ch * c, (ch + 1) * c)
        xf_of = lambda d, g, ch: xf_s[d, ch * 2 * c:(ch + 1) * 2 * c, lanes_of(g)]
        v_bd = {(g, ch): blockdiag(v_s[rows_of(ch), lanes_of(g)]) for g in range(n_groups) for ch in range(sub)}
        aks = [lax.dot_general(xf_of(d, g, ch), blockdiag(kh_s[d, rows_of(ch), lanes_of(g)]), contract_lanes,
                               preferred_element_type=F32) for d, g, ch in probs]
        yield
        abs_ = [lax.dot_general(xf_of(d, g, ch), blockdiag(bh_s[d, rows_of(ch), lanes_of(g)]), contract_lanes,
                                preferred_element_type=F32) for d, g, ch in probs]
        yield
        a_ks = [jnp.concatenate([jnp.where(scan_masks[d][0], ak[0:c], 0.0),
                                 jnp.where(scan_masks[d][1], ak[c:2 * c], 0.0)], axis=0).astype(BF16)
                for (d, g, ch), ak in zip(probs, aks)]
        a_qbs = [jnp.where(scan_masks[d][1], ab[c:2 * c], 0.0).astype(BF16)
                 for (d, g, ch), ab in zip(probs, abs_)]
        pws = [jnp.where(scan_masks[d][0], -ab[0:c], 0.0) for (d, g, ch), ab in zip(probs, abs_)]
        tinvs = [eye_g + pw for pw in pws]
        pws = [jnp.dot(pw.astype(BF16), blockdiag(pw), preferred_element_type=F32) for pw in pws]
        yield
        for _ in range(4):
            stacked = [jnp.dot(jnp.concatenate([t, pw], axis=0).astype(BF16), blockdiag(pw),
                               preferred_element_type=F32) for t, pw in zip(tinvs, pws)]
            tinvs = [t + st[0:c] for t, st in zip(tinvs, stacked)]
            pws = [st[c:2 * c] for st in stacked]
            yield
        tinvs = [(t + jnp.dot(t.astype(BF16), blockdiag(pw), preferred_element_type=F32)).astype(BF16)
                 for t, pw in zip(tinvs, pws)]
        avs = [jnp.dot(a_k, v_bd[g, ch], preferred_element_type=F32) for (d, g, ch), a_k in zip(probs, a_ks)]
        yield
        p1s = [jnp.dot(t, blockdiag(xf_of(d, g, ch)[0:c]), preferred_element_type=F32)
               for (d, g, ch), t in zip(probs, tinvs)]
        p2s = [jnp.dot(t, blockdiag(av[0:c]), preferred_element_type=F32) for t, av in zip(tinvs, avs)]
        yield
        qp1s = [jnp.dot(a_qb, blockdiag(p1), preferred_element_type=F32) for a_qb, p1 in zip(a_qbs, p1s)]
        qp2s = [jnp.dot(a_qb, blockdiag(p2), preferred_element_type=F32) for a_qb, p2 in zip(a_qbs, p2s)]
        for (d, g, ch), av, qp1, qp2 in zip(probs, avs, qp1s, qp2s):
            q_ref[d, rows_of(ch), lanes_of(g)] = (rt_s[d, rows_of(ch), lanes_of(g)] - qp1).astype(q_ref.dtype)
            y0_ref[d, rows_of(ch), lanes_of(g)] = av[c:2 * c] - qp2
        yield
        for d in range(2):
            for pr in range(n_pairs):
                for ch in range(sub):
                    sl = slice(pr * LANES, (pr + 1) * LANES)
                    pidx = probs.index((d, pr * LANES // gw, ch))
                    lo = pr * LANES % gw
                    p12 = jnp.concatenate([p1s[pidx][:, lo:lo + LANES], p2s[pidx][:, lo:lo + LANES]],
                                          axis=1).astype(BF16)
                    bp = lax.dot_general(bbar_s[d, rows_of(ch), sl], p12, contract_rows,
                                         preferred_element_type=F32)
                    kv = lax.dot_general(kbar_s[d, rows_of(ch), sl], v_s[rows_of(ch), sl], contract_rows,
                                         preferred_element_type=F32)
                    w_diag = jnp.where(diag, wend_s[d, ch * 8:ch * 8 + 1, sl], 0.0)
                    m_ref[ch, d, pr] = (w_diag - jnp.where(same_head, bp[:, 0:LANES], 0.0)).astype(m_ref.dtype)
                    n_ref[ch, d, pr] = jnp.where(same_head, kv - bp[:, LANES:2 * LANES], 0.0)
            yield

    staged_next = {}
    _interleave(matmul_stage(), elementwise_stage())
    for ref, key in ((xf_s, "xf"), (kh_s, "kh"), (bh_s, "bh"), (kbar_s, "kbar"), (bbar_s, "bbar"),
                     (rt_s, "rt"), (wend_s, "wend")):
        for d in range(2):
            ref[d] = staged_next[key][d]
    v_s[...] = staged_next["v"]


def _rwkv_prep(bp, mu, w0, w2cat, a0, a2cat, kkw, kaw, rkw, ones_seg, *, ctx_len):
    b, s, pw = bp.shape
    bw = kkw.shape[1]
    n_pairs = bw // LANES
    nc = s // CHUNK
    ctx_chunks = ctx_len // CHUNK
    sub = PREP_CHUNKS
    rows = sub * CHUNK
    assert nc % sub == 0 and ctx_chunks % sub == 0
    n_blocks = nc // sub
    rb = rows // 8
    kern = functools.partial(_rwkv_prep_kernel, ctx_chunks=ctx_chunks, n_chunks=nc)
    const = lambda bi, j: (0, 0)
    cur = lambda j: jnp.minimum(j, n_blocks - 1)
    done = lambda j: jnp.maximum(j - 1, 0)
    return pl.pallas_call(
        kern,
        grid=(b, n_blocks + 1),
        in_specs=[pl.BlockSpec((None, rows, pw), lambda bi, j: (bi, cur(j), 0)),
                  pl.BlockSpec((None, 8, pw), lambda bi, j: (bi, jnp.maximum(cur(j) * rb - 1, 0), 0)),
                  pl.BlockSpec((None, 8, pw),
                               lambda bi, j: (bi, jnp.minimum((cur(j) + 1) * rb, s // 8 - 1), 0)),
                  pl.BlockSpec((1, pw), const),
                  pl.BlockSpec((1, 2 * bw), const),
                  pl.BlockSpec((2 * B_LORA, 2 * bw), const),
                  pl.BlockSpec((1, 2 * bw), const),
                  pl.BlockSpec((2 * B_LORA, 2 * bw), const),
                  pl.BlockSpec((1, bw), const),
                  pl.BlockSpec((1, bw), const),
                  pl.BlockSpec((1, bw), const),
                  pl.BlockSpec((LANES, LANES), const)],
        out_specs=[pl.BlockSpec((None, sub, 2, n_pairs, LANES, LANES), lambda bi, j: (bi, done(j), 0, 0, 0, 0)),
                   pl.BlockSpec((None, sub, 2, n_pairs, LANES, LANES), lambda bi, j: (bi, done(j), 0, 0, 0, 0)),
                   pl.BlockSpec((None, 2, rows, bw), lambda bi, j: (bi, 0, done(j), 0)),
                   pl.BlockSpec((None, 2, rows, bw), lambda bi, j: (bi, 0, done(j), 0)),
                   pl.BlockSpec((None, rows, bw), lambda bi, j: (bi, cur(j), 0))],
        out_shape=[jax.ShapeDtypeStruct((b, nc, 2, n_pairs, LANES, LANES), BF16),
                   jax.ShapeDtypeStruct((b, nc, 2, n_pairs, LANES, LANES), F32),
                   jax.ShapeDtypeStruct((b, 2, s, bw), BF16),
                   jax.ShapeDtypeStruct((b, 2, s, bw), F32),
                   jax.ShapeDtypeStruct((b, s, bw), F32)],
        scratch_shapes=[pltpu.VMEM((2, 2 * rows, bw), BF16),
                        pltpu.VMEM((2, rows, bw), BF16),
                        pltpu.VMEM((2, rows, bw), BF16),
                        pltpu.VMEM((2, rows, bw), BF16),
                        pltpu.VMEM((2, rows, bw), BF16),
                        pltpu.VMEM((2, rows, bw), F32),
                        pltpu.VMEM((rows, bw), BF16),
                        pltpu.VMEM((2, 8 * sub, bw), F32)],
        compiler_params=_cparams(("parallel", "arbitrary")),
        name="rwkv_prep",
    )(bp, bp, bp, mu, w0, w2cat, a0, a2cat, kkw, kaw, rkw, ones_seg)


def _rwkv_scan_kernel(m0_ref, m1_ref, n0_ref, n1_ref, q0_ref, q1_ref, y00_ref, y01_ref,
                      o0_ref, o1_ref, h_ref):
    j = pl.program_id(0)

    @pl.when(j == 0)
    def _():
        h_ref[...] = jnp.zeros_like(h_ref)

    nb, n_pairs = m0_ref.shape[0], m0_ref.shape[1]
    dirs = ((m0_ref, n0_ref, q0_ref, y00_ref, o0_ref), (m1_ref, n1_ref, q1_ref, y01_ref, o1_ref))
    for d, (m_ref, n_ref, q_ref, y0_ref, o_ref) in enumerate(dirs):
        for bi in range(nb):
            for pr in range(n_pairs):
                sl = slice(pr * LANES, (pr + 1) * LANES)
                h = h_ref[d, bi, pr]
                hb = h.astype(BF16)
                o_ref[bi, :, sl] = y0_ref[bi, :, sl] + jnp.dot(
                    q_ref[bi, :, sl], hb, preferred_element_type=F32)
                h_ref[d, bi, pr] = n_ref[bi, pr] + jnp.dot(
                    m_ref[bi, pr], hb, preferred_element_type=F32)


def _rwkv_scan(m, n, q, y0, *, ctx_len):
    b, nc, _, n_pairs, _, _ = m.shape
    s, bw = q.shape[2], q.shape[3]
    cc = ctx_len // CHUNK

    def mem_chunk(d, j):
        if d == 0:
            return j
        return jnp.where(j < cc, cc - 1 - j, nc - 1 + cc - j)

    def mn_spec(d):
        return pl.BlockSpec((b, None, None, n_pairs, LANES, LANES),
                            lambda j: (0, mem_chunk(d, j), d, 0, 0, 0))

    def row_spec(d):
        return pl.BlockSpec((b, None, CHUNK, bw), lambda j: (0, d, mem_chunk(d, j), 0))

    return pl.pallas_call(
        _rwkv_scan_kernel,
        grid=(nc,),
        in_specs=[mn_spec(0), mn_spec(1), mn_spec(0), mn_spec(1),
                  row_spec(0), row_spec(1), row_spec(0), row_spec(1)],
        out_specs=[pl.BlockSpec((b, CHUNK, bw), lambda j: (0, mem_chunk(0, j), 0)),
                   pl.BlockSpec((b, CHUNK, bw), lambda j: (0, mem_chunk(1, j), 0))],
        out_shape=[jax.ShapeDtypeStruct((b, s, bw), F32), jax.ShapeDtypeStruct((b, s, bw), F32)],
        scratch_shapes=[pltpu.VMEM((2, b, n_pairs, LANES, LANES), F32)],
        compiler_params=_cparams(("arbitrary",)),
        name="rwkv_scan",
    )(m, m, n, n, q, q, y0, y0)


def _outproj_kernel(*refs, two_sources, ctx_tiles, rwkv):
    n_src = 2 if two_sources else 1
    src = refs[:n_src]
    at_ref, g_ref, w_ref, mod_ref, gpost_ref = refs[n_src:n_src + 5]
    rest = refs[n_src + 5:]
    out_ref = rest[-1]
    hq, dh, tm = at_ref.shape
    parts = [at_ref[...].reshape(hq * dh, tm).T]
    if rwkv:
        yf_ref, yb_ref, bonus_ref, seg_ref, gnw_ref, gnb_ref = rest[:6]
        seg = seg_ref[...]
        y_sum = bonus_ref[...]
        for y_ref in (yf_ref, yb_ref):
            y = y_ref[...]
            yc = y - _head_mean(y, seg)
            var = _head_mean(yc * yc, seg)
            y_sum = y_sum + yc * lax.rsqrt(var + GN_EPS) * gnw_ref[...] + gnb_ref[...]
        parts.append(y_sum)
    o = jnp.concatenate(parts, axis=1) if len(parts) > 1 else parts[0]
    u = (o * _silu(g_ref[...])).astype(BF16)
    y = jnp.dot(u, w_ref[...], preferred_element_type=F32)
    ms = jnp.mean(y * y, axis=-1, keepdims=True)
    yn = y * lax.rsqrt(ms + RMS_EPS) * gpost_ref[...]
    x = _stream_rows(src[0] if two_sources else None, src[-1], ctx_tiles, pl.program_id(1))
    out_ref[...] = x + mod_ref[2:3, :] * yn


def _outproj(sources, attn_t, gate, w_bf16, modsel, gpost, rwkv_parts, *, ctx_tiles, latent_only):
    two_sources = len(sources) == 2
    b, d = sources[0].shape[0], sources[0].shape[2]
    s = sum(a.shape[1] for a in sources)
    tm = ROW_TILE
    off = ctx_tiles if latent_only else 0
    n_tiles = s // tm - off
    assert not (two_sources and latent_only)
    row = lambda bi, i: (bi, i + off, 0)
    const = lambda bi, i: (0, 0)
    hq, dh = attn_t.shape[1], attn_t.shape[2]
    if two_sources:
        src_specs = _stream_specs(True, tm, d, ctx_tiles)
    else:
        src_specs = [pl.BlockSpec((None, tm, d), row)]
    in_specs = src_specs + [
        pl.BlockSpec((None, hq, dh, tm), lambda bi, i: (bi, 0, 0, i)),
        pl.BlockSpec((None, tm, gate.shape[2]), row),
        pl.BlockSpec(w_bf16.shape, const),
        pl.BlockSpec((None, None, 3, d),
                     lambda bi, i: (bi, jnp.minimum((i + off) // ctx_tiles, 1), 0, 0)),
        pl.BlockSpec((1, d), const)]
    args = [*sources, attn_t, gate, w_bf16, modsel, gpost]
    if rwkv_parts is not None:
        y_f, y_b, bonus, seg, gnw, gnb = rwkv_parts
        in_specs += [pl.BlockSpec((None, tm, y_f.shape[2]), row),
                     pl.BlockSpec((None, tm, y_b.shape[2]), row),
                     pl.BlockSpec((None, tm, bonus.shape[2]), row),
                     pl.BlockSpec((MXU_WIDTH, MXU_WIDTH), const),
                     pl.BlockSpec(gnw.shape, const),
                     pl.BlockSpec(gnb.shape, const)]
        args += [y_f, y_b, bonus, seg, gnw, gnb]
    kern = functools.partial(_outproj_kernel, two_sources=two_sources, ctx_tiles=ctx_tiles,
                             rwkv=rwkv_parts is not None)
    return pl.pallas_call(
        kern,
        grid=(b, n_tiles),
        in_specs=in_specs,
        out_specs=pl.BlockSpec((None, tm, d), lambda bi, i: (bi, i, 0)),
        out_shape=jax.ShapeDtypeStruct((b, n_tiles * tm, d), F32),
        compiler_params=_cparams(("parallel", "parallel")),
        name="outproj",
    )(*args)


def _rope_tables(n_latent, ctx_len):
    t = jnp.arange(n_latent)
    rowp = (t // GRID_W).astype(F32)
    colp = (t % GRID_W).astype(F32)
    axis_dim = HEAD_DIM // 2
    inv = ROPE_THETA ** (-jnp.arange(0, axis_dim, 2, dtype=F32) / axis_dim)
    ang = jnp.concatenate([rowp[:, None] * inv, colp[:, None] * inv], axis=-1)
    cos, sin = jnp.cos(ang), jnp.sin(ang)
    cos = jnp.concatenate([jnp.ones((ctx_len, axis_dim), F32), cos], axis=0)
    sin = jnp.concatenate([jnp.zeros((ctx_len, axis_dim), F32), sin], axis=0)
    cos_h = jnp.repeat(cos, 2, axis=1)
    sin_h = jnp.stack([-sin, sin], axis=-1).reshape(sin.shape[0], HEAD_DIM)
    return jnp.tile(cos_h, (1, LANES // HEAD_DIM)), jnp.tile(sin_h, (1, LANES // HEAD_DIM))


def _block_diag_lora(w2):
    r, w = w2.shape[1], w2.shape[2]
    z = jnp.zeros((r, w), w2.dtype)
    return jnp.concatenate([jnp.concatenate([w2[0], z], axis=1),
                            jnp.concatenate([z, w2[1]], axis=1)], axis=0)


def kernel(x, c, ctx, c_ctx, w_mod, b_mod, g_pre, g_post, w_in_even, w_out_even, qn_a, kn_a, mu_b, w0_b, w2_b, a0_b, a2_b, kk_b, ka_b, rk_b, gn_w_b, gn_b_b, w_in_odd, w_out_odd, qn_c, kn_c, sink_c):
    b, t, d = x.shape
    ctx_len = ctx.shape[1]
    s = ctx_len + t
    assert ctx_len % ROW_TILE == 0 and t % ROW_TILE == 0 and b + 1 <= 8
    ctx_tiles = ctx_len // ROW_TILE
    depth = w_mod.shape[0]

    cc = jnp.concatenate([c, c_ctx[None, :], jnp.zeros((8 - b - 1, d), F32)], axis=0)
    mod = _modulation(cc, w_mod, b_mod)
    mod = mod.reshape(depth, 8, 3, d)
    modsel = jnp.stack([jnp.broadcast_to(mod[:, b][:, None], (depth, b, 3, d)), mod[:, :b]], axis=2)

    cos_t, sin_t = _rope_tables(t, ctx_len)
    seg_mean = jnp.asarray(np.kron(np.eye(MXU_WIDTH // HEAD_DIM),
                                   np.full((HEAD_DIM, HEAD_DIM), 1.0 / HEAD_DIM)), BF16)
    seg_ones = jnp.asarray(np.kron(np.eye(2), np.ones((HEAD_DIM, HEAD_DIM))), BF16)

    bw = kk_b.shape[1]
    a_width = w_out_even.shape[1] - bw
    a_heads = a_width // HEAD_DIM
    n_in = w_in_even.shape[2]
    b_proj = 3 * bw + 4 * B_LORA
    kv_width = (n_in - 2 * a_width - b_proj - bw) // 2
    a_kv_heads = kv_width // HEAD_DIM
    qk_width = a_width + kv_width
    gain0 = jnp.concatenate([jnp.tile(qn_a[0], a_heads), jnp.tile(kn_a[0], a_kv_heads)])[None, :]
    bp_lo = qk_width + kv_width
    g_lo = bp_lo + b_proj
    qh, kh, vth, bproj, gate0 = _inproj(
        [ctx, x], modsel[0], g_pre[0][None, :], w_in_even[0].astype(BF16), cos_t, sin_t, gain0, seg_mean,
        q_width=a_width, qk_width=qk_width, v_width=kv_width,
        splits=((bp_lo, g_lo), (g_lo, n_in)), out_dtypes=(F32, F32), ctx_tiles=ctx_tiles)
    oa_t = _dense_attention(qh, kh, vth, ctx_len=ctx_len)

    m_c, n_c, q_c, y0_c, bonus = _rwkv_prep(
        bproj, mu_b[0][None, :], w0_b[0].reshape(1, 2 * bw), _block_diag_lora(w2_b[0]).astype(BF16),
        a0_b[0].reshape(1, 2 * bw), _block_diag_lora(a2_b[0]).astype(BF16), kk_b[0][None, :], ka_b[0][None, :],
        rk_b[0].reshape(1, bw), seg_ones, ctx_len=ctx_len)
    y_f, y_b = _rwkv_scan(m_c, n_c, q_c, y0_c, ctx_len=ctx_len)

    xc = _outproj([ctx, x], oa_t, gate0, w_out_even[0].astype(BF16), modsel[0], g_post[0][None, :],
                  (y_f, y_b, bonus, seg_mean, gn_w_b[0][None, :], gn_b_b[0][None, :]),
                  ctx_tiles=ctx_tiles, latent_only=False)

    c_heads = sink_c.shape[1]
    c_width = c_heads * HEAD_DIM
    n_in1 = w_in_odd.shape[2]
    ckv_width = (n_in1 - 2 * c_width) // 2
    c_kv_heads = ckv_width // HEAD_DIM
    qk_width1 = c_width + ckv_width
    gain1 = jnp.concatenate([jnp.tile(qn_c[0], c_heads), jnp.tile(kn_c[0], c_kv_heads)])[None, :]
    qh1, kh1, vth1, gate1 = _inproj(
        [xc], modsel[1], g_pre[1][None, :], w_in_odd[0].astype(BF16), cos_t, sin_t, gain1, seg_mean,
        q_width=c_width, qk_width=qk_width1, v_width=ckv_width,
        splits=((qk_width1 + ckv_width, n_in1),), out_dtypes=(F32,), ctx_tiles=ctx_tiles)
    ow_t = _window_attention(qh1, kh1, vth1, sink_c[0], ctx_len=ctx_len)
    return _outproj([xc], ow_t, gate1, w_out_odd[0].astype(BF16), modsel[1], g_post[1][None, :], None,
                    ctx_tiles=ctx_tiles, latent_only=True)
```

```python
import functools

import numpy as np
import jax
import jax.numpy as jnp
from jax import lax
from jax.experimental import pallas as pl
from jax.experimental.pallas import tpu as pltpu

F32 = jnp.float32
BF16 = jnp.bfloat16
HIGHEST = lax.Precision.HIGHEST

HEAD_DIM = 64
LANES = 128
MXU_WIDTH = 256
RWKV_GROUP_LANES = 128
GRID_W = 64
Q_BLOCK = 128
WINDOW = 128
ROPE_THETA = 10000.0
RMS_EPS = 1e-6
GN_EPS = 64e-5
LOG2_E = float(np.log2(np.e))
Q_SCALE = HEAD_DIM ** -0.5 * LOG2_E
B_LORA = 64
CHUNK = 64
PREP_CHUNKS = 4
ROW_TILE = 256
PROJ_COL_CHUNK = 512
DENSE_Q_TILE = 256
DENSE_KV_TILE = 1024
WINDOW_BLOCKS = 16
VMEM_LIMIT = 56 * 1024 * 1024


def _cparams(sem):
    return pltpu.CompilerParams(dimension_semantics=sem, vmem_limit_bytes=VMEM_LIMIT)


def _split_dot(a, g_bf16):
    hi = a.astype(BF16)
    lo = (a - hi.astype(F32)).astype(BF16)
    return (jnp.dot(hi, g_bf16, preferred_element_type=F32)
            + jnp.dot(lo, g_bf16, preferred_element_type=F32))


def _head_mean(x, seg):
    width = x.shape[1]
    cols = []
    for lo in range(0, width, MXU_WIDTH):
        n = min(MXU_WIDTH, width - lo)
        cols.append(jnp.dot(x[:, lo:lo + n].astype(BF16), seg[0:n, 0:n], preferred_element_type=F32))
    return cols[0] if len(cols) == 1 else jnp.concatenate(cols, axis=1)


def _sigmoid(z):
    return 1.0 / (1.0 + jnp.exp(-z))


def _silu(z):
    return z * _sigmoid(z)


def _mod_kernel(c_ref, w_ref, b_ref, o_ref):
    o_ref[...] = jnp.dot(_silu(c_ref[...]), w_ref[...], precision=HIGHEST,
                         preferred_element_type=F32) + b_ref[...]


def _modulation(cc, w_mod, b_mod):
    depth, d, d3 = w_mod.shape
    nj = d3 // d
    return pl.pallas_call(
        _mod_kernel,
        grid=(depth, nj),
        in_specs=[pl.BlockSpec((8, d), lambda l, j: (0, 0)),
                  pl.BlockSpec((None, d, d), lambda l, j: (l, 0, j)),
                  pl.BlockSpec((None, 1, d), lambda l, j: (l, 0, j))],
        out_specs=pl.BlockSpec((None, 8, d), lambda l, j: (l, 0, j)),
        out_shape=jax.ShapeDtypeStruct((depth, 8, d3), F32),
        compiler_params=_cparams(("arbitrary", "arbitrary")),
        name="modulation",
    )(cc, w_mod, b_mod.reshape(depth, 1, d3))


def _stream_rows(ctx_ref, x_ref, ctx_tiles, tile):
    if ctx_ref is None:
        return x_ref[...]
    return jnp.where(tile < ctx_tiles, ctx_ref[...], x_ref[...])


def _stream_specs(two_sources, tm, d, ctx_tiles, tile_of=lambda i: i):
    if not two_sources:
        return [pl.BlockSpec((None, tm, d), lambda bi, i: (bi, tile_of(i), 0))]
    return [pl.BlockSpec((None, tm, d), lambda bi, i: (bi, jnp.minimum(tile_of(i), ctx_tiles - 1), 0)),
            pl.BlockSpec((None, tm, d), lambda bi, i: (bi, jnp.maximum(tile_of(i) - ctx_tiles, 0), 0))]


def _interleave(*stages):
    stages = list(stages)
    while stages:
        for stage in list(stages):
            if next(stage, "done") == "done":
                stages.remove(stage)


def _inproj_kernel(*refs, two_sources, ctx_tiles, n_tiles, q_width, qk_width, v_width, splits):
    n_src = 2 if two_sources else 1
    src = refs[:n_src]
    mod_ref, gpre_ref, w_ref, cos_ref, sin_ref, gain_ref, seg_ref = refs[n_src:n_src + 7]
    q_ref, k_ref, vt_ref = refs[n_src + 7:n_src + 10]
    extra_refs = refs[n_src + 10:-1]
    attn_s = refs[-1]
    step = pl.program_id(1)
    tile = jnp.minimum(step, n_tiles - 1)
    n_attn = qk_width + v_width

    @pl.when(step == 0)
    def _():
        attn_s[...] = jnp.zeros_like(attn_s)

    staged = []

    def project():
        x = _stream_rows(src[0] if two_sources else None, src[-1], ctx_tiles, tile)
        ms = jnp.mean(x * x, axis=-1, keepdims=True)
        h = x * lax.rsqrt(ms + RMS_EPS) * gpre_ref[...]
        hb = (h * (1.0 + mod_ref[1:2, :]) + mod_ref[0:1, :]).astype(BF16)
        yield
        for lo in range(0, n_attn, PROJ_COL_CHUNK):
            hi = min(lo + PROJ_COL_CHUNK, n_attn)
            staged.append((lo, hi, jnp.dot(hb, w_ref[:, lo:hi], preferred_element_type=F32)))
            yield
        for ref, (s_lo, s_hi) in zip(extra_refs, splits):
            for lo in range(s_lo, s_hi, PROJ_COL_CHUNK):
                hi = min(lo + PROJ_COL_CHUNK, s_hi)
                ref[:, lo - s_lo:hi - s_lo] = jnp.dot(hb, w_ref[:, lo:hi],
                                                      preferred_element_type=F32).astype(ref.dtype)
                yield

    def finish_previous():
        cos = cos_ref[...]
        sin = sin_ref[...]
        lane = lax.broadcasted_iota(jnp.int32, (1, LANES), 1)
        even_lane = (lane % 2) == 0
        qk = attn_s[:, 0:qk_width]
        msq_all = _head_mean(qk * qk, seg_ref[...])
        yield
        for g in range(qk_width // LANES):
            xg = attn_s[:, g * LANES:(g + 1) * LANES]
            msq = msq_all[:, g * LANES:(g + 1) * LANES]
            xn = xg * lax.rsqrt(msq + RMS_EPS) * gain_ref[:, g * LANES:(g + 1) * LANES]
            partner = jnp.where(even_lane, pltpu.roll(xn, LANES - 1, 1), pltpu.roll(xn, 1, 1))
            y = xn * cos + partner * sin
            is_q = g * LANES < q_width
            if is_q:
                y = y * Q_SCALE
            ref = q_ref if is_q else k_ref
            h0 = 2 * g if is_q else 2 * (g - q_width // LANES)
            ref[h0] = y[:, 0:HEAD_DIM].astype(ref.dtype)
            ref[h0 + 1] = y[:, HEAD_DIM:LANES].astype(ref.dtype)
            yield
        for g in range(v_width // LANES):
            vt = attn_s[:, qk_width + g * LANES:qk_width + (g + 1) * LANES].T
            vt_ref[2 * g] = vt[0:HEAD_DIM].astype(vt_ref.dtype)
            vt_ref[2 * g + 1] = vt[HEAD_DIM:LANES].astype(vt_ref.dtype)
            yield

    _interleave(project(), finish_previous())
    for lo, hi, acc in staged:
        attn_s[:, lo:hi] = acc


def _inproj(sources, modsel, gpre, w_bf16, cos_t, sin_t, gain, seg, *, q_width, qk_width, v_width,
            splits, out_dtypes, ctx_tiles):
    two_sources = len(sources) == 2
    b, d = sources[0].shape[0], sources[0].shape[2]
    s = sum(a.shape[1] for a in sources)
    n = w_bf16.shape[1]
    tm = ROW_TILE
    hq = q_width // HEAD_DIM
    hk = (qk_width - q_width) // HEAD_DIM
    hv = v_width // HEAD_DIM
    n_tiles = s // tm
    cur = lambda i: jnp.minimum(i, n_tiles - 1)
    done = lambda i: jnp.maximum(i - 1, 0)
    const = lambda bi, i: (0, 0)
    out_shapes = [jax.ShapeDtypeStruct((b, hq, s, HEAD_DIM), BF16),
                  jax.ShapeDtypeStruct((b, hk, s, HEAD_DIM), BF16),
                  jax.ShapeDtypeStruct((b, hv, HEAD_DIM, s), BF16)]
    out_specs = [pl.BlockSpec((None, hq, tm, HEAD_DIM), lambda bi, i: (bi, 0, done(i), 0)),
                 pl.BlockSpec((None, hk, tm, HEAD_DIM), lambda bi, i: (bi, 0, done(i), 0)),
                 pl.BlockSpec((None, hv, HEAD_DIM, tm), lambda bi, i: (bi, 0, 0, done(i)))]
    for (lo, hi), dt in zip(splits, out_dtypes):
        out_shapes.append(jax.ShapeDtypeStruct((b, s, hi - lo), dt))
        out_specs.append(pl.BlockSpec((None, tm, hi - lo), lambda bi, i: (bi, cur(i), 0)))
    kern = functools.partial(_inproj_kernel, two_sources=two_sources, ctx_tiles=ctx_tiles, n_tiles=n_tiles,
                             q_width=q_width, qk_width=qk_width, v_width=v_width, splits=splits)
    return pl.pallas_call(
        kern,
        grid=(b, n_tiles + 1),
        in_specs=_stream_specs(two_sources, tm, d, ctx_tiles, cur) + [
            pl.BlockSpec((None, None, 3, d),
                         lambda bi, i: (bi, jnp.minimum(cur(i) // ctx_tiles, 1), 0, 0)),
            pl.BlockSpec((1, d), const),
            pl.BlockSpec((d, n), const),
            pl.BlockSpec((tm, LANES), lambda bi, i: (done(i), 0)),
            pl.BlockSpec((tm, LANES), lambda bi, i: (done(i), 0)),
            pl.BlockSpec((1, qk_width), const),
            pl.BlockSpec((MXU_WIDTH, MXU_WIDTH), const)],
        out_specs=out_specs,
        out_shape=out_shapes,
        scratch_shapes=[pltpu.VMEM((tm, qk_width + v_width), F32)],
        compiler_params=_cparams(("parallel", "arbitrary")),
        name="inproj",
    )(*sources, modsel, gpre, w_bf16, cos_t, sin_t, gain, seg)


def _dense_attn_kernel(q_ref, k_ref, vt_ref, o_ref, *, tk, ctx_len):
    i = pl.program_id(2)
    r, tq, dh = q_ref.shape
    m_rows = r * tq
    n_keys = k_ref.shape[0]
    q = q_ref[...].reshape(m_rows, dh)
    contract_lanes = (((1,), (1,)), ((), ()))

    def scores(lo, hi):
        return lax.dot_general(k_ref[lo:hi, :], q, contract_lanes, preferred_element_type=F32)

    def attend(bounds):
        m = jnp.full((1, m_rows), -jnp.inf, F32)
        l = jnp.zeros((1, m_rows), F32)
        acc = jnp.zeros((dh, m_rows), F32)
        st = scores(*bounds[0])
        pending = None
        for j in range(len(bounds)):
            if pending is not None:
                (plo, phi), p_prev = pending
                acc = acc + jnp.dot(vt_ref[:, plo:phi], p_prev, preferred_element_type=F32)
            st_next = scores(*bounds[j + 1]) if j + 1 < len(bounds) else None
            m_new = jnp.maximum(m, jnp.max(st, axis=0, keepdims=True))
            alpha = jnp.exp2(m - m_new)
            p = jnp.exp2(st - m_new)
            l = alpha * l + jnp.sum(p, axis=0, keepdims=True)
            acc = acc * alpha
            pending = (bounds[j], p.astype(BF16))
            m, st = m_new, st_next
        (plo, phi), p_prev = pending
        acc = acc + jnp.dot(vt_ref[:, plo:phi], p_prev, preferred_element_type=F32)
        o = acc / l
        for hh in range(r):
            o_ref[hh] = o[:, hh * tq:(hh + 1) * tq].astype(o_ref.dtype)

    ctx_bounds = [(0, ctx_len)]
    all_bounds = ctx_bounds + [(lo, lo + tk) for lo in range(ctx_len, n_keys, tk)]

    @pl.when(i < ctx_len // tq)
    def _():
        attend(ctx_bounds)

    @pl.when(i >= ctx_len // tq)
    def _():
        attend(all_bounds)


def _dense_attention(qh, kh, vth, *, ctx_len):
    b, hq, s, dh = qh.shape
    g = kh.shape[1]
    r = hq // g
    tq = DENSE_Q_TILE
    assert (s - ctx_len) % DENSE_KV_TILE == 0 and ctx_len % tq == 0
    kern = functools.partial(_dense_attn_kernel, tk=DENSE_KV_TILE, ctx_len=ctx_len)
    return pl.pallas_call(
        kern,
        grid=(b, g, s // tq),
        in_specs=[pl.BlockSpec((None, r, tq, dh), lambda bi, gi, i: (bi, gi, i, 0)),
                  pl.BlockSpec((None, None, s, dh), lambda bi, gi, i: (bi, gi, 0, 0)),
                  pl.BlockSpec((None, None, dh, s), lambda bi, gi, i: (bi, gi, 0, 0))],
        out_specs=pl.BlockSpec((None, r, dh, tq), lambda bi, gi, i: (bi, gi, 0, i)),
        out_shape=jax.ShapeDtypeStruct((b, hq, dh, s), F32),
        compiler_params=_cparams(("parallel", "parallel", "parallel")),
        name="dense_attention",
    )(qh, kh, vth)


def _window_attn_kernel(sink_ref, q_ref, bias_first_ref, bias_mid_ref, bias_last_ref,
                        kc_ref, kp_ref, km_ref, kn_ref, vc_ref, vp_ref, vm_ref, vn_ref, o_ref):
    gi = pl.program_id(1)
    _, r, rows, dh = q_ref.shape
    tq = Q_BLOCK
    nb = rows // tq
    contract_lanes = (((1,), (1,)), ((), ()))
    k_all = jnp.concatenate([kp_ref[...], km_ref[0, 0], kn_ref[...]], axis=0)
    vt_all = jnp.concatenate([vp_ref[...], vm_ref[0, 0], vn_ref[...]], axis=1)
    k_ctx, vt_ctx = kc_ref[...], vc_ref[...]
    lane = lax.broadcasted_iota(jnp.int32, (1, r * tq), 1)
    sink = jnp.zeros((1, r * tq), F32)
    for hh in range(r):
        sink = jnp.where(lane // tq == hh, sink_ref[gi * r + hh] * LOG2_E, sink)

    def scores(qb):
        q = jnp.concatenate([q_ref[0, hh, qb * tq:(qb + 1) * tq, :] for hh in range(r)], axis=0)
        bias_ref = bias_first_ref if qb == 0 else (bias_last_ref if qb == nb - 1 else bias_mid_ref)
        s_loc = lax.dot_general(k_all[qb * tq:(qb + 3) * tq], q, contract_lanes,
                                preferred_element_type=F32) + bias_ref[...]
        s_ctx = lax.dot_general(k_ctx, q, contract_lanes, preferred_element_type=F32)
        return s_loc, s_ctx

    def finish(qb, p_loc, p_ctx, l):
        o = (jnp.dot(vt_all[:, qb * tq:(qb + 3) * tq], p_loc, preferred_element_type=F32)
             + jnp.dot(vt_ctx, p_ctx, preferred_element_type=F32)) / l
        for hh in range(r):
            o_ref[hh, :, qb * tq:(qb + 1) * tq] = o[:, hh * tq:(hh + 1) * tq].astype(o_ref.dtype)

    s_cur = scores(0)
    pending = None
    for qb in range(nb):
        if pending is not None:
            finish(*pending)
        s_next = scores(qb + 1) if qb + 1 < nb else None
        s_loc, s_ctx = s_cur
        m = jnp.maximum(jnp.maximum(jnp.max(s_loc, axis=0, keepdims=True),
                                    jnp.max(s_ctx, axis=0, keepdims=True)), sink)
        p_loc = jnp.exp2(s_loc - m)
        p_ctx = jnp.exp2(s_ctx - m)
        l = (jnp.sum(p_loc, axis=0, keepdims=True) + jnp.sum(p_ctx, axis=0, keepdims=True)
             + jnp.exp2(sink - m))
        pending = (qb, p_loc.astype(BF16), p_ctx.astype(BF16), l)
        s_cur = s_next
    finish(*pending)


def _window_bias(tq, r):
    key = np.arange(3 * tq)[:, None] - tq
    qpos = np.arange(tq)[None, :]
    band = np.abs(key - qpos) <= WINDOW
    variants = [band & (key >= 0), band, band & (key < tq)]
    table = np.stack([np.where(np.tile(v, (1, r)), 0.0, -np.inf) for v in variants])
    return jnp.asarray(table, F32)


def _window_attention(qh, kh, vth, sink, *, ctx_len):
    b, hq, s, dh = qh.shape
    g = kh.shape[1]
    r = hq // g
    tq = Q_BLOCK
    t = s - ctx_len
    nb = WINDOW_BLOCKS
    rows = nb * tq
    assert t % rows == 0 and ctx_len % tq == 0
    ns = t // rows
    off = ctx_len // tq
    last = t // tq - 1
    bias = _window_bias(tq, r)
    ctx = lambda bi, gi, i: (bi, gi, 0, 0)
    blk = lambda n_rows: (None, None, n_rows, dh)
    tblk = lambda n_cols: (None, None, dh, n_cols)
    prev_blk = lambda i: jnp.maximum(nb * i - 1, 0) + off
    next_blk = lambda i: jnp.minimum(nb * i + nb, last) + off
    el = pl.Element
    mid = lambda i: pl.multiple_of(ctx_len + i * rows, tq)
    return pl.pallas_call(
        _window_attn_kernel,
        grid=(b, g, ns),
        in_specs=[pl.BlockSpec(memory_space=pltpu.SMEM),
                  pl.BlockSpec((el(1), el(r), el(rows), el(dh)), lambda bi, gi, i: (bi, gi * r, mid(i), 0)),
                  pl.BlockSpec((None, 3 * tq, r * tq), lambda bi, gi, i: (jnp.minimum(i, 1), 0, 0)),
                  pl.BlockSpec((None, 3 * tq, r * tq), lambda bi, gi, i: (1, 0, 0)),
                  pl.BlockSpec((None, 3 * tq, r * tq),
                               lambda bi, gi, i: (jnp.where(i == ns - 1, 2, 1), 0, 0)),
                  pl.BlockSpec(blk(ctx_len), ctx),
                  pl.BlockSpec(blk(tq), lambda bi, gi, i: (bi, gi, prev_blk(i), 0)),
                  pl.BlockSpec((el(1), el(1), el(rows), el(dh)), lambda bi, gi, i: (bi, gi, mid(i), 0)),
                  pl.BlockSpec(blk(tq), lambda bi, gi, i: (bi, gi, next_blk(i), 0)),
                  pl.BlockSpec(tblk(ctx_len), ctx),
                  pl.BlockSpec(tblk(tq), lambda bi, gi, i: (bi, gi, 0, prev_blk(i))),
                  pl.BlockSpec((el(1), el(1), el(dh), el(rows)), lambda bi, gi, i: (bi, gi, 0, mid(i))),
                  pl.BlockSpec(tblk(tq), lambda bi, gi, i: (bi, gi, 0, next_blk(i)))],
        out_specs=pl.BlockSpec((None, r, dh, rows), lambda bi, gi, i: (bi, gi, 0, i)),
        out_shape=jax.ShapeDtypeStruct((b, hq, dh, t), F32),
        compiler_params=_cparams(("parallel", "parallel", "parallel")),
        name="window_attention",
    )(sink, qh, bias, bias, bias, kh, kh, kh, kh, vth, vth, vth, vth)


def _rwkv_prep_kernel(p_ref, hp_ref, hn_ref, mu_ref, w0_ref, w2_ref, a0_ref, a2_ref,
                      kk_ref, ka_ref, rk_ref, ones_ref,
                      m_ref, n_ref, q_ref, y0_ref, bonus_ref,
                      xf_s, kh_s, bh_s, kbar_s, bbar_s, rt_s, v_s, wend_s, *, ctx_chunks, n_chunks):
    step_id = pl.program_id(1)
    sub = PREP_CHUNKS
    c = CHUNK
    rows = sub * c
    jb = jnp.minimum(step_id, n_chunks // sub - 1)
    first_chunk = jb * sub
    last_chunk = first_chunk + sub - 1
    bw = kk_ref.shape[1]
    n_pairs = bw // LANES
    staged = (xf_s, kh_s, bh_s, kbar_s, bbar_s, rt_s, v_s, wend_s)

    @pl.when(step_id == 0)
    def _():
        for ref in staged:
            ref[...] = jnp.zeros_like(ref)

    gw = RWKV_GROUP_LANES
    hpg = gw // HEAD_DIM
    n_groups = bw // gw
    ti = lax.broadcasted_iota(jnp.int32, (rows, rows), 0)
    tj = lax.broadcasted_iota(jnp.int32, (rows, rows), 1)
    same_chunk = (ti // c) == (tj // c)
    ti_g = lax.broadcasted_iota(jnp.int32, (c, gw), 0)
    tj_g = lax.broadcasted_iota(jnp.int32, (c, gw), 1) % c
    eye_g = (ti_g == tj_g).astype(F32)
    gi_r = lax.broadcasted_iota(jnp.int32, (gw, gw), 0)
    gi_c = lax.broadcasted_iota(jnp.int32, (gw, gw), 1)
    group_diag = (gi_r // HEAD_DIM) == (gi_c // HEAD_DIM)
    bi = lax.broadcasted_iota(jnp.int32, (LANES, LANES), 0)
    bj = lax.broadcasted_iota(jnp.int32, (LANES, LANES), 1)
    same_head = (bi // HEAD_DIM) == (bj // HEAD_DIM)
    diag = bi == bj
    scan_masks = [(tj_g < ti_g, tj_g <= ti_g), (tj_g > ti_g, tj_g >= ti_g)]

    def blockdiag(xs):
        xb = xs.astype(BF16)
        return jnp.where(group_diag, jnp.concatenate([xb] * hpg, axis=0), jnp.zeros((), BF16))


    def elementwise_stage():
        p = p_ref[...]
        has_prev = jnp.logical_and(first_chunk != 0, first_chunk != ctx_chunks)
        has_next = jnp.logical_and(last_chunk != ctx_chunks - 1, last_chunk != n_chunks - 1)
        prev_row = jnp.where(has_prev, hp_ref[7:8, :], 0.0)
        next_row = jnp.where(has_next, hn_ref[0:1, :], 0.0)
        rowi = lax.broadcasted_iota(jnp.int32, (rows, 1), 0)
        p_prev = jnp.where(rowi == 0, prev_row, pltpu.roll(p, 1, 0))
        p_next = jnp.where(rowi == rows - 1, next_row, pltpu.roll(p, rows - 1, 0))
        ps = p + mu_ref[...] * (0.5 * (p_prev + p_next) - p)
        r = ps[:, 0:bw]
        k = ps[:, bw:2 * bw]
        v = ps[:, 2 * bw:3 * bw]
        zw = ps[:, 3 * bw:3 * bw + 2 * B_LORA]
        za = ps[:, 3 * bw + 2 * B_LORA:3 * bw + 4 * B_LORA]
        yield

        wl = w0_ref[...] + jnp.dot(jnp.tanh(zw).astype(BF16), w2_ref[...],
                                   preferred_element_type=F32)
        z = -wl
        softplus = jnp.maximum(z, 0.0) + jnp.log(1.0 + jnp.exp(-jnp.abs(z)))
        logw = -jnp.exp(-softplus - 0.5)
        yield
        a = _sigmoid(a0_ref[...] + jnp.dot(za.astype(BF16), a2_ref[...],
                                           preferred_element_type=F32))
        ones_seg = ones_ref[...]
        kf = k * kk_ref[...]
        kf2 = kf * kf
        ss = jnp.concatenate([_split_dot(kf2[:, g * LANES:(g + 1) * LANES], ones_seg)
                              for g in range(n_pairs)], axis=1)
        kk = kf / jnp.maximum(jnp.sqrt(ss), 1e-12)
        yield

        bonus = jnp.zeros((rows, bw), F32)
        chunk_rows = [slice(ch * c, (ch + 1) * c) for ch in range(sub)]
        out = dict(xf=[], kh=[], bh=[], kbar=[], bbar=[], rt=[], wend=[])
        for d in range(2):
            lw = logw[:, d * bw:(d + 1) * bw]
            a_d = a[:, d * bw:(d + 1) * bw]
            b_d = kk * a_d
            kmod = k * (1.0 + (a_d - 1.0) * ka_ref[...])
            rkk = r * kmod * rk_ref[...]
            bsum = jnp.concatenate([_split_dot(rkk[:, g * LANES:(g + 1) * LANES], ones_seg)
                                    for g in range(n_pairs)], axis=1)
            bonus = bonus + bsum * v
            yield

            tri = (same_chunk & ((tj <= ti) if d == 0 else (tj >= ti))).astype(BF16)
            lw_hi = lw.astype(BF16)
            lw_lo = (lw - lw_hi.astype(F32)).astype(BF16)
            cum = (jnp.dot(tri, lw_hi, preferred_element_type=F32)
                   + jnp.dot(tri, lw_lo, preferred_element_type=F32))
            ends = [cum[rs.stop - 1:rs.stop] if d == 0 else cum[rs.start:rs.start + 1] for rs in chunk_rows]
            cum_end = jnp.concatenate([jnp.broadcast_to(e, (c, bw)) for e in ends], axis=0)
            r_t = r * jnp.exp(cum)
            kk_t = kk * jnp.exp(cum - lw)
            yield
            e_out = jnp.exp(-cum)
            tail = jnp.exp(cum_end - cum)
            out["xf"].append(jnp.concatenate([part[rs] for rs in chunk_rows for part in (kk_t, r_t)],
                                             axis=0).astype(BF16))
            out["rt"].append(r_t)
            out["kh"].append((kmod * e_out).astype(BF16))
            out["bh"].append((b_d * e_out).astype(BF16))
            out["kbar"].append((kmod * tail).astype(BF16))
            out["bbar"].append((b_d * tail).astype(BF16))
            out["wend"].append(jnp.concatenate([jnp.broadcast_to(jnp.exp(e), (8, bw)) for e in ends], axis=0))
            yield
        bonus_ref[...] = bonus
        out["v"] = v.astype(BF16)
        staged_next.update(out)

    def matmul_stage():
        contract_lanes = (((1,), (1,)), ((), ()))
        contract_rows = (((0,), (0,)), ((), ()))
        probs = [(d, g, ch) for d in range(2) for g in range(n_groups) for ch in range(sub)]
        lanes_of = lambda g: slice(g * gw, (g + 1) * gw)
        rows_of = lambda ch: slice(ch * c, (ch + 1) * c)
        xf_of = lambda d, g, ch: xf_s[d, ch * 2 * c:(ch + 1) * 2 * c, lanes_of(g)]
        v_bd = {(g, ch): blockdiag(v_s[rows_of(ch), lanes_of(g)]) for g in range(n_groups) for ch in range(sub)}
        aks = [lax.dot_general(xf_of(d, g, ch), blockdiag(kh_s[d, rows_of(ch), lanes_of(g)]), contract_lanes,
                               preferred_element_type=F32) for d, g, ch in probs]
        yield
        abs_ = [lax.dot_general(xf_of(d, g, ch), blockdiag(bh_s[d, rows_of(ch), lanes_of(g)]), contract_lanes,
                                preferred_element_type=F32) for d, g, ch in probs]
        yield
        a_ks = [jnp.concatenate([jnp.where(scan_masks[d][0], ak[0:c], 0.0),
                                 jnp.where(scan_masks[d][1], ak[c:2 * c], 0.0)], axis=0).astype(BF16)
                for (d, g, ch), ak in zip(probs, aks)]
        a_qbs = [jnp.where(scan_masks[d][1], ab[c:2 * c], 0.0).astype(BF16)
                 for (d, g, ch), ab in zip(probs, abs_)]
        pws = [jnp.where(scan_masks[d][0], -ab[0:c], 0.0) for (d, g, ch), ab in zip(probs, abs_)]
        tinvs = [eye_g + pw for pw in pws]
        pws = [jnp.dot(pw.astype(BF16), blockdiag(pw), preferred_element_type=F32) for pw in pws]
        yield
        for _ in range(4):
            stacked = [jnp.dot(jnp.concatenate([t, pw], axis=0).astype(BF16), blockdiag(pw),
                               preferred_element_type=F32) for t, pw in zip(tinvs, pws)]
            tinvs = [t + st[0:c] for t, st in zip(tinvs, stacked)]
            pws = [st[c:2 * c] for st in stacked]
            yield
        tinvs = [(t + jnp.dot(t.astype(BF16), blockdiag(pw), preferred_element_type=F32)).astype(BF16)
                 for t, pw in zip(tinvs, pws)]
        avs = [jnp.dot(a_k, v_bd[g, ch], preferred_element_type=F32) for (d, g, ch), a_k in zip(probs, a_ks)]
        yield
        p1s = [jnp.dot(t, blockdiag(xf_of(d, g, ch)[0:c]), preferred_element_type=F32)
               for (d, g, ch), t in zip(probs, tinvs)]
        p2s = [jnp.dot(t, blockdiag(av[0:c]), preferred_element_type=F32) for t, av in zip(tinvs, avs)]
        yield
        qp1s = [jnp.dot(a_qb, blockdiag(p1), preferred_element_type=F32) for a_qb, p1 in zip(a_qbs, p1s)]
        qp2s = [jnp.dot(a_qb, blockdiag(p2), preferred_element_type=F32) for a_qb, p2 in zip(a_qbs, p2s)]
        for (d, g, ch), av, qp1, qp2 in zip(probs, avs, qp1s, qp2s):
            q_ref[d, rows_of(ch), lanes_of(g)] = (rt_s[d, rows_of(ch), lanes_of(g)] - qp1).astype(q_ref.dtype)
            y0_ref[d, rows_of(ch), lanes_of(g)] = av[c:2 * c] - qp2
        yield
        for d in range(2):
            for pr in range(n_pairs):
                for ch in range(sub):
                    sl = slice(pr * LANES, (pr + 1) * LANES)
                    pidx = probs.index((d, pr * LANES // gw, ch))
                    lo = pr * LANES % gw
                    p12 = jnp.concatenate([p1s[pidx][:, lo:lo + LANES], p2s[pidx][:, lo:lo + LANES]],
                                          axis=1).astype(BF16)
                    bp = lax.dot_general(bbar_s[d, rows_of(ch), sl], p12, contract_rows,
                                         preferred_element_type=F32)
                    kv = lax.dot_general(kbar_s[d, rows_of(ch), sl], v_s[rows_of(ch), sl], contract_rows,
                                         preferred_element_type=F32)
                    w_diag = jnp.where(diag, wend_s[d, ch * 8:ch * 8 + 1, sl], 0.0)
                    m_ref[ch, d, pr] = (w_diag - jnp.where(same_head, bp[:, 0:LANES], 0.0)).astype(m_ref.dtype)
                    n_ref[ch, d, pr] = jnp.where(same_head, kv - bp[:, LANES:2 * LANES], 0.0)
            yield

    staged_next = {}
    _interleave(matmul_stage(), elementwise_stage())
    for ref, key in ((xf_s, "xf"), (kh_s, "kh"), (bh_s, "bh"), (kbar_s, "kbar"), (bbar_s, "bbar"),
                     (rt_s, "rt"), (wend_s, "wend")):
        for d in range(2):
            ref[d] = staged_next[key][d]
    v_s[...] = staged_next["v"]


def _rwkv_prep(bp, mu, w0, w2cat, a0, a2cat, kkw, kaw, rkw, ones_seg, *, ctx_len):
    b, s, pw = bp.shape
    bw = kkw.shape[1]
    n_pairs = bw // LANES
    nc = s // CHUNK
    ctx_chunks = ctx_len // CHUNK
    sub = PREP_CHUNKS
    rows = sub * CHUNK
    assert nc % sub == 0 and ctx_chunks % sub == 0
    n_blocks = nc // sub
    rb = rows // 8
    kern = functools.partial(_rwkv_prep_kernel, ctx_chunks=ctx_chunks, n_chunks=nc)
    const = lambda bi, j: (0, 0)
    cur = lambda j: jnp.minimum(j, n_blocks - 1)
    done = lambda j: jnp.maximum(j - 1, 0)
    return pl.pallas_call(
        kern,
        grid=(b, n_blocks + 1),
        in_specs=[pl.BlockSpec((None, rows, pw), lambda bi, j: (bi, cur(j), 0)),
                  pl.BlockSpec((None, 8, pw), lambda bi, j: (bi, jnp.maximum(cur(j) * rb - 1, 0), 0)),
                  pl.BlockSpec((None, 8, pw),
                               lambda bi, j: (bi, jnp.minimum((cur(j) + 1) * rb, s // 8 - 1), 0)),
                  pl.BlockSpec((1, pw), const),
                  pl.BlockSpec((1, 2 * bw), const),
                  pl.BlockSpec((2 * B_LORA, 2 * bw), const),
                  pl.BlockSpec((1, 2 * bw), const),
                  pl.BlockSpec((2 * B_LORA, 2 * bw), const),
                  pl.BlockSpec((1, bw), const),
                  pl.BlockSpec((1, bw), const),
                  pl.BlockSpec((1, bw), const),
                  pl.BlockSpec((LANES, LANES), const)],
        out_specs=[pl.BlockSpec((None, sub, 2, n_pairs, LANES, LANES), lambda bi, j: (bi, done(j), 0, 0, 0, 0)),
                   pl.BlockSpec((None, sub, 2, n_pairs, LANES, LANES), lambda bi, j: (bi, done(j), 0, 0, 0, 0)),
                   pl.BlockSpec((None, 2, rows, bw), lambda bi, j: (bi, 0, done(j), 0)),
                   pl.BlockSpec((None, 2, rows, bw), lambda bi, j: (bi, 0, done(j), 0)),
                   pl.BlockSpec((None, rows, bw), lambda bi, j: (bi, cur(j), 0))],
        out_shape=[jax.ShapeDtypeStruct((b, nc, 2, n_pairs, LANES, LANES), BF16),
                   jax.ShapeDtypeStruct((b, nc, 2, n_pairs, LANES, LANES), F32),
                   jax.ShapeDtypeStruct((b, 2, s, bw), BF16),
                   jax.ShapeDtypeStruct((b, 2, s, bw), F32),
                   jax.ShapeDtypeStruct((b, s, bw), F32)],
        scratch_shapes=[pltpu.VMEM((2, 2 * rows, bw), BF16),
                        pltpu.VMEM((2, rows, bw), BF16),
                        pltpu.VMEM((2, rows, bw), BF16),
                        pltpu.VMEM((2, rows, bw), BF16),
                        pltpu.VMEM((2, rows, bw), BF16),
                        pltpu.VMEM((2, rows, bw), F32),
                        pltpu.VMEM((rows, bw), BF16),
                        pltpu.VMEM((2, 8 * sub, bw), F32)],
        compiler_params=_cparams(("parallel", "arbitrary")),
        name="rwkv_prep",
    )(bp, bp, bp, mu, w0, w2cat, a0, a2cat, kkw, kaw, rkw, ones_seg)


def _rwkv_scan_kernel(m0_ref, m1_ref, n0_ref, n1_ref, q0_ref, q1_ref, y00_ref, y01_ref,
                      o0_ref, o1_ref, h_ref):
    j = pl.program_id(0)

    @pl.when(j == 0)
    def _():
        h_ref[...] = jnp.zeros_like(h_ref)

    nb, n_pairs = m0_ref.shape[0], m0_ref.shape[1]
    dirs = ((m0_ref, n0_ref, q0_ref, y00_ref, o0_ref), (m1_ref, n1_ref, q1_ref, y01_ref, o1_ref))
    for d, (m_ref, n_ref, q_ref, y0_ref, o_ref) in enumerate(dirs):
        for bi in range(nb):
            for pr in range(n_pairs):
                sl = slice(pr * LANES, (pr + 1) * LANES)
                h = h_ref[d, bi, pr]
                hb = h.astype(BF16)
                o_ref[bi, :, sl] = y0_ref[bi, :, sl] + jnp.dot(
                    q_ref[bi, :, sl], hb, preferred_element_type=F32)
                h_ref[d, bi, pr] = n_ref[bi, pr] + jnp.dot(
                    m_ref[bi, pr], hb, preferred_element_type=F32)


def _rwkv_scan(m, n, q, y0, *, ctx_len):
    b, nc, _, n_pairs, _, _ = m.shape
    s, bw = q.shape[2], q.shape[3]
    cc = ctx_len // CHUNK

    def mem_chunk(d, j):
        if d == 0:
            return j
        return jnp.where(j < cc, cc - 1 - j, nc - 1 + cc - j)

    def mn_spec(d):
        return pl.BlockSpec((b, None, None, n_pairs, LANES, LANES),
                            lambda j: (0, mem_chunk(d, j), d, 0, 0, 0))

    def row_spec(d):
        return pl.BlockSpec((b, None, CHUNK, bw), lambda j: (0, d, mem_chunk(d, j), 0))

    return pl.pallas_call(
        _rwkv_scan_kernel,
        grid=(nc,),
        in_specs=[mn_spec(0), mn_spec(1), mn_spec(0), mn_spec(1),
                  row_spec(0), row_spec(1), row_spec(0), row_spec(1)],
        out_specs=[pl.BlockSpec((b, CHUNK, bw), lambda j: (0, mem_chunk(0, j), 0)),
                   pl.BlockSpec((b, CHUNK, bw), lambda j: (0, mem_chunk(1, j), 0))],
        out_shape=[jax.ShapeDtypeStruct((b, s, bw), F32), jax.ShapeDtypeStruct((b, s, bw), F32)],
        scratch_shapes=[pltpu.VMEM((2, b, n_pairs, LANES, LANES), F32)],
        compiler_params=_cparams(("arbitrary",)),
        name="rwkv_scan",
    )(m, m, n, n, q, q, y0, y0)


def _outproj_kernel(*refs, two_sources, ctx_tiles, rwkv):
    n_src = 2 if two_sources else 1
    src = refs[:n_src]
    at_ref, g_ref, w_ref, mod_ref, gpost_ref = refs[n_src:n_src + 5]
    rest = refs[n_src + 5:]
    out_ref = rest[-1]
    hq, dh, tm = at_ref.shape
    parts = [at_ref[...].reshape(hq * dh, tm).T]
    if rwkv:
        yf_ref, yb_ref, bonus_ref, seg_ref, gnw_ref, gnb_ref = rest[:6]
        seg = seg_ref[...]
        y_sum = bonus_ref[...]
        for y_ref in (yf_ref, yb_ref):
            y = y_ref[...]
            yc = y - _head_mean(y, seg)
            var = _head_mean(yc * yc, seg)
            y_sum = y_sum + yc * lax.rsqrt(var + GN_EPS) * gnw_ref[...] + gnb_ref[...]
        parts.append(y_sum)
    o = jnp.concatenate(parts, axis=1) if len(parts) > 1 else parts[0]
    u = (o * _silu(g_ref[...])).astype(BF16)
    y = jnp.dot(u, w_ref[...], preferred_element_type=F32)
    ms = jnp.mean(y * y, axis=-1, keepdims=True)
    yn = y * lax.rsqrt(ms + RMS_EPS) * gpost_ref[...]
    x = _stream_rows(src[0] if two_sources else None, src[-1], ctx_tiles, pl.program_id(1))
    out_ref[...] = x + mod_ref[2:3, :] * yn


def _outproj(sources, attn_t, gate, w_bf16, modsel, gpost, rwkv_parts, *, ctx_tiles, latent_only):
    two_sources = len(sources) == 2
    b, d = sources[0].shape[0], sources[0].shape[2]
    s = sum(a.shape[1] for a in sources)
    tm = ROW_TILE
    off = ctx_tiles if latent_only else 0
    n_tiles = s // tm - off
    assert not (two_sources and latent_only)
    row = lambda bi, i: (bi, i + off, 0)
    const = lambda bi, i: (0, 0)
    hq, dh = attn_t.shape[1], attn_t.shape[2]
    if two_sources:
        src_specs = _stream_specs(True, tm, d, ctx_tiles)
    else:
        src_specs = [pl.BlockSpec((None, tm, d), row)]
    in_specs = src_specs + [
        pl.BlockSpec((None, hq, dh, tm), lambda bi, i: (bi, 0, 0, i)),
        pl.BlockSpec((None, tm, gate.shape[2]), row),
        pl.BlockSpec(w_bf16.shape, const),
        pl.BlockSpec((None, None, 3, d),
                     lambda bi, i: (bi, jnp.minimum((i + off) // ctx_tiles, 1), 0, 0)),
        pl.BlockSpec((1, d), const)]
    args = [*sources, attn_t, gate, w_bf16, modsel, gpost]
    if rwkv_parts is not None:
        y_f, y_b, bonus, seg, gnw, gnb = rwkv_parts
        in_specs += [pl.BlockSpec((None, tm, y_f.shape[2]), row),
                     pl.BlockSpec((None, tm, y_b.shape[2]), row),
                     pl.BlockSpec((None, tm, bonus.shape[2]), row),
                     pl.BlockSpec((MXU_WIDTH, MXU_WIDTH), const),
                     pl.BlockSpec(gnw.shape, const),
                     pl.BlockSpec(gnb.shape, const)]
        args += [y_f, y_b, bonus, seg, gnw, gnb]
    kern = functools.partial(_outproj_kernel, two_sources=two_sources, ctx_tiles=ctx_tiles,
                             rwkv=rwkv_parts is not None)
    return pl.pallas_call(
        kern,
        grid=(b, n_tiles),
        in_specs=in_specs,
        out_specs=pl.BlockSpec((None, tm, d), lambda bi, i: (bi, i, 0)),
        out_shape=jax.ShapeDtypeStruct((b, n_tiles * tm, d), F32),
        compiler_params=_cparams(("parallel", "parallel")),
        name="outproj",
    )(*args)


def _rope_tables(n_latent, ctx_len):
    t = jnp.arange(n_latent)
    rowp = (t // GRID_W).astype(F32)
    colp = (t % GRID_W).astype(F32)
    axis_dim = HEAD_DIM // 2
    inv = ROPE_THETA ** (-jnp.arange(0, axis_dim, 2, dtype=F32) / axis_dim)
    ang = jnp.concatenate([rowp[:, None] * inv, colp[:, None] * inv], axis=-1)
    cos, sin = jnp.cos(ang), jnp.sin(ang)
    cos = jnp.concatenate([jnp.ones((ctx_len, axis_dim), F32), cos], axis=0)
    sin = jnp.concatenate([jnp.zeros((ctx_len, axis_dim), F32), sin], axis=0)
    cos_h = jnp.repeat(cos, 2, axis=1)
    sin_h = jnp.stack([-sin, sin], axis=-1).reshape(sin.shape[0], HEAD_DIM)
    return jnp.tile(cos_h, (1, LANES // HEAD_DIM)), jnp.tile(sin_h, (1, LANES // HEAD_DIM))


def _block_diag_lora(w2):
    r, w = w2.shape[1], w2.shape[2]
    z = jnp.zeros((r, w), w2.dtype)
    return jnp.concatenate([jnp.concatenate([w2[0], z], axis=1),
                            jnp.concatenate([z, w2[1]], axis=1)], axis=0)


def kernel(x, c, ctx, c_ctx, w_mod, b_mod, g_pre, g_post, w_in_even, w_out_even, qn_a, kn_a, mu_b, w0_b, w2_b, a0_b, a2_b, kk_b, ka_b, rk_b, gn_w_b, gn_b_b, w_in_odd, w_out_odd, qn_c, kn_c, sink_c):
    b, t, d = x.shape
    ctx_len = ctx.shape[1]
    s = ctx_len + t
    assert ctx_len % ROW_TILE == 0 and t % ROW_TILE == 0 and b + 1 <= 8
    ctx_tiles = ctx_len // ROW_TILE
    depth = w_mod.shape[0]

    cc = jnp.concatenate([c, c_ctx[None, :], jnp.zeros((8 - b - 1, d), F32)], axis=0)
    mod = _modulation(cc, w_mod, b_mod)
    mod = mod.reshape(depth, 8, 3, d)
    modsel = jnp.stack([jnp.broadcast_to(mod[:, b][:, None], (depth, b, 3, d)), mod[:, :b]], axis=2)

    cos_t, sin_t = _rope_tables(t, ctx_len)
    seg_mean = jnp.asarray(np.kron(np.eye(MXU_WIDTH // HEAD_DIM),
                                   np.full((HEAD_DIM, HEAD_DIM), 1.0 / HEAD_DIM)), BF16)
    seg_ones = jnp.asarray(np.kron(np.eye(2), np.ones((HEAD_DIM, HEAD_DIM))), BF16)

    bw = kk_b.shape[1]
    a_width = w_out_even.shape[1] - bw
    a_heads = a_width // HEAD_DIM
    n_in = w_in_even.shape[2]
    b_proj = 3 * bw + 4 * B_LORA
    kv_width = (n_in - 2 * a_width - b_proj - bw) // 2
    a_kv_heads = kv_width // HEAD_DIM
    qk_width = a_width + kv_width
    gain0 = jnp.concatenate([jnp.tile(qn_a[0], a_heads), jnp.tile(kn_a[0], a_kv_heads)])[None, :]
    bp_lo = qk_width + kv_width
    g_lo = bp_lo + b_proj
    qh, kh, vth, bproj, gate0 = _inproj(
        [ctx, x], modsel[0], g_pre[0][None, :], w_in_even[0].astype(BF16), cos_t, sin_t, gain0, seg_mean,
        q_width=a_width, qk_width=qk_width, v_width=kv_width,
        splits=((bp_lo, g_lo), (g_lo, n_in)), out_dtypes=(F32, F32), ctx_tiles=ctx_tiles)
    oa_t = _dense_attention(qh, kh, vth, ctx_len=ctx_len)

    m_c, n_c, q_c, y0_c, bonus = _rwkv_prep(
        bproj, mu_b[0][None, :], w0_b[0].reshape(1, 2 * bw), _block_diag_lora(w2_b[0]).astype(BF16),
        a0_b[0].reshape(1, 2 * bw), _block_diag_lora(a2_b[0]).astype(BF16), kk_b[0][None, :], ka_b[0][None, :],
        rk_b[0].reshape(1, bw), seg_ones, ctx_len=ctx_len)
    y_f, y_b = _rwkv_scan(m_c, n_c, q_c, y0_c, ctx_len=ctx_len)

    xc = _outproj([ctx, x], oa_t, gate0, w_out_even[0].astype(BF16), modsel[0], g_post[0][None, :],
                  (y_f, y_b, bonus, seg_mean, gn_w_b[0][None, :], gn_b_b[0][None, :]),
                  ctx_tiles=ctx_tiles, latent_only=False)

    c_heads = sink_c.shape[1]
    c_width = c_heads * HEAD_DIM
    n_in1 = w_in_odd.shape[2]
    ckv_width = (n_in1 - 2 * c_width) // 2
    c_kv_heads = ckv_width // HEAD_DIM
    qk_width1 = c_width + ckv_width
    gain1 = jnp.concatenate([jnp.tile(qn_c[0], c_heads), jnp.tile(kn_c[0], c_kv_heads)])[None, :]
    qh1, kh1, vth1, gate1 = _inproj(
        [xc], modsel[1], g_pre[1][None, :], w_in_odd[0].astype(BF16), cos_t, sin_t, gain1, seg_mean,
        q_width=c_width, qk_width=qk_width1, v_width=ckv_width,
        splits=((qk_width1 + ckv_width, n_in1),), out_dtypes=(F32,), ctx_tiles=ctx_tiles)
    ow_t = _window_attention(qh1, kh1, vth1, sink_c[0], ctx_len=ctx_len)
    return _outproj([xc], ow_t, gate1, w_out_odd[0].astype(BF16), modsel[1], g_post[1][None, :], None,
                    ctx_tiles=ctx_tiles, latent_only=True)
```

```python
import functools

import numpy as np
import jax
import jax.numpy as jnp
from jax import lax
from jax.experimental import pallas as pl
from jax.experimental.pallas import tpu as pltpu

F32 = jnp.float32
BF16 = jnp.bfloat16
HIGHEST = lax.Precision.HIGHEST

HEAD_DIM = 64
LANES = 128
MXU_WIDTH = 256
RWKV_GROUP_LANES = 128
GRID_W = 64
Q_BLOCK = 128
WINDOW = 128
ROPE_THETA = 10000.0
RMS_EPS = 1e-6
GN_EPS = 64e-5
LOG2_E = float(np.log2(np.e))
Q_SCALE = HEAD_DIM ** -0.5 * LOG2_E
B_LORA = 64
CHUNK = 64
PREP_CHUNKS = 2
ROW_TILE = 256
PROJ_COL_CHUNK = 512
DENSE_Q_TILE = 128
DENSE_KV_TILE = 1024
WINDOW_BLOCKS = 16
VMEM_LIMIT = 56 * 1024 * 1024


def _cparams(sem):
    return pltpu.CompilerParams(dimension_semantics=sem, vmem_limit_bytes=VMEM_LIMIT)


def _head_mean(x, seg):
    width = x.shape[1]
    cols = []
    for lo in range(0, width, MXU_WIDTH):
        n = min(MXU_WIDTH, width - lo)
        cols.append(jnp.dot(x[:, lo:lo + n].astype(BF16), seg[0:n, 0:n], preferred_element_type=F32))
    return cols[0] if len(cols) == 1 else jnp.concatenate(cols, axis=1)


def _sigmoid(z):
    return 1.0 / (1.0 + jnp.exp(-z))


def _silu(z):
    return z * _sigmoid(z)


def _mod_kernel(c_ref, w_ref, b_ref, o_ref):
    o_ref[...] = jnp.dot(_silu(c_ref[...]), w_ref[...], precision=HIGHEST,
                         preferred_element_type=F32) + b_ref[...]


def _modulation(cc, w_mod, b_mod):
    depth, d, d3 = w_mod.shape
    nj = d3 // d
    return pl.pallas_call(
        _mod_kernel,
        grid=(depth, nj),
        in_specs=[pl.BlockSpec((8, d), lambda l, j: (0, 0)),
                  pl.BlockSpec((None, d, d), lambda l, j: (l, 0, j)),
                  pl.BlockSpec((None, 1, d), lambda l, j: (l, 0, j))],
        out_specs=pl.BlockSpec((None, 8, d), lambda l, j: (l, 0, j)),
        out_shape=jax.ShapeDtypeStruct((depth, 8, d3), F32),
        compiler_params=_cparams(("arbitrary", "arbitrary")),
        name="modulation",
    )(cc, w_mod, b_mod.reshape(depth, 1, d3))


def _stream_rows(ctx_ref, x_ref, ctx_tiles, tile):
    if ctx_ref is None:
        return x_ref[...]
    return jnp.where(tile < ctx_tiles, ctx_ref[...], x_ref[...])


def _stream_specs(two_sources, tm, d, ctx_tiles, tile_of=lambda i: i):
    if not two_sources:
        return [pl.BlockSpec((None, tm, d), lambda bi, i: (bi, tile_of(i), 0))]
    return [pl.BlockSpec((None, tm, d), lambda bi, i: (bi, jnp.minimum(tile_of(i), ctx_tiles - 1), 0)),
            pl.BlockSpec((None, tm, d), lambda bi, i: (bi, jnp.maximum(tile_of(i) - ctx_tiles, 0), 0))]


def _interleave(*stages):
    stages = list(stages)
    while stages:
        for stage in list(stages):
            if next(stage, "done") == "done":
                stages.remove(stage)


def _inproj_kernel(*refs, two_sources, ctx_tiles, n_tiles, q_width, qk_width, v_width, splits):
    n_src = 2 if two_sources else 1
    src = refs[:n_src]
    mod_ref, gpre_ref, w_ref, cos_ref, sin_ref, gain_ref, seg_ref = refs[n_src:n_src + 7]
    q_ref, k_ref, vt_ref = refs[n_src + 7:n_src + 10]
    extra_refs = refs[n_src + 10:-1]
    attn_s = refs[-1]
    step = pl.program_id(1)
    tile = jnp.minimum(step, n_tiles - 1)
    n_attn = qk_width + v_width

    @pl.when(step == 0)
    def _():
        attn_s[...] = jnp.zeros_like(attn_s)

    staged = []

    def project():
        x = _stream_rows(src[0] if two_sources else None, src[-1], ctx_tiles, tile)
        ms = jnp.mean(x * x, axis=-1, keepdims=True)
        h = x * lax.rsqrt(ms + RMS_EPS) * gpre_ref[...]
        hb = (h * (1.0 + mod_ref[1:2, :]) + mod_ref[0:1, :]).astype(BF16)
        yield
        for lo in range(0, n_attn, PROJ_COL_CHUNK):
            hi = min(lo + PROJ_COL_CHUNK, n_attn)
            staged.append((lo, hi, jnp.dot(hb, w_ref[:, lo:hi], preferred_element_type=F32)))
            yield
        for ref, (s_lo, s_hi) in zip(extra_refs, splits):
            for lo in range(s_lo, s_hi, PROJ_COL_CHUNK):
                hi = min(lo + PROJ_COL_CHUNK, s_hi)
                ref[:, lo - s_lo:hi - s_lo] = jnp.dot(hb, w_ref[:, lo:hi],
                                                      preferred_element_type=F32).astype(ref.dtype)
                yield

    def finish_previous():
        cos = cos_ref[...]
        sin = sin_ref[...]
        lane = lax.broadcasted_iota(jnp.int32, (1, LANES), 1)
        even_lane = (lane % 2) == 0
        qk = attn_s[:, 0:qk_width]
        msq_all = _head_mean(qk * qk, seg_ref[...])
        yield
        for g in range(qk_width // LANES):
            xg = attn_s[:, g * LANES:(g + 1) * LANES]
            msq = msq_all[:, g * LANES:(g + 1) * LANES]
            xn = xg * lax.rsqrt(msq + RMS_EPS) * gain_ref[:, g * LANES:(g + 1) * LANES]
            partner = jnp.where(even_lane, pltpu.roll(xn, LANES - 1, 1), pltpu.roll(xn, 1, 1))
            y = xn * cos + partner * sin
            is_q = g * LANES < q_width
            if is_q:
                y = y * Q_SCALE
            ref = q_ref if is_q else k_ref
            h0 = 2 * g if is_q else 2 * (g - q_width // LANES)
            ref[h0] = y[:, 0:HEAD_DIM].astype(ref.dtype)
            ref[h0 + 1] = y[:, HEAD_DIM:LANES].astype(ref.dtype)
            yield
        for g in range(v_width // LANES):
            vt = attn_s[:, qk_width + g * LANES:qk_width + (g + 1) * LANES].T
            vt_ref[2 * g] = vt[0:HEAD_DIM].astype(vt_ref.dtype)
            vt_ref[2 * g + 1] = vt[HEAD_DIM:LANES].astype(vt_ref.dtype)
            yield

    _interleave(project(), finish_previous())
    for lo, hi, acc in staged:
        attn_s[:, lo:hi] = acc


def _inproj(sources, modsel, gpre, w_bf16, cos_t, sin_t, gain, seg, *, q_width, qk_width, v_width,
            splits, out_dtypes, ctx_tiles):
    two_sources = len(sources) == 2
    b, d = sources[0].shape[0], sources[0].shape[2]
    s = sum(a.shape[1] for a in sources)
    n = w_bf16.shape[1]
    tm = ROW_TILE
    hq = q_width // HEAD_DIM
    hk = (qk_width - q_width) // HEAD_DIM
    hv = v_width // HEAD_DIM
    n_tiles = s // tm
    cur = lambda i: jnp.minimum(i, n_tiles - 1)
    done = lambda i: jnp.maximum(i - 1, 0)
    const = lambda bi, i: (0, 0)
    out_shapes = [jax.ShapeDtypeStruct((b, hq, s, HEAD_DIM), BF16),
                  jax.ShapeDtypeStruct((b, hk, s, HEAD_DIM), BF16),
                  jax.ShapeDtypeStruct((b, hv, HEAD_DIM, s), BF16)]
    out_specs = [pl.BlockSpec((None, hq, tm, HEAD_DIM), lambda bi, i: (bi, 0, done(i), 0)),
                 pl.BlockSpec((None, hk, tm, HEAD_DIM), lambda bi, i: (bi, 0, done(i), 0)),
                 pl.BlockSpec((None, hv, HEAD_DIM, tm), lambda bi, i: (bi, 0, 0, done(i)))]
    for (lo, hi), dt in zip(splits, out_dtypes):
        out_shapes.append(jax.ShapeDtypeStruct((b, s, hi - lo), dt))
        out_specs.append(pl.BlockSpec((None, tm, hi - lo), lambda bi, i: (bi, cur(i), 0)))
    kern = functools.partial(_inproj_kernel, two_sources=two_sources, ctx_tiles=ctx_tiles, n_tiles=n_tiles,
                             q_width=q_width, qk_width=qk_width, v_width=v_width, splits=splits)
    return pl.pallas_call(
        kern,
        grid=(b, n_tiles + 1),
        in_specs=_stream_specs(two_sources, tm, d, ctx_tiles, cur) + [
            pl.BlockSpec((None, None, 3, d),
                         lambda bi, i: (bi, jnp.minimum(cur(i) // ctx_tiles, 1), 0, 0)),
            pl.BlockSpec((1, d), const),
            pl.BlockSpec((d, n), const),
            pl.BlockSpec((tm, LANES), lambda bi, i: (done(i), 0)),
            pl.BlockSpec((tm, LANES), lambda bi, i: (done(i), 0)),
            pl.BlockSpec((1, qk_width), const),
            pl.BlockSpec((MXU_WIDTH, MXU_WIDTH), const)],
        out_specs=out_specs,
        out_shape=out_shapes,
        scratch_shapes=[pltpu.VMEM((tm, qk_width + v_width), F32)],
        compiler_params=_cparams(("parallel", "arbitrary")),
        name="inproj",
    )(*sources, modsel, gpre, w_bf16, cos_t, sin_t, gain, seg)


def _dense_attn_kernel(q_ref, k_ref, vt_ref, o_ref, *, tk, ctx_len):
    i = pl.program_id(2)
    r, tq, dh = q_ref.shape
    m_rows = r * tq
    n_keys = k_ref.shape[0]
    q = q_ref[...].reshape(m_rows, dh)
    contract_lanes = (((1,), (1,)), ((), ()))

    def scores(lo, hi):
        return lax.dot_general(k_ref[lo:hi, :], q, contract_lanes, preferred_element_type=F32)

    def attend(bounds):
        m = jnp.full((1, m_rows), -jnp.inf, F32)
        l = jnp.zeros((1, m_rows), F32)
        acc = jnp.zeros((dh, m_rows), F32)
        st = scores(*bounds[0])
        pending = None
        for j in range(len(bounds)):
            if pending is not None:
                (plo, phi), p_prev = pending
                acc = acc + jnp.dot(vt_ref[:, plo:phi], p_prev, preferred_element_type=F32)
            st_next = scores(*bounds[j + 1]) if j + 1 < len(bounds) else None
            m_new = jnp.maximum(m, jnp.max(st, axis=0, keepdims=True))
            alpha = jnp.exp2(m - m_new)
            p = jnp.exp2(st - m_new)
            l = alpha * l + jnp.sum(p, axis=0, keepdims=True)
            acc = acc * alpha
            pending = (bounds[j], p.astype(BF16))
            m, st = m_new, st_next
        (plo, phi), p_prev = pending
        acc = acc + jnp.dot(vt_ref[:, plo:phi], p_prev, preferred_element_type=F32)
        o = acc / l
        for hh in range(r):
            o_ref[hh] = o[:, hh * tq:(hh + 1) * tq].astype(o_ref.dtype)

    ctx_bounds = [(0, ctx_len)]
    all_bounds = ctx_bounds + [(lo, lo + tk) for lo in range(ctx_len, n_keys, tk)]

    @pl.when(i < ctx_len // tq)
    def _():
        attend(ctx_bounds)

    @pl.when(i >= ctx_len // tq)
    def _():
        attend(all_bounds)


def _dense_attention(qh, kh, vth, *, ctx_len):
    b, hq, s, dh = qh.shape
    g = kh.shape[1]
    r = hq // g
    tq = DENSE_Q_TILE
    assert (s - ctx_len) % DENSE_KV_TILE == 0 and ctx_len % tq == 0
    kern = functools.partial(_dense_attn_kernel, tk=DENSE_KV_TILE, ctx_len=ctx_len)
    return pl.pallas_call(
        kern,
        grid=(b, g, s // tq),
        in_specs=[pl.BlockSpec((None, r, tq, dh), lambda bi, gi, i: (bi, gi, i, 0)),
                  pl.BlockSpec((None, None, s, dh), lambda bi, gi, i: (bi, gi, 0, 0)),
                  pl.BlockSpec((None, None, dh, s), lambda bi, gi, i: (bi, gi, 0, 0))],
        out_specs=pl.BlockSpec((None, r, dh, tq), lambda bi, gi, i: (bi, gi, 0, i)),
        out_shape=jax.ShapeDtypeStruct((b, hq, dh, s), F32),
        compiler_params=_cparams(("parallel", "parallel", "parallel")),
        name="dense_attention",
    )(qh, kh, vth)


def _window_attn_kernel(sink_ref, q_ref, bias_first_ref, bias_mid_ref, bias_last_ref,
                        kc_ref, kp_ref, km_ref, kn_ref, vc_ref, vp_ref, vm_ref, vn_ref, o_ref):
    gi = pl.program_id(1)
    _, r, rows, dh = q_ref.shape
    tq = Q_BLOCK
    nb = rows // tq
    contract_lanes = (((1,), (1,)), ((), ()))
    k_all = jnp.concatenate([kp_ref[...], km_ref[0, 0], kn_ref[...]], axis=0)
    vt_all = jnp.concatenate([vp_ref[...], vm_ref[0, 0], vn_ref[...]], axis=1)
    k_ctx, vt_ctx = kc_ref[...], vc_ref[...]
    lane = lax.broadcasted_iota(jnp.int32, (1, r * tq), 1)
    sink = jnp.zeros((1, r * tq), F32)
    for hh in range(r):
        sink = jnp.where(lane // tq == hh, sink_ref[gi * r + hh] * LOG2_E, sink)

    def scores(qb):
        q = jnp.concatenate([q_ref[0, hh, qb * tq:(qb + 1) * tq, :] for hh in range(r)], axis=0)
        bias_ref = bias_first_ref if qb == 0 else (bias_last_ref if qb == nb - 1 else bias_mid_ref)
        s_loc = lax.dot_general(k_all[qb * tq:(qb + 3) * tq], q, contract_lanes,
                                preferred_element_type=F32) + bias_ref[...]
        s_ctx = lax.dot_general(k_ctx, q, contract_lanes, preferred_element_type=F32)
        return s_loc, s_ctx

    def finish(qb, p_loc, p_ctx, l):
        o = (jnp.dot(vt_all[:, qb * tq:(qb + 3) * tq], p_loc, preferred_element_type=F32)
             + jnp.dot(vt_ctx, p_ctx, preferred_element_type=F32)) / l
        for hh in range(r):
            o_ref[hh, :, qb * tq:(qb + 1) * tq] = o[:, hh * tq:(hh + 1) * tq].astype(o_ref.dtype)

    s_cur = scores(0)
    pending = None
    for qb in range(nb):
        if pending is not None:
            finish(*pending)
        s_next = scores(qb + 1) if qb + 1 < nb else None
        s_loc, s_ctx = s_cur
        m = jnp.maximum(jnp.maximum(jnp.max(s_loc, axis=0, keepdims=True),
                                    jnp.max(s_ctx, axis=0, keepdims=True)), sink)
        p_loc = jnp.exp2(s_loc - m)
        p_ctx = jnp.exp2(s_ctx - m)
        l = (jnp.sum(p_loc, axis=0, keepdims=True) + jnp.sum(p_ctx, axis=0, keepdims=True)
             + jnp.exp2(sink - m))
        pending = (qb, p_loc.astype(BF16), p_ctx.astype(BF16), l)
        s_cur = s_next
    finish(*pending)


def _window_bias(tq, r):
    key = np.arange(3 * tq)[:, None] - tq
    qpos = np.arange(tq)[None, :]
    band = np.abs(key - qpos) <= WINDOW
    variants = [band & (key >= 0), band, band & (key < tq)]
    table = np.stack([np.where(np.tile(v, (1, r)), 0.0, -np.inf) for v in variants])
    return jnp.asarray(table, F32)


def _window_attention(qh, kh, vth, sink, *, ctx_len):
    b, hq, s, dh = qh.shape
    g = kh.shape[1]
    r = hq // g
    tq = Q_BLOCK
    t = s - ctx_len
    nb = WINDOW_BLOCKS
    rows = nb * tq
    assert t % rows == 0 and ctx_len % tq == 0
    ns = t // rows
    off = ctx_len // tq
    last = t // tq - 1
    bias = _window_bias(tq, r)
    ctx = lambda bi, gi, i: (bi, gi, 0, 0)
    blk = lambda n_rows: (None, None, n_rows, dh)
    tblk = lambda n_cols: (None, None, dh, n_cols)
    prev_blk = lambda i: jnp.maximum(nb * i - 1, 0) + off
    next_blk = lambda i: jnp.minimum(nb * i + nb, last) + off
    el = pl.Element
    mid = lambda i: pl.multiple_of(ctx_len + i * rows, tq)
    return pl.pallas_call(
        _window_attn_kernel,
        grid=(b, g, ns),
        in_specs=[pl.BlockSpec(memory_space=pltpu.SMEM),
                  pl.BlockSpec((el(1), el(r), el(rows), el(dh)), lambda bi, gi, i: (bi, gi * r, mid(i), 0)),
                  pl.BlockSpec((None, 3 * tq, r * tq), lambda bi, gi, i: (jnp.minimum(i, 1), 0, 0)),
                  pl.BlockSpec((None, 3 * tq, r * tq), lambda bi, gi, i: (1, 0, 0)),
                  pl.BlockSpec((None, 3 * tq, r * tq),
                               lambda bi, gi, i: (jnp.where(i == ns - 1, 2, 1), 0, 0)),
                  pl.BlockSpec(blk(ctx_len), ctx),
                  pl.BlockSpec(blk(tq), lambda bi, gi, i: (bi, gi, prev_blk(i), 0)),
                  pl.BlockSpec((el(1), el(1), el(rows), el(dh)), lambda bi, gi, i: (bi, gi, mid(i), 0)),
                  pl.BlockSpec(blk(tq), lambda bi, gi, i: (bi, gi, next_blk(i), 0)),
                  pl.BlockSpec(tblk(ctx_len), ctx),
                  pl.BlockSpec(tblk(tq), lambda bi, gi, i: (bi, gi, 0, prev_blk(i))),
                  pl.BlockSpec((el(1), el(1), el(dh), el(rows)), lambda bi, gi, i: (bi, gi, 0, mid(i))),
                  pl.BlockSpec(tblk(tq), lambda bi, gi, i: (bi, gi, 0, next_blk(i)))],
        out_specs=pl.BlockSpec((None, r, dh, rows), lambda bi, gi, i: (bi, gi, 0, i)),
        out_shape=jax.ShapeDtypeStruct((b, hq, dh, t), F32),
        compiler_params=_cparams(("parallel", "parallel", "parallel")),
        name="window_attention",
    )(sink, qh, bias, bias, bias, kh, kh, kh, kh, vth, vth, vth, vth)


def _rwkv_prep_kernel(p_ref, hp_ref, hn_ref, mu_ref, w0_ref, w2_ref, a0_ref, a2_ref,
                      kk_ref, ka_ref, rk_ref, ones_ref,
                      m_ref, n_ref, q_ref, y0_ref, bonus_ref,
                      xf_s, kh_s, bh_s, kbar_s, bbar_s, rt_s, v_s, wend_s, *, ctx_chunks, n_chunks):
    step_id = pl.program_id(1)
    sub = PREP_CHUNKS
    c = CHUNK
    rows = sub * c
    jb = jnp.minimum(step_id, n_chunks // sub - 1)
    first_chunk = jb * sub
    last_chunk = first_chunk + sub - 1
    bw = kk_ref.shape[1]
    n_pairs = bw // LANES
    staged = (xf_s, kh_s, bh_s, kbar_s, bbar_s, rt_s, v_s, wend_s)

    @pl.when(step_id == 0)
    def _():
        for ref in staged:
            ref[...] = jnp.zeros_like(ref)

    gw = RWKV_GROUP_LANES
    hpg = gw // HEAD_DIM
    n_groups = bw // gw
    ti = lax.broadcasted_iota(jnp.int32, (rows, rows), 0)
    tj = lax.broadcasted_iota(jnp.int32, (rows, rows), 1)
    same_chunk = (ti // c) == (tj // c)
    ti_g = lax.broadcasted_iota(jnp.int32, (c, gw), 0)
    tj_g = lax.broadcasted_iota(jnp.int32, (c, gw), 1) % c
    eye_g = (ti_g == tj_g).astype(F32)
    gi_r = lax.broadcasted_iota(jnp.int32, (gw, gw), 0)
    gi_c = lax.broadcasted_iota(jnp.int32, (gw, gw), 1)
    group_diag = (gi_r // HEAD_DIM) == (gi_c // HEAD_DIM)
    bi = lax.broadcasted_iota(jnp.int32, (LANES, LANES), 0)
    bj = lax.broadcasted_iota(jnp.int32, (LANES, LANES), 1)
    same_head = (bi // HEAD_DIM) == (bj // HEAD_DIM)
    diag = bi == bj
    scan_masks = [(tj_g < ti_g, tj_g <= ti_g), (tj_g > ti_g, tj_g >= ti_g)]

    def blockdiag(xs):
        xb = xs.astype(BF16)
        return jnp.where(group_diag, jnp.concatenate([xb] * hpg, axis=0), jnp.zeros((), BF16))


    def elementwise_stage():
        p = p_ref[...]
        has_prev = jnp.logical_and(first_chunk != 0, first_chunk != ctx_chunks)
        has_next = jnp.logical_and(last_chunk != ctx_chunks - 1, last_chunk != n_chunks - 1)
        prev_row = jnp.where(has_prev, hp_ref[7:8, :], 0.0)
        next_row = jnp.where(has_next, hn_ref[0:1, :], 0.0)
        rowi = lax.broadcasted_iota(jnp.int32, (rows, 1), 0)
        p_prev = jnp.where(rowi == 0, prev_row, pltpu.roll(p, 1, 0))
        p_next = jnp.where(rowi == rows - 1, next_row, pltpu.roll(p, rows - 1, 0))
        ps = p + mu_ref[...] * (0.5 * (p_prev + p_next) - p)
        r = ps[:, 0:bw]
        k = ps[:, bw:2 * bw]
        v = ps[:, 2 * bw:3 * bw]
        zw = ps[:, 3 * bw:3 * bw + 2 * B_LORA]
        za = ps[:, 3 * bw + 2 * B_LORA:3 * bw + 4 * B_LORA]
        yield

        wl = w0_ref[...] + jnp.dot(jnp.tanh(zw).astype(BF16), w2_ref[...],
                                   preferred_element_type=F32)
        z = -wl
        softplus = jnp.maximum(z, 0.0) + jnp.log(1.0 + jnp.exp(-jnp.abs(z)))
        logw = -jnp.exp(-softplus - 0.5)
        yield
        a = _sigmoid(a0_ref[...] + jnp.dot(za.astype(BF16), a2_ref[...],
                                           preferred_element_type=F32))
        ones_seg = ones_ref[...]
        kf = k * kk_ref[...]
        kk = kf / jnp.maximum(jnp.sqrt(_head_mean(kf * kf, ones_seg)), 1e-12)
        yield

        bonus = jnp.zeros((rows, bw), F32)
        chunk_rows = [slice(ch * c, (ch + 1) * c) for ch in range(sub)]
        out = dict(xf=[], kh=[], bh=[], kbar=[], bbar=[], rt=[], wend=[])
        for d in range(2):
            lw = logw[:, d * bw:(d + 1) * bw]
            a_d = a[:, d * bw:(d + 1) * bw]
            b_d = kk * a_d
            kmod = k * (1.0 + (a_d - 1.0) * ka_ref[...])
            bonus = bonus + _head_mean(r * kmod * rk_ref[...], ones_seg) * v
            yield

            tri = (same_chunk & ((tj <= ti) if d == 0 else (tj >= ti))).astype(BF16)
            lw_hi = lw.astype(BF16)
            lw_lo = (lw - lw_hi.astype(F32)).astype(BF16)
            cum = (jnp.dot(tri, lw_hi, preferred_element_type=F32)
                   + jnp.dot(tri, lw_lo, preferred_element_type=F32))
            ends = [cum[rs.stop - 1:rs.stop] if d == 0 else cum[rs.start:rs.start + 1] for rs in chunk_rows]
            cum_end = jnp.concatenate([jnp.broadcast_to(e, (c, bw)) for e in ends], axis=0)
            r_t = r * jnp.exp(cum)
            kk_t = kk * jnp.exp(cum - lw)
            yield
            e_out = jnp.exp(-cum)
            tail = jnp.exp(cum_end - cum)
            out["xf"].append(jnp.concatenate([part[rs] for rs in chunk_rows for part in (kk_t, r_t)],
                                             axis=0).astype(BF16))
            out["rt"].append(r_t)
            out["kh"].append((kmod * e_out).astype(BF16))
            out["bh"].append((b_d * e_out).astype(BF16))
            out["kbar"].append((kmod * tail).astype(BF16))
            out["bbar"].append((b_d * tail).astype(BF16))
            out["wend"].append(jnp.concatenate([jnp.broadcast_to(jnp.exp(e), (8, bw)) for e in ends], axis=0))
            yield
        bonus_ref[...] = bonus
        out["v"] = v.astype(BF16)
        staged_next.update(out)

    def matmul_stage():
        contract_lanes = (((1,), (1,)), ((), ()))
        contract_rows = (((0,), (0,)), ((), ()))
        probs = [(d, g, ch) for d in range(2) for g in range(n_groups) for ch in range(sub)]
        lanes_of = lambda g: slice(g * gw, (g + 1) * gw)
        rows_of = lambda ch: slice(ch * c, (ch + 1) * c)
        xf_of = lambda d, g, ch: xf_s[d, ch * 2 * c:(ch + 1) * 2 * c, lanes_of(g)]
        v_bd = {(g, ch): blockdiag(v_s[rows_of(ch), lanes_of(g)]) for g in range(n_groups) for ch in range(sub)}
        aks = [lax.dot_general(xf_of(d, g, ch), blockdiag(kh_s[d, rows_of(ch), lanes_of(g)]), contract_lanes,
                               preferred_element_type=F32) for d, g, ch in probs]
        yield
        abs_ = [lax.dot_general(xf_of(d, g, ch), blockdiag(bh_s[d, rows_of(ch), lanes_of(g)]), contract_lanes,
                                preferred_element_type=F32) for d, g, ch in probs]
        yield
        a_ks = [jnp.concatenate([jnp.where(scan_masks[d][0], ak[0:c], 0.0),
                                 jnp.where(scan_masks[d][1], ak[c:2 * c], 0.0)], axis=0).astype(BF16)
                for (d, g, ch), ak in zip(probs, aks)]
        a_qbs = [jnp.where(scan_masks[d][1], ab[c:2 * c], 0.0).astype(BF16)
                 for (d, g, ch), ab in zip(probs, abs_)]
        pws = [jnp.where(scan_masks[d][0], -ab[0:c], 0.0) for (d, g, ch), ab in zip(probs, abs_)]
        tinvs = [eye_g + pw for pw in pws]
        pws = [jnp.dot(pw.astype(BF16), blockdiag(pw), preferred_element_type=F32) for pw in pws]
        yield
        for _ in range(4):
            stacked = [jnp.dot(jnp.concatenate([t, pw], axis=0).astype(BF16), blockdiag(pw),
                               preferred_element_type=F32) for t, pw in zip(tinvs, pws)]
            tinvs = [t + st[0:c] for t, st in zip(tinvs, stacked)]
            pws = [st[c:2 * c] for st in stacked]
            yield
        tinvs = [(t + jnp.dot(t.astype(BF16), blockdiag(pw), preferred_element_type=F32)).astype(BF16)
                 for t, pw in zip(tinvs, pws)]
        avs = [jnp.dot(a_k, v_bd[g, ch], preferred_element_type=F32) for (d, g, ch), a_k in zip(probs, a_ks)]
        yield
        p1s = [jnp.dot(t, blockdiag(xf_of(d, g, ch)[0:c]), preferred_element_type=F32)
               for (d, g, ch), t in zip(probs, tinvs)]
        p2s = [jnp.dot(t, blockdiag(av[0:c]), preferred_element_type=F32) for t, av in zip(tinvs, avs)]
        yield
        qp1s = [jnp.dot(a_qb, blockdiag(p1), preferred_element_type=F32) for a_qb, p1 in zip(a_qbs, p1s)]
        qp2s = [jnp.dot(a_qb, blockdiag(p2), preferred_element_type=F32) for a_qb, p2 in zip(a_qbs, p2s)]
        for (d, g, ch), av, qp1, qp2 in zip(probs, avs, qp1s, qp2s):
            q_ref[d, rows_of(ch), lanes_of(g)] = (rt_s[d, rows_of(ch), lanes_of(g)] - qp1).astype(q_ref.dtype)
            y0_ref[d, rows_of(ch), lanes_of(g)] = av[c:2 * c] - qp2
        yield
        for d in range(2):
            for pr in range(n_pairs):
                for ch in range(sub):
                    sl = slice(pr * LANES, (pr + 1) * LANES)
                    pidx = probs.index((d, pr * LANES // gw, ch))
                    lo = pr * LANES % gw
                    p12 = jnp.concatenate([p1s[pidx][:, lo:lo + LANES], p2s[pidx][:, lo:lo + LANES]],
                                          axis=1).astype(BF16)
                    bp = lax.dot_general(bbar_s[d, rows_of(ch), sl], p12, contract_rows,
                                         preferred_element_type=F32)
                    kv = lax.dot_general(kbar_s[d, rows_of(ch), sl], v_s[rows_of(ch), sl], contract_rows,
                                         preferred_element_type=F32)
                    w_diag = jnp.where(diag, wend_s[d, ch * 8:ch * 8 + 1, sl], 0.0)
                    m_ref[ch, d, pr] = (w_diag - jnp.where(same_head, bp[:, 0:LANES], 0.0)).astype(m_ref.dtype)
                    n_ref[ch, d, pr] = jnp.where(same_head, kv - bp[:, LANES:2 * LANES], 0.0)
            yield

    staged_next = {}
    _interleave(elementwise_stage(), matmul_stage())
    for ref, key in ((xf_s, "xf"), (kh_s, "kh"), (bh_s, "bh"), (kbar_s, "kbar"), (bbar_s, "bbar"),
                     (rt_s, "rt"), (wend_s, "wend")):
        for d in range(2):
            ref[d] = staged_next[key][d]
    v_s[...] = staged_next["v"]


def _rwkv_prep(bp, mu, w0, w2cat, a0, a2cat, kkw, kaw, rkw, ones_seg, *, ctx_len):
    b, s, pw = bp.shape
    bw = kkw.shape[1]
    n_pairs = bw // LANES
    nc = s // CHUNK
    ctx_chunks = ctx_len // CHUNK
    sub = PREP_CHUNKS
    rows = sub * CHUNK
    assert nc % sub == 0 and ctx_chunks % sub == 0
    n_blocks = nc // sub
    rb = rows // 8
    kern = functools.partial(_rwkv_prep_kernel, ctx_chunks=ctx_chunks, n_chunks=nc)
    const = lambda bi, j: (0, 0)
    cur = lambda j: jnp.minimum(j, n_blocks - 1)
    done = lambda j: jnp.maximum(j - 1, 0)
    return pl.pallas_call(
        kern,
        grid=(b, n_blocks + 1),
        in_specs=[pl.BlockSpec((None, rows, pw), lambda bi, j: (bi, cur(j), 0)),
                  pl.BlockSpec((None, 8, pw), lambda bi, j: (bi, jnp.maximum(cur(j) * rb - 1, 0), 0)),
                  pl.BlockSpec((None, 8, pw),
                               lambda bi, j: (bi, jnp.minimum((cur(j) + 1) * rb, s // 8 - 1), 0)),
                  pl.BlockSpec((1, pw), const),
                  pl.BlockSpec((1, 2 * bw), const),
                  pl.BlockSpec((2 * B_LORA, 2 * bw), const),
                  pl.BlockSpec((1, 2 * bw), const),
                  pl.BlockSpec((2 * B_LORA, 2 * bw), const),
                  pl.BlockSpec((1, bw), const),
                  pl.BlockSpec((1, bw), const),
                  pl.BlockSpec((1, bw), const),
                  pl.BlockSpec((MXU_WIDTH, MXU_WIDTH), const)],
        out_specs=[pl.BlockSpec((None, sub, 2, n_pairs, LANES, LANES), lambda bi, j: (bi, done(j), 0, 0, 0, 0)),
                   pl.BlockSpec((None, sub, 2, n_pairs, LANES, LANES), lambda bi, j: (bi, done(j), 0, 0, 0, 0)),
                   pl.BlockSpec((None, 2, rows, bw), lambda bi, j: (bi, 0, done(j), 0)),
                   pl.BlockSpec((None, 2, rows, bw), lambda bi, j: (bi, 0, done(j), 0)),
                   pl.BlockSpec((None, rows, bw), lambda bi, j: (bi, cur(j), 0))],
        out_shape=[jax.ShapeDtypeStruct((b, nc, 2, n_pairs, LANES, LANES), BF16),
                   jax.ShapeDtypeStruct((b, nc, 2, n_pairs, LANES, LANES), F32),
                   jax.ShapeDtypeStruct((b, 2, s, bw), BF16),
                   jax.ShapeDtypeStruct((b, 2, s, bw), F32),
                   jax.ShapeDtypeStruct((b, s, bw), F32)],
        scratch_shapes=[pltpu.VMEM((2, 2 * rows, bw), BF16),
                        pltpu.VMEM((2, rows, bw), BF16),
                        pltpu.VMEM((2, rows, bw), BF16),
                        pltpu.VMEM((2, rows, bw), BF16),
                        pltpu.VMEM((2, rows, bw), BF16),
                        pltpu.VMEM((2, rows, bw), F32),
                        pltpu.VMEM((rows, bw), BF16),
                        pltpu.VMEM((2, 8 * sub, bw), F32)],
        compiler_params=_cparams(("parallel", "arbitrary")),
        name="rwkv_prep",
    )(bp, bp, bp, mu, w0, w2cat, a0, a2cat, kkw, kaw, rkw, ones_seg)


def _rwkv_scan_kernel(m0_ref, m1_ref, n0_ref, n1_ref, q0_ref, q1_ref, y00_ref, y01_ref,
                      o0_ref, o1_ref, h_ref):
    j = pl.program_id(0)

    @pl.when(j == 0)
    def _():
        h_ref[...] = jnp.zeros_like(h_ref)

    nb, n_pairs = m0_ref.shape[0], m0_ref.shape[1]
    dirs = ((m0_ref, n0_ref, q0_ref, y00_ref, o0_ref), (m1_ref, n1_ref, q1_ref, y01_ref, o1_ref))
    for d, (m_ref, n_ref, q_ref, y0_ref, o_ref) in enumerate(dirs):
        for bi in range(nb):
            for pr in range(n_pairs):
                sl = slice(pr * LANES, (pr + 1) * LANES)
                h = h_ref[d, bi, pr]
                hb = h.astype(BF16)
                o_ref[bi, :, sl] = y0_ref[bi, :, sl] + jnp.dot(
                    q_ref[bi, :, sl], hb, preferred_element_type=F32)
                h_ref[d, bi, pr] = n_ref[bi, pr] + jnp.dot(
                    m_ref[bi, pr], hb, preferred_element_type=F32)


def _rwkv_scan(m, n, q, y0, *, ctx_len):
    b, nc, _, n_pairs, _, _ = m.shape
    s, bw = q.shape[2], q.shape[3]
    cc = ctx_len // CHUNK

    def mem_chunk(d, j):
        if d == 0:
            return j
        return jnp.where(j < cc, cc - 1 - j, nc - 1 + cc - j)

    def mn_spec(d):
        return pl.BlockSpec((b, None, None, n_pairs, LANES, LANES),
                            lambda j: (0, mem_chunk(d, j), d, 0, 0, 0))

    def row_spec(d):
        return pl.BlockSpec((b, None, CHUNK, bw), lambda j: (0, d, mem_chunk(d, j), 0))

    return pl.pallas_call(
        _rwkv_scan_kernel,
        grid=(nc,),
        in_specs=[mn_spec(0), mn_spec(1), mn_spec(0), mn_spec(1),
                  row_spec(0), row_spec(1), row_spec(0), row_spec(1)],
        out_specs=[pl.BlockSpec((b, CHUNK, bw), lambda j: (0, mem_chunk(0, j), 0)),
                   pl.BlockSpec((b, CHUNK, bw), lambda j: (0, mem_chunk(1, j), 0))],
        out_shape=[jax.ShapeDtypeStruct((b, s, bw), F32), jax.ShapeDtypeStruct((b, s, bw), F32)],
        scratch_shapes=[pltpu.VMEM((2, b, n_pairs, LANES, LANES), F32)],
        compiler_params=_cparams(("arbitrary",)),
        name="rwkv_scan",
    )(m, m, n, n, q, q, y0, y0)


def _outproj_kernel(*refs, two_sources, ctx_tiles, rwkv):
    n_src = 2 if two_sources else 1
    src = refs[:n_src]
    at_ref, g_ref, w_ref, mod_ref, gpost_ref = refs[n_src:n_src + 5]
    rest = refs[n_src + 5:]
    out_ref = rest[-1]
    hq, dh, tm = at_ref.shape
    parts = [at_ref[...].reshape(hq * dh, tm).T]
    if rwkv:
        yf_ref, yb_ref, bonus_ref, seg_ref, gnw_ref, gnb_ref = rest[:6]
        seg = seg_ref[...]
        y_sum = bonus_ref[...]
        for y_ref in (yf_ref, yb_ref):
            y = y_ref[...]
            yc = y - _head_mean(y, seg)
            var = _head_mean(yc * yc, seg)
            y_sum = y_sum + yc * lax.rsqrt(var + GN_EPS) * gnw_ref[...] + gnb_ref[...]
        parts.append(y_sum)
    o = jnp.concatenate(parts, axis=1) if len(parts) > 1 else parts[0]
    u = (o * _silu(g_ref[...])).astype(BF16)
    y = jnp.dot(u, w_ref[...], preferred_element_type=F32)
    ms = jnp.mean(y * y, axis=-1, keepdims=True)
    yn = y * lax.rsqrt(ms + RMS_EPS) * gpost_ref[...]
    x = _stream_rows(src[0] if two_sources else None, src[-1], ctx_tiles, pl.program_id(1))
    out_ref[...] = x + mod_ref[2:3, :] * yn


def _outproj(sources, attn_t, gate, w_bf16, modsel, gpost, rwkv_parts, *, ctx_tiles, latent_only):
    two_sources = len(sources) == 2
    b, d = sources[0].shape[0], sources[0].shape[2]
    s = sum(a.shape[1] for a in sources)
    tm = ROW_TILE
    off = ctx_tiles if latent_only else 0
    n_tiles = s // tm - off
    assert not (two_sources and latent_only)
    row = lambda bi, i: (bi, i + off, 0)
    const = lambda bi, i: (0, 0)
    hq, dh = attn_t.shape[1], attn_t.shape[2]
    if two_sources:
        src_specs = _stream_specs(True, tm, d, ctx_tiles)
    else:
        src_specs = [pl.BlockSpec((None, tm, d), row)]
    in_specs = src_specs + [
        pl.BlockSpec((None, hq, dh, tm), lambda bi, i: (bi, 0, 0, i)),
        pl.BlockSpec((None, tm, gate.shape[2]), row),
        pl.BlockSpec(w_bf16.shape, const),
        pl.BlockSpec((None, None, 3, d),
                     lambda bi, i: (bi, jnp.minimum((i + off) // ctx_tiles, 1), 0, 0)),
        pl.BlockSpec((1, d), const)]
    args = [*sources, attn_t, gate, w_bf16, modsel, gpost]
    if rwkv_parts is not None:
        y_f, y_b, bonus, seg, gnw, gnb = rwkv_parts
        in_specs += [pl.BlockSpec((None, tm, y_f.shape[2]), row),
                     pl.BlockSpec((None, tm, y_b.shape[2]), row),
                     pl.BlockSpec((None, tm, bonus.shape[2]), row),
                     pl.BlockSpec((MXU_WIDTH, MXU_WIDTH), const),
                     pl.BlockSpec(gnw.shape, const),
                     pl.BlockSpec(gnb.shape, const)]
        args += [y_f, y_b, bonus, seg, gnw, gnb]
    kern = functools.partial(_outproj_kernel, two_sources=two_sources, ctx_tiles=ctx_tiles,
                             rwkv=rwkv_parts is not None)
    return pl.pallas_call(
        kern,
        grid=(b, n_tiles),
        in_specs=in_specs,
        out_specs=pl.BlockSpec((None, tm, d), lambda bi, i: (bi, i, 0)),
        out_shape=jax.ShapeDtypeStruct((b, n_tiles * tm, d), F32),
        compiler_params=_cparams(("parallel", "parallel")),
        name="outproj",
    )(*args)


def _rope_tables(n_latent, ctx_len):
    t = jnp.arange(n_latent)
    rowp = (t // GRID_W).astype(F32)
    colp = (t % GRID_W).astype(F32)
    axis_dim = HEAD_DIM // 2
    inv = ROPE_THETA ** (-jnp.arange(0, axis_dim, 2, dtype=F32) / axis_dim)
    ang = jnp.concatenate([rowp[:, None] * inv, colp[:, None] * inv], axis=-1)
    cos, sin = jnp.cos(ang), jnp.sin(ang)
    cos = jnp.concatenate([jnp.ones((ctx_len, axis_dim), F32), cos], axis=0)
    sin = jnp.concatenate([jnp.zeros((ctx_len, axis_dim), F32), sin], axis=0)
    cos_h = jnp.repeat(cos, 2, axis=1)
    sin_h = jnp.stack([-sin, sin], axis=-1).reshape(sin.shape[0], HEAD_DIM)
    return jnp.tile(cos_h, (1, LANES // HEAD_DIM)), jnp.tile(sin_h, (1, LANES // HEAD_DIM))


def _block_diag_lora(w2):
    r, w = w2.shape[1], w2.shape[2]
    z = jnp.zeros((r, w), w2.dtype)
    return jnp.concatenate([jnp.concatenate([w2[0], z], axis=1),
                            jnp.concatenate([z, w2[1]], axis=1)], axis=0)


def kernel(x, c, ctx, c_ctx, w_mod, b_mod, g_pre, g_post, w_in_even, w_out_even, qn_a, kn_a, mu_b, w0_b, w2_b, a0_b, a2_b, kk_b, ka_b, rk_b, gn_w_b, gn_b_b, w_in_odd, w_out_odd, qn_c, kn_c, sink_c):
    b, t, d = x.shape
    ctx_len = ctx.shape[1]
    s = ctx_len + t
    assert ctx_len % ROW_TILE == 0 and t % ROW_TILE == 0 and b + 1 <= 8
    ctx_tiles = ctx_len // ROW_TILE
    depth = w_mod.shape[0]

    cc = jnp.concatenate([c, c_ctx[None, :], jnp.zeros((8 - b - 1, d), F32)], axis=0)
    mod = _modulation(cc, w_mod, b_mod)
    mod = mod.reshape(depth, 8, 3, d)
    modsel = jnp.stack([jnp.broadcast_to(mod[:, b][:, None], (depth, b, 3, d)), mod[:, :b]], axis=2)

    cos_t, sin_t = _rope_tables(t, ctx_len)
    seg_mean = jnp.asarray(np.kron(np.eye(MXU_WIDTH // HEAD_DIM),
                                   np.full((HEAD_DIM, HEAD_DIM), 1.0 / HEAD_DIM)), BF16)
    seg_ones = jnp.asarray(np.kron(np.eye(MXU_WIDTH // HEAD_DIM), np.ones((HEAD_DIM, HEAD_DIM))), BF16)

    bw = kk_b.shape[1]
    a_width = w_out_even.shape[1] - bw
    a_heads = a_width // HEAD_DIM
    n_in = w_in_even.shape[2]
    b_proj = 3 * bw + 4 * B_LORA
    kv_width = (n_in - 2 * a_width - b_proj - bw) // 2
    a_kv_heads = kv_width // HEAD_DIM
    qk_width = a_width + kv_width
    gain0 = jnp.concatenate([jnp.tile(qn_a[0], a_heads), jnp.tile(kn_a[0], a_kv_heads)])[None, :]
    bp_lo = qk_width + kv_width
    g_lo = bp_lo + b_proj
    qh, kh, vth, bproj, gate0 = _inproj(
        [ctx, x], modsel[0], g_pre[0][None, :], w_in_even[0].astype(BF16), cos_t, sin_t, gain0, seg_mean,
        q_width=a_width, qk_width=qk_width, v_width=kv_width,
        splits=((bp_lo, g_lo), (g_lo, n_in)), out_dtypes=(F32, F32), ctx_tiles=ctx_tiles)
    oa_t = _dense_attention(qh, kh, vth, ctx_len=ctx_len)

    m_c, n_c, q_c, y0_c, bonus = _rwkv_prep(
        bproj, mu_b[0][None, :], w0_b[0].reshape(1, 2 * bw), _block_diag_lora(w2_b[0]).astype(BF16),
        a0_b[0].reshape(1, 2 * bw), _block_diag_lora(a2_b[0]).astype(BF16), kk_b[0][None, :], ka_b[0][None, :],
        rk_b[0].reshape(1, bw), seg_ones, ctx_len=ctx_len)
    y_f, y_b = _rwkv_scan(m_c, n_c, q_c, y0_c, ctx_len=ctx_len)

    xc = _outproj([ctx, x], oa_t, gate0, w_out_even[0].astype(BF16), modsel[0], g_post[0][None, :],
                  (y_f, y_b, bonus, seg_mean, gn_w_b[0][None, :], gn_b_b[0][None, :]),
                  ctx_tiles=ctx_tiles, latent_only=False)

    c_heads = sink_c.shape[1]
    c_width = c_heads * HEAD_DIM
    n_in1 = w_in_odd.shape[2]
    ckv_width = (n_in1 - 2 * c_width) // 2
    c_kv_heads = ckv_width // HEAD_DIM
    qk_width1 = c_width + ckv_width
    gain1 = jnp.concatenate([jnp.tile(qn_c[0], c_heads), jnp.tile(kn_c[0], c_kv_heads)])[None, :]
    qh1, kh1, vth1, gate1 = _inproj(
        [xc], modsel[1], g_pre[1][None, :], w_in_odd[0].astype(BF16), cos_t, sin_t, gain1, seg_mean,
        q_width=c_width, qk_width=qk_width1, v_width=ckv_width,
        splits=((qk_width1 + ckv_width, n_in1),), out_dtypes=(F32,), ctx_tiles=ctx_tiles)
    ow_t = _window_attention(qh1, kh1, vth1, sink_c[0], ctx_len=ctx_len)
    return _outproj([xc], ow_t, gate1, w_out_odd[0].astype(BF16), modsel[1], g_post[1][None, :], None,
                    ctx_tiles=ctx_tiles, latent_only=True)
```

```python
import functools

import numpy as np
import jax
import jax.numpy as jnp
from jax import lax
from jax.experimental import pallas as pl
from jax.experimental.pallas import tpu as pltpu

F32 = jnp.float32
BF16 = jnp.bfloat16
HIGHEST = lax.Precision.HIGHEST

HEAD_DIM = 64
LANES = 128
MXU_WIDTH = 256
RWKV_GROUP_LANES = 128
GRID_W = 64
Q_BLOCK = 128
WINDOW = 128
ROPE_THETA = 10000.0
RMS_EPS = 1e-6
GN_EPS = 64e-5
LOG2_E = float(np.log2(np.e))
Q_SCALE = HEAD_DIM ** -0.5 * LOG2_E
B_LORA = 64
CHUNK = 64
PREP_CHUNKS = 2
ROW_TILE = 256
PROJ_COL_CHUNK = 512
DENSE_Q_TILE = 128
DENSE_KV_TILE = 1024
WINDOW_BLOCKS = 16
VMEM_LIMIT = 56 * 1024 * 1024


def _cparams(sem):
    return pltpu.CompilerParams(dimension_semantics=sem, vmem_limit_bytes=VMEM_LIMIT)


def _head_mean(x, seg):
    width = x.shape[1]
    cols = []
    for lo in range(0, width, MXU_WIDTH):
        n = min(MXU_WIDTH, width - lo)
        cols.append(jnp.dot(x[:, lo:lo + n].astype(BF16), seg[0:n, 0:n], preferred_element_type=F32))
    return cols[0] if len(cols) == 1 else jnp.concatenate(cols, axis=1)


def _sigmoid(z):
    return 1.0 / (1.0 + jnp.exp(-z))


def _silu(z):
    return z * _sigmoid(z)


def _mod_kernel(c_ref, w_ref, b_ref, o_ref):
    o_ref[...] = jnp.dot(_silu(c_ref[...]), w_ref[...], precision=HIGHEST,
                         preferred_element_type=F32) + b_ref[...]


def _modulation(cc, w_mod, b_mod):
    depth, d, d3 = w_mod.shape
    nj = d3 // d
    return pl.pallas_call(
        _mod_kernel,
        grid=(depth, nj),
        in_specs=[pl.BlockSpec((8, d), lambda l, j: (0, 0)),
                  pl.BlockSpec((None, d, d), lambda l, j: (l, 0, j)),
                  pl.BlockSpec((None, 1, d), lambda l, j: (l, 0, j))],
        out_specs=pl.BlockSpec((None, 8, d), lambda l, j: (l, 0, j)),
        out_shape=jax.ShapeDtypeStruct((depth, 8, d3), F32),
        compiler_params=_cparams(("arbitrary", "arbitrary")),
        name="modulation",
    )(cc, w_mod, b_mod.reshape(depth, 1, d3))


def _stream_rows(ctx_ref, x_ref, ctx_tiles, tile):
    if ctx_ref is None:
        return x_ref[...]
    return jnp.where(tile < ctx_tiles, ctx_ref[...], x_ref[...])


def _stream_specs(two_sources, tm, d, ctx_tiles, tile_of=lambda i: i):
    if not two_sources:
        return [pl.BlockSpec((None, tm, d), lambda bi, i: (bi, tile_of(i), 0))]
    return [pl.BlockSpec((None, tm, d), lambda bi, i: (bi, jnp.minimum(tile_of(i), ctx_tiles - 1), 0)),
            pl.BlockSpec((None, tm, d), lambda bi, i: (bi, jnp.maximum(tile_of(i) - ctx_tiles, 0), 0))]


def _interleave(*stages):
    stages = list(stages)
    while stages:
        for stage in list(stages):
            if next(stage, "done") == "done":
                stages.remove(stage)


def _inproj_kernel(*refs, two_sources, ctx_tiles, n_tiles, q_width, qk_width, v_width, splits):
    n_src = 2 if two_sources else 1
    src = refs[:n_src]
    mod_ref, gpre_ref, w_ref, cos_ref, sin_ref, gain_ref, seg_ref = refs[n_src:n_src + 7]
    q_ref, k_ref, vt_ref = refs[n_src + 7:n_src + 10]
    extra_refs = refs[n_src + 10:-1]
    attn_s = refs[-1]
    step = pl.program_id(1)
    tile = jnp.minimum(step, n_tiles - 1)
    n_attn = qk_width + v_width

    @pl.when(step == 0)
    def _():
        attn_s[...] = jnp.zeros_like(attn_s)

    staged = []

    def project():
        x = _stream_rows(src[0] if two_sources else None, src[-1], ctx_tiles, tile)
        ms = jnp.mean(x * x, axis=-1, keepdims=True)
        h = x * lax.rsqrt(ms + RMS_EPS) * gpre_ref[...]
        hb = (h * (1.0 + mod_ref[1:2, :]) + mod_ref[0:1, :]).astype(BF16)
        yield
        for lo in range(0, n_attn, PROJ_COL_CHUNK):
            hi = min(lo + PROJ_COL_CHUNK, n_attn)
            staged.append((lo, hi, jnp.dot(hb, w_ref[:, lo:hi], preferred_element_type=F32)))
            yield
        for ref, (s_lo, s_hi) in zip(extra_refs, splits):
            for lo in range(s_lo, s_hi, PROJ_COL_CHUNK):
                hi = min(lo + PROJ_COL_CHUNK, s_hi)
                ref[:, lo - s_lo:hi - s_lo] = jnp.dot(hb, w_ref[:, lo:hi],
                                                      preferred_element_type=F32).astype(ref.dtype)
                yield

    def finish_previous():
        cos = cos_ref[...]
        sin = sin_ref[...]
        lane = lax.broadcasted_iota(jnp.int32, (1, LANES), 1)
        even_lane = (lane % 2) == 0
        qk = attn_s[:, 0:qk_width]
        msq_all = _head_mean(qk * qk, seg_ref[...])
        yield
        for g in range(qk_width // LANES):
            xg = attn_s[:, g * LANES:(g + 1) * LANES]
            msq = msq_all[:, g * LANES:(g + 1) * LANES]
            xn = xg * lax.rsqrt(msq + RMS_EPS) * gain_ref[:, g * LANES:(g + 1) * LANES]
            partner = jnp.where(even_lane, pltpu.roll(xn, LANES - 1, 1), pltpu.roll(xn, 1, 1))
            y = xn * cos + partner * sin
            is_q = g * LANES < q_width
            if is_q:
                y = y * Q_SCALE
            ref = q_ref if is_q else k_ref
            h0 = 2 * g if is_q else 2 * (g - q_width // LANES)
            ref[h0] = y[:, 0:HEAD_DIM].astype(ref.dtype)
            ref[h0 + 1] = y[:, HEAD_DIM:LANES].astype(ref.dtype)
            yield
        for g in range(v_width // LANES):
            vt = attn_s[:, qk_width + g * LANES:qk_width + (g + 1) * LANES].T
            vt_ref[2 * g] = vt[0:HEAD_DIM].astype(vt_ref.dtype)
            vt_ref[2 * g + 1] = vt[HEAD_DIM:LANES].astype(vt_ref.dtype)
            yield

    _interleave(project(), finish_previous())
    for lo, hi, acc in staged:
        attn_s[:, lo:hi] = acc


def _inproj(sources, modsel, gpre, w_bf16, cos_t, sin_t, gain, seg, *, q_width, qk_width, v_width,
            splits, out_dtypes, ctx_tiles):
    two_sources = len(sources) == 2
    b, d = sources[0].shape[0], sources[0].shape[2]
    s = sum(a.shape[1] for a in sources)
    n = w_bf16.shape[1]
    tm = ROW_TILE
    hq = q_width // HEAD_DIM
    hk = (qk_width - q_width) // HEAD_DIM
    hv = v_width // HEAD_DIM
    n_tiles = s // tm
    cur = lambda i: jnp.minimum(i, n_tiles - 1)
    done = lambda i: jnp.maximum(i - 1, 0)
    const = lambda bi, i: (0, 0)
    out_shapes = [jax.ShapeDtypeStruct((b, hq, s, HEAD_DIM), BF16),
                  jax.ShapeDtypeStruct((b, hk, s, HEAD_DIM), BF16),
                  jax.ShapeDtypeStruct((b, hv, HEAD_DIM, s), BF16)]
    out_specs = [pl.BlockSpec((None, hq, tm, HEAD_DIM), lambda bi, i: (bi, 0, done(i), 0)),
                 pl.BlockSpec((None, hk, tm, HEAD_DIM), lambda bi, i: (bi, 0, done(i), 0)),
                 pl.BlockSpec((None, hv, HEAD_DIM, tm), lambda bi, i: (bi, 0, 0, done(i)))]
    for (lo, hi), dt in zip(splits, out_dtypes):
        out_shapes.append(jax.ShapeDtypeStruct((b, s, hi - lo), dt))
        out_specs.append(pl.BlockSpec((None, tm, hi - lo), lambda bi, i: (bi, cur(i), 0)))
    kern = functools.partial(_inproj_kernel, two_sources=two_sources, ctx_tiles=ctx_tiles, n_tiles=n_tiles,
                             q_width=q_width, qk_width=qk_width, v_width=v_width, splits=splits)
    return pl.pallas_call(
        kern,
        grid=(b, n_tiles + 1),
        in_specs=_stream_specs(two_sources, tm, d, ctx_tiles, cur) + [
            pl.BlockSpec((None, None, 3, d),
                         lambda bi, i: (bi, jnp.minimum(cur(i) // ctx_tiles, 1), 0, 0)),
            pl.BlockSpec((1, d), const),
            pl.BlockSpec((d, n), const),
            pl.BlockSpec((tm, LANES), lambda bi, i: (done(i), 0)),
            pl.BlockSpec((tm, LANES), lambda bi, i: (done(i), 0)),
            pl.BlockSpec((1, qk_width), const),
            pl.BlockSpec((MXU_WIDTH, MXU_WIDTH), const)],
        out_specs=out_specs,
        out_shape=out_shapes,
        scratch_shapes=[pltpu.VMEM((tm, qk_width + v_width), F32)],
        compiler_params=_cparams(("parallel", "arbitrary")),
        name="inproj",
    )(*sources, modsel, gpre, w_bf16, cos_t, sin_t, gain, seg)


def _dense_attn_kernel(q_ref, k_ref, vt_ref, o_ref, *, tk, ctx_len):
    i = pl.program_id(2)
    r, tq, dh = q_ref.shape
    m_rows = r * tq
    n_keys = k_ref.shape[0]
    q = q_ref[...].reshape(m_rows, dh)
    contract_lanes = (((1,), (1,)), ((), ()))

    def scores(lo, hi):
        return lax.dot_general(k_ref[lo:hi, :], q, contract_lanes, preferred_element_type=F32)

    def attend(bounds):
        m = jnp.full((1, m_rows), -jnp.inf, F32)
        l = jnp.zeros((1, m_rows), F32)
        acc = jnp.zeros((dh, m_rows), F32)
        st = scores(*bounds[0])
        pending = None
        for j in range(len(bounds)):
            if pending is not None:
                (plo, phi), p_prev = pending
                acc = acc + jnp.dot(vt_ref[:, plo:phi], p_prev, preferred_element_type=F32)
            st_next = scores(*bounds[j + 1]) if j + 1 < len(bounds) else None
            m_new = jnp.maximum(m, jnp.max(st, axis=0, keepdims=True))
            alpha = jnp.exp2(m - m_new)
            p = jnp.exp2(st - m_new)
            l = alpha * l + jnp.sum(p, axis=0, keepdims=True)
            acc = acc * alpha
            pending = (bounds[j], p.astype(BF16))
            m, st = m_new, st_next
        (plo, phi), p_prev = pending
        acc = acc + jnp.dot(vt_ref[:, plo:phi], p_prev, preferred_element_type=F32)
        o = acc / l
        for hh in range(r):
            o_ref[hh] = o[:, hh * tq:(hh + 1) * tq].astype(o_ref.dtype)

    ctx_bounds = [(0, ctx_len)]
    all_bounds = [(0, ctx_len + tk)] + [(lo, lo + tk) for lo in range(ctx_len + tk, n_keys, tk)]

    @pl.when(i < ctx_len // tq)
    def _():
        attend(ctx_bounds)

    @pl.when(i >= ctx_len // tq)
    def _():
        attend(all_bounds)


def _dense_attention(qh, kh, vth, *, ctx_len):
    b, hq, s, dh = qh.shape
    g = kh.shape[1]
    r = hq // g
    tq = DENSE_Q_TILE
    assert (s - ctx_len) % DENSE_KV_TILE == 0 and ctx_len % tq == 0
    kern = functools.partial(_dense_attn_kernel, tk=DENSE_KV_TILE, ctx_len=ctx_len)
    return pl.pallas_call(
        kern,
        grid=(b, g, s // tq),
        in_specs=[pl.BlockSpec((None, r, tq, dh), lambda bi, gi, i: (bi, gi, i, 0)),
                  pl.BlockSpec((None, None, s, dh), lambda bi, gi, i: (bi, gi, 0, 0)),
                  pl.BlockSpec((None, None, dh, s), lambda bi, gi, i: (bi, gi, 0, 0))],
        out_specs=pl.BlockSpec((None, r, dh, tq), lambda bi, gi, i: (bi, gi, 0, i)),
        out_shape=jax.ShapeDtypeStruct((b, hq, dh, s), F32),
        compiler_params=_cparams(("parallel", "parallel", "parallel")),
        name="dense_attention",
    )(qh, kh, vth)


def _window_attn_kernel(sink_ref, q_ref, bias_first_ref, bias_mid_ref, bias_last_ref,
                        kc_ref, kp_ref, km_ref, kn_ref, vc_ref, vp_ref, vm_ref, vn_ref, o_ref):
    gi = pl.program_id(1)
    _, r, rows, dh = q_ref.shape
    tq = Q_BLOCK
    nb = rows // tq
    contract_lanes = (((1,), (1,)), ((), ()))
    k_all = jnp.concatenate([kp_ref[...], km_ref[0, 0], kn_ref[...]], axis=0)
    vt_all = jnp.concatenate([vp_ref[...], vm_ref[0, 0], vn_ref[...]], axis=1)
    k_ctx, vt_ctx = kc_ref[...], vc_ref[...]
    lane = lax.broadcasted_iota(jnp.int32, (1, r * tq), 1)
    sink = jnp.zeros((1, r * tq), F32)
    for hh in range(r):
        sink = jnp.where(lane // tq == hh, sink_ref[gi * r + hh] * LOG2_E, sink)

    def scores(qb):
        q = jnp.concatenate([q_ref[0, hh, qb * tq:(qb + 1) * tq, :] for hh in range(r)], axis=0)
        bias_ref = bias_first_ref if qb == 0 else (bias_last_ref if qb == nb - 1 else bias_mid_ref)
        s_loc = lax.dot_general(k_all[qb * tq:(qb + 3) * tq], q, contract_lanes,
                                preferred_element_type=F32) + bias_ref[...]
        s_ctx = lax.dot_general(k_ctx, q, contract_lanes, preferred_element_type=F32)
        return s_loc, s_ctx

    def finish(qb, p_loc, p_ctx, l):
        o = (jnp.dot(vt_all[:, qb * tq:(qb + 3) * tq], p_loc, preferred_element_type=F32)
             + jnp.dot(vt_ctx, p_ctx, preferred_element_type=F32)) / l
        for hh in range(r):
            o_ref[hh, :, qb * tq:(qb + 1) * tq] = o[:, hh * tq:(hh + 1) * tq].astype(o_ref.dtype)

    s_cur = scores(0)
    pending = None
    for qb in range(nb):
        if pending is not None:
            finish(*pending)
        s_next = scores(qb + 1) if qb + 1 < nb else None
        s_loc, s_ctx = s_cur
        m = jnp.maximum(jnp.maximum(jnp.max(s_loc, axis=0, keepdims=True),
                                    jnp.max(s_ctx, axis=0, keepdims=True)), sink)
        p_loc = jnp.exp2(s_loc - m)
        p_ctx = jnp.exp2(s_ctx - m)
        l = (jnp.sum(p_loc, axis=0, keepdims=True) + jnp.sum(p_ctx, axis=0, keepdims=True)
             + jnp.exp2(sink - m))
        pending = (qb, p_loc.astype(BF16), p_ctx.astype(BF16), l)
        s_cur = s_next
    finish(*pending)


def _window_bias(tq, r):
    key = np.arange(3 * tq)[:, None] - tq
    qpos = np.arange(tq)[None, :]
    band = np.abs(key - qpos) <= WINDOW
    variants = [band & (key >= 0), band, band & (key < tq)]
    table = np.stack([np.where(np.tile(v, (1, r)), 0.0, -np.inf) for v in variants])
    return jnp.asarray(table, F32)


def _window_attention(qh, kh, vth, sink, *, ctx_len):
    b, hq, s, dh = qh.shape
    g = kh.shape[1]
    r = hq // g
    tq = Q_BLOCK
    t = s - ctx_len
    nb = WINDOW_BLOCKS
    rows = nb * tq
    assert t % rows == 0 and ctx_len % tq == 0
    ns = t // rows
    off = ctx_len // tq
    last = t // tq - 1
    bias = _window_bias(tq, r)
    ctx = lambda bi, gi, i: (bi, gi, 0, 0)
    blk = lambda n_rows: (None, None, n_rows, dh)
    tblk = lambda n_cols: (None, None, dh, n_cols)
    prev_blk = lambda i: jnp.maximum(nb * i - 1, 0) + off
    next_blk = lambda i: jnp.minimum(nb * i + nb, last) + off
    el = pl.Element
    mid = lambda i: pl.multiple_of(ctx_len + i * rows, tq)
    return pl.pallas_call(
        _window_attn_kernel,
        grid=(b, g, ns),
        in_specs=[pl.BlockSpec(memory_space=pltpu.SMEM),
                  pl.BlockSpec((el(1), el(r), el(rows), el(dh)), lambda bi, gi, i: (bi, gi * r, mid(i), 0)),
                  pl.BlockSpec((None, 3 * tq, r * tq), lambda bi, gi, i: (jnp.minimum(i, 1), 0, 0)),
                  pl.BlockSpec((None, 3 * tq, r * tq), lambda bi, gi, i: (1, 0, 0)),
                  pl.BlockSpec((None, 3 * tq, r * tq),
                               lambda bi, gi, i: (jnp.where(i == ns - 1, 2, 1), 0, 0)),
                  pl.BlockSpec(blk(ctx_len), ctx),
                  pl.BlockSpec(blk(tq), lambda bi, gi, i: (bi, gi, prev_blk(i), 0)),
                  pl.BlockSpec((el(1), el(1), el(rows), el(dh)), lambda bi, gi, i: (bi, gi, mid(i), 0)),
                  pl.BlockSpec(blk(tq), lambda bi, gi, i: (bi, gi, next_blk(i), 0)),
                  pl.BlockSpec(tblk(ctx_len), ctx),
                  pl.BlockSpec(tblk(tq), lambda bi, gi, i: (bi, gi, 0, prev_blk(i))),
                  pl.BlockSpec((el(1), el(1), el(dh), el(rows)), lambda bi, gi, i: (bi, gi, 0, mid(i))),
                  pl.BlockSpec(tblk(tq), lambda bi, gi, i: (bi, gi, 0, next_blk(i)))],
        out_specs=pl.BlockSpec((None, r, dh, rows), lambda bi, gi, i: (bi, gi, 0, i)),
        out_shape=jax.ShapeDtypeStruct((b, hq, dh, t), F32),
        compiler_params=_cparams(("parallel", "parallel", "parallel")),
        name="window_attention",
    )(sink, qh, bias, bias, bias, kh, kh, kh, kh, vth, vth, vth, vth)


def _rwkv_prep_kernel(p_ref, hp_ref, hn_ref, mu_ref, w0_ref, w2_ref, a0_ref, a2_ref,
                      kk_ref, ka_ref, rk_ref, ones_ref,
                      m_ref, n_ref, q_ref, y0_ref, bonus_ref,
                      xf_s, kh_s, bh_s, kbar_s, bbar_s, rt_s, v_s, wend_s, *, ctx_chunks, n_chunks):
    step_id = pl.program_id(1)
    sub = PREP_CHUNKS
    c = CHUNK
    rows = sub * c
    jb = jnp.minimum(step_id, n_chunks // sub - 1)
    first_chunk = jb * sub
    last_chunk = first_chunk + sub - 1
    bw = kk_ref.shape[1]
    n_pairs = bw // LANES
    staged = (xf_s, kh_s, bh_s, kbar_s, bbar_s, rt_s, v_s, wend_s)

    @pl.when(step_id == 0)
    def _():
        for ref in staged:
            ref[...] = jnp.zeros_like(ref)

    gw = RWKV_GROUP_LANES
    hpg = gw // HEAD_DIM
    n_groups = bw // gw
    ti = lax.broadcasted_iota(jnp.int32, (rows, rows), 0)
    tj = lax.broadcasted_iota(jnp.int32, (rows, rows), 1)
    same_chunk = (ti // c) == (tj // c)
    ti_g = lax.broadcasted_iota(jnp.int32, (c, gw), 0)
    tj_g = lax.broadcasted_iota(jnp.int32, (c, gw), 1) % c
    eye_g = (ti_g == tj_g).astype(F32)
    gi_r = lax.broadcasted_iota(jnp.int32, (gw, gw), 0)
    gi_c = lax.broadcasted_iota(jnp.int32, (gw, gw), 1)
    group_diag = (gi_r // HEAD_DIM) == (gi_c // HEAD_DIM)
    bi = lax.broadcasted_iota(jnp.int32, (LANES, LANES), 0)
    bj = lax.broadcasted_iota(jnp.int32, (LANES, LANES), 1)
    same_head = (bi // HEAD_DIM) == (bj // HEAD_DIM)
    diag = bi == bj
    scan_masks = [(tj_g < ti_g, tj_g <= ti_g), (tj_g > ti_g, tj_g >= ti_g)]

    def blockdiag(xs):
        xb = xs.astype(BF16)
        return jnp.where(group_diag, jnp.concatenate([xb] * hpg, axis=0), jnp.zeros((), BF16))


    def elementwise_stage():
        p = p_ref[...]
        has_prev = jnp.logical_and(first_chunk != 0, first_chunk != ctx_chunks)
        has_next = jnp.logical_and(last_chunk != ctx_chunks - 1, last_chunk != n_chunks - 1)
        prev_row = jnp.where(has_prev, hp_ref[7:8, :], 0.0)
        next_row = jnp.where(has_next, hn_ref[0:1, :], 0.0)
        rowi = lax.broadcasted_iota(jnp.int32, (rows, 1), 0)
        p_prev = jnp.where(rowi == 0, prev_row, pltpu.roll(p, 1, 0))
        p_next = jnp.where(rowi == rows - 1, next_row, pltpu.roll(p, rows - 1, 0))
        ps = p + mu_ref[...] * (0.5 * (p_prev + p_next) - p)
        r = ps[:, 0:bw]
        k = ps[:, bw:2 * bw]
        v = ps[:, 2 * bw:3 * bw]
        zw = ps[:, 3 * bw:3 * bw + 2 * B_LORA]
        za = ps[:, 3 * bw + 2 * B_LORA:3 * bw + 4 * B_LORA]
        yield

        wl = w0_ref[...] + jnp.dot(jnp.tanh(zw).astype(BF16), w2_ref[...],
                                   preferred_element_type=F32)
        z = -wl
        softplus = jnp.maximum(z, 0.0) + jnp.log(1.0 + jnp.exp(-jnp.abs(z)))
        logw = -jnp.exp(-softplus - 0.5)
        yield
        a = _sigmoid(a0_ref[...] + jnp.dot(za.astype(BF16), a2_ref[...],
                                           preferred_element_type=F32))
        ones_seg = ones_ref[...]
        kf = k * kk_ref[...]
        kk = kf / jnp.maximum(jnp.sqrt(_head_mean(kf * kf, ones_seg)), 1e-12)
        yield

        bonus = jnp.zeros((rows, bw), F32)
        chunk_rows = [slice(ch * c, (ch + 1) * c) for ch in range(sub)]
        out = dict(xf=[], kh=[], bh=[], kbar=[], bbar=[], rt=[], wend=[])
        for d in range(2):
            lw = logw[:, d * bw:(d + 1) * bw]
            a_d = a[:, d * bw:(d + 1) * bw]
            b_d = kk * a_d
            kmod = k * (1.0 + (a_d - 1.0) * ka_ref[...])
            bonus = bonus + _head_mean(r * kmod * rk_ref[...], ones_seg) * v
            yield

            tri = (same_chunk & ((tj <= ti) if d == 0 else (tj >= ti))).astype(BF16)
            lw_hi = lw.astype(BF16)
            lw_lo = (lw - lw_hi.astype(F32)).astype(BF16)
            cum = (jnp.dot(tri, lw_hi, preferred_element_type=F32)
                   + jnp.dot(tri, lw_lo, preferred_element_type=F32))
            ends = [cum[rs.stop - 1:rs.stop] if d == 0 else cum[rs.start:rs.start + 1] for rs in chunk_rows]
            cum_end = jnp.concatenate([jnp.broadcast_to(e, (c, bw)) for e in ends], axis=0)
            r_t = r * jnp.exp(cum)
            kk_t = kk * jnp.exp(cum - lw)
            yield
            e_out = jnp.exp(-cum)
            tail = jnp.exp(cum_end - cum)
            out["xf"].append(jnp.concatenate([part[rs] for rs in chunk_rows for part in (kk_t, r_t)],
                                             axis=0).astype(BF16))
            out["rt"].append(r_t)
            out["kh"].append((kmod * e_out).astype(BF16))
            out["bh"].append((b_d * e_out).astype(BF16))
            out["kbar"].append((kmod * tail).astype(BF16))
            out["bbar"].append((b_d * tail).astype(BF16))
            out["wend"].append(jnp.concatenate([jnp.broadcast_to(jnp.exp(e), (8, bw)) for e in ends], axis=0))
            yield
        bonus_ref[...] = bonus
        out["v"] = v.astype(BF16)
        staged_next.update(out)

    def matmul_stage():
        contract_lanes = (((1,), (1,)), ((), ()))
        contract_rows = (((0,), (0,)), ((), ()))
        probs = [(d, g, ch) for d in range(2) for g in range(n_groups) for ch in range(sub)]
        lanes_of = lambda g: slice(g * gw, (g + 1) * gw)
        rows_of = lambda ch: slice(ch * c, (ch + 1) * c)
        xf_of = lambda d, g, ch: xf_s[d, ch * 2 * c:(ch + 1) * 2 * c, lanes_of(g)]
        v_bd = {(g, ch): blockdiag(v_s[rows_of(ch), lanes_of(g)]) for g in range(n_groups) for ch in range(sub)}
        aks = [lax.dot_general(xf_of(d, g, ch), blockdiag(kh_s[d, rows_of(ch), lanes_of(g)]), contract_lanes,
                               preferred_element_type=F32) for d, g, ch in probs]
        yield
        abs_ = [lax.dot_general(xf_of(d, g, ch), blockdiag(bh_s[d, rows_of(ch), lanes_of(g)]), contract_lanes,
                                preferred_element_type=F32) for d, g, ch in probs]
        yield
        a_ks = [jnp.concatenate([jnp.where(scan_masks[d][0], ak[0:c], 0.0),
                                 jnp.where(scan_masks[d][1], ak[c:2 * c], 0.0)], axis=0).astype(BF16)
                for (d, g, ch), ak in zip(probs, aks)]
        a_qbs = [jnp.where(scan_masks[d][1], ab[c:2 * c], 0.0).astype(BF16)
                 for (d, g, ch), ab in zip(probs, abs_)]
        pws = [jnp.where(scan_masks[d][0], -ab[0:c], 0.0) for (d, g, ch), ab in zip(probs, abs_)]
        tinvs = [eye_g + pw for pw in pws]
        pws = [jnp.dot(pw.astype(BF16), blockdiag(pw), preferred_element_type=F32) for pw in pws]
        yield
        for _ in range(4):
            stacked = [jnp.dot(jnp.concatenate([t, pw], axis=0).astype(BF16), blockdiag(pw),
                               preferred_element_type=F32) for t, pw in zip(tinvs, pws)]
            tinvs = [t + st[0:c] for t, st in zip(tinvs, stacked)]
            pws = [st[c:2 * c] for st in stacked]
            yield
        tinvs = [(t + jnp.dot(t.astype(BF16), blockdiag(pw), preferred_element_type=F32)).astype(BF16)
                 for t, pw in zip(tinvs, pws)]
        avs = [jnp.dot(a_k, v_bd[g, ch], preferred_element_type=F32) for (d, g, ch), a_k in zip(probs, a_ks)]
        yield
        p1s = [jnp.dot(t, blockdiag(xf_of(d, g, ch)[0:c]), preferred_element_type=F32)
               for (d, g, ch), t in zip(probs, tinvs)]
        p2s = [jnp.dot(t, blockdiag(av[0:c]), preferred_element_type=F32) for t, av in zip(tinvs, avs)]
        yield
        qp1s = [jnp.dot(a_qb, blockdiag(p1), preferred_element_type=F32) for a_qb, p1 in zip(a_qbs, p1s)]
        qp2s = [jnp.dot(a_qb, blockdiag(p2), preferred_element_type=F32) for a_qb, p2 in zip(a_qbs, p2s)]
        for (d, g, ch), av, qp1, qp2 in zip(probs, avs, qp1s, qp2s):
            q_ref[d, rows_of(ch), lanes_of(g)] = (rt_s[d, rows_of(ch), lanes_of(g)] - qp1).astype(q_ref.dtype)
            y0_ref[d, rows_of(ch), lanes_of(g)] = av[c:2 * c] - qp2
        yield
        for d in range(2):
            for pr in range(n_pairs):
                for ch in range(sub):
                    sl = slice(pr * LANES, (pr + 1) * LANES)
                    pidx = probs.index((d, pr * LANES // gw, ch))
                    lo = pr * LANES % gw
                    p12 = jnp.concatenate([p1s[pidx][:, lo:lo + LANES], p2s[pidx][:, lo:lo + LANES]],
                                          axis=1).astype(BF16)
                    bp = lax.dot_general(bbar_s[d, rows_of(ch), sl], p12, contract_rows,
                                         preferred_element_type=F32)
                    kv = lax.dot_general(kbar_s[d, rows_of(ch), sl], v_s[rows_of(ch), sl], contract_rows,
                                         preferred_element_type=F32)
                    w_diag = jnp.where(diag, wend_s[d, ch * 8:ch * 8 + 1, sl], 0.0)
                    m_ref[ch, d, pr] = (w_diag - jnp.where(same_head, bp[:, 0:LANES], 0.0)).astype(m_ref.dtype)
                    n_ref[ch, d, pr] = jnp.where(same_head, kv - bp[:, LANES:2 * LANES], 0.0)
            yield

    staged_next = {}
    _interleave(elementwise_stage(), matmul_stage())
    for ref, key in ((xf_s, "xf"), (kh_s, "kh"), (bh_s, "bh"), (kbar_s, "kbar"), (bbar_s, "bbar"),
                     (rt_s, "rt"), (wend_s, "wend")):
        for d in range(2):
            ref[d] = staged_next[key][d]
    v_s[...] = staged_next["v"]


def _rwkv_prep(bp, mu, w0, w2cat, a0, a2cat, kkw, kaw, rkw, ones_seg, *, ctx_len):
    b, s, pw = bp.shape
    bw = kkw.shape[1]
    n_pairs = bw // LANES
    nc = s // CHUNK
    ctx_chunks = ctx_len // CHUNK
    sub = PREP_CHUNKS
    rows = sub * CHUNK
    assert nc % sub == 0 and ctx_chunks % sub == 0
    n_blocks = nc // sub
    rb = rows // 8
    kern = functools.partial(_rwkv_prep_kernel, ctx_chunks=ctx_chunks, n_chunks=nc)
    const = lambda bi, j: (0, 0)
    cur = lambda j: jnp.minimum(j, n_blocks - 1)
    done = lambda j: jnp.maximum(j - 1, 0)
    return pl.pallas_call(
        kern,
        grid=(b, n_blocks + 1),
        in_specs=[pl.BlockSpec((None, rows, pw), lambda bi, j: (bi, cur(j), 0)),
                  pl.BlockSpec((None, 8, pw), lambda bi, j: (bi, jnp.maximum(cur(j) * rb - 1, 0), 0)),
                  pl.BlockSpec((None, 8, pw),
                               lambda bi, j: (bi, jnp.minimum((cur(j) + 1) * rb, s // 8 - 1), 0)),
                  pl.BlockSpec((1, pw), const),
                  pl.BlockSpec((1, 2 * bw), const),
                  pl.BlockSpec((2 * B_LORA, 2 * bw), const),
                  pl.BlockSpec((1, 2 * bw), const),
                  pl.BlockSpec((2 * B_LORA, 2 * bw), const),
                  pl.BlockSpec((1, bw), const),
                  pl.BlockSpec((1, bw), const),
                  pl.BlockSpec((1, bw), const),
                  pl.BlockSpec((MXU_WIDTH, MXU_WIDTH), const)],
        out_specs=[pl.BlockSpec((None, sub, 2, n_pairs, LANES, LANES), lambda bi, j: (bi, done(j), 0, 0, 0, 0)),
                   pl.BlockSpec((None, sub, 2, n_pairs, LANES, LANES), lambda bi, j: (bi, done(j), 0, 0, 0, 0)),
                   pl.BlockSpec((None, 2, rows, bw), lambda bi, j: (bi, 0, done(j), 0)),
                   pl.BlockSpec((None, 2, rows, bw), lambda bi, j: (bi, 0, done(j), 0)),
                   pl.BlockSpec((None, rows, bw), lambda bi, j: (bi, cur(j), 0))],
        out_shape=[jax.ShapeDtypeStruct((b, nc, 2, n_pairs, LANES, LANES), BF16),
                   jax.ShapeDtypeStruct((b, nc, 2, n_pairs, LANES, LANES), F32),
                   jax.ShapeDtypeStruct((b, 2, s, bw), BF16),
                   jax.ShapeDtypeStruct((b, 2, s, bw), F32),
                   jax.ShapeDtypeStruct((b, s, bw), F32)],
        scratch_shapes=[pltpu.VMEM((2, 2 * rows, bw), BF16),
                        pltpu.VMEM((2, rows, bw), BF16),
                        pltpu.VMEM((2, rows, bw), BF16),
                        pltpu.VMEM((2, rows, bw), BF16),
                        pltpu.VMEM((2, rows, bw), BF16),
                        pltpu.VMEM((2, rows, bw), F32),
                        pltpu.VMEM((rows, bw), BF16),
                        pltpu.VMEM((2, 8 * sub, bw), F32)],
        compiler_params=_cparams(("parallel", "arbitrary")),
        name="rwkv_prep",
    )(bp, bp, bp, mu, w0, w2cat, a0, a2cat, kkw, kaw, rkw, ones_seg)


def _rwkv_scan_kernel(m0_ref, m1_ref, n0_ref, n1_ref, q0_ref, q1_ref, y00_ref, y01_ref,
                      o0_ref, o1_ref, h_ref):
    j = pl.program_id(0)

    @pl.when(j == 0)
    def _():
        h_ref[...] = jnp.zeros_like(h_ref)

    nb, n_pairs = m0_ref.shape[0], m0_ref.shape[1]
    dirs = ((m0_ref, n0_ref, q0_ref, y00_ref, o0_ref), (m1_ref, n1_ref, q1_ref, y01_ref, o1_ref))
    for d, (m_ref, n_ref, q_ref, y0_ref, o_ref) in enumerate(dirs):
        for bi in range(nb):
            for pr in range(n_pairs):
                sl = slice(pr * LANES, (pr + 1) * LANES)
                h = h_ref[d, bi, pr]
                hb = h.astype(BF16)
                o_ref[bi, :, sl] = y0_ref[bi, :, sl] + jnp.dot(
                    q_ref[bi, :, sl], hb, preferred_element_type=F32)
                h_ref[d, bi, pr] = n_ref[bi, pr] + jnp.dot(
                    m_ref[bi, pr], hb, preferred_element_type=F32)


def _rwkv_scan(m, n, q, y0, *, ctx_len):
    b, nc, _, n_pairs, _, _ = m.shape
    s, bw = q.shape[2], q.shape[3]
    cc = ctx_len // CHUNK

    def mem_chunk(d, j):
        if d == 0:
            return j
        return jnp.where(j < cc, cc - 1 - j, nc - 1 + cc - j)

    def mn_spec(d):
        return pl.BlockSpec((b, None, None, n_pairs, LANES, LANES),
                            lambda j: (0, mem_chunk(d, j), d, 0, 0, 0))

    def row_spec(d):
        return pl.BlockSpec((b, None, CHUNK, bw), lambda j: (0, d, mem_chunk(d, j), 0))

    return pl.pallas_call(
        _rwkv_scan_kernel,
        grid=(nc,),
        in_specs=[mn_spec(0), mn_spec(1), mn_spec(0), mn_spec(1),
                  row_spec(0), row_spec(1), row_spec(0), row_spec(1)],
        out_specs=[pl.BlockSpec((b, CHUNK, bw), lambda j: (0, mem_chunk(0, j), 0)),
                   pl.BlockSpec((b, CHUNK, bw), lambda j: (0, mem_chunk(1, j), 0))],
        out_shape=[jax.ShapeDtypeStruct((b, s, bw), F32), jax.ShapeDtypeStruct((b, s, bw), F32)],
        scratch_shapes=[pltpu.VMEM((2, b, n_pairs, LANES, LANES), F32)],
        compiler_params=_cparams(("arbitrary",)),
        name="rwkv_scan",
    )(m, m, n, n, q, q, y0, y0)


def _outproj_kernel(*refs, two_sources, ctx_tiles, rwkv):
    n_src = 2 if two_sources else 1
    src = refs[:n_src]
    at_ref, g_ref, w_ref, mod_ref, gpost_ref = refs[n_src:n_src + 5]
    rest = refs[n_src + 5:]
    out_ref = rest[-1]
    hq, dh, tm = at_ref.shape
    parts = [at_ref[...].reshape(hq * dh, tm).T]
    if rwkv:
        yf_ref, yb_ref, bonus_ref, seg_ref, gnw_ref, gnb_ref = rest[:6]
        seg = seg_ref[...]
        y_sum = bonus_ref[...]
        for y_ref in (yf_ref, yb_ref):
            y = y_ref[...]
            yc = y - _head_mean(y, seg)
            var = _head_mean(yc * yc, seg)
            y_sum = y_sum + yc * lax.rsqrt(var + GN_EPS) * gnw_ref[...] + gnb_ref[...]
        parts.append(y_sum)
    o = jnp.concatenate(parts, axis=1) if len(parts) > 1 else parts[0]
    u = (o * _silu(g_ref[...])).astype(BF16)
    y = jnp.dot(u, w_ref[...], preferred_element_type=F32)
    ms = jnp.mean(y * y, axis=-1, keepdims=True)
    yn = y * lax.rsqrt(ms + RMS_EPS) * gpost_ref[...]
    x = _stream_rows(src[0] if two_sources else None, src[-1], ctx_tiles, pl.program_id(1))
    out_ref[...] = x + mod_ref[2:3, :] * yn


def _outproj(sources, attn_t, gate, w_bf16, modsel, gpost, rwkv_parts, *, ctx_tiles, latent_only):
    two_sources = len(sources) == 2
    b, d = sources[0].shape[0], sources[0].shape[2]
    s = sum(a.shape[1] for a in sources)
    tm = ROW_TILE
    off = ctx_tiles if latent_only else 0
    n_tiles = s // tm - off
    assert not (two_sources and latent_only)
    row = lambda bi, i: (bi, i + off, 0)
    const = lambda bi, i: (0, 0)
    hq, dh = attn_t.shape[1], attn_t.shape[2]
    if two_sources:
        src_specs = _stream_specs(True, tm, d, ctx_tiles)
    else:
        src_specs = [pl.BlockSpec((None, tm, d), row)]
    in_specs = src_specs + [
        pl.BlockSpec((None, hq, dh, tm), lambda bi, i: (bi, 0, 0, i)),
        pl.BlockSpec((None, tm, gate.shape[2]), row),
        pl.BlockSpec(w_bf16.shape, const),
        pl.BlockSpec((None, None, 3, d),
                     lambda bi, i: (bi, jnp.minimum((i + off) // ctx_tiles, 1), 0, 0)),
        pl.BlockSpec((1, d), const)]
    args = [*sources, attn_t, gate, w_bf16, modsel, gpost]
    if rwkv_parts is not None:
        y_f, y_b, bonus, seg, gnw, gnb = rwkv_parts
        in_specs += [pl.BlockSpec((None, tm, y_f.shape[2]), row),
                     pl.BlockSpec((None, tm, y_b.shape[2]), row),
                     pl.BlockSpec((None, tm, bonus.shape[2]), row),
                     pl.BlockSpec((MXU_WIDTH, MXU_WIDTH), const),
                     pl.BlockSpec(gnw.shape, const),
                     pl.BlockSpec(gnb.shape, const)]
        args += [y_f, y_b, bonus, seg, gnw, gnb]
    kern = functools.partial(_outproj_kernel, two_sources=two_sources, ctx_tiles=ctx_tiles,
                             rwkv=rwkv_parts is not None)
    return pl.pallas_call(
        kern,
        grid=(b, n_tiles),
        in_specs=in_specs,
        out_specs=pl.BlockSpec((None, tm, d), lambda bi, i: (bi, i, 0)),
        out_shape=jax.ShapeDtypeStruct((b, n_tiles * tm, d), F32),
        compiler_params=_cparams(("parallel", "parallel")),
        name="outproj",
    )(*args)


def _rope_tables(n_latent, ctx_len):
    t = jnp.arange(n_latent)
    rowp = (t // GRID_W).astype(F32)
    colp = (t % GRID_W).astype(F32)
    axis_dim = HEAD_DIM // 2
    inv = ROPE_THETA ** (-jnp.arange(0, axis_dim, 2, dtype=F32) / axis_dim)
    ang = jnp.concatenate([rowp[:, None] * inv, colp[:, None] * inv], axis=-1)
    cos, sin = jnp.cos(ang), jnp.sin(ang)
    cos = jnp.concatenate([jnp.ones((ctx_len, axis_dim), F32), cos], axis=0)
    sin = jnp.concatenate([jnp.zeros((ctx_len, axis_dim), F32), sin], axis=0)
    cos_h = jnp.repeat(cos, 2, axis=1)
    sin_h = jnp.stack([-sin, sin], axis=-1).reshape(sin.shape[0], HEAD_DIM)
    return jnp.tile(cos_h, (1, LANES // HEAD_DIM)), jnp.tile(sin_h, (1, LANES // HEAD_DIM))


def _block_diag_lora(w2):
    r, w = w2.shape[1], w2.shape[2]
    z = jnp.zeros((r, w), w2.dtype)
    return jnp.concatenate([jnp.concatenate([w2[0], z], axis=1),
                            jnp.concatenate([z, w2[1]], axis=1)], axis=0)


def kernel(x, c, ctx, c_ctx, w_mod, b_mod, g_pre, g_post, w_in_even, w_out_even, qn_a, kn_a, mu_b, w0_b, w2_b, a0_b, a2_b, kk_b, ka_b, rk_b, gn_w_b, gn_b_b, w_in_odd, w_out_odd, qn_c, kn_c, sink_c):
    b, t, d = x.shape
    ctx_len = ctx.shape[1]
    s = ctx_len + t
    assert ctx_len % ROW_TILE == 0 and t % ROW_TILE == 0 and b + 1 <= 8
    ctx_tiles = ctx_len // ROW_TILE
    depth = w_mod.shape[0]

    cc = jnp.concatenate([c, c_ctx[None, :], jnp.zeros((8 - b - 1, d), F32)], axis=0)
    mod = _modulation(cc, w_mod, b_mod)
    mod = mod.reshape(depth, 8, 3, d)
    modsel = jnp.stack([jnp.broadcast_to(mod[:, b][:, None], (depth, b, 3, d)), mod[:, :b]], axis=2)

    cos_t, sin_t = _rope_tables(t, ctx_len)
    seg_mean = jnp.asarray(np.kron(np.eye(MXU_WIDTH // HEAD_DIM),
                                   np.full((HEAD_DIM, HEAD_DIM), 1.0 / HEAD_DIM)), BF16)
    seg_ones = jnp.asarray(np.kron(np.eye(MXU_WIDTH // HEAD_DIM), np.ones((HEAD_DIM, HEAD_DIM))), BF16)

    bw = kk_b.shape[1]
    a_width = w_out_even.shape[1] - bw
    a_heads = a_width // HEAD_DIM
    n_in = w_in_even.shape[2]
    b_proj = 3 * bw + 4 * B_LORA
    kv_width = (n_in - 2 * a_width - b_proj - bw) // 2
    a_kv_heads = kv_width // HEAD_DIM
    qk_width = a_width + kv_width
    gain0 = jnp.concatenate([jnp.tile(qn_a[0], a_heads), jnp.tile(kn_a[0], a_kv_heads)])[None, :]
    bp_lo = qk_width + kv_width
    g_lo = bp_lo + b_proj
    qh, kh, vth, bproj, gate0 = _inproj(
        [ctx, x], modsel[0], g_pre[0][None, :], w_in_even[0].astype(BF16), cos_t, sin_t, gain0, seg_mean,
        q_width=a_width, qk_width=qk_width, v_width=kv_width,
        splits=((bp_lo, g_lo), (g_lo, n_in)), out_dtypes=(F32, F32), ctx_tiles=ctx_tiles)
    oa_t = _dense_attention(qh, kh, vth, ctx_len=ctx_len)

    m_c, n_c, q_c, y0_c, bonus = _rwkv_prep(
        bproj, mu_b[0][None, :], w0_b[0].reshape(1, 2 * bw), _block_diag_lora(w2_b[0]).astype(BF16),
        a0_b[0].reshape(1, 2 * bw), _block_diag_lora(a2_b[0]).astype(BF16), kk_b[0][None, :], ka_b[0][None, :],
        rk_b[0].reshape(1, bw), seg_ones, ctx_len=ctx_len)
    y_f, y_b = _rwkv_scan(m_c, n_c, q_c, y0_c, ctx_len=ctx_len)

    xc = _outproj([ctx, x], oa_t, gate0, w_out_even[0].astype(BF16), modsel[0], g_post[0][None, :],
                  (y_f, y_b, bonus, seg_mean, gn_w_b[0][None, :], gn_b_b[0][None, :]),
                  ctx_tiles=ctx_tiles, latent_only=False)

    c_heads = sink_c.shape[1]
    c_width = c_heads * HEAD_DIM
    n_in1 = w_in_odd.shape[2]
    ckv_width = (n_in1 - 2 * c_width) // 2
    c_kv_heads = ckv_width // HEAD_DIM
    qk_width1 = c_width + ckv_width
    gain1 = jnp.concatenate([jnp.tile(qn_c[0], c_heads), jnp.tile(kn_c[0], c_kv_heads)])[None, :]
    qh1, kh1, vth1, gate1 = _inproj(
        [xc], modsel[1], g_pre[1][None, :], w_in_odd[0].astype(BF16), cos_t, sin_t, gain1, seg_mean,
        q_width=c_width, qk_width=qk_width1, v_width=ckv_width,
        splits=((qk_width1 + ckv_width, n_in1),), out_dtypes=(F32,), ctx_tiles=ctx_tiles)
    ow_t = _window_attention(qh1, kh1, vth1, sink_c[0], ctx_len=ctx_len)
    return _outproj([xc], ow_t, gate1, w_out_odd[0].astype(BF16), modsel[1], g_post[1][None, :], None,
                    ctx_tiles=ctx_tiles, latent_only=True)
```

```python
import functools

import numpy as np
import jax
import jax.numpy as jnp
from jax import lax
from jax.experimental import pallas as pl
from jax.experimental.pallas import tpu as pltpu

F32 = jnp.float32
BF16 = jnp.bfloat16
HIGHEST = lax.Precision.HIGHEST

HEAD_DIM = 64
LANES = 128
MXU_WIDTH = 256
RWKV_GROUP_LANES = 128
GRID_W = 64
Q_BLOCK = 128
WINDOW = 128
ROPE_THETA = 10000.0
RMS_EPS = 1e-6
GN_EPS = 64e-5
LOG2_E = float(np.log2(np.e))
Q_SCALE = HEAD_DIM ** -0.5 * LOG2_E
B_LORA = 64
CHUNK = 64
PREP_CHUNKS = 2
ROW_TILE = 256
PROJ_COL_CHUNK = 512
DENSE_Q_TILE = 128
DENSE_KV_TILE = 1024
WINDOW_BLOCKS = 16
VMEM_LIMIT = 56 * 1024 * 1024


def _cparams(sem):
    return pltpu.CompilerParams(dimension_semantics=sem, vmem_limit_bytes=VMEM_LIMIT)


def _head_mean(x, seg):
    width = x.shape[1]
    cols = []
    for lo in range(0, width, MXU_WIDTH):
        n = min(MXU_WIDTH, width - lo)
        cols.append(jnp.dot(x[:, lo:lo + n].astype(BF16), seg[0:n, 0:n], preferred_element_type=F32))
    return cols[0] if len(cols) == 1 else jnp.concatenate(cols, axis=1)


def _sigmoid(z):
    return 1.0 / (1.0 + jnp.exp(-z))


def _silu(z):
    return z * _sigmoid(z)


def _mod_kernel(c_ref, w_ref, b_ref, o_ref):
    o_ref[...] = jnp.dot(_silu(c_ref[...]), w_ref[...], precision=HIGHEST,
                         preferred_element_type=F32) + b_ref[...]


def _modulation(cc, w_mod, b_mod):
    depth, d, d3 = w_mod.shape
    nj = d3 // d
    return pl.pallas_call(
        _mod_kernel,
        grid=(depth, nj),
        in_specs=[pl.BlockSpec((8, d), lambda l, j: (0, 0)),
                  pl.BlockSpec((None, d, d), lambda l, j: (l, 0, j)),
                  pl.BlockSpec((None, 1, d), lambda l, j: (l, 0, j))],
        out_specs=pl.BlockSpec((None, 8, d), lambda l, j: (l, 0, j)),
        out_shape=jax.ShapeDtypeStruct((depth, 8, d3), F32),
        compiler_params=_cparams(("arbitrary", "arbitrary")),
        name="modulation",
    )(cc, w_mod, b_mod.reshape(depth, 1, d3))


def _stream_rows(ctx_ref, x_ref, ctx_tiles, tile):
    if ctx_ref is None:
        return x_ref[...]
    return jnp.where(tile < ctx_tiles, ctx_ref[...], x_ref[...])


def _stream_specs(two_sources, tm, d, ctx_tiles, tile_of=lambda i: i):
    if not two_sources:
        return [pl.BlockSpec((None, tm, d), lambda bi, i: (bi, tile_of(i), 0))]
    return [pl.BlockSpec((None, tm, d), lambda bi, i: (bi, jnp.minimum(tile_of(i), ctx_tiles - 1), 0)),
            pl.BlockSpec((None, tm, d), lambda bi, i: (bi, jnp.maximum(tile_of(i) - ctx_tiles, 0), 0))]


def _interleave(*stages):
    stages = list(stages)
    while stages:
        for stage in list(stages):
            if next(stage, "done") == "done":
                stages.remove(stage)


def _inproj_kernel(*refs, two_sources, ctx_tiles, n_tiles, q_width, qk_width, v_width, splits):
    n_src = 2 if two_sources else 1
    src = refs[:n_src]
    mod_ref, gpre_ref, w_ref, cos_ref, sin_ref, gain_ref, seg_ref = refs[n_src:n_src + 7]
    q_ref, k_ref, vt_ref = refs[n_src + 7:n_src + 10]
    extra_refs = refs[n_src + 10:-1]
    attn_s = refs[-1]
    step = pl.program_id(1)
    tile = jnp.minimum(step, n_tiles - 1)
    n_attn = qk_width + v_width

    @pl.when(step == 0)
    def _():
        attn_s[...] = jnp.zeros_like(attn_s)

    staged = []

    def project():
        x = _stream_rows(src[0] if two_sources else None, src[-1], ctx_tiles, tile)
        ms = jnp.mean(x * x, axis=-1, keepdims=True)
        h = x * lax.rsqrt(ms + RMS_EPS) * gpre_ref[...]
        hb = (h * (1.0 + mod_ref[1:2, :]) + mod_ref[0:1, :]).astype(BF16)
        yield
        for lo in range(0, n_attn, PROJ_COL_CHUNK):
            hi = min(lo + PROJ_COL_CHUNK, n_attn)
            staged.append((lo, hi, jnp.dot(hb, w_ref[:, lo:hi], preferred_element_type=F32)))
            yield
        for ref, (s_lo, s_hi) in zip(extra_refs, splits):
            for lo in range(s_lo, s_hi, PROJ_COL_CHUNK):
                hi = min(lo + PROJ_COL_CHUNK, s_hi)
                ref[:, lo - s_lo:hi - s_lo] = jnp.dot(hb, w_ref[:, lo:hi],
                                                      preferred_element_type=F32).astype(ref.dtype)
                yield

    def finish_previous():
        cos = cos_ref[...]
        sin = sin_ref[...]
        lane = lax.broadcasted_iota(jnp.int32, (1, LANES), 1)
        even_lane = (lane % 2) == 0
        qk = attn_s[:, 0:qk_width]
        msq_all = _head_mean(qk * qk, seg_ref[...])
        yield
        for g in range(qk_width // LANES):
            xg = attn_s[:, g * LANES:(g + 1) * LANES]
            msq = msq_all[:, g * LANES:(g + 1) * LANES]
            xn = xg * lax.rsqrt(msq + RMS_EPS) * gain_ref[:, g * LANES:(g + 1) * LANES]
            partner = jnp.where(even_lane, pltpu.roll(xn, LANES - 1, 1), pltpu.roll(xn, 1, 1))
            y = xn * cos + partner * sin
            is_q = g * LANES < q_width
            if is_q:
                y = y * Q_SCALE
            ref = q_ref if is_q else k_ref
            h0 = 2 * g if is_q else 2 * (g - q_width // LANES)
            ref[h0] = y[:, 0:HEAD_DIM].astype(ref.dtype)
            ref[h0 + 1] = y[:, HEAD_DIM:LANES].astype(ref.dtype)
            yield
        for g in range(v_width // LANES):
            vt = attn_s[:, qk_width + g * LANES:qk_width + (g + 1) * LANES].T
            vt_ref[2 * g] = vt[0:HEAD_DIM].astype(vt_ref.dtype)
            vt_ref[2 * g + 1] = vt[HEAD_DIM:LANES].astype(vt_ref.dtype)
            yield

    _interleave(project(), finish_previous())
    for lo, hi, acc in staged:
        attn_s[:, lo:hi] = acc


def _inproj(sources, modsel, gpre, w_bf16, cos_t, sin_t, gain, seg, *, q_width, qk_width, v_width,
            splits, out_dtypes, ctx_tiles):
    two_sources = len(sources) == 2
    b, d = sources[0].shape[0], sources[0].shape[2]
    s = sum(a.shape[1] for a in sources)
    n = w_bf16.shape[1]
    tm = ROW_TILE
    hq = q_width // HEAD_DIM
    hk = (qk_width - q_width) // HEAD_DIM
    hv = v_width // HEAD_DIM
    n_tiles = s // tm
    cur = lambda i: jnp.minimum(i, n_tiles - 1)
    done = lambda i: jnp.maximum(i - 1, 0)
    const = lambda bi, i: (0, 0)
    out_shapes = [jax.ShapeDtypeStruct((b, hq, s, HEAD_DIM), BF16),
                  jax.ShapeDtypeStruct((b, hk, s, HEAD_DIM), BF16),
                  jax.ShapeDtypeStruct((b, hv, HEAD_DIM, s), BF16)]
    out_specs = [pl.BlockSpec((None, hq, tm, HEAD_DIM), lambda bi, i: (bi, 0, done(i), 0)),
                 pl.BlockSpec((None, hk, tm, HEAD_DIM), lambda bi, i: (bi, 0, done(i), 0)),
                 pl.BlockSpec((None, hv, HEAD_DIM, tm), lambda bi, i: (bi, 0, 0, done(i)))]
    for (lo, hi), dt in zip(splits, out_dtypes):
        out_shapes.append(jax.ShapeDtypeStruct((b, s, hi - lo), dt))
        out_specs.append(pl.BlockSpec((None, tm, hi - lo), lambda bi, i: (bi, cur(i), 0)))
    kern = functools.partial(_inproj_kernel, two_sources=two_sources, ctx_tiles=ctx_tiles, n_tiles=n_tiles,
                             q_width=q_width, qk_width=qk_width, v_width=v_width, splits=splits)
    return pl.pallas_call(
        kern,
        grid=(b, n_tiles + 1),
        in_specs=_stream_specs(two_sources, tm, d, ctx_tiles, cur) + [
            pl.BlockSpec((None, None, 3, d),
                         lambda bi, i: (bi, jnp.minimum(cur(i) // ctx_tiles, 1), 0, 0)),
            pl.BlockSpec((1, d), const),
            pl.BlockSpec((d, n), const),
            pl.BlockSpec((tm, LANES), lambda bi, i: (done(i), 0)),
            pl.BlockSpec((tm, LANES), lambda bi, i: (done(i), 0)),
            pl.BlockSpec((1, qk_width), const),
            pl.BlockSpec((MXU_WIDTH, MXU_WIDTH), const)],
        out_specs=out_specs,
        out_shape=out_shapes,
        scratch_shapes=[pltpu.VMEM((tm, qk_width + v_width), F32)],
        compiler_params=_cparams(("parallel", "arbitrary")),
        name="inproj",
    )(*sources, modsel, gpre, w_bf16, cos_t, sin_t, gain, seg)


def _dense_attn_kernel(q_ref, k_ref, vt_ref, o_ref, *, tk, ctx_len):
    i = pl.program_id(2)
    r, tq, dh = q_ref.shape
    m_rows = r * tq
    n_keys = k_ref.shape[0]
    q = q_ref[...].reshape(m_rows, dh)
    contract_lanes = (((1,), (1,)), ((), ()))

    def scores(lo, hi):
        return lax.dot_general(k_ref[lo:hi, :], q, contract_lanes, preferred_element_type=F32)

    def attend(bounds):
        m = jnp.full((1, m_rows), -jnp.inf, F32)
        l = jnp.zeros((1, m_rows), F32)
        acc = jnp.zeros((dh, m_rows), F32)
        st = scores(*bounds[0])
        pending = None
        for j in range(len(bounds)):
            if pending is not None:
                (plo, phi), p_prev = pending
                acc = acc + jnp.dot(vt_ref[:, plo:phi], p_prev, preferred_element_type=F32)
            st_next = scores(*bounds[j + 1]) if j + 1 < len(bounds) else None
            m_new = jnp.maximum(m, jnp.max(st, axis=0, keepdims=True))
            alpha = jnp.exp2(m - m_new)
            p = jnp.exp2(st - m_new)
            l = alpha * l + jnp.sum(p, axis=0, keepdims=True)
            acc = acc * alpha
            pending = (bounds[j], p.astype(BF16))
            m, st = m_new, st_next
        (plo, phi), p_prev = pending
        acc = acc + jnp.dot(vt_ref[:, plo:phi], p_prev, preferred_element_type=F32)
        o = acc / l
        for hh in range(r):
            o_ref[hh] = o[:, hh * tq:(hh + 1) * tq].astype(o_ref.dtype)

    ctx_bounds = [(0, ctx_len)]
    all_bounds = ctx_bounds + [(lo, lo + tk) for lo in range(ctx_len, n_keys, tk)]

    @pl.when(i < ctx_len // tq)
    def _():
        attend(ctx_bounds)

    @pl.when(i >= ctx_len // tq)
    def _():
        attend(all_bounds)


def _dense_attention(qh, kh, vth, *, ctx_len):
    b, hq, s, dh = qh.shape
    g = kh.shape[1]
    r = hq // g
    tq = DENSE_Q_TILE
    assert (s - ctx_len) % DENSE_KV_TILE == 0 and ctx_len % tq == 0
    kern = functools.partial(_dense_attn_kernel, tk=DENSE_KV_TILE, ctx_len=ctx_len)
    return pl.pallas_call(
        kern,
        grid=(b, g, s // tq),
        in_specs=[pl.BlockSpec((None, r, tq, dh), lambda bi, gi, i: (bi, gi, i, 0)),
                  pl.BlockSpec((None, None, s, dh), lambda bi, gi, i: (bi, gi, 0, 0)),
                  pl.BlockSpec((None, None, dh, s), lambda bi, gi, i: (bi, gi, 0, 0))],
        out_specs=pl.BlockSpec((None, r, dh, tq), lambda bi, gi, i: (bi, gi, 0, i)),
        out_shape=jax.ShapeDtypeStruct((b, hq, dh, s), F32),
        compiler_params=_cparams(("parallel", "parallel", "parallel")),
        name="dense_attention",
    )(qh, kh, vth)


def _window_attn_kernel(sink_ref, q_ref, bias_first_ref, bias_mid_ref, bias_last_ref,
                        kc_ref, kp_ref, km_ref, kn_ref, vc_ref, vp_ref, vm_ref, vn_ref, o_ref):
    gi = pl.program_id(1)
    _, r, rows, dh = q_ref.shape
    tq = Q_BLOCK
    nb = rows // tq
    contract_lanes = (((1,), (1,)), ((), ()))
    k_all = jnp.concatenate([kp_ref[...], km_ref[0, 0], kn_ref[...]], axis=0)
    vt_all = jnp.concatenate([vp_ref[...], vm_ref[0, 0], vn_ref[...]], axis=1)
    k_ctx, vt_ctx = kc_ref[...], vc_ref[...]
    lane = lax.broadcasted_iota(jnp.int32, (1, r * tq), 1)
    sink = jnp.zeros((1, r * tq), F32)
    for hh in range(r):
        sink = jnp.where(lane // tq == hh, sink_ref[gi * r + hh] * LOG2_E, sink)

    def scores(qb):
        q = jnp.concatenate([q_ref[0, hh, qb * tq:(qb + 1) * tq, :] for hh in range(r)], axis=0)
        bias_ref = bias_first_ref if qb == 0 else (bias_last_ref if qb == nb - 1 else bias_mid_ref)
        s_loc = lax.dot_general(k_all[qb * tq:(qb + 3) * tq], q, contract_lanes,
                                preferred_element_type=F32) + bias_ref[...]
        s_ctx = lax.dot_general(k_ctx, q, contract_lanes, preferred_element_type=F32)
        return s_loc, s_ctx

    def finish(qb, p_loc, p_ctx, l):
        o = (jnp.dot(vt_all[:, qb * tq:(qb + 3) * tq], p_loc, preferred_element_type=F32)
             + jnp.dot(vt_ctx, p_ctx, preferred_element_type=F32)) / l
        for hh in range(r):
            o_ref[hh, :, qb * tq:(qb + 1) * tq] = o[:, hh * tq:(hh + 1) * tq].astype(o_ref.dtype)

    s_cur = scores(0)
    pending = None
    for qb in range(nb):
        if pending is not None:
            finish(*pending)
        s_next = scores(qb + 1) if qb + 1 < nb else None
        s_loc, s_ctx = s_cur
        m = jnp.maximum(jnp.maximum(jnp.max(s_loc, axis=0, keepdims=True),
                                    jnp.max(s_ctx, axis=0, keepdims=True)), sink)
        p_loc = jnp.exp2(s_loc - m)
        p_ctx = jnp.exp2(s_ctx - m)
        l = (jnp.sum(p_loc, axis=0, keepdims=True) + jnp.sum(p_ctx, axis=0, keepdims=True)
             + jnp.exp2(sink - m))
        pending = (qb, p_loc.astype(BF16), p_ctx.astype(BF16), l)
        s_cur = s_next
    finish(*pending)


def _window_bias(tq, r):
    key = np.arange(3 * tq)[:, None] - tq
    qpos = np.arange(tq)[None, :]
    band = np.abs(key - qpos) <= WINDOW
    variants = [band & (key >= 0), band, band & (key < tq)]
    table = np.stack([np.where(np.tile(v, (1, r)), 0.0, -np.inf) for v in variants])
    return jnp.asarray(table, F32)


def _window_attention(qh, kh, vth, sink, *, ctx_len):
    b, hq, s, dh = qh.shape
    g = kh.shape[1]
    r = hq // g
    tq = Q_BLOCK
    t = s - ctx_len
    nb = WINDOW_BLOCKS
    rows = nb * tq
    assert t % rows == 0 and ctx_len % tq == 0
    ns = t // rows
    off = ctx_len // tq
    last = t // tq - 1
    bias = _window_bias(tq, r)
    ctx = lambda bi, gi, i: (bi, gi, 0, 0)
    blk = lambda n_rows: (None, None, n_rows, dh)
    tblk = lambda n_cols: (None, None, dh, n_cols)
    prev_blk = lambda i: jnp.maximum(nb * i - 1, 0) + off
    next_blk = lambda i: jnp.minimum(nb * i + nb, last) + off
    el = pl.Element
    mid = lambda i: pl.multiple_of(ctx_len + i * rows, tq)
    return pl.pallas_call(
        _window_attn_kernel,
        grid=(b, g, ns),
        in_specs=[pl.BlockSpec(memory_space=pltpu.SMEM),
                  pl.BlockSpec((el(1), el(r), el(rows), el(dh)), lambda bi, gi, i: (bi, gi * r, mid(i), 0)),
                  pl.BlockSpec((None, 3 * tq, r * tq), lambda bi, gi, i: (jnp.minimum(i, 1), 0, 0)),
                  pl.BlockSpec((None, 3 * tq, r * tq), lambda bi, gi, i: (1, 0, 0)),
                  pl.BlockSpec((None, 3 * tq, r * tq),
                               lambda bi, gi, i: (jnp.where(i == ns - 1, 2, 1), 0, 0)),
                  pl.BlockSpec(blk(ctx_len), ctx),
                  pl.BlockSpec(blk(tq), lambda bi, gi, i: (bi, gi, prev_blk(i), 0)),
                  pl.BlockSpec((el(1), el(1), el(rows), el(dh)), lambda bi, gi, i: (bi, gi, mid(i), 0)),
                  pl.BlockSpec(blk(tq), lambda bi, gi, i: (bi, gi, next_blk(i), 0)),
                  pl.BlockSpec(tblk(ctx_len), ctx),
                  pl.BlockSpec(tblk(tq), lambda bi, gi, i: (bi, gi, 0, prev_blk(i))),
                  pl.BlockSpec((el(1), el(1), el(dh), el(rows)), lambda bi, gi, i: (bi, gi, 0, mid(i))),
                  pl.BlockSpec(tblk(tq), lambda bi, gi, i: (bi, gi, 0, next_blk(i)))],
        out_specs=pl.BlockSpec((None, r, dh, rows), lambda bi, gi, i: (bi, gi, 0, i)),
        out_shape=jax.ShapeDtypeStruct((b, hq, dh, t), F32),
        compiler_params=_cparams(("parallel", "parallel", "parallel")),
        name="window_attention",
    )(sink, qh, bias, bias, bias, kh, kh, kh, kh, vth, vth, vth, vth)


def _rwkv_prep_kernel(p_ref, hp_ref, hn_ref, mu_ref, w0_ref, w2_ref, a0_ref, a2_ref,
                      kk_ref, ka_ref, rk_ref, ones_ref,
                      m_ref, n_ref, q_ref, y0_ref, bonus_ref,
                      xf_s, kh_s, bh_s, kbar_s, bbar_s, rt_s, v_s, wend_s, *, ctx_chunks, n_chunks):
    step_id = pl.program_id(1)
    sub = PREP_CHUNKS
    c = CHUNK
    rows = sub * c
    jb = jnp.minimum(step_id, n_chunks // sub - 1)
    first_chunk = jb * sub
    last_chunk = first_chunk + sub - 1
    bw = kk_ref.shape[1]
    n_pairs = bw // LANES
    staged = (xf_s, kh_s, bh_s, kbar_s, bbar_s, rt_s, v_s, wend_s)

    @pl.when(step_id == 0)
    def _():
        for ref in staged:
            ref[...] = jnp.zeros_like(ref)

    gw = RWKV_GROUP_LANES
    hpg = gw // HEAD_DIM
    n_groups = bw // gw
    ti = lax.broadcasted_iota(jnp.int32, (rows, rows), 0)
    tj = lax.broadcasted_iota(jnp.int32, (rows, rows), 1)
    same_chunk = (ti // c) == (tj // c)
    ti_g = lax.broadcasted_iota(jnp.int32, (c, gw), 0)
    tj_g = lax.broadcasted_iota(jnp.int32, (c, gw), 1) % c
    eye_g = (ti_g == tj_g).astype(F32)
    gi_r = lax.broadcasted_iota(jnp.int32, (gw, gw), 0)
    gi_c = lax.broadcasted_iota(jnp.int32, (gw, gw), 1)
    group_diag = (gi_r // HEAD_DIM) == (gi_c // HEAD_DIM)
    bi = lax.broadcasted_iota(jnp.int32, (LANES, LANES), 0)
    bj = lax.broadcasted_iota(jnp.int32, (LANES, LANES), 1)
    same_head = (bi // HEAD_DIM) == (bj // HEAD_DIM)
    diag = bi == bj
    scan_masks = [(tj_g < ti_g, tj_g <= ti_g), (tj_g > ti_g, tj_g >= ti_g)]

    def blockdiag(xs):
        xb = xs.astype(BF16)
        return jnp.where(group_diag, jnp.concatenate([xb] * hpg, axis=0), jnp.zeros((), BF16))


    def elementwise_stage():
        p = p_ref[...]
        has_prev = jnp.logical_and(first_chunk != 0, first_chunk != ctx_chunks)
        has_next = jnp.logical_and(last_chunk != ctx_chunks - 1, last_chunk != n_chunks - 1)
        prev_row = jnp.where(has_prev, hp_ref[7:8, :], 0.0)
        next_row = jnp.where(has_next, hn_ref[0:1, :], 0.0)
        rowi = lax.broadcasted_iota(jnp.int32, (rows, 1), 0)
        p_prev = jnp.where(rowi == 0, prev_row, pltpu.roll(p, 1, 0))
        p_next = jnp.where(rowi == rows - 1, next_row, pltpu.roll(p, rows - 1, 0))
        ps = p + mu_ref[...] * (0.5 * (p_prev + p_next) - p)
        r = ps[:, 0:bw]
        k = ps[:, bw:2 * bw]
        v = ps[:, 2 * bw:3 * bw]
        zw = ps[:, 3 * bw:3 * bw + 2 * B_LORA]
        za = ps[:, 3 * bw + 2 * B_LORA:3 * bw + 4 * B_LORA]
        yield

        wl = w0_ref[...] + jnp.dot(jnp.tanh(zw).astype(BF16), w2_ref[...],
                                   preferred_element_type=F32)
        z = -wl
        softplus = jnp.maximum(z, 0.0) + jnp.log(1.0 + jnp.exp(-jnp.abs(z)))
        logw = -jnp.exp(-softplus - 0.5)
        yield
        a = _sigmoid(a0_ref[...] + jnp.dot(za.astype(BF16), a2_ref[...],
                                           preferred_element_type=F32))
        ones_seg = ones_ref[...]
        kf = k * kk_ref[...]
        kk = kf / jnp.maximum(jnp.sqrt(_head_mean(kf * kf, ones_seg)), 1e-12)
        yield

        bonus = jnp.zeros((rows, bw), F32)
        chunk_rows = [slice(ch * c, (ch + 1) * c) for ch in range(sub)]
        out = dict(xf=[], kh=[], bh=[], kbar=[], bbar=[], rt=[], wend=[])
        for d in range(2):
            lw = logw[:, d * bw:(d + 1) * bw]
            a_d = a[:, d * bw:(d + 1) * bw]
            b_d = kk * a_d
            kmod = k * (1.0 + (a_d - 1.0) * ka_ref[...])
            bonus = bonus + _head_mean(r * kmod * rk_ref[...], ones_seg) * v
            yield

            tri = (same_chunk & ((tj <= ti) if d == 0 else (tj >= ti))).astype(BF16)
            lw_hi = lw.astype(BF16)
            lw_lo = (lw - lw_hi.astype(F32)).astype(BF16)
            cum = (jnp.dot(tri, lw_hi, preferred_element_type=F32)
                   + jnp.dot(tri, lw_lo, preferred_element_type=F32))
            ends = [cum[rs.stop - 1:rs.stop] if d == 0 else cum[rs.start:rs.start + 1] for rs in chunk_rows]
            cum_end = jnp.concatenate([jnp.broadcast_to(e, (c, bw)) for e in ends], axis=0)
            r_t = r * jnp.exp(cum)
            kk_t = kk * jnp.exp(cum - lw)
            yield
            e_out = jnp.exp(-cum)
            tail = jnp.exp(cum_end - cum)
            out["xf"].append(jnp.concatenate([part[rs] for rs in chunk_rows for part in (kk_t, r_t)],
                                             axis=0).astype(BF16))
            out["rt"].append(r_t)
            out["kh"].append((kmod * e_out).astype(BF16))
            out["bh"].append((b_d * e_out).astype(BF16))
            out["kbar"].append((kmod * tail).astype(BF16))
            out["bbar"].append((b_d * tail).astype(BF16))
            out["wend"].append(jnp.concatenate([jnp.broadcast_to(jnp.exp(e), (8, bw)) for e in ends], axis=0))
            yield
        bonus_ref[...] = bonus
        out["v"] = v.astype(BF16)
        staged_next.update(out)

    def matmul_stage():
        contract_lanes = (((1,), (1,)), ((), ()))
        contract_rows = (((0,), (0,)), ((), ()))
        probs = [(d, g, ch) for d in range(2) for g in range(n_groups) for ch in range(sub)]
        lanes_of = lambda g: slice(g * gw, (g + 1) * gw)
        rows_of = lambda ch: slice(ch * c, (ch + 1) * c)
        xf_of = lambda d, g, ch: xf_s[d, ch * 2 * c:(ch + 1) * 2 * c, lanes_of(g)]
        v_bd = {(g, ch): blockdiag(v_s[rows_of(ch), lanes_of(g)]) for g in range(n_groups) for ch in range(sub)}
        aks = [lax.dot_general(xf_of(d, g, ch), blockdiag(kh_s[d, rows_of(ch), lanes_of(g)]), contract_lanes,
                               preferred_element_type=F32) for d, g, ch in probs]
        yield
        abs_ = [lax.dot_general(xf_of(d, g, ch), blockdiag(bh_s[d, rows_of(ch), lanes_of(g)]), contract_lanes,
                                preferred_element_type=F32) for d, g, ch in probs]
        yield
        a_ks = [jnp.concatenate([jnp.where(scan_masks[d][0], ak[0:c], 0.0),
                                 jnp.where(scan_masks[d][1], ak[c:2 * c], 0.0)], axis=0).astype(BF16)
                for (d, g, ch), ak in zip(probs, aks)]
        a_qbs = [jnp.where(scan_masks[d][1], ab[c:2 * c], 0.0).astype(BF16)
                 for (d, g, ch), ab in zip(probs, abs_)]
        pws = [jnp.where(scan_masks[d][0], -ab[0:c], 0.0) for (d, g, ch), ab in zip(probs, abs_)]
        tinvs = [eye_g + pw for pw in pws]
        pws = [jnp.dot(pw.astype(BF16), blockdiag(pw), preferred_element_type=F32) for pw in pws]
        yield
        for _ in range(4):
            stacked = [jnp.dot(jnp.concatenate([t, pw], axis=0).astype(BF16), blockdiag(pw),
                               preferred_element_type=F32) for t, pw in zip(tinvs, pws)]
            tinvs = [t + st[0:c] for t, st in zip(tinvs, stacked)]
            pws = [st[c:2 * c] for st in stacked]
            yield
        tinvs = [(t + jnp.dot(t.astype(BF16), blockdiag(pw), preferred_element_type=F32)).astype(BF16)
                 for t, pw in zip(tinvs, pws)]
        avs = [jnp.dot(a_k, v_bd[g, ch], preferred_element_type=F32) for (d, g, ch), a_k in zip(probs, a_ks)]
        yield
        p1s = [jnp.dot(t, blockdiag(xf_of(d, g, ch)[0:c]), preferred_element_type=F32)
               for (d, g, ch), t in zip(probs, tinvs)]
        p2s = [jnp.dot(t, blockdiag(av[0:c]), preferred_element_type=F32) for t, av in zip(tinvs, avs)]
        yield
        qp1s = [jnp.dot(a_qb, blockdiag(p1), preferred_element_type=F32) for a_qb, p1 in zip(a_qbs, p1s)]
        qp2s = [jnp.dot(a_qb, blockdiag(p2), preferred_element_type=F32) for a_qb, p2 in zip(a_qbs, p2s)]
        for (d, g, ch), av, qp1, qp2 in zip(probs, avs, qp1s, qp2s):
            q_ref[d, rows_of(ch), lanes_of(g)] = (rt_s[d, rows_of(ch), lanes_of(g)] - qp1).astype(q_ref.dtype)
            y0_ref[d, rows_of(ch), lanes_of(g)] = av[c:2 * c] - qp2
        yield
        for d in range(2):
            for pr in range(n_pairs):
                for ch in range(sub):
                    sl = slice(pr * LANES, (pr + 1) * LANES)
                    pidx = probs.index((d, pr * LANES // gw, ch))
                    lo = pr * LANES % gw
                    p12 = jnp.concatenate([p1s[pidx][:, lo:lo + LANES], p2s[pidx][:, lo:lo + LANES]],
                                          axis=1).astype(BF16)
                    bp = lax.dot_general(bbar_s[d, rows_of(ch), sl], p12, contract_rows,
                                         preferred_element_type=F32)
                    kv = lax.dot_general(kbar_s[d, rows_of(ch), sl], v_s[rows_of(ch), sl], contract_rows,
                                         preferred_element_type=F32)
                    w_diag = jnp.where(diag, wend_s[d, ch * 8:ch * 8 + 1, sl], 0.0)
                    m_ref[ch, d, pr] = (w_diag - jnp.where(same_head, bp[:, 0:LANES], 0.0)).astype(m_ref.dtype)
                    n_ref[ch, d, pr] = jnp.where(same_head, kv - bp[:, LANES:2 * LANES], 0.0)
            yield

    staged_next = {}
    _interleave(elementwise_stage(), matmul_stage())
    for ref, key in ((xf_s, "xf"), (kh_s, "kh"), (bh_s, "bh"), (kbar_s, "kbar"), (bbar_s, "bbar"),
                     (rt_s, "rt"), (wend_s, "wend")):
        for d in range(2):
            ref[d] = staged_next[key][d]
    v_s[...] = staged_next["v"]


def _rwkv_prep(bp, mu, w0, w2cat, a0, a2cat, kkw, kaw, rkw, ones_seg, *, ctx_len):
    b, s, pw = bp.shape
    bw = kkw.shape[1]
    n_pairs = bw // LANES
    nc = s // CHUNK
    ctx_chunks = ctx_len // CHUNK
    sub = PREP_CHUNKS
    rows = sub * CHUNK
    assert nc % sub == 0 and ctx_chunks % sub == 0
    n_blocks = nc // sub
    rb = rows // 8
    kern = functools.partial(_rwkv_prep_kernel, ctx_chunks=ctx_chunks, n_chunks=nc)
    const = lambda bi, j: (0, 0)
    cur = lambda j: jnp.minimum(j, n_blocks - 1)
    done = lambda j: jnp.maximum(j - 1, 0)
    return pl.pallas_call(
        kern,
        grid=(b, n_blocks + 1),
        in_specs=[pl.BlockSpec((None, rows, pw), lambda bi, j: (bi, cur(j), 0)),
                  pl.BlockSpec((None, 8, pw), lambda bi, j: (bi, jnp.maximum(cur(j) * rb - 1, 0), 0)),
                  pl.BlockSpec((None, 8, pw),
                               lambda bi, j: (bi, jnp.minimum((cur(j) + 1) * rb, s // 8 - 1), 0)),
                  pl.BlockSpec((1, pw), const),
                  pl.BlockSpec((1, 2 * bw), const),
                  pl.BlockSpec((2 * B_LORA, 2 * bw), const),
                  pl.BlockSpec((1, 2 * bw), const),
                  pl.BlockSpec((2 * B_LORA, 2 * bw), const),
                  pl.BlockSpec((1, bw), const),
                  pl.BlockSpec((1, bw), const),
                  pl.BlockSpec((1, bw), const),
                  pl.BlockSpec((MXU_WIDTH, MXU_WIDTH), const)],
        out_specs=[pl.BlockSpec((None, sub, 2, n_pairs, LANES, LANES), lambda bi, j: (bi, done(j), 0, 0, 0, 0)),
                   pl.BlockSpec((None, sub, 2, n_pairs, LANES, LANES), lambda bi, j: (bi, done(j), 0, 0, 0, 0)),
                   pl.BlockSpec((None, 2, rows, bw), lambda bi, j: (bi, 0, done(j), 0)),
                   pl.BlockSpec((None, 2, rows, bw), lambda bi, j: (bi, 0, done(j), 0)),
                   pl.BlockSpec((None, rows, bw), lambda bi, j: (bi, cur(j), 0))],
        out_shape=[jax.ShapeDtypeStruct((b, nc, 2, n_pairs, LANES, LANES), BF16),
                   jax.ShapeDtypeStruct((b, nc, 2, n_pairs, LANES, LANES), F32),
                   jax.ShapeDtypeStruct((b, 2, s, bw), BF16),
                   jax.ShapeDtypeStruct((b, 2, s, bw), F32),
                   jax.ShapeDtypeStruct((b, s, bw), F32)],
        scratch_shapes=[pltpu.VMEM((2, 2 * rows, bw), BF16),
                        pltpu.VMEM((2, rows, bw), BF16),
                        pltpu.VMEM((2, rows, bw), BF16),
                        pltpu.VMEM((2, rows, bw), BF16),
                        pltpu.VMEM((2, rows, bw), BF16),
                        pltpu.VMEM((2, rows, bw), F32),
                        pltpu.VMEM((rows, bw), BF16),
                        pltpu.VMEM((2, 8 * sub, bw), F32)],
        compiler_params=_cparams(("parallel", "arbitrary")),
        name="rwkv_prep",
    )(bp, bp, bp, mu, w0, w2cat, a0, a2cat, kkw, kaw, rkw, ones_seg)


def _rwkv_scan_kernel(m0_ref, m1_ref, n0_ref, n1_ref, q0_ref, q1_ref, y00_ref, y01_ref,
                      o0_ref, o1_ref, h_ref):
    j = pl.program_id(0)

    @pl.when(j == 0)
    def _():
        h_ref[...] = jnp.zeros_like(h_ref)

    nb, n_pairs = m0_ref.shape[0], m0_ref.shape[1]
    dirs = ((m0_ref, n0_ref, q0_ref, y00_ref, o0_ref), (m1_ref, n1_ref, q1_ref, y01_ref, o1_ref))
    for d, (m_ref, n_ref, q_ref, y0_ref, o_ref) in enumerate(dirs):
        for bi in range(nb):
            for pr in range(n_pairs):
                sl = slice(pr * LANES, (pr + 1) * LANES)
                h = h_ref[d, bi, pr]
                hb = h.astype(BF16)
                o_ref[bi, :, sl] = y0_ref[bi, :, sl] + jnp.dot(
                    q_ref[bi, :, sl], hb, preferred_element_type=F32)
                h_ref[d, bi, pr] = n_ref[bi, pr] + jnp.dot(
                    m_ref[bi, pr], hb, preferred_element_type=F32)


def _rwkv_scan(m, n, q, y0, *, ctx_len):
    b, nc, _, n_pairs, _, _ = m.shape
    s, bw = q.shape[2], q.shape[3]
    cc = ctx_len // CHUNK

    def mem_chunk(d, j):
        if d == 0:
            return j
        return jnp.where(j < cc, cc - 1 - j, nc - 1 + cc - j)

    def mn_spec(d):
        return pl.BlockSpec((b, None, None, n_pairs, LANES, LANES),
                            lambda j: (0, mem_chunk(d, j), d, 0, 0, 0))

    def row_spec(d):
        return pl.BlockSpec((b, None, CHUNK, bw), lambda j: (0, d, mem_chunk(d, j), 0))

    return pl.pallas_call(
        _rwkv_scan_kernel,
        grid=(nc,),
        in_specs=[mn_spec(0), mn_spec(1), mn_spec(0), mn_spec(1),
                  row_spec(0), row_spec(1), row_spec(0), row_spec(1)],
        out_specs=[pl.BlockSpec((b, CHUNK, bw), lambda j: (0, mem_chunk(0, j), 0)),
                   pl.BlockSpec((b, CHUNK, bw), lambda j: (0, mem_chunk(1, j), 0))],
        out_shape=[jax.ShapeDtypeStruct((b, s, bw), F32), jax.ShapeDtypeStruct((b, s, bw), F32)],
        scratch_shapes=[pltpu.VMEM((2, b, n_pairs, LANES, LANES), F32)],
        compiler_params=_cparams(("arbitrary",)),
        name="rwkv_scan",
    )(m, m, n, n, q, q, y0, y0)


def _outproj_kernel(*refs, two_sources, ctx_tiles, rwkv):
    n_src = 2 if two_sources else 1
    src = refs[:n_src]
    at_ref, g_ref, w_ref, mod_ref, gpost_ref = refs[n_src:n_src + 5]
    rest = refs[n_src + 5:]
    out_ref = rest[-1]
    hq, dh, tm = at_ref.shape
    parts = [at_ref[...].reshape(hq * dh, tm).T]
    if rwkv:
        yf_ref, yb_ref, bonus_ref, seg_ref, gnw_ref, gnb_ref = rest[:6]
        seg = seg_ref[...]
        y_sum = bonus_ref[...]
        for y_ref in (yf_ref, yb_ref):
            y = y_ref[...]
            yc = y - _head_mean(y, seg)
            var = _head_mean(yc * yc, seg)
            y_sum = y_sum + yc * lax.rsqrt(var + GN_EPS) * gnw_ref[...] + gnb_ref[...]
        parts.append(y_sum)
    o = jnp.concatenate(parts, axis=1) if len(parts) > 1 else parts[0]
    u = (o * _silu(g_ref[...])).astype(BF16)
    y = jnp.dot(u, w_ref[...], preferred_element_type=F32)
    ms = jnp.mean(y * y, axis=-1, keepdims=True)
    yn = y * lax.rsqrt(ms + RMS_EPS) * gpost_ref[...]
    x = _stream_rows(src[0] if two_sources else None, src[-1], ctx_tiles, pl.program_id(1))
    out_ref[...] = x + mod_ref[2:3, :] * yn


def _outproj(sources, attn_t, gate, w_bf16, modsel, gpost, rwkv_parts, *, ctx_tiles, latent_only):
    two_sources = len(sources) == 2
    b, d = sources[0].shape[0], sources[0].shape[2]
    s = sum(a.shape[1] for a in sources)
    tm = ROW_TILE
    off = ctx_tiles if latent_only else 0
    n_tiles = s // tm - off
    assert not (two_sources and latent_only)
    row = lambda bi, i: (bi, i + off, 0)
    const = lambda bi, i: (0, 0)
    hq, dh = attn_t.shape[1], attn_t.shape[2]
    if two_sources:
        src_specs = _stream_specs(True, tm, d, ctx_tiles)
    else:
        src_specs = [pl.BlockSpec((None, tm, d), row)]
    in_specs = src_specs + [
        pl.BlockSpec((None, hq, dh, tm), lambda bi, i: (bi, 0, 0, i)),
        pl.BlockSpec((None, tm, gate.shape[2]), row),
        pl.BlockSpec(w_bf16.shape, const),
        pl.BlockSpec((None, None, 3, d),
                     lambda bi, i: (bi, jnp.minimum((i + off) // ctx_tiles, 1), 0, 0)),
        pl.BlockSpec((1, d), const)]
    args = [*sources, attn_t, gate, w_bf16, modsel, gpost]
    if rwkv_parts is not None:
        y_f, y_b, bonus, seg, gnw, gnb = rwkv_parts
        in_specs += [pl.BlockSpec((None, tm, y_f.shape[2]), row),
                     pl.BlockSpec((None, tm, y_b.shape[2]), row),
                     pl.BlockSpec((None, tm, bonus.shape[2]), row),
                     pl.BlockSpec((MXU_WIDTH, MXU_WIDTH), const),
                     pl.BlockSpec(gnw.shape, const),
                     pl.BlockSpec(gnb.shape, const)]
        args += [y_f, y_b, bonus, seg, gnw, gnb]
    kern = functools.partial(_outproj_kernel, two_sources=two_sources, ctx_tiles=ctx_tiles,
                             rwkv=rwkv_parts is not None)
    return pl.pallas_call(
        kern,
        grid=(b, n_tiles),
        in_specs=in_specs,
        out_specs=pl.BlockSpec((None, tm, d), lambda bi, i: (bi, i, 0)),
        out_shape=jax.ShapeDtypeStruct((b, n_tiles * tm, d), F32),
        compiler_params=_cparams(("parallel", "parallel")),
        name="outproj",
    )(*args)


def _rope_tables(n_latent, ctx_len):
    t = jnp.arange(n_latent)
    rowp = (t // GRID_W).astype(F32)
    colp = (t % GRID_W).astype(F32)
    axis_dim = HEAD_DIM // 2
    inv = ROPE_THETA ** (-jnp.arange(0, axis_dim, 2, dtype=F32) / axis_dim)
    ang = jnp.concatenate([rowp[:, None] * inv, colp[:, None] * inv], axis=-1)
    cos, sin = jnp.cos(ang), jnp.sin(ang)
    cos = jnp.concatenate([jnp.ones((ctx_len, axis_dim), F32), cos], axis=0)
    sin = jnp.concatenate([jnp.zeros((ctx_len, axis_dim), F32), sin], axis=0)
    cos_h = jnp.repeat(cos, 2, axis=1)
    sin_h = jnp.stack([-sin, sin], axis=-1).reshape(sin.shape[0], HEAD_DIM)
    return jnp.tile(cos_h, (1, LANES // HEAD_DIM)), jnp.tile(sin_h, (1, LANES // HEAD_DIM))


def _block_diag_lora(w2):
    r, w = w2.shape[1], w2.shape[2]
    z = jnp.zeros((r, w), w2.dtype)
    return jnp.concatenate([jnp.concatenate([w2[0], z], axis=1),
                            jnp.concatenate([z, w2[1]], axis=1)], axis=0)


def kernel(x, c, ctx, c_ctx, w_mod, b_mod, g_pre, g_post, w_in_even, w_out_even, qn_a, kn_a, mu_b, w0_b, w2_b, a0_b, a2_b, kk_b, ka_b, rk_b, gn_w_b, gn_b_b, w_in_odd, w_out_odd, qn_c, kn_c, sink_c):
    b, t, d = x.shape
    ctx_len = ctx.shape[1]
    s = ctx_len + t
    assert ctx_len % ROW_TILE == 0 and t % ROW_TILE == 0 and b + 1 <= 8
    ctx_tiles = ctx_len // ROW_TILE
    depth = w_mod.shape[0]

    cc = jnp.concatenate([c, c_ctx[None, :], jnp.zeros((8 - b - 1, d), F32)], axis=0)
    mod = _modulation(cc, w_mod, b_mod)
    mod = mod.reshape(depth, 8, 3, d)
    modsel = jnp.stack([jnp.broadcast_to(mod[:, b][:, None], (depth, b, 3, d)), mod[:, :b]], axis=2)

    cos_t, sin_t = _rope_tables(t, ctx_len)
    seg_mean = jnp.asarray(np.kron(np.eye(MXU_WIDTH // HEAD_DIM),
                                   np.full((HEAD_DIM, HEAD_DIM), 1.0 / HEAD_DIM)), BF16)
    seg_ones = jnp.asarray(np.kron(np.eye(MXU_WIDTH // HEAD_DIM), np.ones((HEAD_DIM, HEAD_DIM))), BF16)

    bw = kk_b.shape[1]
    a_width = w_out_even.shape[1] - bw
    a_heads = a_width // HEAD_DIM
    n_in = w_in_even.shape[2]
    b_proj = 3 * bw + 4 * B_LORA
    kv_width = (n_in - 2 * a_width - b_proj - bw) // 2
    a_kv_heads = kv_width // HEAD_DIM
    qk_width = a_width + kv_width
    gain0 = jnp.concatenate([jnp.tile(qn_a[0], a_heads), jnp.tile(kn_a[0], a_kv_heads)])[None, :]
    bp_lo = qk_width + kv_width
    g_lo = bp_lo + b_proj
    qh, kh, vth, bproj, gate0 = _inproj(
        [ctx, x], modsel[0], g_pre[0][None, :], w_in_even[0].astype(BF16), cos_t, sin_t, gain0, seg_mean,
        q_width=a_width, qk_width=qk_width, v_width=kv_width,
        splits=((bp_lo, g_lo), (g_lo, n_in)), out_dtypes=(F32, F32), ctx_tiles=ctx_tiles)
    oa_t = _dense_attention(qh, kh, vth, ctx_len=ctx_len)

    m_c, n_c, q_c, y0_c, bonus = _rwkv_prep(
        bproj, mu_b[0][None, :], w0_b[0].reshape(1, 2 * bw), _block_diag_lora(w2_b[0]).astype(BF16),
        a0_b[0].reshape(1, 2 * bw), _block_diag_lora(a2_b[0]).astype(BF16), kk_b[0][None, :], ka_b[0][None, :],
        rk_b[0].reshape(1, bw), seg_ones, ctx_len=ctx_len)
    y_f, y_b = _rwkv_scan(m_c, n_c, q_c, y0_c, ctx_len=ctx_len)

    xc = _outproj([ctx, x], oa_t, gate0, w_out_even[0].astype(BF16), modsel[0], g_post[0][None, :],
                  (y_f, y_b, bonus, seg_mean, gn_w_b[0][None, :], gn_b_b[0][None, :]),
                  ctx_tiles=ctx_tiles, latent_only=False)

    c_heads = sink_c.shape[1]
    c_width = c_heads * HEAD_DIM
    n_in1 = w_in_odd.shape[2]
    ckv_width = (n_in1 - 2 * c_width) // 2
    c_kv_heads = ckv_width // HEAD_DIM
    qk_width1 = c_width + ckv_width
    gain1 = jnp.concatenate([jnp.tile(qn_c[0], c_heads), jnp.tile(kn_c[0], c_kv_heads)])[None, :]
    qh1, kh1, vth1, gate1 = _inproj(
        [xc], modsel[1], g_pre[1][None, :], w_in_odd[0].astype(BF16), cos_t, sin_t, gain1, seg_mean,
        q_width=c_width, qk_width=qk_width1, v_width=ckv_width,
        splits=((qk_width1 + ckv_width, n_in1),), out_dtypes=(F32,), ctx_tiles=ctx_tiles)
    ow_t = _window_attention(qh1, kh1, vth1, sink_c[0], ctx_len=ctx_len)
    return _outproj([xc], ow_t, gate1, w_out_odd[0].astype(BF16), modsel[1], g_post[1][None, :], None,
                    ctx_tiles=ctx_tiles, latent_only=True)
```

```python
import functools

import numpy as np
import jax
import jax.numpy as jnp
from jax import lax
from jax.experimental import pallas as pl
from jax.experimental.pallas import tpu as pltpu

F32 = jnp.float32
BF16 = jnp.bfloat16
HIGHEST = lax.Precision.HIGHEST

HEAD_DIM = 64
LANES = 128
MXU_WIDTH = 256
RWKV_GROUP_LANES = 128
GRID_W = 64
Q_BLOCK = 128
WINDOW = 128
ROPE_THETA = 10000.0
RMS_EPS = 1e-6
GN_EPS = 64e-5
LOG2_E = float(np.log2(np.e))
Q_SCALE = HEAD_DIM ** -0.5 * LOG2_E
B_LORA = 64
CHUNK = 64
PREP_CHUNKS = 2
ROW_TILE = 256
PROJ_COL_CHUNK = 512
DENSE_Q_TILE = 128
DENSE_KV_TILE = 1024
WINDOW_BLOCKS = 16
VMEM_LIMIT = 56 * 1024 * 1024


def _cparams(sem):
    return pltpu.CompilerParams(dimension_semantics=sem, vmem_limit_bytes=VMEM_LIMIT)


def _head_mean(x, seg):
    width = x.shape[1]
    cols = []
    for lo in range(0, width, MXU_WIDTH):
        n = min(MXU_WIDTH, width - lo)
        cols.append(jnp.dot(x[:, lo:lo + n].astype(BF16), seg[0:n, 0:n], preferred_element_type=F32))
    return cols[0] if len(cols) == 1 else jnp.concatenate(cols, axis=1)


def _sigmoid(z):
    return 1.0 / (1.0 + jnp.exp(-z))


def _silu(z):
    return z * _sigmoid(z)


def _mod_kernel(c_ref, w_ref, b_ref, o_ref):
    o_ref[...] = jnp.dot(_silu(c_ref[...]), w_ref[...], precision=HIGHEST,
                         preferred_element_type=F32) + b_ref[...]


def _modulation(cc, w_mod, b_mod):
    depth, d, d3 = w_mod.shape
    nj = d3 // d
    return pl.pallas_call(
        _mod_kernel,
        grid=(depth, nj),
        in_specs=[pl.BlockSpec((8, d), lambda l, j: (0, 0)),
                  pl.BlockSpec((None, d, d), lambda l, j: (l, 0, j)),
                  pl.BlockSpec((None, 1, d), lambda l, j: (l, 0, j))],
        out_specs=pl.BlockSpec((None, 8, d), lambda l, j: (l, 0, j)),
        out_shape=jax.ShapeDtypeStruct((depth, 8, d3), F32),
        compiler_params=_cparams(("arbitrary", "arbitrary")),
        name="modulation",
    )(cc, w_mod, b_mod.reshape(depth, 1, d3))


def _stream_rows(ctx_ref, x_ref, ctx_tiles, tile):
    if ctx_ref is None:
        return x_ref[...]
    return jnp.where(tile < ctx_tiles, ctx_ref[...], x_ref[...])


def _stream_specs(two_sources, tm, d, ctx_tiles, tile_of=lambda i: i):
    if not two_sources:
        return [pl.BlockSpec((None, tm, d), lambda bi, i: (bi, tile_of(i), 0))]
    return [pl.BlockSpec((None, tm, d), lambda bi, i: (bi, jnp.minimum(tile_of(i), ctx_tiles - 1), 0)),
            pl.BlockSpec((None, tm, d), lambda bi, i: (bi, jnp.maximum(tile_of(i) - ctx_tiles, 0), 0))]


def _interleave(*stages):
    stages = list(stages)
    while stages:
        for stage in list(stages):
            if next(stage, "done") == "done":
                stages.remove(stage)


def _inproj_kernel(*refs, two_sources, ctx_tiles, n_tiles, q_width, qk_width, v_width, splits):
    n_src = 2 if two_sources else 1
    src = refs[:n_src]
    mod_ref, gpre_ref, w_ref, cos_ref, sin_ref, gain_ref, seg_ref = refs[n_src:n_src + 7]
    q_ref, k_ref, vt_ref = refs[n_src + 7:n_src + 10]
    extra_refs = refs[n_src + 10:-1]
    attn_s = refs[-1]
    step = pl.program_id(1)
    tile = jnp.minimum(step, n_tiles - 1)
    n_attn = qk_width + v_width

    @pl.when(step == 0)
    def _():
        attn_s[...] = jnp.zeros_like(attn_s)

    staged = []

    def project():
        x = _stream_rows(src[0] if two_sources else None, src[-1], ctx_tiles, tile)
        ms = jnp.mean(x * x, axis=-1, keepdims=True)
        h = x * lax.rsqrt(ms + RMS_EPS) * gpre_ref[...]
        hb = (h * (1.0 + mod_ref[1:2, :]) + mod_ref[0:1, :]).astype(BF16)
        yield
        for lo in range(0, n_attn, PROJ_COL_CHUNK):
            hi = min(lo + PROJ_COL_CHUNK, n_attn)
            staged.append((lo, hi, jnp.dot(hb, w_ref[:, lo:hi], preferred_element_type=F32)))
            yield
        for ref, (s_lo, s_hi) in zip(extra_refs, splits):
            for lo in range(s_lo, s_hi, PROJ_COL_CHUNK):
                hi = min(lo + PROJ_COL_CHUNK, s_hi)
                ref[:, lo - s_lo:hi - s_lo] = jnp.dot(hb, w_ref[:, lo:hi],
                                                      preferred_element_type=F32).astype(ref.dtype)
                yield

    def finish_previous():
        cos = cos_ref[...]
        sin = sin_ref[...]
        lane = lax.broadcasted_iota(jnp.int32, (1, LANES), 1)
        even_lane = (lane % 2) == 0
        qk = attn_s[:, 0:qk_width]
        msq_all = _head_mean(qk * qk, seg_ref[...])
        yield
        for g in range(qk_width // LANES):
            xg = attn_s[:, g * LANES:(g + 1) * LANES]
            msq = msq_all[:, g * LANES:(g + 1) * LANES]
            xn = xg * lax.rsqrt(msq + RMS_EPS) * gain_ref[:, g * LANES:(g + 1) * LANES]
            partner = jnp.where(even_lane, pltpu.roll(xn, LANES - 1, 1), pltpu.roll(xn, 1, 1))
            y = xn * cos + partner * sin
            is_q = g * LANES < q_width
            if is_q:
                y = y * Q_SCALE
            ref = q_ref if is_q else k_ref
            h0 = 2 * g if is_q else 2 * (g - q_width // LANES)
            ref[h0] = y[:, 0:HEAD_DIM].astype(ref.dtype)
            ref[h0 + 1] = y[:, HEAD_DIM:LANES].astype(ref.dtype)
            yield
        for g in range(v_width // LANES):
            vt = attn_s[:, qk_width + g * LANES:qk_width + (g + 1) * LANES].T
            vt_ref[2 * g] = vt[0:HEAD_DIM].astype(vt_ref.dtype)
            vt_ref[2 * g + 1] = vt[HEAD_DIM:LANES].astype(vt_ref.dtype)
            yield

    _interleave(project(), finish_previous())
    for lo, hi, acc in staged:
        attn_s[:, lo:hi] = acc


def _inproj(sources, modsel, gpre, w_bf16, cos_t, sin_t, gain, seg, *, q_width, qk_width, v_width,
            splits, out_dtypes, ctx_tiles):
    two_sources = len(sources) == 2
    b, d = sources[0].shape[0], sources[0].shape[2]
    s = sum(a.shape[1] for a in sources)
    n = w_bf16.shape[1]
    tm = ROW_TILE
    hq = q_width // HEAD_DIM
    hk = (qk_width - q_width) // HEAD_DIM
    hv = v_width // HEAD_DIM
    n_tiles = s // tm
    cur = lambda i: jnp.minimum(i, n_tiles - 1)
    done = lambda i: jnp.maximum(i - 1, 0)
    const = lambda bi, i: (0, 0)
    out_shapes = [jax.ShapeDtypeStruct((b, hq, s, HEAD_DIM), BF16),
                  jax.ShapeDtypeStruct((b, hk, s, HEAD_DIM), BF16),
                  jax.ShapeDtypeStruct((b, hv, HEAD_DIM, s), BF16)]
    out_specs = [pl.BlockSpec((None, hq, tm, HEAD_DIM), lambda bi, i: (bi, 0, done(i), 0)),
                 pl.BlockSpec((None, hk, tm, HEAD_DIM), lambda bi, i: (bi, 0, done(i), 0)),
                 pl.BlockSpec((None, hv, HEAD_DIM, tm), lambda bi, i: (bi, 0, 0, done(i)))]
    for (lo, hi), dt in zip(splits, out_dtypes):
        out_shapes.append(jax.ShapeDtypeStruct((b, s, hi - lo), dt))
        out_specs.append(pl.BlockSpec((None, tm, hi - lo), lambda bi, i: (bi, cur(i), 0)))
    kern = functools.partial(_inproj_kernel, two_sources=two_sources, ctx_tiles=ctx_tiles, n_tiles=n_tiles,
                             q_width=q_width, qk_width=qk_width, v_width=v_width, splits=splits)
    return pl.pallas_call(
        kern,
        grid=(b, n_tiles + 1),
        in_specs=_stream_specs(two_sources, tm, d, ctx_tiles, cur) + [
            pl.BlockSpec((None, None, 3, d),
                         lambda bi, i: (bi, jnp.minimum(cur(i) // ctx_tiles, 1), 0, 0)),
            pl.BlockSpec((1, d), const),
            pl.BlockSpec((d, n), const),
            pl.BlockSpec((tm, LANES), lambda bi, i: (done(i), 0)),
            pl.BlockSpec((tm, LANES), lambda bi, i: (done(i), 0)),
            pl.BlockSpec((1, qk_width), const),
            pl.BlockSpec((MXU_WIDTH, MXU_WIDTH), const)],
        out_specs=out_specs,
        out_shape=out_shapes,
        scratch_shapes=[pltpu.VMEM((tm, qk_width + v_width), F32)],
        compiler_params=_cparams(("parallel", "arbitrary")),
        name="inproj",
    )(*sources, modsel, gpre, w_bf16, cos_t, sin_t, gain, seg)


def _dense_attn_kernel(q_ref, k_ref, vt_ref, o_ref, *, tk, ctx_len):
    i = pl.program_id(2)
    r, tq, dh = q_ref.shape
    m_rows = r * tq
    n_keys = k_ref.shape[0]
    q = q_ref[...].reshape(m_rows, dh)
    contract_lanes = (((1,), (1,)), ((), ()))

    def scores(lo, hi):
        return lax.dot_general(k_ref[lo:hi, :], q, contract_lanes, preferred_element_type=F32)

    def attend(bounds):
        m = jnp.full((1, m_rows), -jnp.inf, F32)
        l = jnp.zeros((1, m_rows), F32)
        acc = jnp.zeros((dh, m_rows), F32)
        st = scores(*bounds[0])
        pending = None
        for j in range(len(bounds)):
            if pending is not None:
                (plo, phi), p_prev = pending
                acc = acc + jnp.dot(vt_ref[:, plo:phi], p_prev, preferred_element_type=F32)
            st_next = scores(*bounds[j + 1]) if j + 1 < len(bounds) else None
            m_new = jnp.maximum(m, jnp.max(st, axis=0, keepdims=True))
            alpha = jnp.exp2(m - m_new)
            p = jnp.exp2(st - m_new)
            l = alpha * l + jnp.sum(p, axis=0, keepdims=True)
            acc = acc * alpha
            pending = (bounds[j], p.astype(BF16))
            m, st = m_new, st_next
        (plo, phi), p_prev = pending
        acc = acc + jnp.dot(vt_ref[:, plo:phi], p_prev, preferred_element_type=F32)
        o = acc / l
        for hh in range(r):
            o_ref[hh] = o[:, hh * tq:(hh + 1) * tq].astype(o_ref.dtype)

    ctx_bounds = [(0, ctx_len)]
    all_bounds = ctx_bounds + [(lo, lo + tk) for lo in range(ctx_len, n_keys, tk)]

    @pl.when(i < ctx_len // tq)
    def _():
        attend(ctx_bounds)

    @pl.when(i >= ctx_len // tq)
    def _():
        attend(all_bounds)


def _dense_attention(qh, kh, vth, *, ctx_len):
    b, hq, s, dh = qh.shape
    g = kh.shape[1]
    r = hq // g
    tq = DENSE_Q_TILE
    assert (s - ctx_len) % DENSE_KV_TILE == 0 and ctx_len % tq == 0
    kern = functools.partial(_dense_attn_kernel, tk=DENSE_KV_TILE, ctx_len=ctx_len)
    return pl.pallas_call(
        kern,
        grid=(b, g, s // tq),
        in_specs=[pl.BlockSpec((None, r, tq, dh), lambda bi, gi, i: (bi, gi, i, 0)),
                  pl.BlockSpec((None, None, s, dh), lambda bi, gi, i: (bi, gi, 0, 0)),
                  pl.BlockSpec((None, None, dh, s), lambda bi, gi, i: (bi, gi, 0, 0))],
        out_specs=pl.BlockSpec((None, r, dh, tq), lambda bi, gi, i: (bi, gi, 0, i)),
        out_shape=jax.ShapeDtypeStruct((b, hq, dh, s), F32),
        compiler_params=_cparams(("parallel", "parallel", "parallel")),
        name="dense_attention",
    )(qh, kh, vth)


def _window_attn_kernel(sink_ref, q_ref, bias_first_ref, bias_mid_ref, bias_last_ref,
                        kc_ref, kp_ref, km_ref, kn_ref, vc_ref, vp_ref, vm_ref, vn_ref, o_ref):
    gi = pl.program_id(1)
    _, r, rows, dh = q_ref.shape
    tq = Q_BLOCK
    nb = rows // tq
    contract_lanes = (((1,), (1,)), ((), ()))
    k_all = jnp.concatenate([kp_ref[...], km_ref[0, 0], kn_ref[...]], axis=0)
    vt_all = jnp.concatenate([vp_ref[...], vm_ref[0, 0], vn_ref[...]], axis=1)
    k_ctx, vt_ctx = kc_ref[...], vc_ref[...]
    lane = lax.broadcasted_iota(jnp.int32, (1, r * tq), 1)
    sink = jnp.zeros((1, r * tq), F32)
    for hh in range(r):
        sink = jnp.where(lane // tq == hh, sink_ref[gi * r + hh] * LOG2_E, sink)

    def scores(qb):
        q = jnp.concatenate([q_ref[0, hh, qb * tq:(qb + 1) * tq, :] for hh in range(r)], axis=0)
        bias_ref = bias_first_ref if qb == 0 else (bias_last_ref if qb == nb - 1 else bias_mid_ref)
        s_loc = lax.dot_general(k_all[qb * tq:(qb + 3) * tq], q, contract_lanes,
                                preferred_element_type=F32) + bias_ref[...]
        s_ctx = lax.dot_general(k_ctx, q, contract_lanes, preferred_element_type=F32)
        return s_loc, s_ctx

    def finish(qb, p_loc, p_ctx, l):
        o = (jnp.dot(vt_all[:, qb * tq:(qb + 3) * tq], p_loc, preferred_element_type=F32)
             + jnp.dot(vt_ctx, p_ctx, preferred_element_type=F32)) / l
        for hh in range(r):
            o_ref[hh, :, qb * tq:(qb + 1) * tq] = o[:, hh * tq:(hh + 1) * tq].astype(o_ref.dtype)

    s_cur = scores(0)
    pending = None
    for qb in range(nb):
        if pending is not None:
            finish(*pending)
        s_next = scores(qb + 1) if qb + 1 < nb else None
        s_loc, s_ctx = s_cur
        m = jnp.maximum(jnp.maximum(jnp.max(s_loc, axis=0, keepdims=True),
                                    jnp.max(s_ctx, axis=0, keepdims=True)), sink)
        p_loc = jnp.exp2(s_loc - m)
        p_ctx = jnp.exp2(s_ctx - m)
        l = (jnp.sum(p_loc, axis=0, keepdims=True) + jnp.sum(p_ctx, axis=0, keepdims=True)
             + jnp.exp2(sink - m))
        pending = (qb, p_loc.astype(BF16), p_ctx.astype(BF16), l)
        s_cur = s_next
    finish(*pending)


def _window_bias(tq, r):
    key = np.arange(3 * tq)[:, None] - tq
    qpos = np.arange(tq)[None, :]
    band = np.abs(key - qpos) <= WINDOW
    variants = [band & (key >= 0), band, band & (key < tq)]
    table = np.stack([np.where(np.tile(v, (1, r)), 0.0, -np.inf) for v in variants])
    return jnp.asarray(table, F32)


def _window_attention(qh, kh, vth, sink, *, ctx_len):
    b, hq, s, dh = qh.shape
    g = kh.shape[1]
    r = hq // g
    tq = Q_BLOCK
    t = s - ctx_len
    nb = WINDOW_BLOCKS
    rows = nb * tq
    assert t % rows == 0 and ctx_len % tq == 0
    ns = t // rows
    off = ctx_len // tq
    last = t // tq - 1
    bias = _window_bias(tq, r)
    ctx = lambda bi, gi, i: (bi, gi, 0, 0)
    blk = lambda n_rows: (None, None, n_rows, dh)
    tblk = lambda n_cols: (None, None, dh, n_cols)
    prev_blk = lambda i: jnp.maximum(nb * i - 1, 0) + off
    next_blk = lambda i: jnp.minimum(nb * i + nb, last) + off
    el = pl.Element
    mid = lambda i: pl.multiple_of(ctx_len + i * rows, tq)
    return pl.pallas_call(
        _window_attn_kernel,
        grid=(b, g, ns),
        in_specs=[pl.BlockSpec(memory_space=pltpu.SMEM),
                  pl.BlockSpec((el(1), el(r), el(rows), el(dh)), lambda bi, gi, i: (bi, gi * r, mid(i), 0)),
                  pl.BlockSpec((None, 3 * tq, r * tq), lambda bi, gi, i: (jnp.minimum(i, 1), 0, 0)),
                  pl.BlockSpec((None, 3 * tq, r * tq), lambda bi, gi, i: (1, 0, 0)),
                  pl.BlockSpec((None, 3 * tq, r * tq),
                               lambda bi, gi, i: (jnp.where(i == ns - 1, 2, 1), 0, 0)),
                  pl.BlockSpec(blk(ctx_len), ctx),
                  pl.BlockSpec(blk(tq), lambda bi, gi, i: (bi, gi, prev_blk(i), 0)),
                  pl.BlockSpec((el(1), el(1), el(rows), el(dh)), lambda bi, gi, i: (bi, gi, mid(i), 0)),
                  pl.BlockSpec(blk(tq), lambda bi, gi, i: (bi, gi, next_blk(i), 0)),
                  pl.BlockSpec(tblk(ctx_len), ctx),
                  pl.BlockSpec(tblk(tq), lambda bi, gi, i: (bi, gi, 0, prev_blk(i))),
                  pl.BlockSpec((el(1), el(1), el(dh), el(rows)), lambda bi, gi, i: (bi, gi, 0, mid(i))),
                  pl.BlockSpec(tblk(tq), lambda bi, gi, i: (bi, gi, 0, next_blk(i)))],
        out_specs=pl.BlockSpec((None, r, dh, rows), lambda bi, gi, i: (bi, gi, 0, i)),
        out_shape=jax.ShapeDtypeStruct((b, hq, dh, t), F32),
        compiler_params=_cparams(("parallel", "parallel", "parallel")),
        name="window_attention",
    )(sink, qh, bias, bias, bias, kh, kh, kh, kh, vth, vth, vth, vth)


def _rwkv_prep_kernel(p_ref, hp_ref, hn_ref, mu_ref, w0_ref, w2_ref, a0_ref, a2_ref,
                      kk_ref, ka_ref, rk_ref, ones_ref,
                      m_ref, n_ref, q_ref, y0_ref, bonus_ref,
                      xf_s, kh_s, bh_s, kbar_s, bbar_s, rt_s, v_s, wend_s, *, ctx_chunks, n_chunks):
    step_id = pl.program_id(1)
    sub = PREP_CHUNKS
    c = CHUNK
    rows = sub * c
    jb = jnp.minimum(step_id, n_chunks // sub - 1)
    first_chunk = jb * sub
    last_chunk = first_chunk + sub - 1
    bw = kk_ref.shape[1]
    n_pairs = bw // LANES
    staged = (xf_s, kh_s, bh_s, kbar_s, bbar_s, rt_s, v_s, wend_s)

    @pl.when(step_id == 0)
    def _():
        for ref in staged:
            ref[...] = jnp.zeros_like(ref)

    gw = RWKV_GROUP_LANES
    hpg = gw // HEAD_DIM
    n_groups = bw // gw
    ti = lax.broadcasted_iota(jnp.int32, (rows, rows), 0)
    tj = lax.broadcasted_iota(jnp.int32, (rows, rows), 1)
    same_chunk = (ti // c) == (tj // c)
    ti_g = lax.broadcasted_iota(jnp.int32, (c, gw), 0)
    tj_g = lax.broadcasted_iota(jnp.int32, (c, gw), 1) % c
    eye_g = (ti_g == tj_g).astype(F32)
    gi_r = lax.broadcasted_iota(jnp.int32, (gw, gw), 0)
    gi_c = lax.broadcasted_iota(jnp.int32, (gw, gw), 1)
    group_diag = (gi_r // HEAD_DIM) == (gi_c // HEAD_DIM)
    head0_lanes = lax.broadcasted_iota(jnp.int32, (1, LANES), 1) < HEAD_DIM

    def head_blocks(x):
        return jnp.where(head0_lanes, x[0:HEAD_DIM], x[HEAD_DIM:LANES])

    scan_masks = [(tj_g < ti_g, tj_g <= ti_g), (tj_g > ti_g, tj_g >= ti_g)]

    def blockdiag(xs):
        xb = xs.astype(BF16)
        return jnp.where(group_diag, jnp.concatenate([xb] * hpg, axis=0), jnp.zeros((), BF16))


    def elementwise_stage():
        p = p_ref[...]
        has_prev = jnp.logical_and(first_chunk != 0, first_chunk != ctx_chunks)
        has_next = jnp.logical_and(last_chunk != ctx_chunks - 1, last_chunk != n_chunks - 1)
        prev_row = jnp.where(has_prev, hp_ref[7:8, :], 0.0)
        next_row = jnp.where(has_next, hn_ref[0:1, :], 0.0)
        rowi = lax.broadcasted_iota(jnp.int32, (rows, 1), 0)
        p_prev = jnp.where(rowi == 0, prev_row, pltpu.roll(p, 1, 0))
        p_next = jnp.where(rowi == rows - 1, next_row, pltpu.roll(p, rows - 1, 0))
        ps = p + mu_ref[...] * (0.5 * (p_prev + p_next) - p)
        r = ps[:, 0:bw]
        k = ps[:, bw:2 * bw]
        v = ps[:, 2 * bw:3 * bw]
        zw = ps[:, 3 * bw:3 * bw + 2 * B_LORA]
        za = ps[:, 3 * bw + 2 * B_LORA:3 * bw + 4 * B_LORA]
        yield

        wl = w0_ref[...] + jnp.dot(jnp.tanh(zw).astype(BF16), w2_ref[...],
                                   preferred_element_type=F32)
        z = -wl
        softplus = jnp.maximum(z, 0.0) + jnp.log(1.0 + jnp.exp(-jnp.abs(z)))
        logw = -jnp.exp(-softplus - 0.5)
        yield
        a = _sigmoid(a0_ref[...] + jnp.dot(za.astype(BF16), a2_ref[...],
                                           preferred_element_type=F32))
        ones_seg = ones_ref[...]
        kf = k * kk_ref[...]
        kk = kf / jnp.maximum(jnp.sqrt(_head_mean(kf * kf, ones_seg)), 1e-12)
        yield

        bonus = jnp.zeros((rows, bw), F32)
        chunk_rows = [slice(ch * c, (ch + 1) * c) for ch in range(sub)]
        out = dict(xf=[], kh=[], bh=[], kbar=[], bbar=[], rt=[], wend=[])
        for d in range(2):
            lw = logw[:, d * bw:(d + 1) * bw]
            a_d = a[:, d * bw:(d + 1) * bw]
            b_d = kk * a_d
            kmod = k * (1.0 + (a_d - 1.0) * ka_ref[...])
            bonus = bonus + _head_mean(r * kmod * rk_ref[...], ones_seg) * v
            yield

            tri = (same_chunk & ((tj <= ti) if d == 0 else (tj >= ti))).astype(BF16)
            lw_hi = lw.astype(BF16)
            lw_lo = (lw - lw_hi.astype(F32)).astype(BF16)
            cum = (jnp.dot(tri, lw_hi, preferred_element_type=F32)
                   + jnp.dot(tri, lw_lo, preferred_element_type=F32))
            ends = [cum[rs.stop - 1:rs.stop] if d == 0 else cum[rs.start:rs.start + 1] for rs in chunk_rows]
            cum_end = jnp.concatenate([jnp.broadcast_to(e, (c, bw)) for e in ends], axis=0)
            r_t = r * jnp.exp(cum)
            kk_t = kk * jnp.exp(cum - lw)
            yield
            e_out = jnp.exp(-cum)
            tail = jnp.exp(cum_end - cum)
            out["xf"].append(jnp.concatenate([part[rs] for rs in chunk_rows for part in (kk_t, r_t)],
                                             axis=0).astype(BF16))
            out["rt"].append(r_t)
            out["kh"].append((kmod * e_out).astype(BF16))
            out["bh"].append((b_d * e_out).astype(BF16))
            out["kbar"].append((kmod * tail).astype(BF16))
            out["bbar"].append((b_d * tail).astype(BF16))
            out["wend"].append(jnp.concatenate([jnp.broadcast_to(jnp.exp(e), (8, bw)) for e in ends], axis=0))
            yield
        bonus_ref[...] = bonus
        out["v"] = v.astype(BF16)
        staged_next.update(out)

    def matmul_stage():
        contract_lanes = (((1,), (1,)), ((), ()))
        contract_rows = (((0,), (0,)), ((), ()))
        probs = [(d, g, ch) for d in range(2) for g in range(n_groups) for ch in range(sub)]
        lanes_of = lambda g: slice(g * gw, (g + 1) * gw)
        rows_of = lambda ch: slice(ch * c, (ch + 1) * c)
        xf_of = lambda d, g, ch: xf_s[d, ch * 2 * c:(ch + 1) * 2 * c, lanes_of(g)]
        v_bd = {(g, ch): blockdiag(v_s[rows_of(ch), lanes_of(g)]) for g in range(n_groups) for ch in range(sub)}
        aks = [lax.dot_general(xf_of(d, g, ch), blockdiag(kh_s[d, rows_of(ch), lanes_of(g)]), contract_lanes,
                               preferred_element_type=F32) for d, g, ch in probs]
        yield
        abs_ = [lax.dot_general(xf_of(d, g, ch), blockdiag(bh_s[d, rows_of(ch), lanes_of(g)]), contract_lanes,
                                preferred_element_type=F32) for d, g, ch in probs]
        yield
        a_ks = [jnp.concatenate([jnp.where(scan_masks[d][0], ak[0:c], 0.0),
                                 jnp.where(scan_masks[d][1], ak[c:2 * c], 0.0)], axis=0).astype(BF16)
                for (d, g, ch), ak in zip(probs, aks)]
        a_qbs = [jnp.where(scan_masks[d][1], ab[c:2 * c], 0.0).astype(BF16)
                 for (d, g, ch), ab in zip(probs, abs_)]
        pws = [jnp.where(scan_masks[d][0], -ab[0:c], 0.0) for (d, g, ch), ab in zip(probs, abs_)]
        tinvs = [eye_g + pw for pw in pws]
        pws = [jnp.dot(pw.astype(BF16), blockdiag(pw), preferred_element_type=F32) for pw in pws]
        yield
        for _ in range(4):
            stacked = [jnp.dot(jnp.concatenate([t, pw], axis=0).astype(BF16), blockdiag(pw),
                               preferred_element_type=F32) for t, pw in zip(tinvs, pws)]
            tinvs = [t + st[0:c] for t, st in zip(tinvs, stacked)]
            pws = [st[c:2 * c] for st in stacked]
            yield
        tinvs = [(t + jnp.dot(t.astype(BF16), blockdiag(pw), preferred_element_type=F32)).astype(BF16)
                 for t, pw in zip(tinvs, pws)]
        avs = [jnp.dot(a_k, v_bd[g, ch], preferred_element_type=F32) for (d, g, ch), a_k in zip(probs, a_ks)]
        yield
        p1s = [jnp.dot(t, blockdiag(xf_of(d, g, ch)[0:c]), preferred_element_type=F32)
               for (d, g, ch), t in zip(probs, tinvs)]
        p2s = [jnp.dot(t, blockdiag(av[0:c]), preferred_element_type=F32) for t, av in zip(tinvs, avs)]
        yield
        qp1s = [jnp.dot(a_qb, blockdiag(p1), preferred_element_type=F32) for a_qb, p1 in zip(a_qbs, p1s)]
        qp2s = [jnp.dot(a_qb, blockdiag(p2), preferred_element_type=F32) for a_qb, p2 in zip(a_qbs, p2s)]
        for (d, g, ch), av, qp1, qp2 in zip(probs, avs, qp1s, qp2s):
            q_ref[d, rows_of(ch), lanes_of(g)] = (rt_s[d, rows_of(ch), lanes_of(g)] - qp1).astype(q_ref.dtype)
            y0_ref[d, rows_of(ch), lanes_of(g)] = av[c:2 * c] - qp2
        yield
        for d in range(2):
            for pr in range(n_pairs):
                for ch in range(sub):
                    sl = slice(pr * LANES, (pr + 1) * LANES)
                    pidx = probs.index((d, pr * LANES // gw, ch))
                    lo = pr * LANES % gw
                    p12 = jnp.concatenate([p1s[pidx][:, lo:lo + LANES], p2s[pidx][:, lo:lo + LANES]],
                                          axis=1).astype(BF16)
                    bp = lax.dot_general(bbar_s[d, rows_of(ch), sl], p12, contract_rows,
                                         preferred_element_type=F32)
                    kv = lax.dot_general(kbar_s[d, rows_of(ch), sl], v_s[rows_of(ch), sl], contract_rows,
                                         preferred_element_type=F32)
                    w_diag = eye_g * wend_s[d, ch * 8:ch * 8 + 1, sl]
                    m_ref[ch, d, pr] = (w_diag - head_blocks(bp[:, 0:LANES])).astype(m_ref.dtype)
                    n_ref[ch, d, pr] = head_blocks(kv - bp[:, LANES:2 * LANES])
            yield

    staged_next = {}
    _interleave(elementwise_stage(), matmul_stage())
    for ref, key in ((xf_s, "xf"), (kh_s, "kh"), (bh_s, "bh"), (kbar_s, "kbar"), (bbar_s, "bbar"),
                     (rt_s, "rt"), (wend_s, "wend")):
        for d in range(2):
            ref[d] = staged_next[key][d]
    v_s[...] = staged_next["v"]


def _rwkv_prep(bp, mu, w0, w2cat, a0, a2cat, kkw, kaw, rkw, ones_seg, *, ctx_len):
    b, s, pw = bp.shape
    bw = kkw.shape[1]
    n_pairs = bw // LANES
    nc = s // CHUNK
    ctx_chunks = ctx_len // CHUNK
    sub = PREP_CHUNKS
    rows = sub * CHUNK
    assert nc % sub == 0 and ctx_chunks % sub == 0
    n_blocks = nc // sub
    rb = rows // 8
    kern = functools.partial(_rwkv_prep_kernel, ctx_chunks=ctx_chunks, n_chunks=nc)
    const = lambda bi, j: (0, 0)
    cur = lambda j: jnp.minimum(j, n_blocks - 1)
    done = lambda j: jnp.maximum(j - 1, 0)
    return pl.pallas_call(
        kern,
        grid=(b, n_blocks + 1),
        in_specs=[pl.BlockSpec((None, rows, pw), lambda bi, j: (bi, cur(j), 0)),
                  pl.BlockSpec((None, 8, pw), lambda bi, j: (bi, jnp.maximum(cur(j) * rb - 1, 0), 0)),
                  pl.BlockSpec((None, 8, pw),
                               lambda bi, j: (bi, jnp.minimum((cur(j) + 1) * rb, s // 8 - 1), 0)),
                  pl.BlockSpec((1, pw), const),
                  pl.BlockSpec((1, 2 * bw), const),
                  pl.BlockSpec((2 * B_LORA, 2 * bw), const),
                  pl.BlockSpec((1, 2 * bw), const),
                  pl.BlockSpec((2 * B_LORA, 2 * bw), const),
                  pl.BlockSpec((1, bw), const),
                  pl.BlockSpec((1, bw), const),
                  pl.BlockSpec((1, bw), const),
                  pl.BlockSpec((MXU_WIDTH, MXU_WIDTH), const)],
        out_specs=[pl.BlockSpec((None, sub, 2, n_pairs, HEAD_DIM, LANES), lambda bi, j: (bi, done(j), 0, 0, 0, 0)),
                   pl.BlockSpec((None, sub, 2, n_pairs, HEAD_DIM, LANES), lambda bi, j: (bi, done(j), 0, 0, 0, 0)),
                   pl.BlockSpec((None, 2, rows, bw), lambda bi, j: (bi, 0, done(j), 0)),
                   pl.BlockSpec((None, 2, rows, bw), lambda bi, j: (bi, 0, done(j), 0)),
                   pl.BlockSpec((None, rows, bw), lambda bi, j: (bi, cur(j), 0))],
        out_shape=[jax.ShapeDtypeStruct((b, nc, 2, n_pairs, HEAD_DIM, LANES), BF16),
                   jax.ShapeDtypeStruct((b, nc, 2, n_pairs, HEAD_DIM, LANES), F32),
                   jax.ShapeDtypeStruct((b, 2, s, bw), BF16),
                   jax.ShapeDtypeStruct((b, 2, s, bw), F32),
                   jax.ShapeDtypeStruct((b, s, bw), F32)],
        scratch_shapes=[pltpu.VMEM((2, 2 * rows, bw), BF16),
                        pltpu.VMEM((2, rows, bw), BF16),
                        pltpu.VMEM((2, rows, bw), BF16),
                        pltpu.VMEM((2, rows, bw), BF16),
                        pltpu.VMEM((2, rows, bw), BF16),
                        pltpu.VMEM((2, rows, bw), F32),
                        pltpu.VMEM((rows, bw), BF16),
                        pltpu.VMEM((2, 8 * sub, bw), F32)],
        compiler_params=_cparams(("parallel", "arbitrary")),
        name="rwkv_prep",
    )(bp, bp, bp, mu, w0, w2cat, a0, a2cat, kkw, kaw, rkw, ones_seg)


def _rwkv_scan_kernel(m0_ref, m1_ref, n0_ref, n1_ref, q0_ref, q1_ref, y00_ref, y01_ref,
                      o0_ref, o1_ref, h_ref):
    j = pl.program_id(0)

    @pl.when(j == 0)
    def _():
        h_ref[...] = jnp.zeros_like(h_ref)

    nb, n_pairs = m0_ref.shape[0], m0_ref.shape[1]
    gi_r = lax.broadcasted_iota(jnp.int32, (LANES, LANES), 0)
    gi_c = lax.broadcasted_iota(jnp.int32, (LANES, LANES), 1)
    same_head = (gi_r // HEAD_DIM) == (gi_c // HEAD_DIM)
    dirs = ((m0_ref, n0_ref, q0_ref, y00_ref, o0_ref), (m1_ref, n1_ref, q1_ref, y01_ref, o1_ref))
    for d, (m_ref, n_ref, q_ref, y0_ref, o_ref) in enumerate(dirs):
        for bi in range(nb):
            for pr in range(n_pairs):
                sl = slice(pr * LANES, (pr + 1) * LANES)
                hb = h_ref[d, bi, pr].astype(BF16)
                h_bd = jnp.where(same_head, jnp.concatenate([hb] * (LANES // HEAD_DIM), axis=0),
                                 jnp.zeros((), BF16))
                o_ref[bi, :, sl] = y0_ref[bi, :, sl] + jnp.dot(
                    q_ref[bi, :, sl], h_bd, preferred_element_type=F32)
                h_ref[d, bi, pr] = n_ref[bi, pr] + jnp.dot(
                    m_ref[bi, pr], h_bd, preferred_element_type=F32)


def _rwkv_scan(m, n, q, y0, *, ctx_len):
    b, nc, _, n_pairs, _, _ = m.shape
    s, bw = q.shape[2], q.shape[3]
    cc = ctx_len // CHUNK

    def mem_chunk(d, j):
        if d == 0:
            return j
        return jnp.where(j < cc, cc - 1 - j, nc - 1 + cc - j)

    def mn_spec(d):
        return pl.BlockSpec((b, None, None, n_pairs, HEAD_DIM, LANES),
                            lambda j: (0, mem_chunk(d, j), d, 0, 0, 0))

    def row_spec(d):
        return pl.BlockSpec((b, None, CHUNK, bw), lambda j: (0, d, mem_chunk(d, j), 0))

    return pl.pallas_call(
        _rwkv_scan_kernel,
        grid=(nc,),
        in_specs=[mn_spec(0), mn_spec(1), mn_spec(0), mn_spec(1),
                  row_spec(0), row_spec(1), row_spec(0), row_spec(1)],
        out_specs=[pl.BlockSpec((b, CHUNK, bw), lambda j: (0, mem_chunk(0, j), 0)),
                   pl.BlockSpec((b, CHUNK, bw), lambda j: (0, mem_chunk(1, j), 0))],
        out_shape=[jax.ShapeDtypeStruct((b, s, bw), F32), jax.ShapeDtypeStruct((b, s, bw), F32)],
        scratch_shapes=[pltpu.VMEM((2, b, n_pairs, HEAD_DIM, LANES), F32)],
        compiler_params=_cparams(("arbitrary",)),
        name="rwkv_scan",
    )(m, m, n, n, q, q, y0, y0)


def _outproj_kernel(*refs, two_sources, ctx_tiles, rwkv):
    n_src = 2 if two_sources else 1
    src = refs[:n_src]
    at_ref, g_ref, w_ref, mod_ref, gpost_ref = refs[n_src:n_src + 5]
    rest = refs[n_src + 5:]
    out_ref = rest[-1]
    hq, dh, tm = at_ref.shape
    parts = [at_ref[...].reshape(hq * dh, tm).T]
    if rwkv:
        yf_ref, yb_ref, bonus_ref, seg_ref, gnw_ref, gnb_ref = rest[:6]
        seg = seg_ref[...]
        y_sum = bonus_ref[...]
        for y_ref in (yf_ref, yb_ref):
            y = y_ref[...]
            yc = y - _head_mean(y, seg)
            var = _head_mean(yc * yc, seg)
            y_sum = y_sum + yc * lax.rsqrt(var + GN_EPS) * gnw_ref[...] + gnb_ref[...]
        parts.append(y_sum)
    o = jnp.concatenate(parts, axis=1) if len(parts) > 1 else parts[0]
    u = (o * _silu(g_ref[...])).astype(BF16)
    y = jnp.dot(u, w_ref[...], preferred_element_type=F32)
    ms = jnp.mean(y * y, axis=-1, keepdims=True)
    yn = y * lax.rsqrt(ms + RMS_EPS) * gpost_ref[...]
    x = _stream_rows(src[0] if two_sources else None, src[-1], ctx_tiles, pl.program_id(1))
    out_ref[...] = x + mod_ref[2:3, :] * yn


def _outproj(sources, attn_t, gate, w_bf16, modsel, gpost, rwkv_parts, *, ctx_tiles, latent_only):
    two_sources = len(sources) == 2
    b, d = sources[0].shape[0], sources[0].shape[2]
    s = sum(a.shape[1] for a in sources)
    tm = ROW_TILE
    off = ctx_tiles if latent_only else 0
    n_tiles = s // tm - off
    assert not (two_sources and latent_only)
    row = lambda bi, i: (bi, i + off, 0)
    const = lambda bi, i: (0, 0)
    hq, dh = attn_t.shape[1], attn_t.shape[2]
    if two_sources:
        src_specs = _stream_specs(True, tm, d, ctx_tiles)
    else:
        src_specs = [pl.BlockSpec((None, tm, d), row)]
    in_specs = src_specs + [
        pl.BlockSpec((None, hq, dh, tm), lambda bi, i: (bi, 0, 0, i)),
        pl.BlockSpec((None, tm, gate.shape[2]), row),
        pl.BlockSpec(w_bf16.shape, const),
        pl.BlockSpec((None, None, 3, d),
                     lambda bi, i: (bi, jnp.minimum((i + off) // ctx_tiles, 1), 0, 0)),
        pl.BlockSpec((1, d), const)]
    args = [*sources, attn_t, gate, w_bf16, modsel, gpost]
    if rwkv_parts is not None:
        y_f, y_b, bonus, seg, gnw, gnb = rwkv_parts
        in_specs += [pl.BlockSpec((None, tm, y_f.shape[2]), row),
                     pl.BlockSpec((None, tm, y_b.shape[2]), row),
                     pl.BlockSpec((None, tm, bonus.shape[2]), row),
                     pl.BlockSpec((MXU_WIDTH, MXU_WIDTH), const),
                     pl.BlockSpec(gnw.shape, const),
                     pl.BlockSpec(gnb.shape, const)]
        args += [y_f, y_b, bonus, seg, gnw, gnb]
    kern = functools.partial(_outproj_kernel, two_sources=two_sources, ctx_tiles=ctx_tiles,
                             rwkv=rwkv_parts is not None)
    return pl.pallas_call(
        kern,
        grid=(b, n_tiles),
        in_specs=in_specs,
        out_specs=pl.BlockSpec((None, tm, d), lambda bi, i: (bi, i, 0)),
        out_shape=jax.ShapeDtypeStruct((b, n_tiles * tm, d), F32),
        compiler_params=_cparams(("parallel", "parallel")),
        name="outproj",
    )(*args)


def _rope_tables(n_latent, ctx_len):
    t = jnp.arange(n_latent)
    rowp = (t // GRID_W).astype(F32)
    colp = (t % GRID_W).astype(F32)
    axis_dim = HEAD_DIM // 2
    inv = ROPE_THETA ** (-jnp.arange(0, axis_dim, 2, dtype=F32) / axis_dim)
    ang = jnp.concatenate([rowp[:, None] * inv, colp[:, None] * inv], axis=-1)
    cos, sin = jnp.cos(ang), jnp.sin(ang)
    cos = jnp.concatenate([jnp.ones((ctx_len, axis_dim), F32), cos], axis=0)
    sin = jnp.concatenate([jnp.zeros((ctx_len, axis_dim), F32), sin], axis=0)
    cos_h = jnp.repeat(cos, 2, axis=1)
    sin_h = jnp.stack([-sin, sin], axis=-1).reshape(sin.shape[0], HEAD_DIM)
    return jnp.tile(cos_h, (1, LANES // HEAD_DIM)), jnp.tile(sin_h, (1, LANES // HEAD_DIM))


def _block_diag_lora(w2):
    r, w = w2.shape[1], w2.shape[2]
    z = jnp.zeros((r, w), w2.dtype)
    return jnp.concatenate([jnp.concatenate([w2[0], z], axis=1),
                            jnp.concatenate([z, w2[1]], axis=1)], axis=0)


def kernel(x, c, ctx, c_ctx, w_mod, b_mod, g_pre, g_post, w_in_even, w_out_even, qn_a, kn_a, mu_b, w0_b, w2_b, a0_b, a2_b, kk_b, ka_b, rk_b, gn_w_b, gn_b_b, w_in_odd, w_out_odd, qn_c, kn_c, sink_c):
    b, t, d = x.shape
    ctx_len = ctx.shape[1]
    s = ctx_len + t
    assert ctx_len % ROW_TILE == 0 and t % ROW_TILE == 0 and b + 1 <= 8
    ctx_tiles = ctx_len // ROW_TILE
    depth = w_mod.shape[0]

    cc = jnp.concatenate([c, c_ctx[None, :], jnp.zeros((8 - b - 1, d), F32)], axis=0)
    mod = _modulation(cc, w_mod, b_mod)
    mod = mod.reshape(depth, 8, 3, d)
    modsel = jnp.stack([jnp.broadcast_to(mod[:, b][:, None], (depth, b, 3, d)), mod[:, :b]], axis=2)

    cos_t, sin_t = _rope_tables(t, ctx_len)
    seg_mean = jnp.asarray(np.kron(np.eye(MXU_WIDTH // HEAD_DIM),
                                   np.full((HEAD_DIM, HEAD_DIM), 1.0 / HEAD_DIM)), BF16)
    seg_ones = jnp.asarray(np.kron(np.eye(MXU_WIDTH // HEAD_DIM), np.ones((HEAD_DIM, HEAD_DIM))), BF16)

    bw = kk_b.shape[1]
    a_width = w_out_even.shape[1] - bw
    a_heads = a_width // HEAD_DIM
    n_in = w_in_even.shape[2]
    b_proj = 3 * bw + 4 * B_LORA
    kv_width = (n_in - 2 * a_width - b_proj - bw) // 2
    a_kv_heads = kv_width // HEAD_DIM
    qk_width = a_width + kv_width
    gain0 = jnp.concatenate([jnp.tile(qn_a[0], a_heads), jnp.tile(kn_a[0], a_kv_heads)])[None, :]
    bp_lo = qk_width + kv_width
    g_lo = bp_lo + b_proj
    qh, kh, vth, bproj, gate0 = _inproj(
        [ctx, x], modsel[0], g_pre[0][None, :], w_in_even[0].astype(BF16), cos_t, sin_t, gain0, seg_mean,
        q_width=a_width, qk_width=qk_width, v_width=kv_width,
        splits=((bp_lo, g_lo), (g_lo, n_in)), out_dtypes=(F32, F32), ctx_tiles=ctx_tiles)
    oa_t = _dense_attention(qh, kh, vth, ctx_len=ctx_len)

    m_c, n_c, q_c, y0_c, bonus = _rwkv_prep(
        bproj, mu_b[0][None, :], w0_b[0].reshape(1, 2 * bw), _block_diag_lora(w2_b[0]).astype(BF16),
        a0_b[0].reshape(1, 2 * bw), _block_diag_lora(a2_b[0]).astype(BF16), kk_b[0][None, :], ka_b[0][None, :],
        rk_b[0].reshape(1, bw), seg_ones, ctx_len=ctx_len)
    y_f, y_b = _rwkv_scan(m_c, n_c, q_c, y0_c, ctx_len=ctx_len)

    xc = _outproj([ctx, x], oa_t, gate0, w_out_even[0].astype(BF16), modsel[0], g_post[0][None, :],
                  (y_f, y_b, bonus, seg_mean, gn_w_b[0][None, :], gn_b_b[0][None, :]),
                  ctx_tiles=ctx_tiles, latent_only=False)

    c_heads = sink_c.shape[1]
    c_width = c_heads * HEAD_DIM
    n_in1 = w_in_odd.shape[2]
    ckv_width = (n_in1 - 2 * c_width) // 2
    c_kv_heads = ckv_width // HEAD_DIM
    qk_width1 = c_width + ckv_width
    gain1 = jnp.concatenate([jnp.tile(qn_c[0], c_heads), jnp.tile(kn_c[0], c_kv_heads)])[None, :]
    qh1, kh1, vth1, gate1 = _inproj(
        [xc], modsel[1], g_pre[1][None, :], w_in_odd[0].astype(BF16), cos_t, sin_t, gain1, seg_mean,
        q_width=c_width, qk_width=qk_width1, v_width=ckv_width,
        splits=((qk_width1 + ckv_width, n_in1),), out_dtypes=(F32,), ctx_tiles=ctx_tiles)
    ow_t = _window_attention(qh1, kh1, vth1, sink_c[0], ctx_len=ctx_len)
    return _outproj([xc], ow_t, gate1, w_out_odd[0].astype(BF16), modsel[1], g_post[1][None, :], None,
                    ctx_tiles=ctx_tiles, latent_only=True)
```
